```python
import math
import jax
import jax.numpy as jnp
from jax import lax
import numpy as np

D_MODEL = 1024
BATCH = 2
SEQ = 8192
DEPTH = 2


HEAD_DIM = 64
N_HEADS = D_MODEL // HEAD_DIM
MEM_LEN = 256
MEM_HEADS = 4
SB_HEADS = (N_HEADS - MEM_HEADS) // 2
DIFF_HEADS = (N_HEADS - MEM_HEADS) // 2
DIFF_QK_DIM = HEAD_DIM // 2
DIFF_V_DIM = 2 * DIFF_QK_DIM
SWA_Q_HEADS = N_HEADS - MEM_HEADS
SWA_KV_HEADS = SWA_Q_HEADS // 4
SWA_GROUP = SWA_Q_HEADS // SWA_KV_HEADS
WINDOW = 128
BLOCK = 128
ROPE_THETA = 500000.0
ROPE_FRACTION = 4
MIX_WIDTH = N_HEADS * HEAD_DIM
DEEPNORM_ALPHA = (2 * DEPTH) ** 0.25
DEEPNORM_BETA = (8 * DEPTH) ** -0.25
LN_EPS = 1e-5
NEG_BIG = -1e30
N_EVEN = (DEPTH + 1) // 2
N_ODD = DEPTH // 2

EVEN_WIDTHS = (SB_HEADS * HEAD_DIM, SB_HEADS * HEAD_DIM, SB_HEADS * HEAD_DIM,
               DIFF_HEADS * 2 * DIFF_QK_DIM, DIFF_HEADS * 2 * DIFF_QK_DIM, DIFF_HEADS * DIFF_V_DIM,
               MEM_HEADS * HEAD_DIM, MIX_WIDTH)
EVEN_VALUE_SLOTS = (2, 5)
ODD_WIDTHS = (SWA_Q_HEADS * HEAD_DIM, SWA_KV_HEADS * HEAD_DIM, SWA_KV_HEADS * HEAD_DIM,
              MEM_HEADS * HEAD_DIM, MIX_WIDTH)
ODD_VALUE_SLOTS = (2,)
EVEN_IN = sum(EVEN_WIDTHS)
ODD_IN = sum(ODD_WIDTHS)

kernel_name = 'hybrid_stickbreak_diff_swa_memory_deepnorm'


def _split(h, widths):
    outs = []
    start = 0
    for w in widths:
        outs.append(h[..., start:start + w])
        start += w
    return outs


def _layer_norm(x, g, b):
    xf = x.astype(jnp.float32)
    mu = jnp.mean(xf, axis=-1, keepdims=True)
    var = jnp.mean(jnp.square(xf - mu), axis=-1, keepdims=True)
    y = (xf - mu) * lax.rsqrt(var + LN_EPS) * g.astype(jnp.float32) + b.astype(jnp.float32)
    return y.astype(x.dtype)


def _rms_norm(x, g):
    xf = x.astype(jnp.float32)
    y = xf * lax.rsqrt(jnp.mean(jnp.square(xf), axis=-1, keepdims=True) + LN_EPS) * g.astype(jnp.float32)
    return y.astype(x.dtype)


def _partial_rope(x, positions):
    rot = x.shape[-1] // ROPE_FRACTION
    half = rot // 2
    inv_freq = jnp.exp(-(jnp.arange(half, dtype=jnp.float32) / half) * math.log(ROPE_THETA))
    ang = positions.astype(jnp.float32)[:, :, None] * inv_freq[None, None, :]
    new_shape = (ang.shape[0], ang.shape[1]) + (1,) * (x.ndim - 3) + (half,)
    ang = ang.reshape(new_shape)
    cos = jnp.cos(ang).astype(x.dtype)
    sin = jnp.sin(ang).astype(x.dtype)
    x1 = x[..., :half]
    x2 = x[..., half:rot]
    rest = x[..., rot:]
    return jnp.concatenate([x1 * cos - x2 * sin, x2 * cos + x1 * sin, rest], axis=-1)


def _block_outputs(out):
    out = jnp.moveaxis(out, 0, 1)
    return out.reshape((out.shape[0], out.shape[1] * out.shape[2]) + out.shape[3:])


def _stick_breaking_attention(q, k, v):
    s_len, d = q.shape[1], q.shape[-1]
    nb = s_len // BLOCK
    key_idx = jnp.arange(s_len, dtype=jnp.int32)
    scale = d ** -0.5

    def one_block(bi):
        qi = lax.dynamic_slice_in_dim(q, bi * BLOCK, BLOCK, axis=1)
        z = jnp.einsum('bqhd,bkhd->bhqk', qi, k, preferred_element_type=jnp.float32) * scale
        q_idx = bi * BLOCK + jnp.arange(BLOCK, dtype=jnp.int32)
        strict = key_idx[None, :] < q_idx[:, None]
        log_1mb = jnp.where(strict, jax.nn.log_sigmoid(-z), 0.0)
        rc = jnp.flip(jnp.cumsum(jnp.flip(log_1mb, axis=-1), axis=-1), axis=-1)
        after = rc - log_1mb
        w = jnp.where(strict, jnp.exp(jax.nn.log_sigmoid(z) + after), 0.0)
        return jnp.einsum('bhqk,bkhd->bqhd', w.astype(v.dtype), v)

    out = lax.map(one_block, jnp.arange(nb, dtype=jnp.int32))
    return _block_outputs(out)


def _differential_attention(q, k, v, lam):
    s_len, d = q.shape[1], q.shape[-1]
    nb = s_len // BLOCK
    key_idx = jnp.arange(s_len, dtype=jnp.int32)
    scale = d ** -0.5

    def one_block(bi):
        qi = lax.dynamic_slice_in_dim(q, bi * BLOCK, BLOCK, axis=1)
        s = jnp.einsum('bqhcd,bkhcd->bhcqk', qi, k, preferred_element_type=jnp.float32) * scale
        q_idx = bi * BLOCK + jnp.arange(BLOCK, dtype=jnp.int32)
        causal = key_idx[None, :] <= q_idx[:, None]
        p = jax.nn.softmax(jnp.where(causal, s, NEG_BIG), axis=-1)
        a = p[:, :, 0] - lam * p[:, :, 1]
        return jnp.einsum('bhqk,bkhd->bqhd', a.astype(v.dtype), v)

    out = lax.map(one_block, jnp.arange(nb, dtype=jnp.int32))
    return _block_outputs(out)


def _sliding_window_attention(q, k, v, sinks):
    b, s_len = q.shape[0], q.shape[1]
    d = q.shape[-1]
    nb = s_len // BLOCK
    scale = d ** -0.5
    pad = jnp.zeros((b, BLOCK) + k.shape[2:], k.dtype)
    kp = jnp.concatenate([pad, k], axis=1)
    vp = jnp.concatenate([pad.astype(v.dtype), v], axis=1)
    sink = sinks.astype(jnp.float32).reshape(SWA_KV_HEADS, SWA_GROUP)[None, :, :, None, None]
    offs_q = jnp.arange(BLOCK, dtype=jnp.int32)
    offs_k = jnp.arange(2 * BLOCK, dtype=jnp.int32) - BLOCK
    rel = offs_k[None, :] - offs_q[:, None]

    def one_block(bi):
        qi = lax.dynamic_slice_in_dim(q, bi * BLOCK, BLOCK, axis=1)
        kb = lax.dynamic_slice_in_dim(kp, bi * BLOCK, 2 * BLOCK, axis=1)
        vb = lax.dynamic_slice_in_dim(vp, bi * BLOCK, 2 * BLOCK, axis=1)
        s = jnp.einsum('bqhgd,bkhd->bhgqk', qi, kb, preferred_element_type=jnp.float32) * scale
        k_pos = bi * BLOCK + offs_k
        band = (rel <= 0) & (rel > -WINDOW) & (k_pos[None, :] >= 0)
        s = jnp.where(band, s, NEG_BIG)
        m = jnp.maximum(jnp.max(s, axis=-1, keepdims=True), sink)
        p = jnp.exp(s - m)
        w = p / (jnp.sum(p, axis=-1, keepdims=True) + jnp.exp(sink - m))
        return jnp.einsum('bhgqk,bkhd->bqhgd', w.astype(vb.dtype), vb)

    out = lax.map(one_block, jnp.arange(nb, dtype=jnp.int32))
    return _block_outputs(out)


def _memory_attention(q, mem, w_memkv):
    b, s_len = q.shape[0], q.shape[1]
    m_len = mem.shape[1]
    q = q.reshape(b, s_len, MEM_HEADS, HEAD_DIM)
    kv = mem @ w_memkv
    mk = kv[..., :MEM_HEADS * HEAD_DIM].reshape(b, m_len, MEM_HEADS, HEAD_DIM)
    mv = kv[..., MEM_HEADS * HEAD_DIM:].reshape(b, m_len, MEM_HEADS, HEAD_DIM)
    s = jnp.einsum('bshd,bmhd->bhsm', q, mk, preferred_element_type=jnp.float32) * (HEAD_DIM ** -0.5)
    p = jax.nn.softmax(s, axis=-1)
    return jnp.einsum('bhsm,bmhd->bshd', p.astype(mv.dtype), mv)


def _even_layer(x, mem, positions, w_in, w_memkv, diff_lambda, diff_subln, w_out, ln_g, ln_b, layer_idx):
    b, s_len = x.shape[0], x.shape[1]
    h = x @ w_in
    sb_q, sb_k, sb_v, df_q, df_k, df_v, m_q, gate = _split(h, EVEN_WIDTHS)
    sb_o = _stick_breaking_attention(sb_q.reshape(b, s_len, SB_HEADS, HEAD_DIM),
                                     sb_k.reshape(b, s_len, SB_HEADS, HEAD_DIM),
                                     sb_v.reshape(b, s_len, SB_HEADS, HEAD_DIM))
    lambda_init = 0.8 - 0.6 * math.exp(-0.3 * layer_idx)
    dl = diff_lambda.astype(jnp.float32)
    lam = jnp.exp(jnp.sum(dl[0] * dl[1])) - jnp.exp(jnp.sum(dl[2] * dl[3])) + lambda_init
    df_q = _partial_rope(df_q.reshape(b, s_len, DIFF_HEADS, 2, DIFF_QK_DIM), positions)
    df_k = _partial_rope(df_k.reshape(b, s_len, DIFF_HEADS, 2, DIFF_QK_DIM), positions)
    df_o = _differential_attention(df_q, df_k, df_v.reshape(b, s_len, DIFF_HEADS, DIFF_V_DIM), lam)
    df_o = _rms_norm(df_o, diff_subln) * (1.0 - lambda_init)
    mem_o = _memory_attention(m_q, mem, w_memkv)
    mixed = jnp.concatenate([sb_o.reshape(b, s_len, SB_HEADS * HEAD_DIM),
                             df_o.reshape(b, s_len, DIFF_HEADS * DIFF_V_DIM),
                             mem_o.reshape(b, s_len, MEM_HEADS * HEAD_DIM)], axis=-1)
    y = (mixed * jax.nn.silu(gate)) @ w_out
    return _layer_norm(DEEPNORM_ALPHA * x + y, ln_g, ln_b)


def _odd_layer(x, mem, positions, w_in, w_memkv, sinks, w_out, ln_g, ln_b):
    b, s_len = x.shape[0], x.shape[1]
    h = x @ w_in
    c_q, c_k, c_v, m_q, gate = _split(h, ODD_WIDTHS)
    c_q = _partial_rope(c_q.reshape(b, s_len, SWA_KV_HEADS, SWA_GROUP, HEAD_DIM), positions)
    c_k = _partial_rope(c_k.reshape(b, s_len, SWA_KV_HEADS, HEAD_DIM), positions)
    c_o = _sliding_window_attention(c_q, c_k, c_v.reshape(b, s_len, SWA_KV_HEADS, HEAD_DIM), sinks)
    mem_o = _memory_attention(m_q, mem, w_memkv)
    mixed = jnp.concatenate([c_o.reshape(b, s_len, SWA_Q_HEADS * HEAD_DIM),
                             mem_o.reshape(b, s_len, MEM_HEADS * HEAD_DIM)], axis=-1)
    y = (mixed * jax.nn.silu(gate)) @ w_out
    return _layer_norm(DEEPNORM_ALPHA * x + y, ln_g, ln_b)


def _col_scale(widths, value_slots):
    scale = np.ones((sum(widths),), np.float32)
    edges = [0]
    for w in widths:
        edges.append(edges[-1] + w)
    for s in value_slots:
        scale[edges[s]:edges[s + 1]] = DEEPNORM_BETA
    return jnp.asarray(scale)


def setup_inputs(seed: int = 0) -> dict:
    key = jax.random.key(seed)
    ks = jax.random.split(key, 16)

    def nrm(k, shape, std):
        return std * jax.random.normal(k, shape, jnp.float32)

    fan = D_MODEL ** -0.5
    x = nrm(ks[0], (BATCH, SEQ, D_MODEL), 1.0)
    mem = nrm(ks[1], (BATCH, MEM_LEN, D_MODEL), 1.0)
    offsets = jax.random.randint(ks[2], (BATCH, 1), 0, 4096, dtype=jnp.int32)
    positions = offsets + jnp.arange(SEQ, dtype=jnp.int32)[None, :]
    memkv_scale = jnp.concatenate([jnp.ones((MEM_HEADS * HEAD_DIM,), jnp.float32),
                                   jnp.full((MEM_HEADS * HEAD_DIM,), DEEPNORM_BETA, jnp.float32)])
    out_std = MIX_WIDTH ** -0.5 * DEEPNORM_BETA
    return {
        'x': x,
        'mem': mem,
        'positions': positions,
        'w_in_even': nrm(ks[3], (N_EVEN, D_MODEL, EVEN_IN), fan) * _col_scale(EVEN_WIDTHS, EVEN_VALUE_SLOTS),
        'w_memkv_even': nrm(ks[4], (N_EVEN, D_MODEL, 2 * MEM_HEADS * HEAD_DIM), fan) * memkv_scale,
        'diff_lambda_even': nrm(ks[5], (N_EVEN, 4, DIFF_QK_DIM), 0.1),
        'diff_subln_even': 1.0 + nrm(ks[6], (N_EVEN, DIFF_V_DIM), 0.02),
        'w_out_even': nrm(ks[7], (N_EVEN, MIX_WIDTH, D_MODEL), out_std),
        'ln_g_even': 1.0 + nrm(ks[8], (N_EVEN, D_MODEL), 0.02),
        'ln_b_even': nrm(ks[9], (N_EVEN, D_MODEL), 0.02),
        'w_in_odd': nrm(ks[10], (N_ODD, D_MODEL, ODD_IN), fan) * _col_scale(ODD_WIDTHS, ODD_VALUE_SLOTS),
        'w_memkv_odd': nrm(ks[11], (N_ODD, D_MODEL, 2 * MEM_HEADS * HEAD_DIM), fan) * memkv_scale,
        'sinks_odd': nrm(ks[12], (N_ODD, SWA_Q_HEADS), 0.5),
        'w_out_odd': nrm(ks[13], (N_ODD, MIX_WIDTH, D_MODEL), out_std),
        'ln_g_odd': 1.0 + nrm(ks[14], (N_ODD, D_MODEL), 0.02),
        'ln_b_odd': nrm(ks[15], (N_ODD, D_MODEL), 0.02),
    }


def reference(x, mem, positions, w_in_even, w_memkv_even, diff_lambda_even, diff_subln_even,
              w_out_even, ln_g_even, ln_b_even, w_in_odd, w_memkv_odd, sinks_odd, w_out_odd,
              ln_g_odd, ln_b_odd):
    for i in range(DEPTH):
        j = i // 2
        if i % 2 == 0:
            x = _even_layer(x, mem, positions, w_in_even[j], w_memkv_even[j], diff_lambda_even[j],
                            diff_subln_even[j], w_out_even[j], ln_g_even[j], ln_b_even[j], i)
        else:
            x = _odd_layer(x, mem, positions, w_in_odd[j], w_memkv_odd[j], sinks_odd[j],
                           w_out_odd[j], ln_g_odd[j], ln_b_odd[j])
    return x
```

```python
import functools
import math

import jax
import jax.numpy as jnp
import numpy as np
from jax import lax
from jax.experimental import pallas as pl
from jax.experimental.pallas import tpu as pltpu

F32 = jnp.float32
BF16 = jnp.bfloat16

HEAD_DIM = 64
LANES = 128
MEM_HEADS = 4
SB_HEADS = 6
DIFF_HEADS = 6
DIFF_QK_DIM = 32
SWA_Q_HEADS = 12
SWA_KV_HEADS = 3
SWA_GROUP = 4
WINDOW = 128
ROPE_THETA = 500000.0
ROPE_FRACTION = 4
LN_EPS = 1e-5
NEG_BIG = -1e30
VMEM_LIMIT_BYTES = 48 * 1024 * 1024

SB_EXIT_LOG_WEIGHT = -64.0

SW_HEAD_ORDER = (0, 4, 1, 5, 2, 6, 3, 7, 8, 9, 10, 11)


def _nt_dot(a, b):
    return lax.dot_general(a, b, (((1,), (1,)), ((), ())), preferred_element_type=F32)


def _head_pair_rows(q2):
    lane = lax.broadcasted_iota(jnp.int32, q2.shape, 1)
    low = lane < HEAD_DIM
    zero = jnp.zeros_like(q2)
    return jnp.concatenate([jnp.where(low, q2, zero), jnp.where(low, zero, q2)], axis=0)


def _merge_head_pair(o, rows):
    lane = lax.broadcasted_iota(jnp.int32, (rows, LANES), 1)
    return jnp.where(lane < HEAD_DIM, o[:rows], o[rows:])


def _rope_lane_tables(group, pos_f32, inv_freq_row):
    rot = group // ROPE_FRACTION
    half = rot // 2
    lane = lax.broadcasted_iota(jnp.int32, (1, LANES), 1)
    r = lane % group
    ang = pos_f32 * inv_freq_row
    cos = jnp.cos(ang)
    sin = jnp.sin(ang)
    cos_t = jnp.where(r < rot, cos, 1.0)
    sin_first = jnp.where(r < half, -sin, 0.0)
    sin_second = jnp.where((r >= half) & (r < rot), sin, 0.0)
    return cos_t, sin_first, sin_second, half


def _apply_rope(h, tables):
    cos_t, sin_first, sin_second, half = tables
    outs = []
    for c in range(h.shape[1] // LANES):
        blk = h[:, c * LANES:(c + 1) * LANES]
        outs.append(blk * cos_t + pltpu.roll(blk, LANES - half, 1) * sin_first
                    + pltpu.roll(blk, half, 1) * sin_second)
    return jnp.concatenate(outs, axis=1)


def _inv_freq_lanes(group):
    half = group // ROPE_FRACTION // 2
    inv_freq = jnp.exp(-(jnp.arange(half, dtype=F32) / half) * math.log(ROPE_THETA))
    lane = np.arange(LANES)
    return inv_freq[(lane % group) % half][None, :]


def _inproj_even_kernel(x_ref, pos_ref, invf_ref, w_ref, sbq_ref, sbk_ref, sbv_ref,
                        dfq_ref, dfk_ref, dfv_ref, mq_ref, gate_ref):
    xb = x_ref[...].astype(BF16)

    def sec(lo, width):
        return jnp.dot(xb, w_ref[:, lo:lo + width], preferred_element_type=F32)

    w_sb = SB_HEADS * HEAD_DIM
    w_df = DIFF_HEADS * 2 * DIFF_QK_DIM
    w_m = MEM_HEADS * HEAD_DIM
    tables = _rope_lane_tables(DIFF_QK_DIM, pos_ref[...].astype(F32), invf_ref[...])
    lo = 0
    sbq_ref[...] = (sec(lo, w_sb) * HEAD_DIM ** -0.5).astype(BF16); lo += w_sb
    sbk_ref[...] = sec(lo, w_sb).astype(BF16); lo += w_sb
    sbv_ref[...] = sec(lo, w_sb).astype(BF16); lo += w_sb
    dfq_ref[...] = (_apply_rope(sec(lo, w_df), tables) * DIFF_QK_DIM ** -0.5).astype(BF16); lo += w_df
    dfk_ref[...] = _apply_rope(sec(lo, w_df), tables).astype(BF16); lo += w_df
    dfv_ref[...] = sec(lo, w_df).astype(BF16); lo += w_df
    mq_ref[...] = (sec(lo, w_m) * HEAD_DIM ** -0.5).astype(BF16); lo += w_m
    gate_ref[...] = sec(lo, gate_ref.shape[1]).astype(BF16)


def _inproj_odd_kernel(x_ref, pos_ref, invf_ref, w_ref, cq_ref, ck_ref, cv_ref, mq_ref, gate_ref):
    xb = x_ref[...].astype(BF16)

    def sec(lo, width):
        return jnp.dot(xb, w_ref[:, lo:lo + width], preferred_element_type=F32)

    w_q = SWA_Q_HEADS * HEAD_DIM
    w_kv = ck_ref.shape[1]
    w_m = MEM_HEADS * HEAD_DIM
    tables = _rope_lane_tables(HEAD_DIM, pos_ref[...].astype(F32), invf_ref[...])
    lo = 0
    cq_ref[...] = (_apply_rope(sec(lo, w_q), tables) * HEAD_DIM ** -0.5).astype(BF16); lo += w_q
    ck_ref[...] = _apply_rope(sec(lo, w_kv), tables).astype(BF16); lo += w_kv
    cv_ref[...] = sec(lo, w_kv).astype(BF16); lo += w_kv
    mq_ref[...] = (sec(lo, w_m) * HEAD_DIM ** -0.5).astype(BF16); lo += w_m
    gate_ref[...] = sec(lo, gate_ref.shape[1]).astype(BF16)


def _inproj(kernel_fn, name, x2, pos2, invf, w, out_widths, tm):
    m, d = x2.shape
    return pl.pallas_call(
        kernel_fn,
        grid=(m // tm,),
        in_specs=[pl.BlockSpec((tm, d), lambda i: (i, 0)),
                  pl.BlockSpec((tm, 1), lambda i: (i, 0)),
                  pl.BlockSpec((1, LANES), lambda i: (0, 0)),
                  pl.BlockSpec(w.shape, lambda i: (0, 0))],
        out_specs=[pl.BlockSpec((tm, wd), lambda i: (i, 0)) for wd in out_widths],
        out_shape=[jax.ShapeDtypeStruct((m, wd), BF16) for wd in out_widths],
        compiler_params=pltpu.CompilerParams(dimension_semantics=("parallel",),
                                             vmem_limit_bytes=VMEM_LIMIT_BYTES),
        name=name,
    )(x2, pos2, invf, w)


def _memkv_kernel(mem_ref, w_ref, mk_ref, mv_ref):
    kv = jnp.dot(mem_ref[...].astype(BF16), w_ref[...], preferred_element_type=F32)
    half = mk_ref.shape[1]
    mk_ref[...] = kv[:, :half].astype(BF16)
    mv_ref[...] = kv[:, half:].astype(BF16)


def _memkv(mem2, w):
    rows = mem2.shape[0]
    half = w.shape[1] // 2
    return pl.pallas_call(
        _memkv_kernel,
        out_shape=[jax.ShapeDtypeStruct((rows, half), BF16)] * 2,
        compiler_params=pltpu.CompilerParams(vmem_limit_bytes=VMEM_LIMIT_BYTES),
        name="memkv",
    )(mem2, w)


def _sb_kernel(q_ref, k_ref, v_ref, o_ref, *, blk):
    qi = pl.program_id(2)
    qq = _head_pair_rows(q_ref[...])
    rows = lax.broadcasted_iota(jnp.int32, (2 * blk, blk), 0)
    q_off = jnp.where(rows >= blk, rows - blk, rows)
    k_off = lax.broadcasted_iota(jnp.int32, (2 * blk, blk), 1)
    later = (lax.broadcasted_iota(jnp.int32, (blk, blk), 0)
             > lax.broadcasted_iota(jnp.int32, (blk, blk), 1)).astype(BF16)

    def cond(c):
        j, carry, _ = c
        return (j >= 0) & (jnp.max(carry) > SB_EXIT_LOG_WEIGHT)

    def body(c):
        j, carry, acc = c
        start = pl.multiple_of(j * blk, blk)
        k2 = k_ref[pl.ds(start, blk), :]
        v2 = v_ref[pl.ds(start, blk), :]
        z = _nt_dot(qq, k2)
        strict = (k_off + (j - qi) * blk) < q_off
        soft = jnp.log1p(jnp.exp(-jnp.abs(z)))
        log_1mb = jnp.where(strict, -jnp.maximum(z, 0.0) - soft, 0.0)
        log_b = jnp.minimum(z, 0.0) - soft
        hi = log_1mb.astype(BF16)
        lo = (log_1mb - hi.astype(F32)).astype(BF16)
        after = (jnp.dot(hi, later, preferred_element_type=F32)
                 + jnp.dot(lo, later, preferred_element_type=F32))
        w = jnp.where(strict, jnp.exp(log_b + after + carry), 0.0)
        acc = acc + jnp.dot(w.astype(BF16), v2, preferred_element_type=F32)
        carry = carry + jnp.sum(log_1mb, axis=1, keepdims=True)
        return j - 1, carry, acc

    init = (qi, jnp.zeros((2 * blk, 1), F32), jnp.zeros((2 * blk, LANES), F32))
    _, _, acc = lax.while_loop(cond, body, init)
    o_ref[...] = _merge_head_pair(acc, blk).astype(o_ref.dtype)


def _pair_attention_call(kernel_fn, name, q, k, v, extra_inputs, extra_specs, blk, kv_block_of):
    b, s, wq = q.shape
    grid = (b, wq // LANES, s // blk)
    q_spec = pl.BlockSpec((None, blk, LANES), lambda bi, hp, qi: (bi, qi, hp))
    kv_spec = pl.BlockSpec((None, s, LANES), lambda bi, hp, qi: (bi, 0, kv_block_of(hp)))
    return pl.pallas_call(
        kernel_fn,
        grid=grid,
        in_specs=list(extra_specs) + [q_spec, kv_spec, kv_spec],
        out_specs=q_spec,
        out_shape=jax.ShapeDtypeStruct(q.shape, BF16),
        compiler_params=pltpu.CompilerParams(
            dimension_semantics=("parallel", "parallel", "arbitrary"),
            vmem_limit_bytes=VMEM_LIMIT_BYTES),
        name=name,
    )(*extra_inputs, q, k, v)


def _diff_kernel(dl_ref, subln_ref, q_ref, k_ref, v_ref, o_ref, *, blk, lambda_init):
    qi = pl.program_id(2)
    q2 = q_ref[...]
    lane = lax.broadcasted_iota(jnp.int32, q2.shape, 1)
    zero = jnp.zeros_like(q2)
    n_maps = LANES // DIFF_QK_DIM
    qq = jnp.concatenate([jnp.where(lane // DIFF_QK_DIM == c, q2, zero) for c in range(n_maps)], axis=0)
    rows_n = n_maps * blk

    def step(j, carry, masked):
        m, l, acc = carry
        start = pl.multiple_of(j * blk, blk)
        k2 = k_ref[pl.ds(start, blk), :]
        v2 = v_ref[pl.ds(start, blk), :]
        s = _nt_dot(qq, k2)
        if masked:
            q_off = lax.broadcasted_iota(jnp.int32, (rows_n, blk), 0) % blk
            k_off = lax.broadcasted_iota(jnp.int32, (rows_n, blk), 1)
            s = jnp.where(k_off <= q_off, s, NEG_BIG)
        m_new = jnp.maximum(m, jnp.max(s, axis=1, keepdims=True))
        p = jnp.exp(s - m_new)
        alpha = jnp.exp(m - m_new)
        l = alpha * l + jnp.sum(p, axis=1, keepdims=True)
        acc = alpha * acc + jnp.dot(p.astype(BF16), v2, preferred_element_type=F32)
        return m_new, l, acc

    init = (jnp.full((rows_n, 1), NEG_BIG, F32), jnp.zeros((rows_n, 1), F32),
            jnp.zeros((rows_n, LANES), F32))
    carry = lax.fori_loop(0, qi, lambda j, c: step(j, c, False), init)
    _, l, acc = step(qi, carry, True)
    o = acc / l

    dl = dl_ref[...]
    lam = (jnp.exp(jnp.sum(dl[0:1] * dl[1:2], axis=1, keepdims=True))
           - jnp.exp(jnp.sum(dl[2:3] * dl[3:4], axis=1, keepdims=True)) + lambda_init)
    d_a = o[0 * blk:1 * blk] - lam * o[1 * blk:2 * blk]
    d_b = o[2 * blk:3 * blk] - lam * o[3 * blk:4 * blk]
    low = lax.broadcasted_iota(jnp.int32, (blk, LANES), 1) < HEAD_DIM
    d = jnp.where(low, d_a, d_b)
    d2 = d * d
    ms_a = jnp.sum(jnp.where(low, d2, 0.0), axis=1, keepdims=True)
    ms_b = jnp.sum(jnp.where(low, 0.0, d2), axis=1, keepdims=True)
    ms = jnp.where(low, ms_a, ms_b) * (1.0 / HEAD_DIM)
    y = d * lax.rsqrt(ms + LN_EPS) * subln_ref[...] * (1.0 - lambda_init)
    o_ref[...] = y.astype(o_ref.dtype)


def _swa_kernel(sinks_ref, q_ref, k_ref, v_ref, o_ref, *, blk):
    hp = pl.program_id(1)
    qi = pl.program_id(2)
    r0 = pl.multiple_of(qi * blk, blk)
    p0 = pl.multiple_of(jnp.maximum(r0 - WINDOW, 0), WINDOW)
    kk = jnp.concatenate([k_ref[pl.ds(p0, WINDOW), :], k_ref[pl.ds(r0, blk), :]], axis=0)
    vv = jnp.concatenate([v_ref[pl.ds(p0, WINDOW), :], v_ref[pl.ds(r0, blk), :]], axis=0)
    qq = _head_pair_rows(q_ref[...])
    s = _nt_dot(qq, kk)
    shape = (2 * blk, WINDOW + blk)
    rows = lax.broadcasted_iota(jnp.int32, shape, 0)
    q_off = jnp.where(rows >= blk, rows - blk, rows)
    col = lax.broadcasted_iota(jnp.int32, shape, 1)
    rel = (col - WINDOW) - q_off
    band = (rel <= 0) & (rel > -WINDOW) & (col + (r0 - WINDOW) >= 0)
    s = jnp.where(band, s, NEG_BIG)
    row1 = lax.broadcasted_iota(jnp.int32, (2 * blk, 1), 0)
    sink = jnp.where(row1 >= blk, sinks_ref[2 * hp + 1], sinks_ref[2 * hp])
    m = jnp.maximum(jnp.max(s, axis=1, keepdims=True), sink)
    p = jnp.exp(s - m)
    denom = jnp.sum(p, axis=1, keepdims=True) + jnp.exp(sink - m)
    o = jnp.dot(p.astype(BF16), vv, preferred_element_type=F32) / denom
    o_ref[...] = _merge_head_pair(o, blk).astype(o_ref.dtype)


def _out_kernel(*refs, n_attn, alpha):
    x_ref = refs[0]
    attn_refs = refs[1:1 + n_attn]
    mq_ref, gate_ref, mk_ref, mv_ref, w_ref, g_ref, b_ref, o_ref = refs[1 + n_attn:]
    tm = x_ref.shape[0]

    parts = [r[...].astype(F32) for r in attn_refs]
    for hb in range(mq_ref.shape[1] // LANES):
        qq = _head_pair_rows(mq_ref[:, hb * LANES:(hb + 1) * LANES])
        s = _nt_dot(qq, mk_ref[:, hb * LANES:(hb + 1) * LANES])
        p = jnp.exp(s - jnp.max(s, axis=1, keepdims=True))
        l = jnp.sum(p, axis=1, keepdims=True)
        o = jnp.dot(p.astype(BF16), mv_ref[:, hb * LANES:(hb + 1) * LANES],
                    preferred_element_type=F32) / l
        parts.append(_merge_head_pair(o, tm))
    mixed = jnp.concatenate(parts, axis=1)
    gate = gate_ref[...].astype(F32)
    gated = mixed * (gate / (1.0 + jnp.exp(-gate)))
    y = jnp.dot(gated.astype(BF16), w_ref[...], preferred_element_type=F32)
    r = alpha * x_ref[...] + y
    mu = jnp.mean(r, axis=1, keepdims=True)
    c = r - mu
    var = jnp.mean(c * c, axis=1, keepdims=True)
    o_ref[...] = c * lax.rsqrt(var + LN_EPS) * g_ref[...] + b_ref[...]


def _out_layer(x2, attn_outs, mq, gate, mk, mv, w_out, ln_g, ln_b, seq, tm, alpha, name):
    m, d = x2.shape
    mem_len = mk.shape[0] // (m // seq)
    steps_per_batch = seq // tm

    def row(wd):
        return pl.BlockSpec((tm, wd), lambda i: (i, 0))

    def whole(a):
        return pl.BlockSpec(a.shape, lambda i: (0, 0))

    mem_spec = pl.BlockSpec((mem_len, mk.shape[1]), lambda i: (i // steps_per_batch, 0))
    in_specs = ([row(d)] + [row(a.shape[1]) for a in attn_outs]
                + [row(mq.shape[1]), row(gate.shape[1]), mem_spec, mem_spec,
                   whole(w_out), whole(ln_g), whole(ln_b)])
    return pl.pallas_call(
        functools.partial(_out_kernel, n_attn=len(attn_outs), alpha=alpha),
        grid=(m // tm,),
        in_specs=in_specs,
        out_specs=row(d),
        out_shape=jax.ShapeDtypeStruct((m, d), F32),
        compiler_params=pltpu.CompilerParams(dimension_semantics=("parallel",),
                                             vmem_limit_bytes=VMEM_LIMIT_BYTES),
        name=name,
    )(x2, *attn_outs, mq, gate, mk, mv, w_out, ln_g, ln_b)


ROW_TILE = 256
SB_BLOCK = 128
DIFF_BLOCK = 256
SWA_BLOCK = 256


def _even_layer(x2, mem2, pos2, w_in, w_memkv, diff_lambda, diff_subln, w_out, ln_g, ln_b,
                layer_idx, batch, seq, alpha):
    lambda_init = 0.8 - 0.6 * math.exp(-0.3 * layer_idx)
    sbq, sbk, sbv, dfq, dfk, dfv, mq, gate = _inproj(
        _inproj_even_kernel, "inproj_even", x2, pos2, _inv_freq_lanes(DIFF_QK_DIM), w_in.astype(BF16),
        (384, 384, 384, 384, 384, 384, 256, w_out.shape[0]), ROW_TILE)
    mk, mv = _memkv(mem2, w_memkv.astype(BF16))

    def b3(a):
        return a.reshape(batch, seq, a.shape[1])

    sb_o = _pair_attention_call(functools.partial(_sb_kernel, blk=SB_BLOCK), "stick_breaking",
                                b3(sbq), b3(sbk), b3(sbv), (), (), SB_BLOCK, lambda hp: hp)
    subln_lanes = jnp.tile(diff_subln.astype(F32), LANES // HEAD_DIM)[None, :]
    small = [pl.BlockSpec(diff_lambda.shape, lambda bi, hp, qi: (0, 0)),
             pl.BlockSpec((1, LANES), lambda bi, hp, qi: (0, 0))]
    df_o = _pair_attention_call(
        functools.partial(_diff_kernel, blk=DIFF_BLOCK, lambda_init=lambda_init), "differential",
        b3(dfq), b3(dfk), b3(dfv), (diff_lambda.astype(F32), subln_lanes), small, DIFF_BLOCK, lambda hp: hp)
    return _out_layer(x2, [sb_o.reshape(x2.shape[0], -1), df_o.reshape(x2.shape[0], -1)], mq, gate, mk, mv,
                      w_out.astype(BF16), ln_g[None, :], ln_b[None, :], seq, ROW_TILE, alpha, "out_even")


def _odd_layer(x2, mem2, pos2, w_in, w_memkv, sinks, w_out, ln_g, ln_b, batch, seq, alpha):
    w_q = SWA_Q_HEADS * HEAD_DIM
    w_kv = SWA_KV_HEADS * HEAD_DIM
    w_m = MEM_HEADS * HEAD_DIM
    order = np.asarray(SW_HEAD_ORDER)
    head_cols = (order[:, None] * HEAD_DIM + np.arange(HEAD_DIM)[None, :]).reshape(-1)
    gate_lo = w_q + 2 * w_kv + w_m
    last_kv = slice((SWA_KV_HEADS - 1) * HEAD_DIM, w_kv)
    k_w = w_in[:, w_q:w_q + w_kv]
    v_w = w_in[:, w_q + w_kv:w_q + 2 * w_kv]
    w_perm = jnp.concatenate([
        w_in[:, :w_q][:, head_cols], k_w, k_w[:, last_kv], v_w, v_w[:, last_kv],
        w_in[:, w_q + 2 * w_kv:gate_lo], w_in[:, gate_lo:gate_lo + w_q][:, head_cols],
        w_in[:, gate_lo + w_q:]], axis=1).astype(BF16)
    w_out_perm = jnp.concatenate([w_out[:w_q][head_cols], w_out[w_q:]], axis=0).astype(BF16)

    cq, ck, cv, mq, gate = _inproj(
        _inproj_odd_kernel, "inproj_odd", x2, pos2, _inv_freq_lanes(HEAD_DIM), w_perm,
        (w_q, 2 * LANES, 2 * LANES, w_m, w_out.shape[0]), ROW_TILE)
    mk, mv = _memkv(mem2, w_memkv.astype(BF16))

    def b3(a):
        return a.reshape(batch, seq, a.shape[1])

    pairs_per_kv_block = LANES // HEAD_DIM * SWA_GROUP // 2
    sink_spec = [pl.BlockSpec(memory_space=pltpu.SMEM)]
    c_o = _pair_attention_call(functools.partial(_swa_kernel, blk=SWA_BLOCK), "sliding_window",
                               b3(cq), b3(ck), b3(cv), (sinks.astype(F32)[order],), sink_spec, SWA_BLOCK,
                               lambda hp: hp // pairs_per_kv_block)
    return _out_layer(x2, [c_o.reshape(x2.shape[0], -1)], mq, gate, mk, mv, w_out_perm,
                      ln_g[None, :], ln_b[None, :], seq, ROW_TILE, alpha, "out_odd")


def kernel(x, mem, positions, w_in_even, w_memkv_even, diff_lambda_even, diff_subln_even, w_out_even,
           ln_g_even, ln_b_even, w_in_odd, w_memkv_odd, sinks_odd, w_out_odd, ln_g_odd, ln_b_odd):
    batch, seq, d = x.shape
    depth = w_in_even.shape[0] + w_in_odd.shape[0]
    alpha = (2 * depth) ** 0.25
    x2 = x.reshape(batch * seq, d)
    mem2 = mem.reshape(batch * mem.shape[1], d)
    pos2 = positions.reshape(batch * seq, 1)
    for i in range(depth):
        j = i // 2
        if i % 2 == 0:
            x2 = _even_layer(x2, mem2, pos2, w_in_even[j], w_memkv_even[j], diff_lambda_even[j],
                             diff_subln_even[j], w_out_even[j], ln_g_even[j], ln_b_even[j], i,
                             batch, seq, alpha)
        else:
            x2 = _odd_layer(x2, mem2, pos2, w_in_odd[j], w_memkv_odd[j], sinks_odd[j], w_out_odd[j],
                            ln_g_odd[j], ln_b_odd[j], batch, seq, alpha)
    return x2.reshape(batch, seq, d)
```

```python
import functools
import math

import jax
import jax.numpy as jnp
import numpy as np
from jax import lax
from jax.experimental import pallas as pl
from jax.experimental.pallas import tpu as pltpu

F32 = jnp.float32
BF16 = jnp.bfloat16

HEAD_DIM = 64
LANES = 128
MEM_HEADS = 4
SB_HEADS = 6
DIFF_HEADS = 6
DIFF_QK_DIM = 32
SWA_Q_HEADS = 12
SWA_KV_HEADS = 3
SWA_GROUP = 4
WINDOW = 128
ROPE_THETA = 500000.0
ROPE_FRACTION = 4
LN_EPS = 1e-5
NEG_BIG = -1e30
VMEM_LIMIT_BYTES = 48 * 1024 * 1024

SB_EXIT_LOG_WEIGHT = -64.0

SW_HEAD_ORDER = (0, 4, 1, 5, 2, 6, 3, 7, 8, 9, 10, 11)


def _nt_dot(a, b):
    return lax.dot_general(a, b, (((1,), (1,)), ((), ())), preferred_element_type=F32)


def _head_pair_rows(q2):
    lane = lax.broadcasted_iota(jnp.int32, q2.shape, 1)
    low = lane < HEAD_DIM
    zero = jnp.zeros_like(q2)
    return jnp.concatenate([jnp.where(low, q2, zero), jnp.where(low, zero, q2)], axis=0)


def _merge_head_pair(o, rows):
    lane = lax.broadcasted_iota(jnp.int32, (rows, LANES), 1)
    return jnp.where(lane < HEAD_DIM, o[:rows], o[rows:])


def _rope_lane_tables(group, pos_f32, inv_freq_row):
    rot = group // ROPE_FRACTION
    half = rot // 2
    lane = lax.broadcasted_iota(jnp.int32, (1, LANES), 1)
    r = lane % group
    ang = pos_f32 * inv_freq_row
    cos = jnp.cos(ang)
    sin = jnp.sin(ang)
    cos_t = jnp.where(r < rot, cos, 1.0)
    sin_first = jnp.where(r < half, -sin, 0.0)
    sin_second = jnp.where((r >= half) & (r < rot), sin, 0.0)
    return cos_t, sin_first, sin_second, half


def _apply_rope(h, tables):
    cos_t, sin_first, sin_second, half = tables
    outs = []
    for c in range(h.shape[1] // LANES):
        blk = h[:, c * LANES:(c + 1) * LANES]
        outs.append(blk * cos_t + pltpu.roll(blk, LANES - half, 1) * sin_first
                    + pltpu.roll(blk, half, 1) * sin_second)
    return jnp.concatenate(outs, axis=1)


def _inv_freq_lanes(group):
    half = group // ROPE_FRACTION // 2
    inv_freq = jnp.exp(-(jnp.arange(half, dtype=F32) / half) * math.log(ROPE_THETA))
    lane = np.arange(LANES)
    return inv_freq[(lane % group) % half][None, :]


def _inproj_even_kernel(x_ref, pos_ref, invf_ref, w_ref, sbq_ref, sbk_ref, sbv_ref,
                        dfq_ref, dfk_ref, dfv_ref, mq_ref, gate_ref):
    xb = x_ref[...].astype(BF16)

    def sec(lo, width):
        return jnp.dot(xb, w_ref[:, lo:lo + width], preferred_element_type=F32)

    w_sb = SB_HEADS * HEAD_DIM
    w_df = DIFF_HEADS * 2 * DIFF_QK_DIM
    w_m = MEM_HEADS * HEAD_DIM
    tables = _rope_lane_tables(DIFF_QK_DIM, pos_ref[...].astype(F32), invf_ref[...])
    lo = 0
    sbq_ref[...] = (sec(lo, w_sb) * HEAD_DIM ** -0.5).astype(BF16); lo += w_sb
    sbk_ref[...] = sec(lo, w_sb).astype(BF16); lo += w_sb
    sbv_ref[...] = sec(lo, w_sb).astype(BF16); lo += w_sb
    q_scale = DIFF_QK_DIM ** -0.5 * math.log2(math.e)
    dfq_ref[...] = (_apply_rope(sec(lo, w_df), tables) * q_scale).astype(BF16); lo += w_df
    dfk_ref[...] = _apply_rope(sec(lo, w_df), tables).astype(BF16); lo += w_df
    w_dfv = dfv_ref.shape[1]
    ones_lanes = lax.broadcasted_iota(jnp.int32, (1, w_dfv), 1) % LANES >= HEAD_DIM
    dfv_ref[...] = jnp.where(ones_lanes, 1.0, sec(lo, w_dfv)).astype(BF16); lo += w_dfv
    mq_ref[...] = (sec(lo, w_m) * HEAD_DIM ** -0.5).astype(BF16); lo += w_m
    gate_ref[...] = sec(lo, gate_ref.shape[1]).astype(BF16)


def _inproj_odd_kernel(x_ref, pos_ref, invf_ref, w_ref, cq_ref, ck_ref, cv_ref, mq_ref, gate_ref):
    xb = x_ref[...].astype(BF16)

    def sec(lo, width):
        return jnp.dot(xb, w_ref[:, lo:lo + width], preferred_element_type=F32)

    w_q = SWA_Q_HEADS * HEAD_DIM
    w_kv = ck_ref.shape[1]
    w_m = MEM_HEADS * HEAD_DIM
    tables = _rope_lane_tables(HEAD_DIM, pos_ref[...].astype(F32), invf_ref[...])
    lo = 0
    cq_ref[...] = (_apply_rope(sec(lo, w_q), tables) * HEAD_DIM ** -0.5).astype(BF16); lo += w_q
    ck_ref[...] = _apply_rope(sec(lo, w_kv), tables).astype(BF16); lo += w_kv
    cv_ref[...] = sec(lo, w_kv).astype(BF16); lo += w_kv
    mq_ref[...] = (sec(lo, w_m) * HEAD_DIM ** -0.5).astype(BF16); lo += w_m
    gate_ref[...] = sec(lo, gate_ref.shape[1]).astype(BF16)


def _inproj(kernel_fn, name, x2, pos2, invf, w, out_widths, tm):
    m, d = x2.shape
    return pl.pallas_call(
        kernel_fn,
        grid=(m // tm,),
        in_specs=[pl.BlockSpec((tm, d), lambda i: (i, 0)),
                  pl.BlockSpec((tm, 1), lambda i: (i, 0)),
                  pl.BlockSpec((1, LANES), lambda i: (0, 0)),
                  pl.BlockSpec(w.shape, lambda i: (0, 0))],
        out_specs=[pl.BlockSpec((tm, wd), lambda i: (i, 0)) for wd in out_widths],
        out_shape=[jax.ShapeDtypeStruct((m, wd), BF16) for wd in out_widths],
        compiler_params=pltpu.CompilerParams(dimension_semantics=("parallel",),
                                             vmem_limit_bytes=VMEM_LIMIT_BYTES),
        name=name,
    )(x2, pos2, invf, w)


def _memkv_kernel(mem_ref, w_ref, mk_ref, mv_ref):
    kv = jnp.dot(mem_ref[...].astype(BF16), w_ref[...], preferred_element_type=F32)
    half = mk_ref.shape[1]
    mk_ref[...] = kv[:, :half].astype(BF16)
    mv_ref[...] = kv[:, half:].astype(BF16)


def _memkv(mem2, w):
    rows = mem2.shape[0]
    half = w.shape[1] // 2
    return pl.pallas_call(
        _memkv_kernel,
        out_shape=[jax.ShapeDtypeStruct((rows, half), BF16)] * 2,
        compiler_params=pltpu.CompilerParams(vmem_limit_bytes=VMEM_LIMIT_BYTES),
        name="memkv",
    )(mem2, w)


def _sb_kernel(q_ref, k_ref, v_ref, o_ref, *, blk):
    qi = pl.program_id(2)
    qq = _head_pair_rows(q_ref[...])
    rows = lax.broadcasted_iota(jnp.int32, (2 * blk, blk), 0)
    q_off = jnp.where(rows >= blk, rows - blk, rows)
    k_off = lax.broadcasted_iota(jnp.int32, (2 * blk, blk), 1)
    later = (lax.broadcasted_iota(jnp.int32, (blk, blk), 0)
             > lax.broadcasted_iota(jnp.int32, (blk, blk), 1)).astype(BF16)

    def cond(c):
        j, carry, _ = c
        return (j >= 0) & (jnp.max(carry) > SB_EXIT_LOG_WEIGHT)

    def body(c):
        j, carry, acc = c
        start = pl.multiple_of(j * blk, blk)
        k2 = k_ref[pl.ds(start, blk), :]
        v2 = v_ref[pl.ds(start, blk), :]
        z = _nt_dot(qq, k2)
        strict = (k_off + (j - qi) * blk) < q_off
        soft = jnp.log1p(jnp.exp(-jnp.abs(z)))
        log_1mb = jnp.where(strict, -jnp.maximum(z, 0.0) - soft, 0.0)
        log_b = jnp.minimum(z, 0.0) - soft
        hi = log_1mb.astype(BF16)
        lo = (log_1mb - hi.astype(F32)).astype(BF16)
        after = (jnp.dot(hi, later, preferred_element_type=F32)
                 + jnp.dot(lo, later, preferred_element_type=F32))
        w = jnp.where(strict, jnp.exp(log_b + after + carry), 0.0)
        acc = acc + jnp.dot(w.astype(BF16), v2, preferred_element_type=F32)
        carry = carry + jnp.sum(log_1mb, axis=1, keepdims=True)
        return j - 1, carry, acc

    init = (qi, jnp.zeros((2 * blk, 1), F32), jnp.zeros((2 * blk, LANES), F32))
    _, _, acc = lax.while_loop(cond, body, init)
    o_ref[...] = _merge_head_pair(acc, blk).astype(o_ref.dtype)


def _pair_attention_call(kernel_fn, name, q, k, v, extra_inputs, extra_specs, blk, kv_block_of):
    b, s, wq = q.shape
    grid = (b, wq // LANES, s // blk)
    q_spec = pl.BlockSpec((None, blk, LANES), lambda bi, hp, qi: (bi, qi, hp))
    kv_spec = pl.BlockSpec((None, s, LANES), lambda bi, hp, qi: (bi, 0, kv_block_of(hp)))
    return pl.pallas_call(
        kernel_fn,
        grid=grid,
        in_specs=list(extra_specs) + [q_spec, kv_spec, kv_spec],
        out_specs=q_spec,
        out_shape=jax.ShapeDtypeStruct(q.shape, BF16),
        compiler_params=pltpu.CompilerParams(
            dimension_semantics=("parallel", "parallel", "arbitrary"),
            vmem_limit_bytes=VMEM_LIMIT_BYTES),
        name=name,
    )(*extra_inputs, q, k, v)


def _diff_kernel(dl_ref, subln_ref, q_ref, k_ref, v_ref, o_ref, *, blk, lambda_init):
    qi = pl.program_id(2)
    q2 = q_ref[...]
    lane = lax.broadcasted_iota(jnp.int32, q2.shape, 1)
    zero = jnp.zeros_like(q2)
    n_maps = LANES // DIFF_QK_DIM
    qq = jnp.concatenate([jnp.where(lane // DIFF_QK_DIM == c, q2, zero) for c in range(n_maps)], axis=0)
    rows_n = n_maps * blk
    half = rows_n // 2

    def step(j, carry, masked):
        m, acc = carry
        start = pl.multiple_of(j * blk, blk)
        s = _nt_dot(qq, k_ref[pl.ds(start, blk), :])
        if masked:
            q_off = lax.broadcasted_iota(jnp.int32, (rows_n, blk), 0) % blk
            k_off = lax.broadcasted_iota(jnp.int32, (rows_n, blk), 1)
            s = jnp.where(k_off <= q_off, s, NEG_BIG)
        m_new = jnp.maximum(m, jnp.max(s, axis=1, keepdims=True))
        p = jnp.exp2(s - m_new).astype(BF16)
        alpha = jnp.exp2(m - m_new)
        pv = jnp.concatenate(
            [jnp.dot(p[:half], v_ref[pl.ds(start, blk), :LANES], preferred_element_type=F32),
             jnp.dot(p[half:], v_ref[pl.ds(start, blk), LANES:], preferred_element_type=F32)], axis=0)
        return m_new, alpha * acc + pv

    init = (jnp.full((rows_n, 1), NEG_BIG, F32), jnp.zeros((rows_n, LANES), F32))
    carry = lax.fori_loop(0, qi, lambda j, c: step(j, c, False), init)
    _, acc = step(qi, carry, True)
    o = acc / acc[:, HEAD_DIM:HEAD_DIM + 1]

    dl = dl_ref[...]
    lam = (jnp.exp(jnp.sum(dl[0:1] * dl[1:2], axis=1, keepdims=True))
           - jnp.exp(jnp.sum(dl[2:3] * dl[3:4], axis=1, keepdims=True)) + lambda_init)
    low = lax.broadcasted_iota(jnp.int32, (blk, LANES), 1) < HEAD_DIM
    normed = []
    for h in range(2):
        d = o[2 * h * blk:(2 * h + 1) * blk] - lam * o[(2 * h + 1) * blk:(2 * h + 2) * blk]
        ms = jnp.sum(jnp.where(low, d * d, 0.0), axis=1, keepdims=True) * (1.0 / HEAD_DIM)
        normed.append((d * lax.rsqrt(ms + LN_EPS))[:, :HEAD_DIM])
    y = jnp.concatenate(normed, axis=1) * subln_ref[...] * (1.0 - lambda_init)
    o_ref[...] = y.astype(o_ref.dtype)


def _diff_attention(q, k, v_ones, diff_lambda, subln_lanes, blk, lambda_init):
    b, s, wq = q.shape
    q_spec = pl.BlockSpec((None, blk, LANES), lambda bi, hp, qi: (bi, qi, hp))
    k_spec = pl.BlockSpec((None, s, LANES), lambda bi, hp, qi: (bi, 0, hp))
    v_spec = pl.BlockSpec((None, s, 2 * LANES), lambda bi, hp, qi: (bi, 0, hp))
    return pl.pallas_call(
        functools.partial(_diff_kernel, blk=blk, lambda_init=lambda_init),
        grid=(b, wq // LANES, s // blk),
        in_specs=[pl.BlockSpec(diff_lambda.shape, lambda bi, hp, qi: (0, 0)),
                  pl.BlockSpec((1, LANES), lambda bi, hp, qi: (0, 0)), q_spec, k_spec, v_spec],
        out_specs=q_spec,
        out_shape=jax.ShapeDtypeStruct(q.shape, BF16),
        compiler_params=pltpu.CompilerParams(
            dimension_semantics=("parallel", "parallel", "arbitrary"),
            vmem_limit_bytes=VMEM_LIMIT_BYTES),
        name="differential",
    )(diff_lambda, subln_lanes, q, k, v_ones)


def _swa_kernel(sinks_ref, q_ref, k_ref, v_ref, o_ref, *, blk):
    hp = pl.program_id(1)
    qi = pl.program_id(2)
    r0 = pl.multiple_of(qi * blk, blk)
    p0 = pl.multiple_of(jnp.maximum(r0 - WINDOW, 0), WINDOW)
    kk = jnp.concatenate([k_ref[pl.ds(p0, WINDOW), :], k_ref[pl.ds(r0, blk), :]], axis=0)
    vv = jnp.concatenate([v_ref[pl.ds(p0, WINDOW), :], v_ref[pl.ds(r0, blk), :]], axis=0)
    qq = _head_pair_rows(q_ref[...])
    s = _nt_dot(qq, kk)
    shape = (2 * blk, WINDOW + blk)
    rows = lax.broadcasted_iota(jnp.int32, shape, 0)
    q_off = jnp.where(rows >= blk, rows - blk, rows)
    col = lax.broadcasted_iota(jnp.int32, shape, 1)
    rel = (col - WINDOW) - q_off
    band = (rel <= 0) & (rel > -WINDOW) & (col + (r0 - WINDOW) >= 0)
    s = jnp.where(band, s, NEG_BIG)
    row1 = lax.broadcasted_iota(jnp.int32, (2 * blk, 1), 0)
    sink = jnp.where(row1 >= blk, sinks_ref[2 * hp + 1], sinks_ref[2 * hp])
    m = jnp.maximum(jnp.max(s, axis=1, keepdims=True), sink)
    p = jnp.exp(s - m)
    denom = jnp.sum(p, axis=1, keepdims=True) + jnp.exp(sink - m)
    o = jnp.dot(p.astype(BF16), vv, preferred_element_type=F32) / denom
    o_ref[...] = _merge_head_pair(o, blk).astype(o_ref.dtype)


def _out_kernel(*refs, n_attn, alpha):
    x_ref = refs[0]
    attn_refs = refs[1:1 + n_attn]
    mq_ref, gate_ref, mk_ref, mv_ref, w_ref, g_ref, b_ref, o_ref = refs[1 + n_attn:]
    tm = x_ref.shape[0]

    parts = [r[...].astype(F32) for r in attn_refs]
    for hb in range(mq_ref.shape[1] // LANES):
        qq = _head_pair_rows(mq_ref[:, hb * LANES:(hb + 1) * LANES])
        s = _nt_dot(qq, mk_ref[:, hb * LANES:(hb + 1) * LANES])
        p = jnp.exp(s - jnp.max(s, axis=1, keepdims=True))
        l = jnp.sum(p, axis=1, keepdims=True)
        o = jnp.dot(p.astype(BF16), mv_ref[:, hb * LANES:(hb + 1) * LANES],
                    preferred_element_type=F32) / l
        parts.append(_merge_head_pair(o, tm))
    mixed = jnp.concatenate(parts, axis=1)
    gate = gate_ref[...].astype(F32)
    gated = mixed * (gate / (1.0 + jnp.exp(-gate)))
    y = jnp.dot(gated.astype(BF16), w_ref[...], preferred_element_type=F32)
    r = alpha * x_ref[...] + y
    mu = jnp.mean(r, axis=1, keepdims=True)
    c = r - mu
    var = jnp.mean(c * c, axis=1, keepdims=True)
    o_ref[...] = c * lax.rsqrt(var + LN_EPS) * g_ref[...] + b_ref[...]


def _out_layer(x2, attn_outs, mq, gate, mk, mv, w_out, ln_g, ln_b, seq, tm, alpha, name):
    m, d = x2.shape
    mem_len = mk.shape[0] // (m // seq)
    steps_per_batch = seq // tm

    def row(wd):
        return pl.BlockSpec((tm, wd), lambda i: (i, 0))

    def whole(a):
        return pl.BlockSpec(a.shape, lambda i: (0, 0))

    mem_spec = pl.BlockSpec((mem_len, mk.shape[1]), lambda i: (i // steps_per_batch, 0))
    in_specs = ([row(d)] + [row(a.shape[1]) for a in attn_outs]
                + [row(mq.shape[1]), row(gate.shape[1]), mem_spec, mem_spec,
                   whole(w_out), whole(ln_g), whole(ln_b)])
    return pl.pallas_call(
        functools.partial(_out_kernel, n_attn=len(attn_outs), alpha=alpha),
        grid=(m // tm,),
        in_specs=in_specs,
        out_specs=row(d),
        out_shape=jax.ShapeDtypeStruct((m, d), F32),
        compiler_params=pltpu.CompilerParams(dimension_semantics=("parallel",),
                                             vmem_limit_bytes=VMEM_LIMIT_BYTES),
        name=name,
    )(x2, *attn_outs, mq, gate, mk, mv, w_out, ln_g, ln_b)


ROW_TILE = 256
SB_BLOCK = 128
DIFF_BLOCK = 512
SWA_BLOCK = 256


def _even_layer(x2, mem2, pos2, w_in, w_memkv, diff_lambda, diff_subln, w_out, ln_g, ln_b,
                layer_idx, batch, seq, alpha):
    lambda_init = 0.8 - 0.6 * math.exp(-0.3 * layer_idx)
    w_sb = SB_HEADS * HEAD_DIM
    w_df = DIFF_HEADS * HEAD_DIM
    v_lo = 3 * w_sb + 2 * w_df
    d_model = w_in.shape[0]
    w_dfv = jnp.pad(w_in[:, v_lo:v_lo + w_df].reshape(d_model, DIFF_HEADS, HEAD_DIM),
                    ((0, 0), (0, 0), (0, LANES - HEAD_DIM))).reshape(d_model, DIFF_HEADS * LANES)
    w_in_p = jnp.concatenate([w_in[:, :v_lo], w_dfv, w_in[:, v_lo + w_df:]], axis=1).astype(BF16)
    sbq, sbk, sbv, dfq, dfk, dfv, mq, gate = _inproj(
        _inproj_even_kernel, "inproj_even", x2, pos2, _inv_freq_lanes(DIFF_QK_DIM), w_in_p,
        (w_sb, w_sb, w_sb, w_df, w_df, DIFF_HEADS * LANES, MEM_HEADS * HEAD_DIM, w_out.shape[0]), ROW_TILE)
    mk, mv = _memkv(mem2, w_memkv.astype(BF16))

    def b3(a):
        return a.reshape(batch, seq, a.shape[1])

    sb_o = _pair_attention_call(functools.partial(_sb_kernel, blk=SB_BLOCK), "stick_breaking",
                                b3(sbq), b3(sbk), b3(sbv), (), (), SB_BLOCK, lambda hp: hp)
    subln_lanes = jnp.tile(diff_subln.astype(F32), LANES // HEAD_DIM)[None, :]
    df_o = _diff_attention(b3(dfq), b3(dfk), b3(dfv), diff_lambda.astype(F32), subln_lanes,
                           DIFF_BLOCK, lambda_init)
    return _out_layer(x2, [sb_o.reshape(x2.shape[0], -1), df_o.reshape(x2.shape[0], -1)], mq, gate, mk, mv,
                      w_out.astype(BF16), ln_g[None, :], ln_b[None, :], seq, ROW_TILE, alpha, "out_even")


def _odd_layer(x2, mem2, pos2, w_in, w_memkv, sinks, w_out, ln_g, ln_b, batch, seq, alpha):
    w_q = SWA_Q_HEADS * HEAD_DIM
    w_kv = SWA_KV_HEADS * HEAD_DIM
    w_m = MEM_HEADS * HEAD_DIM
    order = np.asarray(SW_HEAD_ORDER)
    head_cols = (order[:, None] * HEAD_DIM + np.arange(HEAD_DIM)[None, :]).reshape(-1)
    gate_lo = w_q + 2 * w_kv + w_m
    last_kv = slice((SWA_KV_HEADS - 1) * HEAD_DIM, w_kv)
    k_w = w_in[:, w_q:w_q + w_kv]
    v_w = w_in[:, w_q + w_kv:w_q + 2 * w_kv]
    w_perm = jnp.concatenate([
        w_in[:, :w_q][:, head_cols], k_w, k_w[:, last_kv], v_w, v_w[:, last_kv],
        w_in[:, w_q + 2 * w_kv:gate_lo], w_in[:, gate_lo:gate_lo + w_q][:, head_cols],
        w_in[:, gate_lo + w_q:]], axis=1).astype(BF16)
    w_out_perm = jnp.concatenate([w_out[:w_q][head_cols], w_out[w_q:]], axis=0).astype(BF16)

    cq, ck, cv, mq, gate = _inproj(
        _inproj_odd_kernel, "inproj_odd", x2, pos2, _inv_freq_lanes(HEAD_DIM), w_perm,
        (w_q, 2 * LANES, 2 * LANES, w_m, w_out.shape[0]), ROW_TILE)
    mk, mv = _memkv(mem2, w_memkv.astype(BF16))

    def b3(a):
        return a.reshape(batch, seq, a.shape[1])

    pairs_per_kv_block = LANES // HEAD_DIM * SWA_GROUP // 2
    sink_spec = [pl.BlockSpec(memory_space=pltpu.SMEM)]
    c_o = _pair_attention_call(functools.partial(_swa_kernel, blk=SWA_BLOCK), "sliding_window",
                               b3(cq), b3(ck), b3(cv), (sinks.astype(F32)[order],), sink_spec, SWA_BLOCK,
                               lambda hp: hp // pairs_per_kv_block)
    return _out_layer(x2, [c_o.reshape(x2.shape[0], -1)], mq, gate, mk, mv, w_out_perm,
                      ln_g[None, :], ln_b[None, :], seq, ROW_TILE, alpha, "out_odd")


def kernel(x, mem, positions, w_in_even, w_memkv_even, diff_lambda_even, diff_subln_even, w_out_even,
           ln_g_even, ln_b_even, w_in_odd, w_memkv_odd, sinks_odd, w_out_odd, ln_g_odd, ln_b_odd):
    batch, seq, d = x.shape
    depth = w_in_even.shape[0] + w_in_odd.shape[0]
    alpha = (2 * depth) ** 0.25
    x2 = x.reshape(batch * seq, d)
    mem2 = mem.reshape(batch * mem.shape[1], d)
    pos2 = positions.reshape(batch * seq, 1)
    for i in range(depth):
        j = i // 2
        if i % 2 == 0:
            x2 = _even_layer(x2, mem2, pos2, w_in_even[j], w_memkv_even[j], diff_lambda_even[j],
                             diff_subln_even[j], w_out_even[j], ln_g_even[j], ln_b_even[j], i,
                             batch, seq, alpha)
        else:
            x2 = _odd_layer(x2, mem2, pos2, w_in_odd[j], w_memkv_odd[j], sinks_odd[j], w_out_odd[j],
                            ln_g_odd[j], ln_b_odd[j], batch, seq, alpha)
    return x2.reshape(batch, seq, d)
```

```python
import functools
import math

import jax
import jax.numpy as jnp
import numpy as np
from jax import lax
from jax.experimental import pallas as pl
from jax.experimental.pallas import tpu as pltpu

F32 = jnp.float32
BF16 = jnp.bfloat16

HEAD_DIM = 64
LANES = 128
MEM_HEADS = 4
SB_HEADS = 6
DIFF_HEADS = 6
DIFF_QK_DIM = 32
SWA_Q_HEADS = 12
SWA_KV_HEADS = 3
SWA_GROUP = 4
WINDOW = 128
ROPE_THETA = 500000.0
ROPE_FRACTION = 4
LN_EPS = 1e-5
NEG_BIG = -1e30
VMEM_LIMIT_BYTES = 48 * 1024 * 1024

SB_EXIT_LOG_WEIGHT = -64.0

SW_HEAD_ORDER = (0, 4, 1, 5, 2, 6, 3, 7, 8, 9, 10, 11)


def _nt_dot(a, b):
    return lax.dot_general(a, b, (((1,), (1,)), ((), ())), preferred_element_type=F32)


def _head_pair_rows(q2):
    lane = lax.broadcasted_iota(jnp.int32, q2.shape, 1)
    low = lane < HEAD_DIM
    zero = jnp.zeros_like(q2)
    return jnp.concatenate([jnp.where(low, q2, zero), jnp.where(low, zero, q2)], axis=0)


def _merge_head_pair(o, rows):
    lane = lax.broadcasted_iota(jnp.int32, (rows, LANES), 1)
    return jnp.where(lane < HEAD_DIM, o[:rows], o[rows:])


def _rope_lane_tables(group, pos_f32, inv_freq_row):
    rot = group // ROPE_FRACTION
    half = rot // 2
    lane = lax.broadcasted_iota(jnp.int32, (1, LANES), 1)
    r = lane % group
    ang = pos_f32 * inv_freq_row
    cos = jnp.cos(ang)
    sin = jnp.sin(ang)
    cos_t = jnp.where(r < rot, cos, 1.0)
    sin_first = jnp.where(r < half, -sin, 0.0)
    sin_second = jnp.where((r >= half) & (r < rot), sin, 0.0)
    return cos_t, sin_first, sin_second, half


def _apply_rope(h, tables):
    cos_t, sin_first, sin_second, half = tables
    outs = []
    for c in range(h.shape[1] // LANES):
        blk = h[:, c * LANES:(c + 1) * LANES]
        outs.append(blk * cos_t + pltpu.roll(blk, LANES - half, 1) * sin_first
                    + pltpu.roll(blk, half, 1) * sin_second)
    return jnp.concatenate(outs, axis=1)


def _inv_freq_lanes(group):
    half = group // ROPE_FRACTION // 2
    inv_freq = jnp.exp(-(jnp.arange(half, dtype=F32) / half) * math.log(ROPE_THETA))
    lane = np.arange(LANES)
    return inv_freq[(lane % group) % half][None, :]


def _inproj_even_kernel(x_ref, pos_ref, invf_ref, w_ref, sbq_ref, sbk_ref, sbv_ref,
                        dfq_ref, dfk_ref, dfv_ref, mq_ref, gate_ref):
    xb = x_ref[...].astype(BF16)

    def sec(lo, width):
        return jnp.dot(xb, w_ref[:, lo:lo + width], preferred_element_type=F32)

    w_sb = SB_HEADS * HEAD_DIM
    w_df = DIFF_HEADS * 2 * DIFF_QK_DIM
    w_m = MEM_HEADS * HEAD_DIM
    tables = _rope_lane_tables(DIFF_QK_DIM, pos_ref[...].astype(F32), invf_ref[...])
    lo = 0
    sbq_ref[...] = (sec(lo, w_sb) * HEAD_DIM ** -0.5).astype(BF16); lo += w_sb
    sbk_ref[...] = sec(lo, w_sb).astype(BF16); lo += w_sb
    sbv_ref[...] = sec(lo, w_sb).astype(BF16); lo += w_sb
    q_scale = DIFF_QK_DIM ** -0.5 * math.log2(math.e)
    dfq_ref[...] = (_apply_rope(sec(lo, w_df), tables) * q_scale).astype(BF16); lo += w_df
    dfk_ref[...] = _apply_rope(sec(lo, w_df), tables).astype(BF16); lo += w_df
    w_dfv = dfv_ref.shape[1]
    ones_lanes = lax.broadcasted_iota(jnp.int32, (1, w_dfv), 1) % LANES >= HEAD_DIM
    dfv_ref[...] = jnp.where(ones_lanes, 1.0, sec(lo, w_dfv)).astype(BF16); lo += w_dfv
    mq_ref[...] = (sec(lo, w_m) * HEAD_DIM ** -0.5).astype(BF16); lo += w_m
    gate_ref[...] = sec(lo, gate_ref.shape[1]).astype(BF16)


def _inproj_odd_kernel(x_ref, pos_ref, invf_ref, w_ref, cq_ref, ck_ref, cv_ref, mq_ref, gate_ref):
    xb = x_ref[...].astype(BF16)

    def sec(lo, width):
        return jnp.dot(xb, w_ref[:, lo:lo + width], preferred_element_type=F32)

    w_q = SWA_Q_HEADS * HEAD_DIM
    w_kv = ck_ref.shape[1]
    w_m = MEM_HEADS * HEAD_DIM
    tables = _rope_lane_tables(HEAD_DIM, pos_ref[...].astype(F32), invf_ref[...])
    lo = 0
    cq_ref[...] = (_apply_rope(sec(lo, w_q), tables) * HEAD_DIM ** -0.5).astype(BF16); lo += w_q
    ck_ref[...] = _apply_rope(sec(lo, w_kv), tables).astype(BF16); lo += w_kv
    cv_ref[...] = sec(lo, w_kv).astype(BF16); lo += w_kv
    mq_ref[...] = (sec(lo, w_m) * HEAD_DIM ** -0.5).astype(BF16); lo += w_m
    gate_ref[...] = sec(lo, gate_ref.shape[1]).astype(BF16)


def _inproj(kernel_fn, name, x2, pos2, invf, w, out_widths, tm):
    m, d = x2.shape
    return pl.pallas_call(
        kernel_fn,
        grid=(m // tm,),
        in_specs=[pl.BlockSpec((tm, d), lambda i: (i, 0)),
                  pl.BlockSpec((tm, 1), lambda i: (i, 0)),
                  pl.BlockSpec((1, LANES), lambda i: (0, 0)),
                  pl.BlockSpec(w.shape, lambda i: (0, 0))],
        out_specs=[pl.BlockSpec((tm, wd), lambda i: (i, 0)) for wd in out_widths],
        out_shape=[jax.ShapeDtypeStruct((m, wd), BF16) for wd in out_widths],
        compiler_params=pltpu.CompilerParams(dimension_semantics=("parallel",),
                                             vmem_limit_bytes=VMEM_LIMIT_BYTES),
        name=name,
    )(x2, pos2, invf, w)


def _memkv_kernel(mem_ref, w_ref, mk_ref, mv_ref):
    kv = jnp.dot(mem_ref[...].astype(BF16), w_ref[...], preferred_element_type=F32)
    half = mk_ref.shape[1]
    mk_ref[...] = kv[:, :half].astype(BF16)
    mv_ref[...] = kv[:, half:].astype(BF16)


def _memkv(mem2, w):
    rows = mem2.shape[0]
    half = w.shape[1] // 2
    return pl.pallas_call(
        _memkv_kernel,
        out_shape=[jax.ShapeDtypeStruct((rows, half), BF16)] * 2,
        compiler_params=pltpu.CompilerParams(vmem_limit_bytes=VMEM_LIMIT_BYTES),
        name="memkv",
    )(mem2, w)


def _strict_lower(n):
    return (lax.broadcasted_iota(jnp.int32, (n, n), 0)
            > lax.broadcasted_iota(jnp.int32, (n, n), 1)).astype(BF16)


def _sb_tile(qq, kk, vv, strict, later, carry):
    z = _nt_dot(qq, kk)
    soft = jnp.log1p(jnp.exp(-jnp.abs(z)))
    log_1mb = -jnp.maximum(z, 0.0) - soft
    if strict is not None:
        log_1mb = jnp.where(strict, log_1mb, 0.0)
    log_b = jnp.minimum(z, 0.0) - soft
    hi = log_1mb.astype(BF16)
    lo = (log_1mb - hi.astype(F32)).astype(BF16)
    after = (jnp.dot(hi, later, preferred_element_type=F32)
             + jnp.dot(lo, later, preferred_element_type=F32))
    w = jnp.exp(log_b + after + carry)
    if strict is not None:
        w = jnp.where(strict, w, 0.0)
    return (jnp.dot(w.astype(BF16), vv, preferred_element_type=F32),
            jnp.sum(log_1mb, axis=1, keepdims=True))


def _sb_kernel(q_ref, k_ref, v_ref, o_ref, *, blk):
    qi = pl.program_id(1)
    n_pairs = q_ref.shape[1] // LANES
    r0 = pl.multiple_of(qi * blk, blk)
    p0 = pl.multiple_of(jnp.maximum(r0 - blk, 0), blk)

    def lanes(hp):
        return slice(hp * LANES, (hp + 1) * LANES)

    rows = lax.broadcasted_iota(jnp.int32, (2 * blk, 2 * blk), 0)
    q_off = jnp.where(rows >= blk, rows - blk, rows)
    col = lax.broadcasted_iota(jnp.int32, (2 * blk, 2 * blk), 1)
    strict = (col - blk < q_off) & (col + (r0 - blk) >= 0)
    later2 = _strict_lower(2 * blk)
    qqs = [_head_pair_rows(q_ref[:, lanes(hp)]) for hp in range(n_pairs)]
    no_carry = jnp.zeros((2 * blk, 1), F32)
    accs, carries = [], []
    for hp in range(n_pairs):
        kk = jnp.concatenate([k_ref[pl.ds(p0, blk), lanes(hp)], k_ref[pl.ds(r0, blk), lanes(hp)]], axis=0)
        vv = jnp.concatenate([v_ref[pl.ds(p0, blk), lanes(hp)], v_ref[pl.ds(r0, blk), lanes(hp)]], axis=0)
        pv, dsum = _sb_tile(qqs[hp], kk, vv, strict, later2, no_carry)
        accs.append(pv)
        carries.append(dsum)

    later1 = _strict_lower(blk)

    def cond(c):
        j, carries, _ = c
        return (j >= 0) & (jnp.max(functools.reduce(jnp.maximum, carries)) > SB_EXIT_LOG_WEIGHT)

    def body(c):
        j, carries, accs = c
        start = pl.multiple_of(j * blk, blk)
        new_carries, new_accs = [], []
        for hp in range(n_pairs):
            pv, dsum = _sb_tile(qqs[hp], k_ref[pl.ds(start, blk), lanes(hp)],
                                v_ref[pl.ds(start, blk), lanes(hp)], None, later1, carries[hp])
            new_accs.append(accs[hp] + pv)
            new_carries.append(carries[hp] + dsum)
        return j - 1, tuple(new_carries), tuple(new_accs)

    _, _, accs = lax.while_loop(cond, body, (qi - 2, tuple(carries), tuple(accs)))
    o_ref[...] = jnp.concatenate([_merge_head_pair(a, blk) for a in accs], axis=1).astype(o_ref.dtype)


def _row_block_attention(kernel_fn, name, q, k, v, extra_inputs, extra_specs, blk):
    b, s, wq = q.shape
    q_spec = pl.BlockSpec((None, blk, wq), lambda bi, qi: (bi, qi, 0))
    kv_spec = pl.BlockSpec((None, s, k.shape[2]), lambda bi, qi: (bi, 0, 0))
    return pl.pallas_call(
        kernel_fn,
        grid=(b, s // blk),
        in_specs=list(extra_specs) + [q_spec, kv_spec, kv_spec],
        out_specs=q_spec,
        out_shape=jax.ShapeDtypeStruct(q.shape, BF16),
        compiler_params=pltpu.CompilerParams(
            dimension_semantics=("parallel", "arbitrary"),
            vmem_limit_bytes=VMEM_LIMIT_BYTES),
        name=name,
    )(*extra_inputs, q, k, v)


def _diff_kernel(dl_ref, subln_ref, q_ref, k_ref, v_ref, o_ref, *, blk, lambda_init):
    qi = pl.program_id(2)
    q2 = q_ref[...]
    lane = lax.broadcasted_iota(jnp.int32, q2.shape, 1)
    zero = jnp.zeros_like(q2)
    n_maps = LANES // DIFF_QK_DIM
    qq = jnp.concatenate([jnp.where(lane // DIFF_QK_DIM == c, q2, zero) for c in range(n_maps)], axis=0)
    rows_n = n_maps * blk
    half = rows_n // 2

    def step(j, carry, masked):
        m, acc = carry
        start = pl.multiple_of(j * blk, blk)
        s = _nt_dot(qq, k_ref[pl.ds(start, blk), :])
        if masked:
            q_off = lax.broadcasted_iota(jnp.int32, (rows_n, blk), 0) % blk
            k_off = lax.broadcasted_iota(jnp.int32, (rows_n, blk), 1)
            s = jnp.where(k_off <= q_off, s, NEG_BIG)
        m_new = jnp.maximum(m, jnp.max(s, axis=1, keepdims=True))
        p = jnp.exp2(s - m_new).astype(BF16)
        alpha = jnp.exp2(m - m_new)
        pv = jnp.concatenate(
            [jnp.dot(p[:half], v_ref[pl.ds(start, blk), :LANES], preferred_element_type=F32),
             jnp.dot(p[half:], v_ref[pl.ds(start, blk), LANES:], preferred_element_type=F32)], axis=0)
        return m_new, alpha * acc + pv

    init = (jnp.full((rows_n, 1), NEG_BIG, F32), jnp.zeros((rows_n, LANES), F32))
    carry = lax.fori_loop(0, qi, lambda j, c: step(j, c, False), init)
    _, acc = step(qi, carry, True)
    o = acc / acc[:, HEAD_DIM:HEAD_DIM + 1]

    dl = dl_ref[...]
    lam = (jnp.exp(jnp.sum(dl[0:1] * dl[1:2], axis=1, keepdims=True))
           - jnp.exp(jnp.sum(dl[2:3] * dl[3:4], axis=1, keepdims=True)) + lambda_init)
    low = lax.broadcasted_iota(jnp.int32, (blk, LANES), 1) < HEAD_DIM
    normed = []
    for h in range(2):
        d = o[2 * h * blk:(2 * h + 1) * blk] - lam * o[(2 * h + 1) * blk:(2 * h + 2) * blk]
        ms = jnp.sum(jnp.where(low, d * d, 0.0), axis=1, keepdims=True) * (1.0 / HEAD_DIM)
        normed.append((d * lax.rsqrt(ms + LN_EPS))[:, :HEAD_DIM])
    y = jnp.concatenate(normed, axis=1) * subln_ref[...] * (1.0 - lambda_init)
    o_ref[...] = y.astype(o_ref.dtype)


def _diff_attention(q, k, v_ones, diff_lambda, subln_lanes, blk, lambda_init):
    b, s, wq = q.shape
    q_spec = pl.BlockSpec((None, blk, LANES), lambda bi, hp, qi: (bi, qi, hp))
    k_spec = pl.BlockSpec((None, s, LANES), lambda bi, hp, qi: (bi, 0, hp))
    v_spec = pl.BlockSpec((None, s, 2 * LANES), lambda bi, hp, qi: (bi, 0, hp))
    return pl.pallas_call(
        functools.partial(_diff_kernel, blk=blk, lambda_init=lambda_init),
        grid=(b, wq // LANES, s // blk),
        in_specs=[pl.BlockSpec(diff_lambda.shape, lambda bi, hp, qi: (0, 0)),
                  pl.BlockSpec((1, LANES), lambda bi, hp, qi: (0, 0)), q_spec, k_spec, v_spec],
        out_specs=q_spec,
        out_shape=jax.ShapeDtypeStruct(q.shape, BF16),
        compiler_params=pltpu.CompilerParams(
            dimension_semantics=("parallel", "parallel", "arbitrary"),
            vmem_limit_bytes=VMEM_LIMIT_BYTES),
        name="differential",
    )(diff_lambda, subln_lanes, q, k, v_ones)


def _swa_kernel(sinks_ref, q_ref, k_ref, v_ref, o_ref, *, blk, pairs_per_kv_block):
    qi = pl.program_id(1)
    r0 = pl.multiple_of(qi * blk, blk)
    p0 = pl.multiple_of(jnp.maximum(r0 - WINDOW, 0), WINDOW)

    def lanes(hb):
        return slice(hb * LANES, (hb + 1) * LANES)

    shape = (2 * blk, WINDOW + blk)
    rows = lax.broadcasted_iota(jnp.int32, shape, 0)
    q_off = jnp.where(rows >= blk, rows - blk, rows)
    col = lax.broadcasted_iota(jnp.int32, shape, 1)
    rel = (col - WINDOW) - q_off
    band = (rel <= 0) & (rel > -WINDOW) & (col + (r0 - WINDOW) >= 0)
    second_head = lax.broadcasted_iota(jnp.int32, (2 * blk, 1), 0) >= blk
    outs = []
    for hb in range(q_ref.shape[1] // LANES):
        kvb = hb // pairs_per_kv_block
        kk = jnp.concatenate([k_ref[pl.ds(p0, WINDOW), lanes(kvb)], k_ref[pl.ds(r0, blk), lanes(kvb)]], axis=0)
        vv = jnp.concatenate([v_ref[pl.ds(p0, WINDOW), lanes(kvb)], v_ref[pl.ds(r0, blk), lanes(kvb)]], axis=0)
        qq = _head_pair_rows(q_ref[:, lanes(hb)])
        s = jnp.where(band, _nt_dot(qq, kk), NEG_BIG)
        sink = jnp.where(second_head, sinks_ref[2 * hb + 1], sinks_ref[2 * hb])
        m = jnp.maximum(jnp.max(s, axis=1, keepdims=True), sink)
        p = jnp.exp(s - m)
        denom = jnp.sum(p, axis=1, keepdims=True) + jnp.exp(sink - m)
        o = jnp.dot(p.astype(BF16), vv, preferred_element_type=F32) / denom
        outs.append(_merge_head_pair(o, blk))
    o_ref[...] = jnp.concatenate(outs, axis=1).astype(o_ref.dtype)


def _out_kernel(*refs, n_attn, alpha):
    x_ref = refs[0]
    attn_refs = refs[1:1 + n_attn]
    mq_ref, gate_ref, mk_ref, mv_ref, w_ref, g_ref, b_ref, o_ref = refs[1 + n_attn:]
    tm = x_ref.shape[0]

    parts = [r[...].astype(F32) for r in attn_refs]
    for hb in range(mq_ref.shape[1] // LANES):
        qq = _head_pair_rows(mq_ref[:, hb * LANES:(hb + 1) * LANES])
        s = _nt_dot(qq, mk_ref[:, hb * LANES:(hb + 1) * LANES])
        p = jnp.exp(s - jnp.max(s, axis=1, keepdims=True))
        l = jnp.sum(p, axis=1, keepdims=True)
        o = jnp.dot(p.astype(BF16), mv_ref[:, hb * LANES:(hb + 1) * LANES],
                    preferred_element_type=F32) / l
        parts.append(_merge_head_pair(o, tm))
    mixed = jnp.concatenate(parts, axis=1)
    gate = gate_ref[...].astype(F32)
    gated = mixed * (gate / (1.0 + jnp.exp(-gate)))
    y = jnp.dot(gated.astype(BF16), w_ref[...], preferred_element_type=F32)
    r = alpha * x_ref[...] + y
    mu = jnp.mean(r, axis=1, keepdims=True)
    c = r - mu
    var = jnp.mean(c * c, axis=1, keepdims=True)
    o_ref[...] = c * lax.rsqrt(var + LN_EPS) * g_ref[...] + b_ref[...]


def _out_layer(x2, attn_outs, mq, gate, mk, mv, w_out, ln_g, ln_b, seq, tm, alpha, name):
    m, d = x2.shape
    mem_len = mk.shape[0] // (m // seq)
    steps_per_batch = seq // tm

    def row(wd):
        return pl.BlockSpec((tm, wd), lambda i: (i, 0))

    def whole(a):
        return pl.BlockSpec(a.shape, lambda i: (0, 0))

    mem_spec = pl.BlockSpec((mem_len, mk.shape[1]), lambda i: (i // steps_per_batch, 0))
    in_specs = ([row(d)] + [row(a.shape[1]) for a in attn_outs]
                + [row(mq.shape[1]), row(gate.shape[1]), mem_spec, mem_spec,
                   whole(w_out), whole(ln_g), whole(ln_b)])
    return pl.pallas_call(
        functools.partial(_out_kernel, n_attn=len(attn_outs), alpha=alpha),
        grid=(m // tm,),
        in_specs=in_specs,
        out_specs=row(d),
        out_shape=jax.ShapeDtypeStruct((m, d), F32),
        compiler_params=pltpu.CompilerParams(dimension_semantics=("parallel",),
                                             vmem_limit_bytes=VMEM_LIMIT_BYTES),
        name=name,
    )(x2, *attn_outs, mq, gate, mk, mv, w_out, ln_g, ln_b)


ROW_TILE = 256
SB_BLOCK = 128
DIFF_BLOCK = 512
SWA_BLOCK = 256


def _even_layer(x2, mem2, pos2, w_in, w_memkv, diff_lambda, diff_subln, w_out, ln_g, ln_b,
                layer_idx, batch, seq, alpha):
    lambda_init = 0.8 - 0.6 * math.exp(-0.3 * layer_idx)
    w_sb = SB_HEADS * HEAD_DIM
    w_df = DIFF_HEADS * HEAD_DIM
    v_lo = 3 * w_sb + 2 * w_df
    d_model = w_in.shape[0]
    w_dfv = jnp.pad(w_in[:, v_lo:v_lo + w_df].reshape(d_model, DIFF_HEADS, HEAD_DIM),
                    ((0, 0), (0, 0), (0, LANES - HEAD_DIM))).reshape(d_model, DIFF_HEADS * LANES)
    w_in_p = jnp.concatenate([w_in[:, :v_lo], w_dfv, w_in[:, v_lo + w_df:]], axis=1).astype(BF16)
    sbq, sbk, sbv, dfq, dfk, dfv, mq, gate = _inproj(
        _inproj_even_kernel, "inproj_even", x2, pos2, _inv_freq_lanes(DIFF_QK_DIM), w_in_p,
        (w_sb, w_sb, w_sb, w_df, w_df, DIFF_HEADS * LANES, MEM_HEADS * HEAD_DIM, w_out.shape[0]), ROW_TILE)
    mk, mv = _memkv(mem2, w_memkv.astype(BF16))

    def b3(a):
        return a.reshape(batch, seq, a.shape[1])

    sb_o = _row_block_attention(functools.partial(_sb_kernel, blk=SB_BLOCK), "stick_breaking",
                                b3(sbq), b3(sbk), b3(sbv), (), (), SB_BLOCK)
    subln_lanes = jnp.tile(diff_subln.astype(F32), LANES // HEAD_DIM)[None, :]
    df_o = _diff_attention(b3(dfq), b3(dfk), b3(dfv), diff_lambda.astype(F32), subln_lanes,
                           DIFF_BLOCK, lambda_init)
    return _out_layer(x2, [sb_o.reshape(x2.shape[0], -1), df_o.reshape(x2.shape[0], -1)], mq, gate, mk, mv,
                      w_out.astype(BF16), ln_g[None, :], ln_b[None, :], seq, ROW_TILE, alpha, "out_even")


def _odd_layer(x2, mem2, pos2, w_in, w_memkv, sinks, w_out, ln_g, ln_b, batch, seq, alpha):
    w_q = SWA_Q_HEADS * HEAD_DIM
    w_kv = SWA_KV_HEADS * HEAD_DIM
    w_m = MEM_HEADS * HEAD_DIM
    order = np.asarray(SW_HEAD_ORDER)
    head_cols = (order[:, None] * HEAD_DIM + np.arange(HEAD_DIM)[None, :]).reshape(-1)
    gate_lo = w_q + 2 * w_kv + w_m
    last_kv = slice((SWA_KV_HEADS - 1) * HEAD_DIM, w_kv)
    k_w = w_in[:, w_q:w_q + w_kv]
    v_w = w_in[:, w_q + w_kv:w_q + 2 * w_kv]
    w_perm = jnp.concatenate([
        w_in[:, :w_q][:, head_cols], k_w, k_w[:, last_kv], v_w, v_w[:, last_kv],
        w_in[:, w_q + 2 * w_kv:gate_lo], w_in[:, gate_lo:gate_lo + w_q][:, head_cols],
        w_in[:, gate_lo + w_q:]], axis=1).astype(BF16)
    w_out_perm = jnp.concatenate([w_out[:w_q][head_cols], w_out[w_q:]], axis=0).astype(BF16)

    cq, ck, cv, mq, gate = _inproj(
        _inproj_odd_kernel, "inproj_odd", x2, pos2, _inv_freq_lanes(HEAD_DIM), w_perm,
        (w_q, 2 * LANES, 2 * LANES, w_m, w_out.shape[0]), ROW_TILE)
    mk, mv = _memkv(mem2, w_memkv.astype(BF16))

    def b3(a):
        return a.reshape(batch, seq, a.shape[1])

    pairs_per_kv_block = LANES // HEAD_DIM * SWA_GROUP // 2
    sink_spec = [pl.BlockSpec(memory_space=pltpu.SMEM)]
    c_o = _row_block_attention(
        functools.partial(_swa_kernel, blk=SWA_BLOCK, pairs_per_kv_block=pairs_per_kv_block),
        "sliding_window", b3(cq), b3(ck), b3(cv), (sinks.astype(F32)[order],), sink_spec, SWA_BLOCK)
    return _out_layer(x2, [c_o.reshape(x2.shape[0], -1)], mq, gate, mk, mv, w_out_perm,
                      ln_g[None, :], ln_b[None, :], seq, ROW_TILE, alpha, "out_odd")


def kernel(x, mem, positions, w_in_even, w_memkv_even, diff_lambda_even, diff_subln_even, w_out_even,
           ln_g_even, ln_b_even, w_in_odd, w_memkv_odd, sinks_odd, w_out_odd, ln_g_odd, ln_b_odd):
    batch, seq, d = x.shape
    depth = w_in_even.shape[0] + w_in_odd.shape[0]
    alpha = (2 * depth) ** 0.25
    x2 = x.reshape(batch * seq, d)
    mem2 = mem.reshape(batch * mem.shape[1], d)
    pos2 = positions.reshape(batch * seq, 1)
    for i in range(depth):
        j = i // 2
        if i % 2 == 0:
            x2 = _even_layer(x2, mem2, pos2, w_in_even[j], w_memkv_even[j], diff_lambda_even[j],
                             diff_subln_even[j], w_out_even[j], ln_g_even[j], ln_b_even[j], i,
                             batch, seq, alpha)
        else:
            x2 = _odd_layer(x2, mem2, pos2, w_in_odd[j], w_memkv_odd[j], sinks_odd[j], w_out_odd[j],
                            ln_g_odd[j], ln_b_odd[j], batch, seq, alpha)
    return x2.reshape(batch, seq, d)
```

```python
import functools
import math

import jax
import jax.numpy as jnp
import numpy as np
from jax import lax
from jax.experimental import pallas as pl
from jax.experimental.pallas import tpu as pltpu

F32 = jnp.float32
BF16 = jnp.bfloat16

HEAD_DIM = 64
LANES = 128
MEM_HEADS = 4
SB_HEADS = 6
DIFF_HEADS = 6
DIFF_QK_DIM = 32
SWA_Q_HEADS = 12
SWA_KV_HEADS = 3
SWA_GROUP = 4
WINDOW = 128
ROPE_THETA = 500000.0
ROPE_FRACTION = 4
LN_EPS = 1e-5
NEG_BIG = -1e30
VMEM_LIMIT_BYTES = 48 * 1024 * 1024

SB_EXIT_LOG_WEIGHT = -64.0

SW_HEAD_ORDER = (0, 4, 1, 5, 2, 6, 3, 7, 8, 9, 10, 11)


def _nt_dot(a, b):
    return lax.dot_general(a, b, (((1,), (1,)), ((), ())), preferred_element_type=F32)


def _head_pair_rows(q2):
    lane = lax.broadcasted_iota(jnp.int32, q2.shape, 1)
    low = lane < HEAD_DIM
    zero = jnp.zeros_like(q2)
    return jnp.concatenate([jnp.where(low, q2, zero), jnp.where(low, zero, q2)], axis=0)


def _merge_head_pair(o, rows):
    lane = lax.broadcasted_iota(jnp.int32, (rows, LANES), 1)
    return jnp.where(lane < HEAD_DIM, o[:rows], o[rows:])


def _rope_lane_tables(group, pos_f32, inv_freq_row):
    rot = group // ROPE_FRACTION
    half = rot // 2
    lane = lax.broadcasted_iota(jnp.int32, (1, LANES), 1)
    r = lane % group
    ang = pos_f32 * inv_freq_row
    cos = jnp.cos(ang)
    sin = jnp.sin(ang)
    cos_t = jnp.where(r < rot, cos, 1.0)
    sin_first = jnp.where(r < half, -sin, 0.0)
    sin_second = jnp.where((r >= half) & (r < rot), sin, 0.0)
    return cos_t, sin_first, sin_second, half


def _apply_rope(h, tables):
    cos_t, sin_first, sin_second, half = tables
    outs = []
    for c in range(h.shape[1] // LANES):
        blk = h[:, c * LANES:(c + 1) * LANES]
        outs.append(blk * cos_t + pltpu.roll(blk, LANES - half, 1) * sin_first
                    + pltpu.roll(blk, half, 1) * sin_second)
    return jnp.concatenate(outs, axis=1)


def _inv_freq_lanes(group):
    half = group // ROPE_FRACTION // 2
    inv_freq = jnp.exp(-(jnp.arange(half, dtype=F32) / half) * math.log(ROPE_THETA))
    lane = np.arange(LANES)
    return inv_freq[(lane % group) % half][None, :]


def _inproj_even_kernel(x_ref, pos_ref, invf_ref, w_ref, sbq_ref, sbk_ref, sbv_ref,
                        dfq_ref, dfk_ref, dfv_ref, mq_ref, gate_ref):
    h = jnp.dot(x_ref[...].astype(BF16), w_ref[...], preferred_element_type=F32)
    edges = [0]

    def sec(width):
        edges.append(edges[-1] + width)
        return h[:, edges[-2]:edges[-1]]

    w_sb = SB_HEADS * HEAD_DIM
    w_df = DIFF_HEADS * 2 * DIFF_QK_DIM
    tables = _rope_lane_tables(DIFF_QK_DIM, pos_ref[...].astype(F32), invf_ref[...])
    sbq_ref[...] = (sec(w_sb) * HEAD_DIM ** -0.5).astype(BF16)
    sbk_ref[...] = sec(w_sb).astype(BF16)
    sbv_ref[...] = sec(w_sb).astype(BF16)
    q_scale = DIFF_QK_DIM ** -0.5 * math.log2(math.e)
    dfq_ref[...] = (_apply_rope(sec(w_df), tables) * q_scale).astype(BF16)
    dfk_ref[...] = _apply_rope(sec(w_df), tables).astype(BF16)
    dfv_ref[...] = sec(w_df).astype(BF16)
    mq_ref[...] = (sec(MEM_HEADS * HEAD_DIM) * HEAD_DIM ** -0.5).astype(BF16)
    gate_ref[...] = sec(gate_ref.shape[1]).astype(BF16)


def _inproj_odd_kernel(x_ref, pos_ref, invf_ref, w_ref, cq_ref, ck_ref, cv_ref, mq_ref, gate_ref):
    h = jnp.dot(x_ref[...].astype(BF16), w_ref[...], preferred_element_type=F32)
    edges = [0]

    def sec(width):
        edges.append(edges[-1] + width)
        return h[:, edges[-2]:edges[-1]]

    w_kv = ck_ref.shape[1]
    tables = _rope_lane_tables(HEAD_DIM, pos_ref[...].astype(F32), invf_ref[...])
    cq_ref[...] = (_apply_rope(sec(SWA_Q_HEADS * HEAD_DIM), tables) * HEAD_DIM ** -0.5).astype(BF16)
    ck_ref[...] = _apply_rope(sec(w_kv), tables).astype(BF16)
    cv_ref[...] = sec(w_kv).astype(BF16)
    mq_ref[...] = (sec(MEM_HEADS * HEAD_DIM) * HEAD_DIM ** -0.5).astype(BF16)
    gate_ref[...] = sec(gate_ref.shape[1]).astype(BF16)


def _inproj(kernel_fn, name, x2, pos2, invf, w, out_widths, tm):
    m, d = x2.shape
    return pl.pallas_call(
        kernel_fn,
        grid=(m // tm,),
        in_specs=[pl.BlockSpec((tm, d), lambda i: (i, 0)),
                  pl.BlockSpec((tm, 1), lambda i: (i, 0)),
                  pl.BlockSpec((1, LANES), lambda i: (0, 0)),
                  pl.BlockSpec(w.shape, lambda i: (0, 0))],
        out_specs=[pl.BlockSpec((tm, wd), lambda i: (i, 0)) for wd in out_widths],
        out_shape=[jax.ShapeDtypeStruct((m, wd), BF16) for wd in out_widths],
        compiler_params=pltpu.CompilerParams(dimension_semantics=("parallel",),
                                             vmem_limit_bytes=VMEM_LIMIT_BYTES),
        name=name,
    )(x2, pos2, invf, w)


def _memkv_kernel(mem_ref, w_ref, mk_ref, mv_ref):
    kv = jnp.dot(mem_ref[...].astype(BF16), w_ref[...], preferred_element_type=F32)
    half = mk_ref.shape[1]
    mk_ref[...] = kv[:, :half].astype(BF16)
    mv_ref[...] = kv[:, half:].astype(BF16)


def _memkv(mem2, w):
    rows = mem2.shape[0]
    half = w.shape[1] // 2
    return pl.pallas_call(
        _memkv_kernel,
        out_shape=[jax.ShapeDtypeStruct((rows, half), BF16)] * 2,
        compiler_params=pltpu.CompilerParams(vmem_limit_bytes=VMEM_LIMIT_BYTES),
        name="memkv",
    )(mem2, w)


def _strict_lower(n):
    return (lax.broadcasted_iota(jnp.int32, (n, n), 0)
            > lax.broadcasted_iota(jnp.int32, (n, n), 1)).astype(BF16)


def _sb_tile(qq, kk, vv, strict, later, carry):
    z = _nt_dot(qq, kk)
    soft = jnp.log1p(jnp.exp(-jnp.abs(z)))
    log_1mb = -jnp.maximum(z, 0.0) - soft
    if strict is not None:
        log_1mb = jnp.where(strict, log_1mb, 0.0)
    log_b = jnp.minimum(z, 0.0) - soft
    hi = log_1mb.astype(BF16)
    lo = (log_1mb - hi.astype(F32)).astype(BF16)
    after = (jnp.dot(hi, later, preferred_element_type=F32)
             + jnp.dot(lo, later, preferred_element_type=F32))
    w = jnp.exp(log_b + after + carry)
    if strict is not None:
        w = jnp.where(strict, w, 0.0)
    return (jnp.dot(w.astype(BF16), vv, preferred_element_type=F32),
            jnp.sum(log_1mb, axis=1, keepdims=True))


def _sb_kernel(q_ref, k_ref, v_ref, o_ref, *, blk):
    qi = pl.program_id(1)
    n_pairs = q_ref.shape[1] // LANES
    r0 = pl.multiple_of(qi * blk, blk)
    p0 = pl.multiple_of(jnp.maximum(r0 - blk, 0), blk)

    def lanes(hp):
        return slice(hp * LANES, (hp + 1) * LANES)

    rows = lax.broadcasted_iota(jnp.int32, (2 * blk, 2 * blk), 0)
    q_off = jnp.where(rows >= blk, rows - blk, rows)
    col = lax.broadcasted_iota(jnp.int32, (2 * blk, 2 * blk), 1)
    strict = (col - blk < q_off) & (col + (r0 - blk) >= 0)
    later2 = _strict_lower(2 * blk)
    qqs = [_head_pair_rows(q_ref[:, lanes(hp)]) for hp in range(n_pairs)]
    no_carry = jnp.zeros((2 * blk, 1), F32)
    accs, carries = [], []
    for hp in range(n_pairs):
        kk = jnp.concatenate([k_ref[pl.ds(p0, blk), lanes(hp)], k_ref[pl.ds(r0, blk), lanes(hp)]], axis=0)
        vv = jnp.concatenate([v_ref[pl.ds(p0, blk), lanes(hp)], v_ref[pl.ds(r0, blk), lanes(hp)]], axis=0)
        pv, dsum = _sb_tile(qqs[hp], kk, vv, strict, later2, no_carry)
        accs.append(pv)
        carries.append(dsum)

    later1 = _strict_lower(blk)

    def cond(c):
        j, carries, _ = c
        return (j >= 0) & (jnp.max(functools.reduce(jnp.maximum, carries)) > SB_EXIT_LOG_WEIGHT)

    def body(c):
        j, carries, accs = c
        start = pl.multiple_of(j * blk, blk)
        new_carries, new_accs = [], []
        for hp in range(n_pairs):
            pv, dsum = _sb_tile(qqs[hp], k_ref[pl.ds(start, blk), lanes(hp)],
                                v_ref[pl.ds(start, blk), lanes(hp)], None, later1, carries[hp])
            new_accs.append(accs[hp] + pv)
            new_carries.append(carries[hp] + dsum)
        return j - 1, tuple(new_carries), tuple(new_accs)

    _, _, accs = lax.while_loop(cond, body, (qi - 2, tuple(carries), tuple(accs)))
    o_ref[...] = jnp.concatenate([_merge_head_pair(a, blk) for a in accs], axis=1).astype(o_ref.dtype)


def _row_block_attention(kernel_fn, name, q, k, v, extra_inputs, extra_specs, blk):
    b, s, wq = q.shape
    q_spec = pl.BlockSpec((None, blk, wq), lambda bi, qi: (bi, qi, 0))
    kv_spec = pl.BlockSpec((None, s, k.shape[2]), lambda bi, qi: (bi, 0, 0))
    return pl.pallas_call(
        kernel_fn,
        grid=(b, s // blk),
        in_specs=list(extra_specs) + [q_spec, kv_spec, kv_spec],
        out_specs=q_spec,
        out_shape=jax.ShapeDtypeStruct(q.shape, BF16),
        compiler_params=pltpu.CompilerParams(
            dimension_semantics=("parallel", "arbitrary"),
            vmem_limit_bytes=VMEM_LIMIT_BYTES),
        name=name,
    )(*extra_inputs, q, k, v)


def _diff_kernel(dl_ref, subln_ref, q_ref, k_ref, v_ref, o_ref, vp_ref, *, blk, lambda_init):
    qi = pl.program_id(2)
    seq = v_ref.shape[0]

    @pl.when(qi == 0)
    def _():
        low = lax.broadcasted_iota(jnp.int32, (blk, LANES), 1) < HEAD_DIM

        def build(c, _):
            rows = pl.ds(pl.multiple_of(c * blk, blk), blk)
            vf = v_ref[rows, :].astype(F32)
            vp_ref[rows, :LANES] = jnp.where(low, vf, 1.0).astype(BF16)
            vp_ref[rows, LANES:] = jnp.where(low, pltpu.roll(vf, HEAD_DIM, 1), 1.0).astype(BF16)
            return 0

        lax.fori_loop(0, seq // blk, build, 0)

    q2 = q_ref[...]
    lane = lax.broadcasted_iota(jnp.int32, q2.shape, 1)
    zero = jnp.zeros_like(q2)
    n_maps = LANES // DIFF_QK_DIM
    qq = jnp.concatenate([jnp.where(lane // DIFF_QK_DIM == c, q2, zero) for c in range(n_maps)], axis=0)
    rows_n = n_maps * blk
    half = rows_n // 2

    def step(start, width, carry, q_shift):
        m, acc = carry
        keys = pl.ds(pl.multiple_of(start, blk), width)
        s = _nt_dot(qq, k_ref[keys, :])
        if q_shift is not None:
            q_off = lax.broadcasted_iota(jnp.int32, (rows_n, width), 0) % blk
            k_off = lax.broadcasted_iota(jnp.int32, (rows_n, width), 1)
            s = jnp.where(k_off <= q_off + q_shift, s, NEG_BIG)
        m_new = jnp.maximum(m, jnp.max(s, axis=1, keepdims=True))
        p = jnp.exp2(s - m_new).astype(BF16)
        alpha = jnp.exp2(m - m_new)
        pv = jnp.concatenate(
            [jnp.dot(p[:half], vp_ref[keys, :LANES], preferred_element_type=F32),
             jnp.dot(p[half:], vp_ref[keys, LANES:], preferred_element_type=F32)], axis=0)
        return m_new, alpha * acc + pv

    def finish(acc):
        dl = dl_ref[...]
        lam = (jnp.exp(jnp.sum(dl[0:1] * dl[1:2], axis=1, keepdims=True))
               - jnp.exp(jnp.sum(dl[2:3] * dl[3:4], axis=1, keepdims=True)) + lambda_init)
        low = lax.broadcasted_iota(jnp.int32, (blk, LANES), 1) < HEAD_DIM
        ones = jnp.ones((LANES, LANES), BF16)
        normed = []
        for h in range(2):
            n0 = acc[2 * h * blk:(2 * h + 1) * blk]
            n1 = acc[(2 * h + 1) * blk:(2 * h + 2) * blk]
            l0 = pltpu.roll(n0, HEAD_DIM, 1)
            l1 = pltpu.roll(n1, HEAD_DIM, 1)
            d = n0 - (lam * l0 / l1) * n1
            dsq = jnp.where(low, d * d, 0.0)
            hi = dsq.astype(BF16)
            lo = (dsq - hi.astype(F32)).astype(BF16)
            ms = (jnp.dot(hi, ones, preferred_element_type=F32)
                  + jnp.dot(lo, ones, preferred_element_type=F32)) * (1.0 / HEAD_DIM)
            normed.append(d * lax.rsqrt(ms + LN_EPS * l0 * l0))
        y = jnp.where(low, normed[0], pltpu.roll(normed[1], HEAD_DIM, 1))
        o_ref[...] = (y * subln_ref[...] * (1.0 - lambda_init)).astype(o_ref.dtype)

    carry = (jnp.full((rows_n, 1), NEG_BIG, F32), jnp.zeros((rows_n, LANES), F32))
    carry = lax.fori_loop(0, qi // 2, lambda j, c: step(j * 2 * blk, 2 * blk, c, None), carry)

    @pl.when(qi % 2 == 0)
    def _():
        finish(step(qi * blk, blk, carry, 0)[1])

    @pl.when(qi % 2 == 1)
    def _():
        finish(step((qi - 1) * blk, 2 * blk, carry, blk)[1])


def _diff_attention(q, k, v, diff_lambda, subln_lanes, blk, lambda_init):
    b, s, wq = q.shape
    q_spec = pl.BlockSpec((None, blk, LANES), lambda bi, hp, qi: (bi, qi, hp))
    kv_spec = pl.BlockSpec((None, s, LANES), lambda bi, hp, qi: (bi, 0, hp))
    return pl.pallas_call(
        functools.partial(_diff_kernel, blk=blk, lambda_init=lambda_init),
        grid=(b, wq // LANES, s // blk),
        in_specs=[pl.BlockSpec(diff_lambda.shape, lambda bi, hp, qi: (0, 0)),
                  pl.BlockSpec((1, LANES), lambda bi, hp, qi: (0, 0)), q_spec, kv_spec, kv_spec],
        out_specs=q_spec,
        out_shape=jax.ShapeDtypeStruct(q.shape, BF16),
        scratch_shapes=[pltpu.VMEM((s, 2 * LANES), BF16)],
        compiler_params=pltpu.CompilerParams(
            dimension_semantics=("parallel", "parallel", "arbitrary"),
            vmem_limit_bytes=VMEM_LIMIT_BYTES),
        name="differential",
    )(diff_lambda, subln_lanes, q, k, v)


def _swa_kernel(sinks_ref, q_ref, k_ref, v_ref, o_ref, *, blk, pairs_per_kv_block):
    qi = pl.program_id(1)
    r0 = pl.multiple_of(qi * blk, blk)
    p0 = pl.multiple_of(jnp.maximum(r0 - WINDOW, 0), WINDOW)

    def lanes(hb):
        return slice(hb * LANES, (hb + 1) * LANES)

    shape = (2 * blk, WINDOW + blk)
    rows = lax.broadcasted_iota(jnp.int32, shape, 0)
    q_off = jnp.where(rows >= blk, rows - blk, rows)
    col = lax.broadcasted_iota(jnp.int32, shape, 1)
    rel = (col - WINDOW) - q_off
    band = (rel <= 0) & (rel > -WINDOW) & (col + (r0 - WINDOW) >= 0)
    second_head = lax.broadcasted_iota(jnp.int32, (2 * blk, 1), 0) >= blk
    outs = []
    for hb in range(q_ref.shape[1] // LANES):
        kvb = hb // pairs_per_kv_block
        kk = jnp.concatenate([k_ref[pl.ds(p0, WINDOW), lanes(kvb)], k_ref[pl.ds(r0, blk), lanes(kvb)]], axis=0)
        vv = jnp.concatenate([v_ref[pl.ds(p0, WINDOW), lanes(kvb)], v_ref[pl.ds(r0, blk), lanes(kvb)]], axis=0)
        qq = _head_pair_rows(q_ref[:, lanes(hb)])
        s = jnp.where(band, _nt_dot(qq, kk), NEG_BIG)
        sink = jnp.where(second_head, sinks_ref[2 * hb + 1], sinks_ref[2 * hb])
        m = jnp.maximum(jnp.max(s, axis=1, keepdims=True), sink)
        p = jnp.exp(s - m)
        denom = jnp.sum(p, axis=1, keepdims=True) + jnp.exp(sink - m)
        o = jnp.dot(p.astype(BF16), vv, preferred_element_type=F32) / denom
        outs.append(_merge_head_pair(o, blk))
    o_ref[...] = jnp.concatenate(outs, axis=1).astype(o_ref.dtype)


def _out_kernel(*refs, n_attn, alpha):
    x_ref = refs[0]
    attn_refs = refs[1:1 + n_attn]
    mq_ref, gate_ref, mk_ref, mv_ref, w_ref, g_ref, b_ref, o_ref = refs[1 + n_attn:]
    tm = x_ref.shape[0]

    parts = [r[...].astype(F32) for r in attn_refs]
    for hb in range(mq_ref.shape[1] // LANES):
        qq = _head_pair_rows(mq_ref[:, hb * LANES:(hb + 1) * LANES])
        s = _nt_dot(qq, mk_ref[:, hb * LANES:(hb + 1) * LANES])
        p = jnp.exp(s - jnp.max(s, axis=1, keepdims=True))
        l = jnp.sum(p, axis=1, keepdims=True)
        o = jnp.dot(p.astype(BF16), mv_ref[:, hb * LANES:(hb + 1) * LANES],
                    preferred_element_type=F32) / l
        parts.append(_merge_head_pair(o, tm))
    mixed = jnp.concatenate(parts, axis=1)
    gate = gate_ref[...].astype(F32)
    gated = mixed * (gate / (1.0 + jnp.exp(-gate)))
    y = jnp.dot(gated.astype(BF16), w_ref[...], preferred_element_type=F32)
    r = alpha * x_ref[...] + y
    mu = jnp.mean(r, axis=1, keepdims=True)
    c = r - mu
    var = jnp.mean(c * c, axis=1, keepdims=True)
    o_ref[...] = c * lax.rsqrt(var + LN_EPS) * g_ref[...] + b_ref[...]


def _out_layer(x2, attn_outs, mq, gate, mk, mv, w_out, ln_g, ln_b, seq, tm, alpha, name):
    m, d = x2.shape
    mem_len = mk.shape[0] // (m // seq)
    steps_per_batch = seq // tm

    def row(wd):
        return pl.BlockSpec((tm, wd), lambda i: (i, 0))

    def whole(a):
        return pl.BlockSpec(a.shape, lambda i: (0, 0))

    mem_spec = pl.BlockSpec((mem_len, mk.shape[1]), lambda i: (i // steps_per_batch, 0))
    in_specs = ([row(d)] + [row(a.shape[1]) for a in attn_outs]
                + [row(mq.shape[1]), row(gate.shape[1]), mem_spec, mem_spec,
                   whole(w_out), whole(ln_g), whole(ln_b)])
    return pl.pallas_call(
        functools.partial(_out_kernel, n_attn=len(attn_outs), alpha=alpha),
        grid=(m // tm,),
        in_specs=in_specs,
        out_specs=row(d),
        out_shape=jax.ShapeDtypeStruct((m, d), F32),
        compiler_params=pltpu.CompilerParams(dimension_semantics=("parallel",),
                                             vmem_limit_bytes=VMEM_LIMIT_BYTES),
        name=name,
    )(x2, *attn_outs, mq, gate, mk, mv, w_out, ln_g, ln_b)


ROW_TILE = 256
SB_BLOCK = 128
DIFF_BLOCK = 512
SWA_BLOCK = 256


def _even_layer(x2, mem2, pos2, w_in, w_memkv, diff_lambda, diff_subln, w_out, ln_g, ln_b,
                layer_idx, batch, seq, alpha):
    lambda_init = 0.8 - 0.6 * math.exp(-0.3 * layer_idx)
    w_sb = SB_HEADS * HEAD_DIM
    w_df = DIFF_HEADS * HEAD_DIM
    sbq, sbk, sbv, dfq, dfk, dfv, mq, gate = _inproj(
        _inproj_even_kernel, "inproj_even", x2, pos2, _inv_freq_lanes(DIFF_QK_DIM), w_in.astype(BF16),
        (w_sb, w_sb, w_sb, w_df, w_df, w_df, MEM_HEADS * HEAD_DIM, w_out.shape[0]), ROW_TILE)
    mk, mv = _memkv(mem2, w_memkv.astype(BF16))

    def b3(a):
        return a.reshape(batch, seq, a.shape[1])

    sb_o = _row_block_attention(functools.partial(_sb_kernel, blk=SB_BLOCK), "stick_breaking",
                                b3(sbq), b3(sbk), b3(sbv), (), (), SB_BLOCK)
    subln_lanes = jnp.tile(diff_subln.astype(F32), LANES // HEAD_DIM)[None, :]
    df_o = _diff_attention(b3(dfq), b3(dfk), b3(dfv), diff_lambda.astype(F32), subln_lanes,
                           DIFF_BLOCK, lambda_init)
    return _out_layer(x2, [sb_o.reshape(x2.shape[0], -1), df_o.reshape(x2.shape[0], -1)], mq, gate, mk, mv,
                      w_out.astype(BF16), ln_g[None, :], ln_b[None, :], seq, ROW_TILE, alpha, "out_even")


def _odd_layer(x2, mem2, pos2, w_in, w_memkv, sinks, w_out, ln_g, ln_b, batch, seq, alpha):
    w_q = SWA_Q_HEADS * HEAD_DIM
    w_kv = SWA_KV_HEADS * HEAD_DIM
    w_m = MEM_HEADS * HEAD_DIM
    order = np.asarray(SW_HEAD_ORDER)
    head_cols = (order[:, None] * HEAD_DIM + np.arange(HEAD_DIM)[None, :]).reshape(-1)
    gate_lo = w_q + 2 * w_kv + w_m
    last_kv = slice((SWA_KV_HEADS - 1) * HEAD_DIM, w_kv)
    k_w = w_in[:, w_q:w_q + w_kv]
    v_w = w_in[:, w_q + w_kv:w_q + 2 * w_kv]
    w_perm = jnp.concatenate([
        w_in[:, :w_q][:, head_cols], k_w, k_w[:, last_kv], v_w, v_w[:, last_kv],
        w_in[:, w_q + 2 * w_kv:gate_lo], w_in[:, gate_lo:gate_lo + w_q][:, head_cols],
        w_in[:, gate_lo + w_q:]], axis=1).astype(BF16)
    w_out_perm = jnp.concatenate([w_out[:w_q][head_cols], w_out[w_q:]], axis=0).astype(BF16)

    cq, ck, cv, mq, gate = _inproj(
        _inproj_odd_kernel, "inproj_odd", x2, pos2, _inv_freq_lanes(HEAD_DIM), w_perm,
        (w_q, 2 * LANES, 2 * LANES, w_m, w_out.shape[0]), ROW_TILE)
    mk, mv = _memkv(mem2, w_memkv.astype(BF16))

    def b3(a):
        return a.reshape(batch, seq, a.shape[1])

    pairs_per_kv_block = LANES // HEAD_DIM * SWA_GROUP // 2
    sink_spec = [pl.BlockSpec(memory_space=pltpu.SMEM)]
    c_o = _row_block_attention(
        functools.partial(_swa_kernel, blk=SWA_BLOCK, pairs_per_kv_block=pairs_per_kv_block),
        "sliding_window", b3(cq), b3(ck), b3(cv), (sinks.astype(F32)[order],), sink_spec, SWA_BLOCK)
    return _out_layer(x2, [c_o.reshape(x2.shape[0], -1)], mq, gate, mk, mv, w_out_perm,
                      ln_g[None, :], ln_b[None, :], seq, ROW_TILE, alpha, "out_odd")


def kernel(x, mem, positions, w_in_even, w_memkv_even, diff_lambda_even, diff_subln_even, w_out_even,
           ln_g_even, ln_b_even, w_in_odd, w_memkv_odd, sinks_odd, w_out_odd, ln_g_odd, ln_b_odd):
    batch, seq, d = x.shape
    depth = w_in_even.shape[0] + w_in_odd.shape[0]
    alpha = (2 * depth) ** 0.25
    x2 = x.reshape(batch * seq, d)
    mem2 = mem.reshape(batch * mem.shape[1], d)
    pos2 = positions.reshape(batch * seq, 1)
    for i in range(depth):
        j = i // 2
        if i % 2 == 0:
            x2 = _even_layer(x2, mem2, pos2, w_in_even[j], w_memkv_even[j], diff_lambda_even[j],
                             diff_subln_even[j], w_out_even[j], ln_g_even[j], ln_b_even[j], i,
                             batch, seq, alpha)
        else:
            x2 = _odd_layer(x2, mem2, pos2, w_in_odd[j], w_memkv_odd[j], sinks_odd[j], w_out_odd[j],
                            ln_g_odd[j], ln_b_odd[j], batch, seq, alpha)
    return x2.reshape(batch, seq, d)
```

```python
import functools
import math

import jax
import jax.numpy as jnp
import numpy as np
from jax import lax
from jax.experimental import pallas as pl
from jax.experimental.pallas import tpu as pltpu

F32 = jnp.float32
BF16 = jnp.bfloat16

HEAD_DIM = 64
LANES = 128
MEM_HEADS = 4
SB_HEADS = 6
DIFF_HEADS = 6
DIFF_QK_DIM = 32
SWA_Q_HEADS = 12
SWA_KV_HEADS = 3
SWA_GROUP = 4
WINDOW = 128
ROPE_THETA = 500000.0
ROPE_FRACTION = 4
LN_EPS = 1e-5
NEG_BIG = -1e30
VMEM_LIMIT_BYTES = 48 * 1024 * 1024

SB_EXIT_LOG_WEIGHT = -64.0

SW_HEAD_ORDER = (0, 4, 1, 5, 2, 6, 3, 7, 8, 9, 10, 11)


def _nt_dot(a, b):
    return lax.dot_general(a, b, (((1,), (1,)), ((), ())), preferred_element_type=F32)


def _head_pair_rows(q2):
    lane = lax.broadcasted_iota(jnp.int32, q2.shape, 1)
    low = lane < HEAD_DIM
    zero = jnp.zeros_like(q2)
    return jnp.concatenate([jnp.where(low, q2, zero), jnp.where(low, zero, q2)], axis=0)


def _merge_head_pair(o, rows):
    lane = lax.broadcasted_iota(jnp.int32, (rows, LANES), 1)
    return jnp.where(lane < HEAD_DIM, o[:rows], o[rows:])


def _rope_lane_tables(group, pos_f32, inv_freq_row):
    rot = group // ROPE_FRACTION
    half = rot // 2
    lane = lax.broadcasted_iota(jnp.int32, (1, LANES), 1)
    r = lane % group
    ang = pos_f32 * inv_freq_row
    cos = jnp.cos(ang)
    sin = jnp.sin(ang)
    cos_t = jnp.where(r < rot, cos, 1.0)
    sin_first = jnp.where(r < half, -sin, 0.0)
    sin_second = jnp.where((r >= half) & (r < rot), sin, 0.0)
    return cos_t, sin_first, sin_second, half


def _apply_rope(h, tables):
    cos_t, sin_first, sin_second, half = tables
    outs = []
    for c in range(h.shape[1] // LANES):
        blk = h[:, c * LANES:(c + 1) * LANES]
        outs.append(blk * cos_t + pltpu.roll(blk, LANES - half, 1) * sin_first
                    + pltpu.roll(blk, half, 1) * sin_second)
    return jnp.concatenate(outs, axis=1)


def _inv_freq_lanes(group):
    half = group // ROPE_FRACTION // 2
    inv_freq = jnp.exp(-(jnp.arange(half, dtype=F32) / half) * math.log(ROPE_THETA))
    lane = np.arange(LANES)
    return inv_freq[(lane % group) % half][None, :]


def _inproj_even_kernel(x_ref, pos_ref, invf_ref, w_ref, sbq_ref, sbk_ref, sbv_ref,
                        dfq_ref, dfk_ref, dfv_ref, mq_ref, gate_ref):
    h = jnp.dot(x_ref[...].astype(BF16), w_ref[...], preferred_element_type=F32)
    edges = [0]

    def sec(width):
        edges.append(edges[-1] + width)
        return h[:, edges[-2]:edges[-1]]

    w_sb = SB_HEADS * HEAD_DIM
    w_df = DIFF_HEADS * 2 * DIFF_QK_DIM
    tables = _rope_lane_tables(DIFF_QK_DIM, pos_ref[...].astype(F32), invf_ref[...])
    sbq_ref[...] = (sec(w_sb) * HEAD_DIM ** -0.5).astype(BF16)
    sbk_ref[...] = sec(w_sb).astype(BF16)
    sbv_ref[...] = sec(w_sb).astype(BF16)
    q_scale = DIFF_QK_DIM ** -0.5 * math.log2(math.e)
    dfq_ref[...] = (_apply_rope(sec(w_df), tables) * q_scale).astype(BF16)
    dfk_ref[...] = _apply_rope(sec(w_df), tables).astype(BF16)
    dfv_ref[...] = sec(w_df).astype(BF16)
    mq_ref[...] = (sec(MEM_HEADS * HEAD_DIM) * HEAD_DIM ** -0.5).astype(BF16)
    gate_ref[...] = sec(gate_ref.shape[1]).astype(BF16)


def _inproj_odd_kernel(x_ref, pos_ref, invf_ref, w_ref, cq_ref, ck_ref, cv_ref, mq_ref, gate_ref):
    h = jnp.dot(x_ref[...].astype(BF16), w_ref[...], preferred_element_type=F32)
    edges = [0]

    def sec(width):
        edges.append(edges[-1] + width)
        return h[:, edges[-2]:edges[-1]]

    w_kv = ck_ref.shape[1]
    tables = _rope_lane_tables(HEAD_DIM, pos_ref[...].astype(F32), invf_ref[...])
    cq_ref[...] = (_apply_rope(sec(SWA_Q_HEADS * HEAD_DIM), tables) * HEAD_DIM ** -0.5).astype(BF16)
    ck_ref[...] = _apply_rope(sec(w_kv), tables).astype(BF16)
    cv_ref[...] = sec(w_kv).astype(BF16)
    mq_ref[...] = (sec(MEM_HEADS * HEAD_DIM) * HEAD_DIM ** -0.5).astype(BF16)
    gate_ref[...] = sec(gate_ref.shape[1]).astype(BF16)


def _inproj(kernel_fn, name, x2, pos2, invf, w, out_widths, tm):
    m, d = x2.shape
    return pl.pallas_call(
        kernel_fn,
        grid=(m // tm,),
        in_specs=[pl.BlockSpec((tm, d), lambda i: (i, 0)),
                  pl.BlockSpec((tm, 1), lambda i: (i, 0)),
                  pl.BlockSpec((1, LANES), lambda i: (0, 0)),
                  pl.BlockSpec(w.shape, lambda i: (0, 0))],
        out_specs=[pl.BlockSpec((tm, wd), lambda i: (i, 0)) for wd in out_widths],
        out_shape=[jax.ShapeDtypeStruct((m, wd), BF16) for wd in out_widths],
        compiler_params=pltpu.CompilerParams(dimension_semantics=("parallel",),
                                             vmem_limit_bytes=VMEM_LIMIT_BYTES),
        name=name,
    )(x2, pos2, invf, w)


def _memkv_kernel(mem_ref, w_ref, mk_ref, mv_ref):
    kv = jnp.dot(mem_ref[...].astype(BF16), w_ref[...], preferred_element_type=F32)
    half = mk_ref.shape[1]
    mk_ref[...] = kv[:, :half].astype(BF16)
    mv_ref[...] = kv[:, half:].astype(BF16)


def _memkv(mem2, w):
    rows = mem2.shape[0]
    half = w.shape[1] // 2
    return pl.pallas_call(
        _memkv_kernel,
        out_shape=[jax.ShapeDtypeStruct((rows, half), BF16)] * 2,
        compiler_params=pltpu.CompilerParams(vmem_limit_bytes=VMEM_LIMIT_BYTES),
        name="memkv",
    )(mem2, w)


def _strict_lower(n):
    return (lax.broadcasted_iota(jnp.int32, (n, n), 0)
            > lax.broadcasted_iota(jnp.int32, (n, n), 1)).astype(BF16)


def _sb_tile(qq, kk, vv, strict, later, carry):
    z = _nt_dot(qq, kk)
    soft = jnp.log1p(jnp.exp(-jnp.abs(z)))
    log_1mb = -jnp.maximum(z, 0.0) - soft
    if strict is not None:
        log_1mb = jnp.where(strict, log_1mb, 0.0)
    log_b = jnp.minimum(z, 0.0) - soft
    hi = log_1mb.astype(BF16)
    lo = (log_1mb - hi.astype(F32)).astype(BF16)
    after = (jnp.dot(hi, later, preferred_element_type=F32)
             + jnp.dot(lo, later, preferred_element_type=F32))
    w = jnp.exp(log_b + after + carry)
    if strict is not None:
        w = jnp.where(strict, w, 0.0)
    return (jnp.dot(w.astype(BF16), vv, preferred_element_type=F32),
            jnp.sum(log_1mb, axis=1, keepdims=True))


def _sb_kernel(q_ref, k_ref, v_ref, o_ref, *, blk):
    n_pairs = q_ref.shape[1] // LANES
    n_sub = q_ref.shape[0] // blk

    def lanes(hp):
        return slice(hp * LANES, (hp + 1) * LANES)

    rows = lax.broadcasted_iota(jnp.int32, (2 * blk, 2 * blk), 0)
    q_off = jnp.where(rows >= blk, rows - blk, rows)
    col = lax.broadcasted_iota(jnp.int32, (2 * blk, 2 * blk), 1)
    later2 = _strict_lower(2 * blk)
    later1 = _strict_lower(blk)
    no_carry = jnp.zeros((2 * blk, 1), F32)

    def first_tiles(sub):
        bi = pl.program_id(1) * n_sub + sub
        r0 = pl.multiple_of(bi * blk, blk)
        p0 = pl.multiple_of(jnp.maximum(r0 - blk, 0), blk)
        strict = (col - blk < q_off) & (col + (r0 - blk) >= 0)
        qqs, accs, carries = [], [], []
        for hp in range(n_pairs):
            qq = _head_pair_rows(q_ref[sub * blk:(sub + 1) * blk, lanes(hp)])
            kk = jnp.concatenate([k_ref[pl.ds(p0, blk), lanes(hp)], k_ref[pl.ds(r0, blk), lanes(hp)]], axis=0)
            vv = jnp.concatenate([v_ref[pl.ds(p0, blk), lanes(hp)], v_ref[pl.ds(r0, blk), lanes(hp)]], axis=0)
            pv, dsum = _sb_tile(qq, kk, vv, strict, later2, no_carry)
            qqs.append(qq)
            accs.append(pv)
            carries.append(dsum)
        return bi, qqs, carries, accs

    def finish(sub, bi, qqs, carries, accs):
        def cond(c):
            j, carries, _ = c
            return (j >= 0) & (jnp.max(functools.reduce(jnp.maximum, carries)) > SB_EXIT_LOG_WEIGHT)

        def body(c):
            j, carries, accs = c
            start = pl.multiple_of(j * blk, blk)
            new_carries, new_accs = [], []
            for hp in range(n_pairs):
                pv, dsum = _sb_tile(qqs[hp], k_ref[pl.ds(start, blk), lanes(hp)],
                                    v_ref[pl.ds(start, blk), lanes(hp)], None, later1, carries[hp])
                new_accs.append(accs[hp] + pv)
                new_carries.append(carries[hp] + dsum)
            return j - 1, tuple(new_carries), tuple(new_accs)

        _, _, accs = lax.while_loop(cond, body, (bi - 2, tuple(carries), tuple(accs)))
        o_ref[sub * blk:(sub + 1) * blk, :] = jnp.concatenate(
            [_merge_head_pair(a, blk) for a in accs], axis=1).astype(o_ref.dtype)

    started = [first_tiles(sub) for sub in range(n_sub)]
    for sub in range(n_sub):
        finish(sub, *started[sub])


def _row_block_attention(kernel_fn, name, q, k, v, extra_inputs, extra_specs, blk):
    b, s, wq = q.shape
    q_spec = pl.BlockSpec((None, blk, wq), lambda bi, qi: (bi, qi, 0))
    kv_spec = pl.BlockSpec((None, s, k.shape[2]), lambda bi, qi: (bi, 0, 0))
    return pl.pallas_call(
        kernel_fn,
        grid=(b, s // blk),
        in_specs=list(extra_specs) + [q_spec, kv_spec, kv_spec],
        out_specs=q_spec,
        out_shape=jax.ShapeDtypeStruct(q.shape, BF16),
        compiler_params=pltpu.CompilerParams(
            dimension_semantics=("parallel", "arbitrary"),
            vmem_limit_bytes=VMEM_LIMIT_BYTES),
        name=name,
    )(*extra_inputs, q, k, v)


def _diff_kernel(dl_ref, subln_ref, q_ref, k_ref, v_ref, o_ref, vp_ref, *, blk, lambda_init):
    qi = pl.program_id(2)
    seq = v_ref.shape[0]

    @pl.when(qi == 0)
    def _():
        low = lax.broadcasted_iota(jnp.int32, (blk, LANES), 1) < HEAD_DIM

        def build(c, _):
            rows = pl.ds(pl.multiple_of(c * blk, blk), blk)
            vf = v_ref[rows, :].astype(F32)
            vp_ref[rows, :LANES] = jnp.where(low, vf, 1.0).astype(BF16)
            vp_ref[rows, LANES:] = jnp.where(low, pltpu.roll(vf, HEAD_DIM, 1), 1.0).astype(BF16)
            return 0

        lax.fori_loop(0, seq // blk, build, 0)

    q2 = q_ref[...]
    lane = lax.broadcasted_iota(jnp.int32, q2.shape, 1)
    zero = jnp.zeros_like(q2)
    n_maps = LANES // DIFF_QK_DIM
    qq = jnp.concatenate([jnp.where(lane // DIFF_QK_DIM == c, q2, zero) for c in range(n_maps)], axis=0)
    rows_n = n_maps * blk
    half = rows_n // 2

    def step(start, width, carry, q_shift):
        m, acc = carry
        keys = pl.ds(pl.multiple_of(start, blk), width)
        s = _nt_dot(qq, k_ref[keys, :])
        if q_shift is not None:
            q_off = lax.broadcasted_iota(jnp.int32, (rows_n, width), 0) % blk
            k_off = lax.broadcasted_iota(jnp.int32, (rows_n, width), 1)
            s = jnp.where(k_off <= q_off + q_shift, s, NEG_BIG)
        m_new = jnp.maximum(m, jnp.max(s, axis=1, keepdims=True))
        p = jnp.exp2(s - m_new).astype(BF16)
        alpha = jnp.exp2(m - m_new)
        pv = jnp.concatenate(
            [jnp.dot(p[:half], vp_ref[keys, :LANES], preferred_element_type=F32),
             jnp.dot(p[half:], vp_ref[keys, LANES:], preferred_element_type=F32)], axis=0)
        return m_new, alpha * acc + pv

    def finish(acc):
        dl = dl_ref[...]
        lam = (jnp.exp(jnp.sum(dl[0:1] * dl[1:2], axis=1, keepdims=True))
               - jnp.exp(jnp.sum(dl[2:3] * dl[3:4], axis=1, keepdims=True)) + lambda_init)
        low = lax.broadcasted_iota(jnp.int32, (blk, LANES), 1) < HEAD_DIM
        ones = jnp.ones((LANES, LANES), BF16)
        normed = []
        for h in range(2):
            n0 = acc[2 * h * blk:(2 * h + 1) * blk]
            n1 = acc[(2 * h + 1) * blk:(2 * h + 2) * blk]
            l0 = pltpu.roll(n0, HEAD_DIM, 1)
            l1 = pltpu.roll(n1, HEAD_DIM, 1)
            d = n0 - (lam * l0 / l1) * n1
            dsq = jnp.where(low, d * d, 0.0)
            hi = dsq.astype(BF16)
            lo = (dsq - hi.astype(F32)).astype(BF16)
            ms = (jnp.dot(hi, ones, preferred_element_type=F32)
                  + jnp.dot(lo, ones, preferred_element_type=F32)) * (1.0 / HEAD_DIM)
            normed.append(d * lax.rsqrt(ms + LN_EPS * l0 * l0))
        y = jnp.where(low, normed[0], pltpu.roll(normed[1], HEAD_DIM, 1))
        o_ref[...] = (y * subln_ref[...] * (1.0 - lambda_init)).astype(o_ref.dtype)

    carry = (jnp.full((rows_n, 1), NEG_BIG, F32), jnp.zeros((rows_n, LANES), F32))
    carry = lax.fori_loop(0, qi // 2, lambda j, c: step(j * 2 * blk, 2 * blk, c, None), carry)

    @pl.when(qi % 2 == 0)
    def _():
        finish(step(qi * blk, blk, carry, 0)[1])

    @pl.when(qi % 2 == 1)
    def _():
        finish(step((qi - 1) * blk, 2 * blk, carry, blk)[1])


def _diff_attention(q, k, v, diff_lambda, subln_lanes, blk, lambda_init):
    b, s, wq = q.shape
    q_spec = pl.BlockSpec((None, blk, LANES), lambda bi, hp, qi: (bi, qi, hp))
    kv_spec = pl.BlockSpec((None, s, LANES), lambda bi, hp, qi: (bi, 0, hp))
    return pl.pallas_call(
        functools.partial(_diff_kernel, blk=blk, lambda_init=lambda_init),
        grid=(b, wq // LANES, s // blk),
        in_specs=[pl.BlockSpec(diff_lambda.shape, lambda bi, hp, qi: (0, 0)),
                  pl.BlockSpec((1, LANES), lambda bi, hp, qi: (0, 0)), q_spec, kv_spec, kv_spec],
        out_specs=q_spec,
        out_shape=jax.ShapeDtypeStruct(q.shape, BF16),
        scratch_shapes=[pltpu.VMEM((s, 2 * LANES), BF16)],
        compiler_params=pltpu.CompilerParams(
            dimension_semantics=("parallel", "parallel", "arbitrary"),
            vmem_limit_bytes=VMEM_LIMIT_BYTES),
        name="differential",
    )(diff_lambda, subln_lanes, q, k, v)


def _swa_kernel(sinks_ref, q_ref, k_ref, v_ref, o_ref, *, blk, pairs_per_kv_block):
    qi = pl.program_id(1)
    r0 = pl.multiple_of(qi * blk, blk)
    p0 = pl.multiple_of(jnp.maximum(r0 - WINDOW, 0), WINDOW)

    def lanes(hb):
        return slice(hb * LANES, (hb + 1) * LANES)

    shape = (2 * blk, WINDOW + blk)
    rows = lax.broadcasted_iota(jnp.int32, shape, 0)
    q_off = jnp.where(rows >= blk, rows - blk, rows)
    col = lax.broadcasted_iota(jnp.int32, shape, 1)
    rel = (col - WINDOW) - q_off
    band = (rel <= 0) & (rel > -WINDOW) & (col + (r0 - WINDOW) >= 0)
    second_head = lax.broadcasted_iota(jnp.int32, (2 * blk, 1), 0) >= blk
    outs = []
    for hb in range(q_ref.shape[1] // LANES):
        kvb = hb // pairs_per_kv_block
        kk = jnp.concatenate([k_ref[pl.ds(p0, WINDOW), lanes(kvb)], k_ref[pl.ds(r0, blk), lanes(kvb)]], axis=0)
        vv = jnp.concatenate([v_ref[pl.ds(p0, WINDOW), lanes(kvb)], v_ref[pl.ds(r0, blk), lanes(kvb)]], axis=0)
        qq = _head_pair_rows(q_ref[:, lanes(hb)])
        s = jnp.where(band, _nt_dot(qq, kk), NEG_BIG)
        sink = jnp.where(second_head, sinks_ref[2 * hb + 1], sinks_ref[2 * hb])
        m = jnp.maximum(jnp.max(s, axis=1, keepdims=True), sink)
        p = jnp.exp(s - m)
        denom = jnp.sum(p, axis=1, keepdims=True) + jnp.exp(sink - m)
        o = jnp.dot(p.astype(BF16), vv, preferred_element_type=F32) / denom
        outs.append(_merge_head_pair(o, blk))
    o_ref[...] = jnp.concatenate(outs, axis=1).astype(o_ref.dtype)


def _out_kernel(*refs, n_attn, alpha):
    x_ref = refs[0]
    attn_refs = refs[1:1 + n_attn]
    mq_ref, gate_ref, mk_ref, mv_ref, w_ref, g_ref, b_ref, o_ref = refs[1 + n_attn:]
    tm = x_ref.shape[0]

    parts = [r[...].astype(F32) for r in attn_refs]
    for hb in range(mq_ref.shape[1] // LANES):
        qq = _head_pair_rows(mq_ref[:, hb * LANES:(hb + 1) * LANES])
        s = _nt_dot(qq, mk_ref[:, hb * LANES:(hb + 1) * LANES])
        p = jnp.exp(s - jnp.max(s, axis=1, keepdims=True))
        l = jnp.sum(p, axis=1, keepdims=True)
        o = jnp.dot(p.astype(BF16), mv_ref[:, hb * LANES:(hb + 1) * LANES],
                    preferred_element_type=F32) / l
        parts.append(_merge_head_pair(o, tm))
    mixed = jnp.concatenate(parts, axis=1)
    gate = gate_ref[...].astype(F32)
    gated = mixed * (gate / (1.0 + jnp.exp(-gate)))
    y = jnp.dot(gated.astype(BF16), w_ref[...], preferred_element_type=F32)
    r = alpha * x_ref[...] + y
    mu = jnp.mean(r, axis=1, keepdims=True)
    c = r - mu
    var = jnp.mean(c * c, axis=1, keepdims=True)
    o_ref[...] = c * lax.rsqrt(var + LN_EPS) * g_ref[...] + b_ref[...]


def _out_layer(x2, attn_outs, mq, gate, mk, mv, w_out, ln_g, ln_b, seq, tm, alpha, name):
    m, d = x2.shape
    mem_len = mk.shape[0] // (m // seq)
    steps_per_batch = seq // tm

    def row(wd):
        return pl.BlockSpec((tm, wd), lambda i: (i, 0))

    def whole(a):
        return pl.BlockSpec(a.shape, lambda i: (0, 0))

    mem_spec = pl.BlockSpec((mem_len, mk.shape[1]), lambda i: (i // steps_per_batch, 0))
    in_specs = ([row(d)] + [row(a.shape[1]) for a in attn_outs]
                + [row(mq.shape[1]), row(gate.shape[1]), mem_spec, mem_spec,
                   whole(w_out), whole(ln_g), whole(ln_b)])
    return pl.pallas_call(
        functools.partial(_out_kernel, n_attn=len(attn_outs), alpha=alpha),
        grid=(m // tm,),
        in_specs=in_specs,
        out_specs=row(d),
        out_shape=jax.ShapeDtypeStruct((m, d), F32),
        compiler_params=pltpu.CompilerParams(dimension_semantics=("parallel",),
                                             vmem_limit_bytes=VMEM_LIMIT_BYTES),
        name=name,
    )(x2, *attn_outs, mq, gate, mk, mv, w_out, ln_g, ln_b)


ROW_TILE = 512
SB_BLOCK = 128
SB_ROWS_PER_STEP = 256
DIFF_BLOCK = 512
SWA_BLOCK = 256


def _even_layer(x2, mem2, pos2, w_in, w_memkv, diff_lambda, diff_subln, w_out, ln_g, ln_b,
                layer_idx, batch, seq, alpha):
    lambda_init = 0.8 - 0.6 * math.exp(-0.3 * layer_idx)
    w_sb = SB_HEADS * HEAD_DIM
    w_df = DIFF_HEADS * HEAD_DIM
    sbq, sbk, sbv, dfq, dfk, dfv, mq, gate = _inproj(
        _inproj_even_kernel, "inproj_even", x2, pos2, _inv_freq_lanes(DIFF_QK_DIM), w_in.astype(BF16),
        (w_sb, w_sb, w_sb, w_df, w_df, w_df, MEM_HEADS * HEAD_DIM, w_out.shape[0]), ROW_TILE)
    mk, mv = _memkv(mem2, w_memkv.astype(BF16))

    def b3(a):
        return a.reshape(batch, seq, a.shape[1])

    sb_o = _row_block_attention(functools.partial(_sb_kernel, blk=SB_BLOCK), "stick_breaking",
                                b3(sbq), b3(sbk), b3(sbv), (), (), SB_ROWS_PER_STEP)
    subln_lanes = jnp.tile(diff_subln.astype(F32), LANES // HEAD_DIM)[None, :]
    df_o = _diff_attention(b3(dfq), b3(dfk), b3(dfv), diff_lambda.astype(F32), subln_lanes,
                           DIFF_BLOCK, lambda_init)
    return _out_layer(x2, [sb_o.reshape(x2.shape[0], -1), df_o.reshape(x2.shape[0], -1)], mq, gate, mk, mv,
                      w_out.astype(BF16), ln_g[None, :], ln_b[None, :], seq, ROW_TILE, alpha, "out_even")


def _odd_layer(x2, mem2, pos2, w_in, w_memkv, sinks, w_out, ln_g, ln_b, batch, seq, alpha):
    w_q = SWA_Q_HEADS * HEAD_DIM
    w_kv = SWA_KV_HEADS * HEAD_DIM
    w_m = MEM_HEADS * HEAD_DIM
    order = np.asarray(SW_HEAD_ORDER)
    head_cols = (order[:, None] * HEAD_DIM + np.arange(HEAD_DIM)[None, :]).reshape(-1)
    gate_lo = w_q + 2 * w_kv + w_m
    last_kv = slice((SWA_KV_HEADS - 1) * HEAD_DIM, w_kv)
    k_w = w_in[:, w_q:w_q + w_kv]
    v_w = w_in[:, w_q + w_kv:w_q + 2 * w_kv]
    w_perm = jnp.concatenate([
        w_in[:, :w_q][:, head_cols], k_w, k_w[:, last_kv], v_w, v_w[:, last_kv],
        w_in[:, w_q + 2 * w_kv:gate_lo], w_in[:, gate_lo:gate_lo + w_q][:, head_cols],
        w_in[:, gate_lo + w_q:]], axis=1).astype(BF16)
    w_out_perm = jnp.concatenate([w_out[:w_q][head_cols], w_out[w_q:]], axis=0).astype(BF16)

    cq, ck, cv, mq, gate = _inproj(
        _inproj_odd_kernel, "inproj_odd", x2, pos2, _inv_freq_lanes(HEAD_DIM), w_perm,
        (w_q, 2 * LANES, 2 * LANES, w_m, w_out.shape[0]), ROW_TILE)
    mk, mv = _memkv(mem2, w_memkv.astype(BF16))

    def b3(a):
        return a.reshape(batch, seq, a.shape[1])

    pairs_per_kv_block = LANES // HEAD_DIM * SWA_GROUP // 2
    sink_spec = [pl.BlockSpec(memory_space=pltpu.SMEM)]
    c_o = _row_block_attention(
        functools.partial(_swa_kernel, blk=SWA_BLOCK, pairs_per_kv_block=pairs_per_kv_block),
        "sliding_window", b3(cq), b3(ck), b3(cv), (sinks.astype(F32)[order],), sink_spec, SWA_BLOCK)
    return _out_layer(x2, [c_o.reshape(x2.shape[0], -1)], mq, gate, mk, mv, w_out_perm,
                      ln_g[None, :], ln_b[None, :], seq, ROW_TILE, alpha, "out_odd")


def kernel(x, mem, positions, w_in_even, w_memkv_even, diff_lambda_even, diff_subln_even, w_out_even,
           ln_g_even, ln_b_even, w_in_odd, w_memkv_odd, sinks_odd, w_out_odd, ln_g_odd, ln_b_odd):
    batch, seq, d = x.shape
    depth = w_in_even.shape[0] + w_in_odd.shape[0]
    alpha = (2 * depth) ** 0.25
    x2 = x.reshape(batch * seq, d)
    mem2 = mem.reshape(batch * mem.shape[1], d)
    pos2 = positions.reshape(batch * seq, 1)
    for i in range(depth):
        j = i // 2
        if i % 2 == 0:
            x2 = _even_layer(x2, mem2, pos2, w_in_even[j], w_memkv_even[j], diff_lambda_even[j],
                             diff_subln_even[j], w_out_even[j], ln_g_even[j], ln_b_even[j], i,
                             batch, seq, alpha)
        else:
            x2 = _odd_layer(x2, mem2, pos2, w_in_odd[j], w_memkv_odd[j], sinks_odd[j], w_out_odd[j],
                            ln_g_odd[j], ln_b_odd[j], batch, seq, alpha)
    return x2.reshape(batch, seq, d)
```

```python
import functools
import math

import jax
import jax.numpy as jnp
import numpy as np
from jax import lax
from jax.experimental import pallas as pl
from jax.experimental.pallas import tpu as pltpu

F32 = jnp.float32
BF16 = jnp.bfloat16

HEAD_DIM = 64
LANES = 128
MEM_HEADS = 4
SB_HEADS = 6
DIFF_HEADS = 6
DIFF_QK_DIM = 32
SWA_Q_HEADS = 12
SWA_KV_HEADS = 3
SWA_GROUP = 4
WINDOW = 128
ROPE_THETA = 500000.0
ROPE_FRACTION = 4
LN_EPS = 1e-5
NEG_BIG = -1e30
VMEM_LIMIT_BYTES = 48 * 1024 * 1024

SB_EXIT_LOG_WEIGHT = -64.0

SW_HEAD_ORDER = (0, 4, 1, 5, 2, 6, 3, 7, 8, 9, 10, 11)


def _nt_dot(a, b):
    return lax.dot_general(a, b, (((1,), (1,)), ((), ())), preferred_element_type=F32)


def _head_pair_rows(q2):
    lane = lax.broadcasted_iota(jnp.int32, q2.shape, 1)
    low = lane < HEAD_DIM
    zero = jnp.zeros_like(q2)
    return jnp.concatenate([jnp.where(low, q2, zero), jnp.where(low, zero, q2)], axis=0)


def _merge_head_pair(o, rows):
    lane = lax.broadcasted_iota(jnp.int32, (rows, LANES), 1)
    return jnp.where(lane < HEAD_DIM, o[:rows], o[rows:])


def _rope_lane_tables(group, pos_f32, inv_freq_row):
    rot = group // ROPE_FRACTION
    half = rot // 2
    lane = lax.broadcasted_iota(jnp.int32, (1, LANES), 1)
    r = lane % group
    ang = pos_f32 * inv_freq_row
    cos = jnp.cos(ang)
    sin = jnp.sin(ang)
    cos_t = jnp.where(r < rot, cos, 1.0)
    sin_first = jnp.where(r < half, -sin, 0.0)
    sin_second = jnp.where((r >= half) & (r < rot), sin, 0.0)
    return cos_t, sin_first, sin_second, half


def _apply_rope(h, tables):
    cos_t, sin_first, sin_second, half = tables
    outs = []
    for c in range(h.shape[1] // LANES):
        blk = h[:, c * LANES:(c + 1) * LANES]
        outs.append(blk * cos_t + pltpu.roll(blk, LANES - half, 1) * sin_first
                    + pltpu.roll(blk, half, 1) * sin_second)
    return jnp.concatenate(outs, axis=1)


def _inv_freq_lanes(group):
    half = group // ROPE_FRACTION // 2
    inv_freq = jnp.exp(-(jnp.arange(half, dtype=F32) / half) * math.log(ROPE_THETA))
    lane = np.arange(LANES)
    return inv_freq[(lane % group) % half][None, :]


def _inproj_even_kernel(x_ref, pos_ref, invf_ref, w_ref, sbq_ref, sbk_ref, sbv_ref,
                        dfq_ref, dfk_ref, dfv_ref, mq_ref, gate_ref):
    h = jnp.dot(x_ref[...].astype(BF16), w_ref[...], preferred_element_type=F32)
    edges = [0]

    def sec(width):
        edges.append(edges[-1] + width)
        return h[:, edges[-2]:edges[-1]]

    w_sb = SB_HEADS * HEAD_DIM
    w_df = DIFF_HEADS * 2 * DIFF_QK_DIM
    tables = _rope_lane_tables(DIFF_QK_DIM, pos_ref[...].astype(F32), invf_ref[...])
    sbq_ref[...] = (sec(w_sb) * HEAD_DIM ** -0.5).astype(BF16)
    sbk_ref[...] = sec(w_sb).astype(BF16)
    sbv_ref[...] = sec(w_sb).astype(BF16)
    q_scale = DIFF_QK_DIM ** -0.5 * math.log2(math.e)
    dfq_ref[...] = (_apply_rope(sec(w_df), tables) * q_scale).astype(BF16)
    dfk_ref[...] = _apply_rope(sec(w_df), tables).astype(BF16)
    dfv_ref[...] = sec(w_df).astype(BF16)
    mq_ref[...] = (sec(MEM_HEADS * HEAD_DIM) * HEAD_DIM ** -0.5).astype(BF16)
    gate_ref[...] = sec(gate_ref.shape[1]).astype(BF16)


def _inproj_odd_kernel(x_ref, pos_ref, invf_ref, w_ref, cq_ref, ck_ref, cv_ref, mq_ref, gate_ref):
    h = jnp.dot(x_ref[...].astype(BF16), w_ref[...], preferred_element_type=F32)
    edges = [0]

    def sec(width):
        edges.append(edges[-1] + width)
        return h[:, edges[-2]:edges[-1]]

    w_kv = ck_ref.shape[1]
    tables = _rope_lane_tables(HEAD_DIM, pos_ref[...].astype(F32), invf_ref[...])
    cq_ref[...] = (_apply_rope(sec(SWA_Q_HEADS * HEAD_DIM), tables) * HEAD_DIM ** -0.5).astype(BF16)
    ck_ref[...] = _apply_rope(sec(w_kv), tables).astype(BF16)
    cv_ref[...] = sec(w_kv).astype(BF16)
    mq_ref[...] = (sec(MEM_HEADS * HEAD_DIM) * HEAD_DIM ** -0.5).astype(BF16)
    gate_ref[...] = sec(gate_ref.shape[1]).astype(BF16)


def _inproj(kernel_fn, name, x2, pos2, invf, w, out_widths, tm):
    m, d = x2.shape
    return pl.pallas_call(
        kernel_fn,
        grid=(m // tm,),
        in_specs=[pl.BlockSpec((tm, d), lambda i: (i, 0)),
                  pl.BlockSpec((tm, 1), lambda i: (i, 0)),
                  pl.BlockSpec((1, LANES), lambda i: (0, 0)),
                  pl.BlockSpec(w.shape, lambda i: (0, 0))],
        out_specs=[pl.BlockSpec((tm, wd), lambda i: (i, 0)) for wd in out_widths],
        out_shape=[jax.ShapeDtypeStruct((m, wd), BF16) for wd in out_widths],
        compiler_params=pltpu.CompilerParams(dimension_semantics=("parallel",),
                                             vmem_limit_bytes=VMEM_LIMIT_BYTES),
        name=name,
    )(x2, pos2, invf, w)


def _memkv_kernel(mem_ref, w_ref, mk_ref, mv_ref):
    kv = jnp.dot(mem_ref[...].astype(BF16), w_ref[...], preferred_element_type=F32)
    half = mk_ref.shape[1]
    mk_ref[...] = kv[:, :half].astype(BF16)
    mv_ref[...] = kv[:, half:].astype(BF16)


def _memkv(mem2, w):
    rows = mem2.shape[0]
    half = w.shape[1] // 2
    return pl.pallas_call(
        _memkv_kernel,
        out_shape=[jax.ShapeDtypeStruct((rows, half), BF16)] * 2,
        compiler_params=pltpu.CompilerParams(vmem_limit_bytes=VMEM_LIMIT_BYTES),
        name="memkv",
    )(mem2, w)


def _strict_lower(n):
    return (lax.broadcasted_iota(jnp.int32, (n, n), 0)
            > lax.broadcasted_iota(jnp.int32, (n, n), 1)).astype(BF16)


def _sb_tile(qq, kk, vv, strict, later, carry):
    z = _nt_dot(qq, kk)
    soft = jnp.log1p(jnp.exp(-jnp.abs(z)))
    log_1mb = -jnp.maximum(z, 0.0) - soft
    if strict is not None:
        log_1mb = jnp.where(strict, log_1mb, 0.0)
    log_b = jnp.minimum(z, 0.0) - soft
    hi = log_1mb.astype(BF16)
    lo = (log_1mb - hi.astype(F32)).astype(BF16)
    after = (jnp.dot(hi, later, preferred_element_type=F32)
             + jnp.dot(lo, later, preferred_element_type=F32))
    w = jnp.exp(log_b + after + carry)
    if strict is not None:
        w = jnp.where(strict, w, 0.0)
    return (jnp.dot(w.astype(BF16), vv, preferred_element_type=F32),
            jnp.sum(log_1mb, axis=1, keepdims=True))


def _sb_kernel(q_ref, k_ref, v_ref, o_ref, *, blk):
    n_pairs = q_ref.shape[1] // LANES
    n_sub = q_ref.shape[0] // blk

    def lanes(hp):
        return slice(hp * LANES, (hp + 1) * LANES)

    rows = lax.broadcasted_iota(jnp.int32, (2 * blk, 2 * blk), 0)
    q_off = jnp.where(rows >= blk, rows - blk, rows)
    col = lax.broadcasted_iota(jnp.int32, (2 * blk, 2 * blk), 1)
    later2 = _strict_lower(2 * blk)
    later1 = _strict_lower(blk)
    no_carry = jnp.zeros((2 * blk, 1), F32)

    def first_tiles(sub):
        bi = pl.program_id(1) * n_sub + sub
        r0 = pl.multiple_of(bi * blk, blk)
        p0 = pl.multiple_of(jnp.maximum(r0 - blk, 0), blk)
        strict = (col - blk < q_off) & (col + (r0 - blk) >= 0)
        qqs, accs, carries = [], [], []
        for hp in range(n_pairs):
            qq = _head_pair_rows(q_ref[sub * blk:(sub + 1) * blk, lanes(hp)])
            kk = jnp.concatenate([k_ref[pl.ds(p0, blk), lanes(hp)], k_ref[pl.ds(r0, blk), lanes(hp)]], axis=0)
            vv = jnp.concatenate([v_ref[pl.ds(p0, blk), lanes(hp)], v_ref[pl.ds(r0, blk), lanes(hp)]], axis=0)
            pv, dsum = _sb_tile(qq, kk, vv, strict, later2, no_carry)
            qqs.append(qq)
            accs.append(pv)
            carries.append(dsum)
        return bi, qqs, carries, accs

    def finish(sub, bi, qqs, carries, accs):
        def cond(c):
            j, carries, _ = c
            return (j >= 0) & (jnp.max(functools.reduce(jnp.maximum, carries)) > SB_EXIT_LOG_WEIGHT)

        def body(c):
            j, carries, accs = c
            start = pl.multiple_of(j * blk, blk)
            new_carries, new_accs = [], []
            for hp in range(n_pairs):
                pv, dsum = _sb_tile(qqs[hp], k_ref[pl.ds(start, blk), lanes(hp)],
                                    v_ref[pl.ds(start, blk), lanes(hp)], None, later1, carries[hp])
                new_accs.append(accs[hp] + pv)
                new_carries.append(carries[hp] + dsum)
            return j - 1, tuple(new_carries), tuple(new_accs)

        _, _, accs = lax.while_loop(cond, body, (bi - 2, tuple(carries), tuple(accs)))
        o_ref[sub * blk:(sub + 1) * blk, :] = jnp.concatenate(
            [_merge_head_pair(a, blk) for a in accs], axis=1).astype(o_ref.dtype)

    started = [first_tiles(sub) for sub in range(n_sub)]
    for sub in range(n_sub):
        finish(sub, *started[sub])


def _row_block_attention(kernel_fn, name, q, k, v, extra_inputs, extra_specs, blk):
    b, s, wq = q.shape
    q_spec = pl.BlockSpec((None, blk, wq), lambda bi, qi: (bi, qi, 0))
    kv_spec = pl.BlockSpec((None, s, k.shape[2]), lambda bi, qi: (bi, 0, 0))
    return pl.pallas_call(
        kernel_fn,
        grid=(b, s // blk),
        in_specs=list(extra_specs) + [q_spec, kv_spec, kv_spec],
        out_specs=q_spec,
        out_shape=jax.ShapeDtypeStruct(q.shape, BF16),
        compiler_params=pltpu.CompilerParams(
            dimension_semantics=("parallel", "arbitrary"),
            vmem_limit_bytes=VMEM_LIMIT_BYTES),
        name=name,
    )(*extra_inputs, q, k, v)


def _diff_kernel(dl_ref, subln_ref, q_ref, k_ref, v_ref, o_ref, vp_ref, sa_ref, sb_ref, acc_ref, *, blk, lambda_init):
    qi = pl.program_id(2)
    seq = v_ref.shape[0]

    @pl.when(qi == 0)
    def _():
        low = lax.broadcasted_iota(jnp.int32, (blk, LANES), 1) < HEAD_DIM

        def build(c, _):
            rows = pl.ds(pl.multiple_of(c * blk, blk), blk)
            vf = v_ref[rows, :].astype(F32)
            vp_ref[rows, :LANES] = jnp.where(low, vf, 1.0).astype(BF16)
            vp_ref[rows, LANES:] = jnp.where(low, pltpu.roll(vf, HEAD_DIM, 1), 1.0).astype(BF16)
            return 0

        lax.fori_loop(0, seq // blk, build, 0)

    q2 = q_ref[...]
    lane = lax.broadcasted_iota(jnp.int32, q2.shape, 1)
    zero = jnp.zeros_like(q2)
    n_maps = LANES // DIFF_QK_DIM
    qq = jnp.concatenate([jnp.where(lane // DIFF_QK_DIM == c, q2, zero) for c in range(n_maps)], axis=0)
    rows_n = n_maps * blk
    half = rows_n // 2

    wide = 2 * blk

    def scores(s_ref, block):
        s_ref[...] = _nt_dot(qq, k_ref[pl.ds(pl.multiple_of(block * wide, wide), wide), :])

    def update(s_ref, block, width, m, q_shift):
        keys = pl.ds(pl.multiple_of(block * wide, wide), width)
        s = s_ref[:, :width]
        if q_shift is not None:
            q_off = lax.broadcasted_iota(jnp.int32, (rows_n, width), 0) % blk
            k_off = lax.broadcasted_iota(jnp.int32, (rows_n, width), 1)
            s = jnp.where(k_off <= q_off + q_shift, s, NEG_BIG)
        m_new = jnp.maximum(m, jnp.max(s, axis=1, keepdims=True))
        p = jnp.exp2(s - m_new).astype(BF16)
        alpha = jnp.exp2(m - m_new)
        for h, v_lanes in enumerate((slice(0, LANES), slice(LANES, 2 * LANES))):
            rows = slice(h * half, (h + 1) * half)
            acc_ref[rows, :] = alpha[rows] * acc_ref[rows, :] + jnp.dot(
                p[rows], vp_ref[keys, v_lanes], preferred_element_type=F32)
        return m_new

    def finish():
        acc = acc_ref[...]
        dl = dl_ref[...]
        lam = (jnp.exp(jnp.sum(dl[0:1] * dl[1:2], axis=1, keepdims=True))
               - jnp.exp(jnp.sum(dl[2:3] * dl[3:4], axis=1, keepdims=True)) + lambda_init)
        low = lax.broadcasted_iota(jnp.int32, (blk, LANES), 1) < HEAD_DIM
        ones = jnp.ones((LANES, LANES), BF16)
        normed = []
        for h in range(2):
            n0 = acc[2 * h * blk:(2 * h + 1) * blk]
            n1 = acc[(2 * h + 1) * blk:(2 * h + 2) * blk]
            l0 = pltpu.roll(n0, HEAD_DIM, 1)
            l1 = pltpu.roll(n1, HEAD_DIM, 1)
            d = n0 - (lam * l0 / l1) * n1
            dsq = jnp.where(low, d * d, 0.0)
            hi = dsq.astype(BF16)
            lo = (dsq - hi.astype(F32)).astype(BF16)
            ms = (jnp.dot(hi, ones, preferred_element_type=F32)
                  + jnp.dot(lo, ones, preferred_element_type=F32)) * (1.0 / HEAD_DIM)
            normed.append(d * lax.rsqrt(ms + LN_EPS * l0 * l0))
        y = jnp.where(low, normed[0], pltpu.roll(normed[1], HEAD_DIM, 1))
        o_ref[...] = (y * subln_ref[...] * (1.0 - lambda_init)).astype(o_ref.dtype)

    n_wide = qi // 2
    acc_ref[...] = jnp.zeros(acc_ref.shape, F32)
    scores(sa_ref, 0)

    def body(i, m):
        scores(sb_ref, 2 * i + 1)
        m = update(sa_ref, 2 * i, wide, m, None)
        scores(sa_ref, 2 * i + 2)
        return update(sb_ref, 2 * i + 1, wide, m, None)

    m = lax.fori_loop(0, n_wide // 2, body, jnp.full((rows_n, 1), NEG_BIG, F32))

    def last(s_ref, m):
        if_even = functools.partial(update, s_ref, n_wide, blk, m, 0)
        if_odd = functools.partial(update, s_ref, n_wide, wide, m, blk)
        for parity, final_update in ((0, if_even), (1, if_odd)):
            @pl.when(qi % 2 == parity)
            def _():
                final_update()
                finish()

    @pl.when(n_wide % 2 == 0)
    def _():
        last(sa_ref, m)

    @pl.when(n_wide % 2 == 1)
    def _():
        scores(sb_ref, n_wide)
        last(sb_ref, update(sa_ref, n_wide - 1, wide, m, None))


def _diff_attention(q, k, v, diff_lambda, subln_lanes, blk, lambda_init):
    b, s, wq = q.shape
    assert s % (2 * blk) == 0
    rows_n = LANES // DIFF_QK_DIM * blk
    q_spec = pl.BlockSpec((None, blk, LANES), lambda bi, hp, qi: (bi, qi, hp))
    kv_spec = pl.BlockSpec((None, s, LANES), lambda bi, hp, qi: (bi, 0, hp))
    return pl.pallas_call(
        functools.partial(_diff_kernel, blk=blk, lambda_init=lambda_init),
        grid=(b, wq // LANES, s // blk),
        in_specs=[pl.BlockSpec(diff_lambda.shape, lambda bi, hp, qi: (0, 0)),
                  pl.BlockSpec((1, LANES), lambda bi, hp, qi: (0, 0)), q_spec, kv_spec, kv_spec],
        out_specs=q_spec,
        out_shape=jax.ShapeDtypeStruct(q.shape, BF16),
        scratch_shapes=[pltpu.VMEM((s, 2 * LANES), BF16),
                        pltpu.VMEM((rows_n, 2 * blk), F32),
                        pltpu.VMEM((rows_n, 2 * blk), F32),
                        pltpu.VMEM((rows_n, LANES), F32)],
        compiler_params=pltpu.CompilerParams(
            dimension_semantics=("parallel", "parallel", "arbitrary"),
            vmem_limit_bytes=VMEM_LIMIT_BYTES),
        name="differential",
    )(diff_lambda, subln_lanes, q, k, v)


def _swa_kernel(sinks_ref, q_ref, k_ref, v_ref, o_ref, *, blk, pairs_per_kv_block):
    qi = pl.program_id(1)
    r0 = pl.multiple_of(qi * blk, blk)
    p0 = pl.multiple_of(jnp.maximum(r0 - WINDOW, 0), WINDOW)

    def lanes(hb):
        return slice(hb * LANES, (hb + 1) * LANES)

    shape = (2 * blk, WINDOW + blk)
    rows = lax.broadcasted_iota(jnp.int32, shape, 0)
    q_off = jnp.where(rows >= blk, rows - blk, rows)
    col = lax.broadcasted_iota(jnp.int32, shape, 1)
    rel = (col - WINDOW) - q_off
    band = (rel <= 0) & (rel > -WINDOW) & (col + (r0 - WINDOW) >= 0)
    second_head = lax.broadcasted_iota(jnp.int32, (2 * blk, 1), 0) >= blk
    outs = []
    for hb in range(q_ref.shape[1] // LANES):
        kvb = hb // pairs_per_kv_block
        kk = jnp.concatenate([k_ref[pl.ds(p0, WINDOW), lanes(kvb)], k_ref[pl.ds(r0, blk), lanes(kvb)]], axis=0)
        vv = jnp.concatenate([v_ref[pl.ds(p0, WINDOW), lanes(kvb)], v_ref[pl.ds(r0, blk), lanes(kvb)]], axis=0)
        qq = _head_pair_rows(q_ref[:, lanes(hb)])
        s = jnp.where(band, _nt_dot(qq, kk), NEG_BIG)
        sink = jnp.where(second_head, sinks_ref[2 * hb + 1], sinks_ref[2 * hb])
        m = jnp.maximum(jnp.max(s, axis=1, keepdims=True), sink)
        p = jnp.exp(s - m)
        denom = jnp.sum(p, axis=1, keepdims=True) + jnp.exp(sink - m)
        o = jnp.dot(p.astype(BF16), vv, preferred_element_type=F32) / denom
        outs.append(_merge_head_pair(o, blk))
    o_ref[...] = jnp.concatenate(outs, axis=1).astype(o_ref.dtype)


def _out_kernel(*refs, n_attn, alpha):
    x_ref = refs[0]
    attn_refs = refs[1:1 + n_attn]
    mq_ref, gate_ref, mk_ref, mv_ref, w_ref, g_ref, b_ref, o_ref = refs[1 + n_attn:]
    tm = x_ref.shape[0]

    parts = [r[...].astype(F32) for r in attn_refs]
    for hb in range(mq_ref.shape[1] // LANES):
        qq = _head_pair_rows(mq_ref[:, hb * LANES:(hb + 1) * LANES])
        s = _nt_dot(qq, mk_ref[:, hb * LANES:(hb + 1) * LANES])
        p = jnp.exp(s - jnp.max(s, axis=1, keepdims=True))
        l = jnp.sum(p, axis=1, keepdims=True)
        o = jnp.dot(p.astype(BF16), mv_ref[:, hb * LANES:(hb + 1) * LANES],
                    preferred_element_type=F32) / l
        parts.append(_merge_head_pair(o, tm))
    mixed = jnp.concatenate(parts, axis=1)
    gate = gate_ref[...].astype(F32)
    gated = mixed * (gate / (1.0 + jnp.exp(-gate)))
    y = jnp.dot(gated.astype(BF16), w_ref[...], preferred_element_type=F32)
    r = alpha * x_ref[...] + y
    mu = jnp.mean(r, axis=1, keepdims=True)
    c = r - mu
    var = jnp.mean(c * c, axis=1, keepdims=True)
    o_ref[...] = c * lax.rsqrt(var + LN_EPS) * g_ref[...] + b_ref[...]


def _out_layer(x2, attn_outs, mq, gate, mk, mv, w_out, ln_g, ln_b, seq, tm, alpha, name):
    m, d = x2.shape
    mem_len = mk.shape[0] // (m // seq)
    steps_per_batch = seq // tm

    def row(wd):
        return pl.BlockSpec((tm, wd), lambda i: (i, 0))

    def whole(a):
        return pl.BlockSpec(a.shape, lambda i: (0, 0))

    mem_spec = pl.BlockSpec((mem_len, mk.shape[1]), lambda i: (i // steps_per_batch, 0))
    in_specs = ([row(d)] + [row(a.shape[1]) for a in attn_outs]
                + [row(mq.shape[1]), row(gate.shape[1]), mem_spec, mem_spec,
                   whole(w_out), whole(ln_g), whole(ln_b)])
    return pl.pallas_call(
        functools.partial(_out_kernel, n_attn=len(attn_outs), alpha=alpha),
        grid=(m // tm,),
        in_specs=in_specs,
        out_specs=row(d),
        out_shape=jax.ShapeDtypeStruct((m, d), F32),
        compiler_params=pltpu.CompilerParams(dimension_semantics=("parallel",),
                                             vmem_limit_bytes=VMEM_LIMIT_BYTES),
        name=name,
    )(x2, *attn_outs, mq, gate, mk, mv, w_out, ln_g, ln_b)


ROW_TILE = 512
SB_BLOCK = 128
SB_ROWS_PER_STEP = 256
DIFF_BLOCK = 512
SWA_BLOCK = 256


def _even_layer(x2, mem2, pos2, w_in, w_memkv, diff_lambda, diff_subln, w_out, ln_g, ln_b,
                layer_idx, batch, seq, alpha):
    lambda_init = 0.8 - 0.6 * math.exp(-0.3 * layer_idx)
    w_sb = SB_HEADS * HEAD_DIM
    w_df = DIFF_HEADS * HEAD_DIM
    sbq, sbk, sbv, dfq, dfk, dfv, mq, gate = _inproj(
        _inproj_even_kernel, "inproj_even", x2, pos2, _inv_freq_lanes(DIFF_QK_DIM), w_in.astype(BF16),
        (w_sb, w_sb, w_sb, w_df, w_df, w_df, MEM_HEADS * HEAD_DIM, w_out.shape[0]), ROW_TILE)
    mk, mv = _memkv(mem2, w_memkv.astype(BF16))

    def b3(a):
        return a.reshape(batch, seq, a.shape[1])

    sb_o = _row_block_attention(functools.partial(_sb_kernel, blk=SB_BLOCK), "stick_breaking",
                                b3(sbq), b3(sbk), b3(sbv), (), (), SB_ROWS_PER_STEP)
    subln_lanes = jnp.tile(diff_subln.astype(F32), LANES // HEAD_DIM)[None, :]
    df_o = _diff_attention(b3(dfq), b3(dfk), b3(dfv), diff_lambda.astype(F32), subln_lanes,
                           DIFF_BLOCK, lambda_init)
    return _out_layer(x2, [sb_o.reshape(x2.shape[0], -1), df_o.reshape(x2.shape[0], -1)], mq, gate, mk, mv,
                      w_out.astype(BF16), ln_g[None, :], ln_b[None, :], seq, ROW_TILE, alpha, "out_even")


def _odd_layer(x2, mem2, pos2, w_in, w_memkv, sinks, w_out, ln_g, ln_b, batch, seq, alpha):
    w_q = SWA_Q_HEADS * HEAD_DIM
    w_kv = SWA_KV_HEADS * HEAD_DIM
    w_m = MEM_HEADS * HEAD_DIM
    order = np.asarray(SW_HEAD_ORDER)
    head_cols = (order[:, None] * HEAD_DIM + np.arange(HEAD_DIM)[None, :]).reshape(-1)
    gate_lo = w_q + 2 * w_kv + w_m
    last_kv = slice((SWA_KV_HEADS - 1) * HEAD_DIM, w_kv)
    k_w = w_in[:, w_q:w_q + w_kv]
    v_w = w_in[:, w_q + w_kv:w_q + 2 * w_kv]
    w_perm = jnp.concatenate([
        w_in[:, :w_q][:, head_cols], k_w, k_w[:, last_kv], v_w, v_w[:, last_kv],
        w_in[:, w_q + 2 * w_kv:gate_lo], w_in[:, gate_lo:gate_lo + w_q][:, head_cols],
        w_in[:, gate_lo + w_q:]], axis=1).astype(BF16)
    w_out_perm = jnp.concatenate([w_out[:w_q][head_cols], w_out[w_q:]], axis=0).astype(BF16)

    cq, ck, cv, mq, gate = _inproj(
        _inproj_odd_kernel, "inproj_odd", x2, pos2, _inv_freq_lanes(HEAD_DIM), w_perm,
        (w_q, 2 * LANES, 2 * LANES, w_m, w_out.shape[0]), ROW_TILE)
    mk, mv = _memkv(mem2, w_memkv.astype(BF16))

    def b3(a):
        return a.reshape(batch, seq, a.shape[1])

    pairs_per_kv_block = LANES // HEAD_DIM * SWA_GROUP // 2
    sink_spec = [pl.BlockSpec(memory_space=pltpu.SMEM)]
    c_o = _row_block_attention(
        functools.partial(_swa_kernel, blk=SWA_BLOCK, pairs_per_kv_block=pairs_per_kv_block),
        "sliding_window", b3(cq), b3(ck), b3(cv), (sinks.astype(F32)[order],), sink_spec, SWA_BLOCK)
    return _out_layer(x2, [c_o.reshape(x2.shape[0], -1)], mq, gate, mk, mv, w_out_perm,
                      ln_g[None, :], ln_b[None, :], seq, ROW_TILE, alpha, "out_odd")


def kernel(x, mem, positions, w_in_even, w_memkv_even, diff_lambda_even, diff_subln_even, w_out_even,
           ln_g_even, ln_b_even, w_in_odd, w_memkv_odd, sinks_odd, w_out_odd, ln_g_odd, ln_b_odd):
    batch, seq, d = x.shape
    depth = w_in_even.shape[0] + w_in_odd.shape[0]
    alpha = (2 * depth) ** 0.25
    x2 = x.reshape(batch * seq, d)
    mem2 = mem.reshape(batch * mem.shape[1], d)
    pos2 = positions.reshape(batch * seq, 1)
    for i in range(depth):
        j = i // 2
        if i % 2 == 0:
            x2 = _even_layer(x2, mem2, pos2, w_in_even[j], w_memkv_even[j], diff_lambda_even[j],
                             diff_subln_even[j], w_out_even[j], ln_g_even[j], ln_b_even[j], i,
                             batch, seq, alpha)
        else:
            x2 = _odd_layer(x2, mem2, pos2, w_in_odd[j], w_memkv_odd[j], sinks_odd[j], w_out_odd[j],
                            ln_g_odd[j], ln_b_odd[j], batch, seq, alpha)
    return x2.reshape(batch, seq, d)
```

```python
import functools
import math

import jax
import jax.numpy as jnp
import numpy as np
from jax import lax
from jax.experimental import pallas as pl
from jax.experimental.pallas import tpu as pltpu

F32 = jnp.float32
BF16 = jnp.bfloat16

HEAD_DIM = 64
LANES = 128
MEM_HEADS = 4
SB_HEADS = 6
DIFF_HEADS = 6
DIFF_QK_DIM = 32
SWA_Q_HEADS = 12
SWA_KV_HEADS = 3
SWA_GROUP = 4
WINDOW = 128
ROPE_THETA = 500000.0
ROPE_FRACTION = 4
LN_EPS = 1e-5
NEG_BIG = -1e30
VMEM_LIMIT_BYTES = 48 * 1024 * 1024

SB_EXIT_LOG2_WEIGHT = -92.0
LOG2_E = math.log2(math.e)

SW_HEAD_ORDER = (0, 4, 1, 5, 2, 6, 3, 7, 8, 9, 10, 11)


def _nt_dot(a, b):
    return lax.dot_general(a, b, (((1,), (1,)), ((), ())), preferred_element_type=F32)


def _head_pair_rows(q2):
    lane = lax.broadcasted_iota(jnp.int32, q2.shape, 1)
    low = lane < HEAD_DIM
    zero = jnp.zeros_like(q2)
    return jnp.concatenate([jnp.where(low, q2, zero), jnp.where(low, zero, q2)], axis=0)


def _merge_head_pair(o, rows):
    lane = lax.broadcasted_iota(jnp.int32, (rows, LANES), 1)
    return jnp.where(lane < HEAD_DIM, o[:rows], o[rows:])


def _rope_lane_tables(group, pos_f32, inv_freq_row):
    rot = group // ROPE_FRACTION
    half = rot // 2
    lane = lax.broadcasted_iota(jnp.int32, (1, LANES), 1)
    r = lane % group
    ang = pos_f32 * inv_freq_row
    cos = jnp.cos(ang)
    sin = jnp.sin(ang)
    cos_t = jnp.where(r < rot, cos, 1.0)
    sin_first = jnp.where(r < half, -sin, 0.0)
    sin_second = jnp.where((r >= half) & (r < rot), sin, 0.0)
    return cos_t, sin_first, sin_second, half


def _apply_rope(h, tables):
    cos_t, sin_first, sin_second, half = tables
    outs = []
    for c in range(h.shape[1] // LANES):
        blk = h[:, c * LANES:(c + 1) * LANES]
        outs.append(blk * cos_t + pltpu.roll(blk, LANES - half, 1) * sin_first
                    + pltpu.roll(blk, half, 1) * sin_second)
    return jnp.concatenate(outs, axis=1)


def _inv_freq_lanes(group):
    half = group // ROPE_FRACTION // 2
    inv_freq = jnp.exp(-(jnp.arange(half, dtype=F32) / half) * math.log(ROPE_THETA))
    lane = np.arange(LANES)
    return inv_freq[(lane % group) % half][None, :]


def _inproj_even_kernel(x_ref, pos_ref, invf_ref, w_ref, sbq_ref, sbk_ref, sbv_ref,
                        dfq_ref, dfk_ref, dfv_ref, mq_ref, gate_ref):
    h = jnp.dot(x_ref[...].astype(BF16), w_ref[...], preferred_element_type=F32)
    edges = [0]

    def sec(width):
        edges.append(edges[-1] + width)
        return h[:, edges[-2]:edges[-1]]

    w_sb = SB_HEADS * HEAD_DIM
    w_df = DIFF_HEADS * 2 * DIFF_QK_DIM
    tables = _rope_lane_tables(DIFF_QK_DIM, pos_ref[...].astype(F32), invf_ref[...])
    sbq_ref[...] = (sec(w_sb) * (HEAD_DIM ** -0.5 * LOG2_E)).astype(BF16)
    sbk_ref[...] = sec(w_sb).astype(BF16)
    sbv_ref[...] = sec(w_sb).astype(BF16)
    q_scale = DIFF_QK_DIM ** -0.5 * math.log2(math.e)
    dfq_ref[...] = (_apply_rope(sec(w_df), tables) * q_scale).astype(BF16)
    dfk_ref[...] = _apply_rope(sec(w_df), tables).astype(BF16)
    dfv_ref[...] = sec(w_df).astype(BF16)
    mq_ref[...] = (sec(MEM_HEADS * HEAD_DIM) * HEAD_DIM ** -0.5).astype(BF16)
    gate_ref[...] = sec(gate_ref.shape[1]).astype(BF16)


def _inproj_odd_kernel(x_ref, pos_ref, invf_ref, w_ref, cq_ref, ck_ref, cv_ref, mq_ref, gate_ref):
    h = jnp.dot(x_ref[...].astype(BF16), w_ref[...], preferred_element_type=F32)
    edges = [0]

    def sec(width):
        edges.append(edges[-1] + width)
        return h[:, edges[-2]:edges[-1]]

    w_kv = ck_ref.shape[1]
    tables = _rope_lane_tables(HEAD_DIM, pos_ref[...].astype(F32), invf_ref[...])
    cq_ref[...] = (_apply_rope(sec(SWA_Q_HEADS * HEAD_DIM), tables)
                   * (HEAD_DIM ** -0.5 * LOG2_E)).astype(BF16)
    ck_ref[...] = _apply_rope(sec(w_kv), tables).astype(BF16)
    cv_ref[...] = sec(w_kv).astype(BF16)
    mq_ref[...] = (sec(MEM_HEADS * HEAD_DIM) * HEAD_DIM ** -0.5).astype(BF16)
    gate_ref[...] = sec(gate_ref.shape[1]).astype(BF16)


def _inproj(kernel_fn, name, x2, pos2, invf, w, out_widths, tm):
    m, d = x2.shape
    return pl.pallas_call(
        kernel_fn,
        grid=(m // tm,),
        in_specs=[pl.BlockSpec((tm, d), lambda i: (i, 0)),
                  pl.BlockSpec((tm, 1), lambda i: (i, 0)),
                  pl.BlockSpec((1, LANES), lambda i: (0, 0)),
                  pl.BlockSpec(w.shape, lambda i: (0, 0))],
        out_specs=[pl.BlockSpec((tm, wd), lambda i: (i, 0)) for wd in out_widths],
        out_shape=[jax.ShapeDtypeStruct((m, wd), BF16) for wd in out_widths],
        compiler_params=pltpu.CompilerParams(dimension_semantics=("parallel",),
                                             vmem_limit_bytes=VMEM_LIMIT_BYTES),
        name=name,
    )(x2, pos2, invf, w)


def _memkv_kernel(mem_ref, w_ref, mk_ref, mv_ref):
    kv = jnp.dot(mem_ref[...].astype(BF16), w_ref[...], preferred_element_type=F32)
    half = mk_ref.shape[1]
    mk_ref[...] = kv[:, :half].astype(BF16)
    mv_ref[...] = kv[:, half:].astype(BF16)


def _memkv(mem2, w):
    rows = mem2.shape[0]
    half = w.shape[1] // 2
    return pl.pallas_call(
        _memkv_kernel,
        out_shape=[jax.ShapeDtypeStruct((rows, half), BF16)] * 2,
        compiler_params=pltpu.CompilerParams(vmem_limit_bytes=VMEM_LIMIT_BYTES),
        name="memkv",
    )(mem2, w)


def _strict_lower(n):
    return (lax.broadcasted_iota(jnp.int32, (n, n), 0)
            > lax.broadcasted_iota(jnp.int32, (n, n), 1)).astype(BF16)


def _sb_tiles(chains, later):
    zs = [_nt_dot(qq, kk) for qq, kk, _, _, _ in chains]
    logs = []
    for z, (_, _, _, strict, _) in zip(zs, chains):
        soft = jnp.log(1.0 + jnp.exp2(-jnp.abs(z))) * LOG2_E
        log_1mb = -jnp.maximum(z, 0.0) - soft
        if strict is not None:
            log_1mb = jnp.where(strict, log_1mb, 0.0)
        hi = log_1mb.astype(BF16)
        lo = (log_1mb - hi.astype(F32)).astype(BF16)
        logs.append((jnp.minimum(z, 0.0) - soft, log_1mb, hi, lo))
    afters = [jnp.dot(hi, later, preferred_element_type=F32) + jnp.dot(lo, later, preferred_element_type=F32)
              for _, _, hi, lo in logs]
    outs = []
    for (log_b, log_1mb, _, _), after, (_, _, vv, strict, carry) in zip(logs, afters, chains):
        w = jnp.exp2(log_b + after + carry)
        if strict is not None:
            w = jnp.where(strict, w, 0.0)
        outs.append((jnp.dot(w.astype(BF16), vv, preferred_element_type=F32),
                     jnp.sum(log_1mb, axis=1, keepdims=True)))
    return outs


def _sb_kernel(q_ref, k_ref, v_ref, o_ref, *, blk):
    n_pairs = q_ref.shape[1] // LANES
    n_sub = q_ref.shape[0] // blk

    def lanes(hp):
        return slice(hp * LANES, (hp + 1) * LANES)

    rows = lax.broadcasted_iota(jnp.int32, (2 * blk, 2 * blk), 0)
    q_off = jnp.where(rows >= blk, rows - blk, rows)
    col = lax.broadcasted_iota(jnp.int32, (2 * blk, 2 * blk), 1)
    later2 = _strict_lower(2 * blk)
    later1 = _strict_lower(blk)
    no_carry = jnp.zeros((2 * blk, 1), F32)

    first = []
    for sub in range(n_sub):
        r0 = pl.multiple_of((pl.program_id(1) * n_sub + sub) * blk, blk)
        p0 = pl.multiple_of(jnp.maximum(r0 - blk, 0), blk)
        strict = (col - blk < q_off) & (col + (r0 - blk) >= 0)
        for hp in range(n_pairs):
            qq = _head_pair_rows(q_ref[sub * blk:(sub + 1) * blk, lanes(hp)])
            kk = jnp.concatenate([k_ref[pl.ds(p0, blk), lanes(hp)], k_ref[pl.ds(r0, blk), lanes(hp)]], axis=0)
            vv = jnp.concatenate([v_ref[pl.ds(p0, blk), lanes(hp)], v_ref[pl.ds(r0, blk), lanes(hp)]], axis=0)
            first.append((qq, kk, vv, strict, no_carry))
    started = _sb_tiles(first, later2)

    def finish(sub):
        bi = pl.program_id(1) * n_sub + sub
        mine = slice(sub * n_pairs, (sub + 1) * n_pairs)
        qqs = [chain[0] for chain in first[mine]]
        accs = tuple(pv for pv, _ in started[mine])
        carries = tuple(dsum for _, dsum in started[mine])

        def cond(c):
            j, carries, _ = c
            return (j >= 0) & (jnp.max(functools.reduce(jnp.maximum, carries)) > SB_EXIT_LOG2_WEIGHT)

        def body(c):
            j, carries, accs = c
            start = pl.multiple_of(j * blk, blk)
            outs = _sb_tiles([(qqs[hp], k_ref[pl.ds(start, blk), lanes(hp)], v_ref[pl.ds(start, blk), lanes(hp)],
                               None, carries[hp]) for hp in range(n_pairs)], later1)
            return (j - 1, tuple(c + dsum for c, (_, dsum) in zip(carries, outs)),
                    tuple(a + pv for a, (pv, _) in zip(accs, outs)))

        _, _, accs = lax.while_loop(cond, body, (bi - 2, carries, accs))
        o_ref[sub * blk:(sub + 1) * blk, :] = jnp.concatenate(
            [_merge_head_pair(a, blk) for a in accs], axis=1).astype(o_ref.dtype)

    for sub in range(n_sub):
        finish(sub)


def _row_block_attention(kernel_fn, name, q, k, v, extra_inputs, extra_specs, blk):
    b, s, wq = q.shape
    q_spec = pl.BlockSpec((None, blk, wq), lambda bi, qi: (bi, qi, 0))
    kv_spec = pl.BlockSpec((None, s, k.shape[2]), lambda bi, qi: (bi, 0, 0))
    return pl.pallas_call(
        kernel_fn,
        grid=(b, s // blk),
        in_specs=list(extra_specs) + [q_spec, kv_spec, kv_spec],
        out_specs=q_spec,
        out_shape=jax.ShapeDtypeStruct(q.shape, BF16),
        compiler_params=pltpu.CompilerParams(
            dimension_semantics=("parallel", "arbitrary"),
            vmem_limit_bytes=VMEM_LIMIT_BYTES),
        name=name,
    )(*extra_inputs, q, k, v)


def _diff_kernel(dl_ref, subln_ref, q_ref, k_ref, v_ref, o_ref, vp_ref, sa_ref, sb_ref, acc_ref, *, blk, lambda_init):
    qi = pl.program_id(2)
    seq = v_ref.shape[0]

    @pl.when(qi == 0)
    def _():
        low = lax.broadcasted_iota(jnp.int32, (blk, LANES), 1) < HEAD_DIM

        def build(c, _):
            rows = pl.ds(pl.multiple_of(c * blk, blk), blk)
            vf = v_ref[rows, :].astype(F32)
            vp_ref[rows, :LANES] = jnp.where(low, vf, 1.0).astype(BF16)
            vp_ref[rows, LANES:] = jnp.where(low, pltpu.roll(vf, HEAD_DIM, 1), 1.0).astype(BF16)
            return 0

        lax.fori_loop(0, seq // blk, build, 0)

    q2 = q_ref[...]
    lane = lax.broadcasted_iota(jnp.int32, q2.shape, 1)
    zero = jnp.zeros_like(q2)
    n_maps = LANES // DIFF_QK_DIM
    qq = jnp.concatenate([jnp.where(lane // DIFF_QK_DIM == c, q2, zero) for c in range(n_maps)], axis=0)
    rows_n = n_maps * blk
    half = rows_n // 2

    wide = 2 * blk

    def scores(s_ref, block):
        s_ref[...] = _nt_dot(qq, k_ref[pl.ds(pl.multiple_of(block * wide, wide), wide), :])

    def update(s_ref, block, width, m, q_shift):
        keys = pl.ds(pl.multiple_of(block * wide, wide), width)
        s = s_ref[:, :width]
        if q_shift is not None:
            q_off = lax.broadcasted_iota(jnp.int32, (rows_n, width), 0) % blk
            k_off = lax.broadcasted_iota(jnp.int32, (rows_n, width), 1)
            s = jnp.where(k_off <= q_off + q_shift, s, NEG_BIG)
        m_new = jnp.maximum(m, jnp.max(s, axis=1, keepdims=True))
        p = jnp.exp2(s - m_new).astype(BF16)
        alpha = jnp.exp2(m - m_new)
        for h, v_lanes in enumerate((slice(0, LANES), slice(LANES, 2 * LANES))):
            rows = slice(h * half, (h + 1) * half)
            acc_ref[rows, :] = alpha[rows] * acc_ref[rows, :] + jnp.dot(
                p[rows], vp_ref[keys, v_lanes], preferred_element_type=F32)
        return m_new

    def finish():
        acc = acc_ref[...]
        dl = dl_ref[...]
        lam = (jnp.exp(jnp.sum(dl[0:1] * dl[1:2], axis=1, keepdims=True))
               - jnp.exp(jnp.sum(dl[2:3] * dl[3:4], axis=1, keepdims=True)) + lambda_init)
        low = lax.broadcasted_iota(jnp.int32, (blk, LANES), 1) < HEAD_DIM
        ones = jnp.ones((LANES, LANES), BF16)
        normed = []
        for h in range(2):
            n0 = acc[2 * h * blk:(2 * h + 1) * blk]
            n1 = acc[(2 * h + 1) * blk:(2 * h + 2) * blk]
            l0 = pltpu.roll(n0, HEAD_DIM, 1)
            l1 = pltpu.roll(n1, HEAD_DIM, 1)
            d = n0 - (lam * l0 / l1) * n1
            dsq = jnp.where(low, d * d, 0.0)
            hi = dsq.astype(BF16)
            lo = (dsq - hi.astype(F32)).astype(BF16)
            ms = (jnp.dot(hi, ones, preferred_element_type=F32)
                  + jnp.dot(lo, ones, preferred_element_type=F32)) * (1.0 / HEAD_DIM)
            normed.append(d * lax.rsqrt(ms + LN_EPS * l0 * l0))
        y = jnp.where(low, normed[0], pltpu.roll(normed[1], HEAD_DIM, 1))
        o_ref[...] = (y * subln_ref[...] * (1.0 - lambda_init)).astype(o_ref.dtype)

    n_wide = qi // 2
    acc_ref[...] = jnp.zeros(acc_ref.shape, F32)
    scores(sa_ref, 0)

    def body(i, m):
        scores(sb_ref, 2 * i + 1)
        m = update(sa_ref, 2 * i, wide, m, None)
        scores(sa_ref, 2 * i + 2)
        return update(sb_ref, 2 * i + 1, wide, m, None)

    m = lax.fori_loop(0, n_wide // 2, body, jnp.full((rows_n, 1), NEG_BIG, F32))

    def last(s_ref, m):
        if_even = functools.partial(update, s_ref, n_wide, blk, m, 0)
        if_odd = functools.partial(update, s_ref, n_wide, wide, m, blk)
        for parity, final_update in ((0, if_even), (1, if_odd)):
            @pl.when(qi % 2 == parity)
            def _():
                final_update()
                finish()

    @pl.when(n_wide % 2 == 0)
    def _():
        last(sa_ref, m)

    @pl.when(n_wide % 2 == 1)
    def _():
        scores(sb_ref, n_wide)
        last(sb_ref, update(sa_ref, n_wide - 1, wide, m, None))


def _diff_attention(q, k, v, diff_lambda, subln_lanes, blk, lambda_init):
    b, s, wq = q.shape
    assert s % (2 * blk) == 0
    rows_n = LANES // DIFF_QK_DIM * blk
    q_spec = pl.BlockSpec((None, blk, LANES), lambda bi, hp, qi: (bi, qi, hp))
    kv_spec = pl.BlockSpec((None, s, LANES), lambda bi, hp, qi: (bi, 0, hp))
    return pl.pallas_call(
        functools.partial(_diff_kernel, blk=blk, lambda_init=lambda_init),
        grid=(b, wq // LANES, s // blk),
        in_specs=[pl.BlockSpec(diff_lambda.shape, lambda bi, hp, qi: (0, 0)),
                  pl.BlockSpec((1, LANES), lambda bi, hp, qi: (0, 0)), q_spec, kv_spec, kv_spec],
        out_specs=q_spec,
        out_shape=jax.ShapeDtypeStruct(q.shape, BF16),
        scratch_shapes=[pltpu.VMEM((s, 2 * LANES), BF16),
                        pltpu.VMEM((rows_n, 2 * blk), F32),
                        pltpu.VMEM((rows_n, 2 * blk), F32),
                        pltpu.VMEM((rows_n, LANES), F32)],
        compiler_params=pltpu.CompilerParams(
            dimension_semantics=("parallel", "parallel", "arbitrary"),
            vmem_limit_bytes=VMEM_LIMIT_BYTES),
        name="differential",
    )(diff_lambda, subln_lanes, q, k, v)


def _swa_kernel(sinks_ref, q_ref, k_ref, v_ref, o_ref, *, blk, pairs_per_kv_block):
    n_sub = q_ref.shape[0] // blk

    def lanes(hb):
        return slice(hb * LANES, (hb + 1) * LANES)

    shape = (2 * blk, WINDOW + blk)
    rows = lax.broadcasted_iota(jnp.int32, shape, 0)
    q_off = jnp.where(rows >= blk, rows - blk, rows)
    col = lax.broadcasted_iota(jnp.int32, shape, 1)
    rel = (col - WINDOW) - q_off
    in_window = (rel <= 0) & (rel > -WINDOW)
    second_head = lax.broadcasted_iota(jnp.int32, (2 * blk, 1), 0) >= blk
    n_blocks = q_ref.shape[1] // LANES
    chains = []
    for sub in range(n_sub):
        r0 = pl.multiple_of((pl.program_id(1) * n_sub + sub) * blk, blk)
        p0 = pl.multiple_of(jnp.maximum(r0 - WINDOW, 0), WINDOW)
        band = in_window & (col + (r0 - WINDOW) >= 0)
        for hb in range(n_blocks):
            kvb = hb // pairs_per_kv_block
            kk = jnp.concatenate([k_ref[pl.ds(p0, WINDOW), lanes(kvb)], k_ref[pl.ds(r0, blk), lanes(kvb)]], axis=0)
            vv = jnp.concatenate([v_ref[pl.ds(p0, WINDOW), lanes(kvb)], v_ref[pl.ds(r0, blk), lanes(kvb)]], axis=0)
            qq = _head_pair_rows(q_ref[sub * blk:(sub + 1) * blk, lanes(hb)])
            sink = jnp.where(second_head, sinks_ref[2 * hb + 1], sinks_ref[2 * hb]) * LOG2_E
            chains.append((qq, kk, vv, band, sink))
    scores = [jnp.where(band, _nt_dot(qq, kk), NEG_BIG) for qq, kk, _, band, _ in chains]
    probs = []
    for s, (_, _, _, _, sink) in zip(scores, chains):
        m = jnp.maximum(jnp.max(s, axis=1, keepdims=True), sink)
        p = jnp.exp2(s - m)
        probs.append((p.astype(BF16), jnp.sum(p, axis=1, keepdims=True) + jnp.exp2(sink - m)))
    outs = [_merge_head_pair(jnp.dot(p, vv, preferred_element_type=F32) / denom, blk)
            for (p, denom), (_, _, vv, _, _) in zip(probs, chains)]
    for sub in range(n_sub):
        o_ref[sub * blk:(sub + 1) * blk, :] = jnp.concatenate(
            outs[sub * n_blocks:(sub + 1) * n_blocks], axis=1).astype(o_ref.dtype)


def _out_kernel(*refs, n_attn, alpha):
    x_ref = refs[0]
    attn_refs = refs[1:1 + n_attn]
    mq_ref, gate_ref, mk_ref, mv_ref, w_ref, g_ref, b_ref, o_ref = refs[1 + n_attn:]
    tm = x_ref.shape[0]

    parts = [r[...].astype(F32) for r in attn_refs]
    for hb in range(mq_ref.shape[1] // LANES):
        qq = _head_pair_rows(mq_ref[:, hb * LANES:(hb + 1) * LANES])
        s = _nt_dot(qq, mk_ref[:, hb * LANES:(hb + 1) * LANES])
        p = jnp.exp(s - jnp.max(s, axis=1, keepdims=True))
        l = jnp.sum(p, axis=1, keepdims=True)
        o = jnp.dot(p.astype(BF16), mv_ref[:, hb * LANES:(hb + 1) * LANES],
                    preferred_element_type=F32) / l
        parts.append(_merge_head_pair(o, tm))
    mixed = jnp.concatenate(parts, axis=1)
    gate = gate_ref[...].astype(F32)
    gated = mixed * (gate / (1.0 + jnp.exp(-gate)))
    y = jnp.dot(gated.astype(BF16), w_ref[...], preferred_element_type=F32)
    r = alpha * x_ref[...] + y
    mu = jnp.mean(r, axis=1, keepdims=True)
    c = r - mu
    var = jnp.mean(c * c, axis=1, keepdims=True)
    o_ref[...] = c * lax.rsqrt(var + LN_EPS) * g_ref[...] + b_ref[...]


def _out_layer(x2, attn_outs, mq, gate, mk, mv, w_out, ln_g, ln_b, seq, tm, alpha, name):
    m, d = x2.shape
    mem_len = mk.shape[0] // (m // seq)
    steps_per_batch = seq // tm

    def row(wd):
        return pl.BlockSpec((tm, wd), lambda i: (i, 0))

    def whole(a):
        return pl.BlockSpec(a.shape, lambda i: (0, 0))

    mem_spec = pl.BlockSpec((mem_len, mk.shape[1]), lambda i: (i // steps_per_batch, 0))
    in_specs = ([row(d)] + [row(a.shape[1]) for a in attn_outs]
                + [row(mq.shape[1]), row(gate.shape[1]), mem_spec, mem_spec,
                   whole(w_out), whole(ln_g), whole(ln_b)])
    return pl.pallas_call(
        functools.partial(_out_kernel, n_attn=len(attn_outs), alpha=alpha),
        grid=(m // tm,),
        in_specs=in_specs,
        out_specs=row(d),
        out_shape=jax.ShapeDtypeStruct((m, d), F32),
        compiler_params=pltpu.CompilerParams(dimension_semantics=("parallel",),
                                             vmem_limit_bytes=VMEM_LIMIT_BYTES),
        name=name,
    )(x2, *attn_outs, mq, gate, mk, mv, w_out, ln_g, ln_b)


ROW_TILE = 512
SB_BLOCK = 128
SB_ROWS_PER_STEP = 256
DIFF_BLOCK = 512
SWA_BLOCK = 128
SWA_ROWS_PER_STEP = 256


def _even_layer(x2, mem2, pos2, w_in, w_memkv, diff_lambda, diff_subln, w_out, ln_g, ln_b,
                layer_idx, batch, seq, alpha):
    lambda_init = 0.8 - 0.6 * math.exp(-0.3 * layer_idx)
    w_sb = SB_HEADS * HEAD_DIM
    w_df = DIFF_HEADS * HEAD_DIM
    sbq, sbk, sbv, dfq, dfk, dfv, mq, gate = _inproj(
        _inproj_even_kernel, "inproj_even", x2, pos2, _inv_freq_lanes(DIFF_QK_DIM), w_in.astype(BF16),
        (w_sb, w_sb, w_sb, w_df, w_df, w_df, MEM_HEADS * HEAD_DIM, w_out.shape[0]), ROW_TILE)
    mk, mv = _memkv(mem2, w_memkv.astype(BF16))

    def b3(a):
        return a.reshape(batch, seq, a.shape[1])

    sb_o = _row_block_attention(functools.partial(_sb_kernel, blk=SB_BLOCK), "stick_breaking",
                                b3(sbq), b3(sbk), b3(sbv), (), (), SB_ROWS_PER_STEP)
    subln_lanes = jnp.tile(diff_subln.astype(F32), LANES // HEAD_DIM)[None, :]
    df_o = _diff_attention(b3(dfq), b3(dfk), b3(dfv), diff_lambda.astype(F32), subln_lanes,
                           DIFF_BLOCK, lambda_init)
    return _out_layer(x2, [sb_o.reshape(x2.shape[0], -1), df_o.reshape(x2.shape[0], -1)], mq, gate, mk, mv,
                      w_out.astype(BF16), ln_g[None, :], ln_b[None, :], seq, ROW_TILE, alpha, "out_even")


def _odd_layer(x2, mem2, pos2, w_in, w_memkv, sinks, w_out, ln_g, ln_b, batch, seq, alpha):
    w_q = SWA_Q_HEADS * HEAD_DIM
    w_kv = SWA_KV_HEADS * HEAD_DIM
    w_m = MEM_HEADS * HEAD_DIM
    order = np.asarray(SW_HEAD_ORDER)
    head_cols = (order[:, None] * HEAD_DIM + np.arange(HEAD_DIM)[None, :]).reshape(-1)
    gate_lo = w_q + 2 * w_kv + w_m
    last_kv = slice((SWA_KV_HEADS - 1) * HEAD_DIM, w_kv)
    k_w = w_in[:, w_q:w_q + w_kv]
    v_w = w_in[:, w_q + w_kv:w_q + 2 * w_kv]
    w_perm = jnp.concatenate([
        w_in[:, :w_q][:, head_cols], k_w, k_w[:, last_kv], v_w, v_w[:, last_kv],
        w_in[:, w_q + 2 * w_kv:gate_lo], w_in[:, gate_lo:gate_lo + w_q][:, head_cols],
        w_in[:, gate_lo + w_q:]], axis=1).astype(BF16)
    w_out_perm = jnp.concatenate([w_out[:w_q][head_cols], w_out[w_q:]], axis=0).astype(BF16)

    cq, ck, cv, mq, gate = _inproj(
        _inproj_odd_kernel, "inproj_odd", x2, pos2, _inv_freq_lanes(HEAD_DIM), w_perm,
        (w_q, 2 * LANES, 2 * LANES, w_m, w_out.shape[0]), ROW_TILE)
    mk, mv = _memkv(mem2, w_memkv.astype(BF16))

    def b3(a):
        return a.reshape(batch, seq, a.shape[1])

    pairs_per_kv_block = LANES // HEAD_DIM * SWA_GROUP // 2
    sink_spec = [pl.BlockSpec(memory_space=pltpu.SMEM)]
    c_o = _row_block_attention(
        functools.partial(_swa_kernel, blk=SWA_BLOCK, pairs_per_kv_block=pairs_per_kv_block),
        "sliding_window", b3(cq), b3(ck), b3(cv), (sinks.astype(F32)[order],), sink_spec,
        SWA_ROWS_PER_STEP)
    return _out_layer(x2, [c_o.reshape(x2.shape[0], -1)], mq, gate, mk, mv, w_out_perm,
                      ln_g[None, :], ln_b[None, :], seq, ROW_TILE, alpha, "out_odd")


def kernel(x, mem, positions, w_in_even, w_memkv_even, diff_lambda_even, diff_subln_even, w_out_even,
           ln_g_even, ln_b_even, w_in_odd, w_memkv_odd, sinks_odd, w_out_odd, ln_g_odd, ln_b_odd):
    batch, seq, d = x.shape
    depth = w_in_even.shape[0] + w_in_odd.shape[0]
    alpha = (2 * depth) ** 0.25
    x2 = x.reshape(batch * seq, d)
    mem2 = mem.reshape(batch * mem.shape[1], d)
    pos2 = positions.reshape(batch * seq, 1)
    for i in range(depth):
        j = i // 2
        if i % 2 == 0:
            x2 = _even_layer(x2, mem2, pos2, w_in_even[j], w_memkv_even[j], diff_lambda_even[j],
                             diff_subln_even[j], w_out_even[j], ln_g_even[j], ln_b_even[j], i,
                             batch, seq, alpha)
        else:
            x2 = _odd_layer(x2, mem2, pos2, w_in_odd[j], w_memkv_odd[j], sinks_odd[j], w_out_odd[j],
                            ln_g_odd[j], ln_b_odd[j], batch, seq, alpha)
    return x2.reshape(batch, seq, d)
```

```python
import functools
import math

import jax
import jax.numpy as jnp
import numpy as np
from jax import lax
from jax.experimental import pallas as pl
from jax.experimental.pallas import tpu as pltpu

F32 = jnp.float32
BF16 = jnp.bfloat16

HEAD_DIM = 64
LANES = 128
MEM_HEADS = 4
SB_HEADS = 6
DIFF_HEADS = 6
DIFF_QK_DIM = 32
SWA_Q_HEADS = 12
SWA_KV_HEADS = 3
SWA_GROUP = 4
WINDOW = 128
ROPE_THETA = 500000.0
ROPE_FRACTION = 4
LN_EPS = 1e-5
NEG_BIG = -1e30
VMEM_LIMIT_BYTES = 48 * 1024 * 1024

SB_EXIT_LOG2_WEIGHT = -92.0
LOG2_E = math.log2(math.e)

SW_HEAD_ORDER = (0, 4, 1, 5, 2, 6, 3, 7, 8, 9, 10, 11)


def _nt_dot(a, b):
    return lax.dot_general(a, b, (((1,), (1,)), ((), ())), preferred_element_type=F32)


def _head_pair_rows(q2):
    lane = lax.broadcasted_iota(jnp.int32, q2.shape, 1)
    low = lane < HEAD_DIM
    zero = jnp.zeros_like(q2)
    return jnp.concatenate([jnp.where(low, q2, zero), jnp.where(low, zero, q2)], axis=0)


def _merge_head_pair(o, rows):
    lane = lax.broadcasted_iota(jnp.int32, (rows, LANES), 1)
    return jnp.where(lane < HEAD_DIM, o[:rows], o[rows:])


def _rope_lane_tables(group, pos_f32, inv_freq_row):
    rot = group // ROPE_FRACTION
    half = rot // 2
    lane = lax.broadcasted_iota(jnp.int32, (1, LANES), 1)
    r = lane % group
    ang = pos_f32 * inv_freq_row
    cos = jnp.cos(ang)
    sin = jnp.sin(ang)
    cos_t = jnp.where(r < rot, cos, 1.0)
    sin_first = jnp.where(r < half, -sin, 0.0)
    sin_second = jnp.where((r >= half) & (r < rot), sin, 0.0)
    return cos_t, sin_first, sin_second, half


def _apply_rope(h, tables):
    cos_t, sin_first, sin_second, half = tables
    outs = []
    for c in range(h.shape[1] // LANES):
        blk = h[:, c * LANES:(c + 1) * LANES]
        outs.append(blk * cos_t + pltpu.roll(blk, LANES - half, 1) * sin_first
                    + pltpu.roll(blk, half, 1) * sin_second)
    return jnp.concatenate(outs, axis=1)


def _inv_freq_lanes(group):
    half = group // ROPE_FRACTION // 2
    inv_freq = jnp.exp(-(jnp.arange(half, dtype=F32) / half) * math.log(ROPE_THETA))
    lane = np.arange(LANES)
    return inv_freq[(lane % group) % half][None, :]


def _inproj_even_kernel(x_ref, pos_ref, invf_ref, w_ref, sbq_ref, sbk_ref, sbv_ref,
                        dfq_ref, dfk_ref, dfv_ref, mq_ref, gate_ref):
    h = jnp.dot(x_ref[...].astype(BF16), w_ref[...], preferred_element_type=F32)
    edges = [0]

    def sec(width):
        edges.append(edges[-1] + width)
        return h[:, edges[-2]:edges[-1]]

    w_sb = SB_HEADS * HEAD_DIM
    w_df = DIFF_HEADS * 2 * DIFF_QK_DIM
    tables = _rope_lane_tables(DIFF_QK_DIM, pos_ref[...].astype(F32), invf_ref[...])
    sbq_ref[...] = (sec(w_sb) * (HEAD_DIM ** -0.5 * LOG2_E)).astype(BF16)
    sbk_ref[...] = sec(w_sb).astype(BF16)
    sbv_ref[...] = sec(w_sb).astype(BF16)
    q_scale = DIFF_QK_DIM ** -0.5 * math.log2(math.e)
    dfq_ref[...] = (_apply_rope(sec(w_df), tables) * q_scale).astype(BF16)
    dfk_ref[...] = _apply_rope(sec(w_df), tables).astype(BF16)
    dfv_ref[...] = sec(w_df).astype(BF16)
    mq_ref[...] = (sec(MEM_HEADS * HEAD_DIM) * (HEAD_DIM ** -0.5 * LOG2_E)).astype(BF16)
    gate_ref[...] = sec(gate_ref.shape[1]).astype(BF16)


def _inproj_odd_kernel(x_ref, pos_ref, invf_ref, w_ref, cq_ref, ck_ref, cv_ref, mq_ref, gate_ref):
    h = jnp.dot(x_ref[...].astype(BF16), w_ref[...], preferred_element_type=F32)
    edges = [0]

    def sec(width):
        edges.append(edges[-1] + width)
        return h[:, edges[-2]:edges[-1]]

    w_kv = ck_ref.shape[1]
    tables = _rope_lane_tables(HEAD_DIM, pos_ref[...].astype(F32), invf_ref[...])
    cq_ref[...] = (_apply_rope(sec(SWA_Q_HEADS * HEAD_DIM), tables)
                   * (HEAD_DIM ** -0.5 * LOG2_E)).astype(BF16)
    ck_ref[...] = _apply_rope(sec(w_kv), tables).astype(BF16)
    cv_ref[...] = sec(w_kv).astype(BF16)
    mq_ref[...] = (sec(MEM_HEADS * HEAD_DIM) * (HEAD_DIM ** -0.5 * LOG2_E)).astype(BF16)
    gate_ref[...] = sec(gate_ref.shape[1]).astype(BF16)


def _inproj(kernel_fn, name, x2, pos2, invf, w, out_widths, tm):
    m, d = x2.shape
    return pl.pallas_call(
        kernel_fn,
        grid=(m // tm,),
        in_specs=[pl.BlockSpec((tm, d), lambda i: (i, 0)),
                  pl.BlockSpec((tm, 1), lambda i: (i, 0)),
                  pl.BlockSpec((1, LANES), lambda i: (0, 0)),
                  pl.BlockSpec(w.shape, lambda i: (0, 0))],
        out_specs=[pl.BlockSpec((tm, wd), lambda i: (i, 0)) for wd in out_widths],
        out_shape=[jax.ShapeDtypeStruct((m, wd), BF16) for wd in out_widths],
        compiler_params=pltpu.CompilerParams(dimension_semantics=("parallel",),
                                             vmem_limit_bytes=VMEM_LIMIT_BYTES),
        name=name,
    )(x2, pos2, invf, w)


def _memkv_kernel(mem_ref, w_ref, mk_ref, mv_ref):
    kv = jnp.dot(mem_ref[...].astype(BF16), w_ref[...], preferred_element_type=F32)
    half = mk_ref.shape[1]
    mk_ref[...] = kv[:, :half].astype(BF16)
    mv_ref[...] = kv[:, half:].astype(BF16)


def _memkv(mem2, w):
    rows = mem2.shape[0]
    half = w.shape[1] // 2
    return pl.pallas_call(
        _memkv_kernel,
        out_shape=[jax.ShapeDtypeStruct((rows, half), BF16)] * 2,
        compiler_params=pltpu.CompilerParams(vmem_limit_bytes=VMEM_LIMIT_BYTES),
        name="memkv",
    )(mem2, w)


def _strict_lower(n):
    return (lax.broadcasted_iota(jnp.int32, (n, n), 0)
            > lax.broadcasted_iota(jnp.int32, (n, n), 1)).astype(BF16)


def _sb_tiles(chains, later):
    zs = [_nt_dot(qq, kk) for qq, kk, _, _, _ in chains]
    logs = []
    for z, (_, _, _, strict, _) in zip(zs, chains):
        soft = jnp.log(1.0 + jnp.exp2(-jnp.abs(z))) * LOG2_E
        log_1mb = -jnp.maximum(z, 0.0) - soft
        if strict is not None:
            log_1mb = jnp.where(strict, log_1mb, 0.0)
        hi = log_1mb.astype(BF16)
        lo = (log_1mb - hi.astype(F32)).astype(BF16)
        logs.append((jnp.minimum(z, 0.0) - soft, log_1mb, hi, lo))
    afters = [jnp.dot(hi, later, preferred_element_type=F32) + jnp.dot(lo, later, preferred_element_type=F32)
              for _, _, hi, lo in logs]
    outs = []
    for (log_b, log_1mb, _, _), after, (_, _, vv, strict, carry) in zip(logs, afters, chains):
        w = jnp.exp2(log_b + after + carry)
        if strict is not None:
            w = jnp.where(strict, w, 0.0)
        outs.append((jnp.dot(w.astype(BF16), vv, preferred_element_type=F32),
                     jnp.sum(log_1mb, axis=1, keepdims=True)))
    return outs


def _sb_kernel(q_ref, k_ref, v_ref, o_ref, *, blk):
    n_pairs = q_ref.shape[1] // LANES
    n_sub = q_ref.shape[0] // blk

    def lanes(hp):
        return slice(hp * LANES, (hp + 1) * LANES)

    rows = lax.broadcasted_iota(jnp.int32, (2 * blk, 2 * blk), 0)
    q_off = jnp.where(rows >= blk, rows - blk, rows)
    col = lax.broadcasted_iota(jnp.int32, (2 * blk, 2 * blk), 1)
    later2 = _strict_lower(2 * blk)
    later1 = _strict_lower(blk)
    no_carry = jnp.zeros((2 * blk, 1), F32)

    first = []
    for sub in range(n_sub):
        r0 = pl.multiple_of((pl.program_id(1) * n_sub + sub) * blk, blk)
        p0 = pl.multiple_of(jnp.maximum(r0 - blk, 0), blk)
        strict = (col - blk < q_off) & (col + (r0 - blk) >= 0)
        for hp in range(n_pairs):
            qq = _head_pair_rows(q_ref[sub * blk:(sub + 1) * blk, lanes(hp)])
            kk = jnp.concatenate([k_ref[pl.ds(p0, blk), lanes(hp)], k_ref[pl.ds(r0, blk), lanes(hp)]], axis=0)
            vv = jnp.concatenate([v_ref[pl.ds(p0, blk), lanes(hp)], v_ref[pl.ds(r0, blk), lanes(hp)]], axis=0)
            first.append((qq, kk, vv, strict, no_carry))
    started = _sb_tiles(first, later2)

    def finish(sub):
        bi = pl.program_id(1) * n_sub + sub
        mine = slice(sub * n_pairs, (sub + 1) * n_pairs)
        qqs = [chain[0] for chain in first[mine]]
        accs = tuple(pv for pv, _ in started[mine])
        carries = tuple(dsum for _, dsum in started[mine])

        def cond(c):
            j, carries, _ = c
            return (j >= 0) & (jnp.max(functools.reduce(jnp.maximum, carries)) > SB_EXIT_LOG2_WEIGHT)

        def body(c):
            j, carries, accs = c
            start = pl.multiple_of(j * blk, blk)
            outs = _sb_tiles([(qqs[hp], k_ref[pl.ds(start, blk), lanes(hp)], v_ref[pl.ds(start, blk), lanes(hp)],
                               None, carries[hp]) for hp in range(n_pairs)], later1)
            return (j - 1, tuple(c + dsum for c, (_, dsum) in zip(carries, outs)),
                    tuple(a + pv for a, (pv, _) in zip(accs, outs)))

        _, _, accs = lax.while_loop(cond, body, (bi - 2, carries, accs))
        o_ref[sub * blk:(sub + 1) * blk, :] = jnp.concatenate(
            [_merge_head_pair(a, blk) for a in accs], axis=1).astype(o_ref.dtype)

    for sub in range(n_sub):
        finish(sub)


def _row_block_attention(kernel_fn, name, q, k, v, extra_inputs, extra_specs, blk):
    b, s, wq = q.shape
    q_spec = pl.BlockSpec((None, blk, wq), lambda bi, qi: (bi, qi, 0))
    kv_spec = pl.BlockSpec((None, s, k.shape[2]), lambda bi, qi: (bi, 0, 0))
    return pl.pallas_call(
        kernel_fn,
        grid=(b, s // blk),
        in_specs=list(extra_specs) + [q_spec, kv_spec, kv_spec],
        out_specs=q_spec,
        out_shape=jax.ShapeDtypeStruct(q.shape, BF16),
        compiler_params=pltpu.CompilerParams(
            dimension_semantics=("parallel", "arbitrary"),
            vmem_limit_bytes=VMEM_LIMIT_BYTES),
        name=name,
    )(*extra_inputs, q, k, v)


def _diff_kernel(dl_ref, subln_ref, q_ref, k_ref, v_ref, o_ref, vp_ref, sa_ref, sb_ref, acc_ref, *, blk, lambda_init):
    qi = pl.program_id(2)
    seq = v_ref.shape[0]

    @pl.when(qi == 0)
    def _():
        low = lax.broadcasted_iota(jnp.int32, (blk, LANES), 1) < HEAD_DIM

        def build(c, _):
            rows = pl.ds(pl.multiple_of(c * blk, blk), blk)
            vf = v_ref[rows, :].astype(F32)
            vp_ref[rows, :LANES] = jnp.where(low, vf, 1.0).astype(BF16)
            vp_ref[rows, LANES:] = jnp.where(low, pltpu.roll(vf, HEAD_DIM, 1), 1.0).astype(BF16)
            return 0

        lax.fori_loop(0, seq // blk, build, 0)

    q2 = q_ref[...]
    lane = lax.broadcasted_iota(jnp.int32, q2.shape, 1)
    zero = jnp.zeros_like(q2)
    n_maps = LANES // DIFF_QK_DIM
    qq = jnp.concatenate([jnp.where(lane // DIFF_QK_DIM == c, q2, zero) for c in range(n_maps)], axis=0)
    rows_n = n_maps * blk
    half = rows_n // 2

    wide = 2 * blk

    def scores(s_ref, block):
        s_ref[...] = _nt_dot(qq, k_ref[pl.ds(pl.multiple_of(block * wide, wide), wide), :])

    def update(s_ref, block, width, m, q_shift):
        keys = pl.ds(pl.multiple_of(block * wide, wide), width)
        s = s_ref[:, :width]
        if q_shift is not None:
            q_off = lax.broadcasted_iota(jnp.int32, (rows_n, width), 0) % blk
            k_off = lax.broadcasted_iota(jnp.int32, (rows_n, width), 1)
            s = jnp.where(k_off <= q_off + q_shift, s, NEG_BIG)
        m_new = jnp.maximum(m, jnp.max(s, axis=1, keepdims=True))
        p = jnp.exp2(s - m_new).astype(BF16)
        alpha = jnp.exp2(m - m_new)
        for h, v_lanes in enumerate((slice(0, LANES), slice(LANES, 2 * LANES))):
            rows = slice(h * half, (h + 1) * half)
            acc_ref[rows, :] = alpha[rows] * acc_ref[rows, :] + jnp.dot(
                p[rows], vp_ref[keys, v_lanes], preferred_element_type=F32)
        return m_new

    def finish():
        acc = acc_ref[...]
        dl = dl_ref[...]
        lam = (jnp.exp(jnp.sum(dl[0:1] * dl[1:2], axis=1, keepdims=True))
               - jnp.exp(jnp.sum(dl[2:3] * dl[3:4], axis=1, keepdims=True)) + lambda_init)
        low = lax.broadcasted_iota(jnp.int32, (blk, LANES), 1) < HEAD_DIM
        ones = jnp.ones((LANES, LANES), BF16)
        normed = []
        for h in range(2):
            n0 = acc[2 * h * blk:(2 * h + 1) * blk]
            n1 = acc[(2 * h + 1) * blk:(2 * h + 2) * blk]
            l0 = pltpu.roll(n0, HEAD_DIM, 1)
            l1 = pltpu.roll(n1, HEAD_DIM, 1)
            d = n0 - (lam * l0 / l1) * n1
            dsq = jnp.where(low, d * d, 0.0)
            hi = dsq.astype(BF16)
            lo = (dsq - hi.astype(F32)).astype(BF16)
            ms = (jnp.dot(hi, ones, preferred_element_type=F32)
                  + jnp.dot(lo, ones, preferred_element_type=F32)) * (1.0 / HEAD_DIM)
            normed.append(d * lax.rsqrt(ms + LN_EPS * l0 * l0))
        y = jnp.where(low, normed[0], pltpu.roll(normed[1], HEAD_DIM, 1))
        o_ref[...] = (y * subln_ref[...] * (1.0 - lambda_init)).astype(o_ref.dtype)

    n_wide = qi // 2
    acc_ref[...] = jnp.zeros(acc_ref.shape, F32)
    scores(sa_ref, 0)

    def body(i, m):
        scores(sb_ref, 2 * i + 1)
        m = update(sa_ref, 2 * i, wide, m, None)
        scores(sa_ref, 2 * i + 2)
        return update(sb_ref, 2 * i + 1, wide, m, None)

    m = lax.fori_loop(0, n_wide // 2, body, jnp.full((rows_n, 1), NEG_BIG, F32))

    def last(s_ref, m):
        if_even = functools.partial(update, s_ref, n_wide, blk, m, 0)
        if_odd = functools.partial(update, s_ref, n_wide, wide, m, blk)
        for parity, final_update in ((0, if_even), (1, if_odd)):
            @pl.when(qi % 2 == parity)
            def _():
                final_update()
                finish()

    @pl.when(n_wide % 2 == 0)
    def _():
        last(sa_ref, m)

    @pl.when(n_wide % 2 == 1)
    def _():
        scores(sb_ref, n_wide)
        last(sb_ref, update(sa_ref, n_wide - 1, wide, m, None))


def _diff_attention(q, k, v, diff_lambda, subln_lanes, blk, lambda_init):
    b, s, wq = q.shape
    assert s % (2 * blk) == 0
    rows_n = LANES // DIFF_QK_DIM * blk
    q_spec = pl.BlockSpec((None, blk, LANES), lambda bi, hp, qi: (bi, qi, hp))
    kv_spec = pl.BlockSpec((None, s, LANES), lambda bi, hp, qi: (bi, 0, hp))
    return pl.pallas_call(
        functools.partial(_diff_kernel, blk=blk, lambda_init=lambda_init),
        grid=(b, wq // LANES, s // blk),
        in_specs=[pl.BlockSpec(diff_lambda.shape, lambda bi, hp, qi: (0, 0)),
                  pl.BlockSpec((1, LANES), lambda bi, hp, qi: (0, 0)), q_spec, kv_spec, kv_spec],
        out_specs=q_spec,
        out_shape=jax.ShapeDtypeStruct(q.shape, BF16),
        scratch_shapes=[pltpu.VMEM((s, 2 * LANES), BF16),
                        pltpu.VMEM((rows_n, 2 * blk), F32),
                        pltpu.VMEM((rows_n, 2 * blk), F32),
                        pltpu.VMEM((rows_n, LANES), F32)],
        compiler_params=pltpu.CompilerParams(
            dimension_semantics=("parallel", "parallel", "arbitrary"),
            vmem_limit_bytes=VMEM_LIMIT_BYTES),
        name="differential",
    )(diff_lambda, subln_lanes, q, k, v)


def _swa_kernel(sinks_ref, q_ref, k_ref, v_ref, o_ref, *, blk, pairs_per_kv_block):
    n_sub = q_ref.shape[0] // blk

    def lanes(hb):
        return slice(hb * LANES, (hb + 1) * LANES)

    shape = (2 * blk, WINDOW + blk)
    rows = lax.broadcasted_iota(jnp.int32, shape, 0)
    q_off = jnp.where(rows >= blk, rows - blk, rows)
    col = lax.broadcasted_iota(jnp.int32, shape, 1)
    rel = (col - WINDOW) - q_off
    in_window = (rel <= 0) & (rel > -WINDOW)
    second_head = lax.broadcasted_iota(jnp.int32, (2 * blk, 1), 0) >= blk
    n_blocks = q_ref.shape[1] // LANES
    chains = []
    for sub in range(n_sub):
        r0 = pl.multiple_of((pl.program_id(1) * n_sub + sub) * blk, blk)
        p0 = pl.multiple_of(jnp.maximum(r0 - WINDOW, 0), WINDOW)
        band = in_window & (col + (r0 - WINDOW) >= 0)
        for hb in range(n_blocks):
            kvb = hb // pairs_per_kv_block
            kk = jnp.concatenate([k_ref[pl.ds(p0, WINDOW), lanes(kvb)], k_ref[pl.ds(r0, blk), lanes(kvb)]], axis=0)
            vv = jnp.concatenate([v_ref[pl.ds(p0, WINDOW), lanes(kvb)], v_ref[pl.ds(r0, blk), lanes(kvb)]], axis=0)
            qq = _head_pair_rows(q_ref[sub * blk:(sub + 1) * blk, lanes(hb)])
            sink = jnp.where(second_head, sinks_ref[2 * hb + 1], sinks_ref[2 * hb]) * LOG2_E
            chains.append((qq, kk, vv, band, sink))
    scores = [jnp.where(band, _nt_dot(qq, kk), NEG_BIG) for qq, kk, _, band, _ in chains]
    probs = []
    for s, (_, _, _, _, sink) in zip(scores, chains):
        m = jnp.maximum(jnp.max(s, axis=1, keepdims=True), sink)
        p = jnp.exp2(s - m)
        probs.append((p.astype(BF16), jnp.sum(p, axis=1, keepdims=True) + jnp.exp2(sink - m)))
    outs = [_merge_head_pair(jnp.dot(p, vv, preferred_element_type=F32) / denom, blk)
            for (p, denom), (_, _, vv, _, _) in zip(probs, chains)]
    for sub in range(n_sub):
        o_ref[sub * blk:(sub + 1) * blk, :] = jnp.concatenate(
            outs[sub * n_blocks:(sub + 1) * n_blocks], axis=1).astype(o_ref.dtype)


def _out_kernel(*refs, n_attn, alpha):
    x_ref = refs[0]
    attn_refs = refs[1:1 + n_attn]
    mq_ref, gate_ref, mk_ref, mv_ref, w_ref, g_ref, b_ref, o_ref = refs[1 + n_attn:]
    tm = x_ref.shape[0]

    parts = [r[...].astype(F32) for r in attn_refs]
    blocks = [slice(hb * LANES, (hb + 1) * LANES) for hb in range(mq_ref.shape[1] // LANES)]
    scores = [_nt_dot(_head_pair_rows(mq_ref[:, c]), mk_ref[:, c]) for c in blocks]
    probs = [jnp.exp2(s - jnp.max(s, axis=1, keepdims=True)) for s in scores]
    for p, c in zip(probs, blocks):
        o = jnp.dot(p.astype(BF16), mv_ref[:, c], preferred_element_type=F32) / jnp.sum(p, axis=1, keepdims=True)
        parts.append(_merge_head_pair(o, tm))
    mixed = jnp.concatenate(parts, axis=1)
    gate = gate_ref[...].astype(F32)
    gated = mixed * (gate / (1.0 + jnp.exp(-gate)))
    y = jnp.dot(gated.astype(BF16), w_ref[...], preferred_element_type=F32)
    r = alpha * x_ref[...] + y
    mu = jnp.mean(r, axis=1, keepdims=True)
    c = r - mu
    var = jnp.mean(c * c, axis=1, keepdims=True)
    o_ref[...] = c * lax.rsqrt(var + LN_EPS) * g_ref[...] + b_ref[...]


def _out_layer(x2, attn_outs, mq, gate, mk, mv, w_out, ln_g, ln_b, seq, tm, alpha, name):
    m, d = x2.shape
    mem_len = mk.shape[0] // (m // seq)
    steps_per_batch = seq // tm

    def row(wd):
        return pl.BlockSpec((tm, wd), lambda i: (i, 0))

    def whole(a):
        return pl.BlockSpec(a.shape, lambda i: (0, 0))

    mem_spec = pl.BlockSpec((mem_len, mk.shape[1]), lambda i: (i // steps_per_batch, 0))
    in_specs = ([row(d)] + [row(a.shape[1]) for a in attn_outs]
                + [row(mq.shape[1]), row(gate.shape[1]), mem_spec, mem_spec,
                   whole(w_out), whole(ln_g), whole(ln_b)])
    return pl.pallas_call(
        functools.partial(_out_kernel, n_attn=len(attn_outs), alpha=alpha),
        grid=(m // tm,),
        in_specs=in_specs,
        out_specs=row(d),
        out_shape=jax.ShapeDtypeStruct((m, d), F32),
        compiler_params=pltpu.CompilerParams(dimension_semantics=("parallel",),
                                             vmem_limit_bytes=VMEM_LIMIT_BYTES),
        name=name,
    )(x2, *attn_outs, mq, gate, mk, mv, w_out, ln_g, ln_b)


ROW_TILE = 512
SB_BLOCK = 128
SB_ROWS_PER_STEP = 512
DIFF_BLOCK = 512
SWA_BLOCK = 128
SWA_ROWS_PER_STEP = 512


def _even_layer(x2, mem2, pos2, w_in, w_memkv, diff_lambda, diff_subln, w_out, ln_g, ln_b,
                layer_idx, batch, seq, alpha):
    lambda_init = 0.8 - 0.6 * math.exp(-0.3 * layer_idx)
    w_sb = SB_HEADS * HEAD_DIM
    w_df = DIFF_HEADS * HEAD_DIM
    sbq, sbk, sbv, dfq, dfk, dfv, mq, gate = _inproj(
        _inproj_even_kernel, "inproj_even", x2, pos2, _inv_freq_lanes(DIFF_QK_DIM), w_in.astype(BF16),
        (w_sb, w_sb, w_sb, w_df, w_df, w_df, MEM_HEADS * HEAD_DIM, w_out.shape[0]), ROW_TILE)
    mk, mv = _memkv(mem2, w_memkv.astype(BF16))

    def b3(a):
        return a.reshape(batch, seq, a.shape[1])

    sb_o = _row_block_attention(functools.partial(_sb_kernel, blk=SB_BLOCK), "stick_breaking",
                                b3(sbq), b3(sbk), b3(sbv), (), (), SB_ROWS_PER_STEP)
    subln_lanes = jnp.tile(diff_subln.astype(F32), LANES // HEAD_DIM)[None, :]
    df_o = _diff_attention(b3(dfq), b3(dfk), b3(dfv), diff_lambda.astype(F32), subln_lanes,
                           DIFF_BLOCK, lambda_init)
    return _out_layer(x2, [sb_o.reshape(x2.shape[0], -1), df_o.reshape(x2.shape[0], -1)], mq, gate, mk, mv,
                      w_out.astype(BF16), ln_g[None, :], ln_b[None, :], seq, ROW_TILE, alpha, "out_even")


def _odd_layer(x2, mem2, pos2, w_in, w_memkv, sinks, w_out, ln_g, ln_b, batch, seq, alpha):
    w_q = SWA_Q_HEADS * HEAD_DIM
    w_kv = SWA_KV_HEADS * HEAD_DIM
    w_m = MEM_HEADS * HEAD_DIM
    def heads_in_order(a, axis, lo):
        return [lax.slice_in_dim(a, lo + h * HEAD_DIM, lo + (h + 1) * HEAD_DIM, axis=axis) for h in SW_HEAD_ORDER]

    gate_lo = w_q + 2 * w_kv + w_m
    last_kv = slice((SWA_KV_HEADS - 1) * HEAD_DIM, w_kv)
    w_in = w_in.astype(BF16)
    w_out = w_out.astype(BF16)
    k_w = w_in[:, w_q:w_q + w_kv]
    v_w = w_in[:, w_q + w_kv:w_q + 2 * w_kv]
    w_perm = jnp.concatenate(
        heads_in_order(w_in, 1, 0) + [k_w, k_w[:, last_kv], v_w, v_w[:, last_kv], w_in[:, w_q + 2 * w_kv:gate_lo]]
        + heads_in_order(w_in, 1, gate_lo) + [w_in[:, gate_lo + w_q:]], axis=1)
    w_out_perm = jnp.concatenate(heads_in_order(w_out, 0, 0) + [w_out[w_q:]], axis=0)
    sinks_perm = jnp.stack([sinks[h] for h in SW_HEAD_ORDER]).astype(F32)

    cq, ck, cv, mq, gate = _inproj(
        _inproj_odd_kernel, "inproj_odd", x2, pos2, _inv_freq_lanes(HEAD_DIM), w_perm,
        (w_q, 2 * LANES, 2 * LANES, w_m, w_out.shape[0]), ROW_TILE)
    mk, mv = _memkv(mem2, w_memkv.astype(BF16))

    def b3(a):
        return a.reshape(batch, seq, a.shape[1])

    pairs_per_kv_block = LANES // HEAD_DIM * SWA_GROUP // 2
    sink_spec = [pl.BlockSpec(memory_space=pltpu.SMEM)]
    c_o = _row_block_attention(
        functools.partial(_swa_kernel, blk=SWA_BLOCK, pairs_per_kv_block=pairs_per_kv_block),
        "sliding_window", b3(cq), b3(ck), b3(cv), (sinks_perm,), sink_spec,
        SWA_ROWS_PER_STEP)
    return _out_layer(x2, [c_o.reshape(x2.shape[0], -1)], mq, gate, mk, mv, w_out_perm,
                      ln_g[None, :], ln_b[None, :], seq, ROW_TILE, alpha, "out_odd")


def kernel(x, mem, positions, w_in_even, w_memkv_even, diff_lambda_even, diff_subln_even, w_out_even,
           ln_g_even, ln_b_even, w_in_odd, w_memkv_odd, sinks_odd, w_out_odd, ln_g_odd, ln_b_odd):
    batch, seq, d = x.shape
    depth = w_in_even.shape[0] + w_in_odd.shape[0]
    alpha = (2 * depth) ** 0.25
    x2 = x.reshape(batch * seq, d)
    mem2 = mem.reshape(batch * mem.shape[1], d)
    pos2 = positions.reshape(batch * seq, 1)
    for i in range(depth):
        j = i // 2
        if i % 2 == 0:
            x2 = _even_layer(x2, mem2, pos2, w_in_even[j], w_memkv_even[j], diff_lambda_even[j],
                             diff_subln_even[j], w_out_even[j], ln_g_even[j], ln_b_even[j], i,
                             batch, seq, alpha)
        else:
            x2 = _odd_layer(x2, mem2, pos2, w_in_odd[j], w_memkv_odd[j], sinks_odd[j], w_out_odd[j],
                            ln_g_odd[j], ln_b_odd[j], batch, seq, alpha)
    return x2.reshape(batch, seq, d)
```

```python
import functools
import math

import jax
import jax.numpy as jnp
import numpy as np
from jax import lax
from jax.experimental import pallas as pl
from jax.experimental.pallas import tpu as pltpu

F32 = jnp.float32
BF16 = jnp.bfloat16

HEAD_DIM = 64
LANES = 128
MEM_HEADS = 4
SB_HEADS = 6
DIFF_HEADS = 6
DIFF_QK_DIM = 32
SWA_Q_HEADS = 12
SWA_KV_HEADS = 3
SWA_GROUP = 4
WINDOW = 128
ROPE_THETA = 500000.0
ROPE_FRACTION = 4
LN_EPS = 1e-5
NEG_BIG = -1e30
VMEM_LIMIT_BYTES = 48 * 1024 * 1024

SB_EXIT_LOG2_WEIGHT = -92.0
LOG2_E = math.log2(math.e)

SW_HEAD_ORDER = (0, 4, 1, 5, 2, 6, 3, 7, 8, 9, 10, 11)


def _nt_dot(a, b):
    return lax.dot_general(a, b, (((1,), (1,)), ((), ())), preferred_element_type=F32)


def _head_pair_rows(q2):
    lane = lax.broadcasted_iota(jnp.int32, q2.shape, 1)
    low = lane < HEAD_DIM
    zero = jnp.zeros_like(q2)
    return jnp.concatenate([jnp.where(low, q2, zero), jnp.where(low, zero, q2)], axis=0)


def _merge_head_pair(o, rows):
    lane = lax.broadcasted_iota(jnp.int32, (rows, LANES), 1)
    return jnp.where(lane < HEAD_DIM, o[:rows], o[rows:])


def _rope_lane_tables(group, pos_f32, inv_freq_row):
    rot = group // ROPE_FRACTION
    half = rot // 2
    lane = lax.broadcasted_iota(jnp.int32, (1, LANES), 1)
    r = lane % group
    ang = pos_f32 * inv_freq_row
    cos = jnp.cos(ang)
    sin = jnp.sin(ang)
    cos_t = jnp.where(r < rot, cos, 1.0)
    sin_first = jnp.where(r < half, -sin, 0.0)
    sin_second = jnp.where((r >= half) & (r < rot), sin, 0.0)
    return cos_t, sin_first, sin_second, half


def _apply_rope(h, tables):
    cos_t, sin_first, sin_second, half = tables
    outs = []
    for c in range(h.shape[1] // LANES):
        blk = h[:, c * LANES:(c + 1) * LANES]
        outs.append(blk * cos_t + pltpu.roll(blk, LANES - half, 1) * sin_first
                    + pltpu.roll(blk, half, 1) * sin_second)
    return jnp.concatenate(outs, axis=1)


def _inv_freq_lanes(group):
    half = group // ROPE_FRACTION // 2
    inv_freq = jnp.exp(-(jnp.arange(half, dtype=F32) / half) * math.log(ROPE_THETA))
    lane = np.arange(LANES)
    return inv_freq[(lane % group) % half][None, :]


def _inproj_even_kernel(x_ref, pos_ref, invf_ref, w_ref, sbq_ref, sbk_ref, sbv_ref,
                        dfq_ref, dfk_ref, dfv_ref, mq_ref, gate_ref):
    h = jnp.dot(x_ref[...].astype(BF16), w_ref[...], preferred_element_type=F32)
    edges = [0]

    def sec(width):
        edges.append(edges[-1] + width)
        return h[:, edges[-2]:edges[-1]]

    w_sb = SB_HEADS * HEAD_DIM
    w_df = DIFF_HEADS * 2 * DIFF_QK_DIM
    tables = _rope_lane_tables(DIFF_QK_DIM, pos_ref[...].astype(F32), invf_ref[...])
    sbq_ref[...] = (sec(w_sb) * (HEAD_DIM ** -0.5 * LOG2_E)).astype(BF16)
    sbk_ref[...] = sec(w_sb).astype(BF16)
    sbv_ref[...] = sec(w_sb).astype(BF16)
    q_scale = DIFF_QK_DIM ** -0.5 * math.log2(math.e)
    dfq_ref[...] = (_apply_rope(sec(w_df), tables) * q_scale).astype(BF16)
    dfk_ref[...] = _apply_rope(sec(w_df), tables).astype(BF16)
    dfv_ref[...] = sec(w_df).astype(BF16)
    mq_ref[...] = (sec(MEM_HEADS * HEAD_DIM) * (HEAD_DIM ** -0.5 * LOG2_E)).astype(BF16)
    gate_ref[...] = sec(gate_ref.shape[1]).astype(BF16)


def _inproj_odd_kernel(x_ref, pos_ref, invf_ref, w_ref, cq_ref, ck_ref, cv_ref, mq_ref, gate_ref):
    h = jnp.dot(x_ref[...].astype(BF16), w_ref[...], preferred_element_type=F32)
    edges = [0]

    def sec(width):
        edges.append(edges[-1] + width)
        return h[:, edges[-2]:edges[-1]]

    w_kv = ck_ref.shape[1]
    tables = _rope_lane_tables(HEAD_DIM, pos_ref[...].astype(F32), invf_ref[...])
    cq_ref[...] = (_apply_rope(sec(SWA_Q_HEADS * HEAD_DIM), tables)
                   * (HEAD_DIM ** -0.5 * LOG2_E)).astype(BF16)
    ck_ref[...] = _apply_rope(sec(w_kv), tables).astype(BF16)
    cv_ref[...] = sec(w_kv).astype(BF16)
    mq_ref[...] = (sec(MEM_HEADS * HEAD_DIM) * (HEAD_DIM ** -0.5 * LOG2_E)).astype(BF16)
    gate_ref[...] = sec(gate_ref.shape[1]).astype(BF16)


def _inproj(kernel_fn, name, x2, pos2, invf, w, out_widths, tm):
    m, d = x2.shape
    return pl.pallas_call(
        kernel_fn,
        grid=(m // tm,),
        in_specs=[pl.BlockSpec((tm, d), lambda i: (i, 0)),
                  pl.BlockSpec((tm, 1), lambda i: (i, 0)),
                  pl.BlockSpec((1, LANES), lambda i: (0, 0)),
                  pl.BlockSpec(w.shape, lambda i: (0, 0))],
        out_specs=[pl.BlockSpec((tm, wd), lambda i: (i, 0)) for wd in out_widths],
        out_shape=[jax.ShapeDtypeStruct((m, wd), BF16) for wd in out_widths],
        compiler_params=pltpu.CompilerParams(dimension_semantics=("parallel",),
                                             vmem_limit_bytes=VMEM_LIMIT_BYTES),
        name=name,
    )(x2, pos2, invf, w)


def _memkv_kernel(mem_ref, w_ref, mk_ref, mv_ref):
    kv = jnp.dot(mem_ref[...].astype(BF16), w_ref[...], preferred_element_type=F32)
    half = mk_ref.shape[1]
    mk_ref[...] = kv[:, :half].astype(BF16)
    mv_ref[...] = kv[:, half:].astype(BF16)


def _memkv(mem2, w):
    rows = mem2.shape[0]
    half = w.shape[1] // 2
    return pl.pallas_call(
        _memkv_kernel,
        out_shape=[jax.ShapeDtypeStruct((rows, half), BF16)] * 2,
        compiler_params=pltpu.CompilerParams(vmem_limit_bytes=VMEM_LIMIT_BYTES),
        name="memkv",
    )(mem2, w)


def _strict_lower(n):
    return (lax.broadcasted_iota(jnp.int32, (n, n), 0)
            > lax.broadcasted_iota(jnp.int32, (n, n), 1)).astype(BF16)


def _sb_tiles(chains, later):
    zs = [_nt_dot(qq, kk) for qq, kk, _, _, _ in chains]
    logs = []
    for z, (_, _, _, strict, _) in zip(zs, chains):
        soft = jnp.log(1.0 + jnp.exp2(-jnp.abs(z))) * LOG2_E
        log_1mb = -jnp.maximum(z, 0.0) - soft
        if strict is not None:
            log_1mb = jnp.where(strict, log_1mb, 0.0)
        hi = log_1mb.astype(BF16)
        lo = (log_1mb - hi.astype(F32)).astype(BF16)
        logs.append((jnp.minimum(z, 0.0) - soft, log_1mb, hi, lo))
    afters = [jnp.dot(hi, later, preferred_element_type=F32) + jnp.dot(lo, later, preferred_element_type=F32)
              for _, _, hi, lo in logs]
    outs = []
    for (log_b, log_1mb, _, _), after, (_, _, vv, strict, carry) in zip(logs, afters, chains):
        w = jnp.exp2(log_b + after + carry)
        if strict is not None:
            w = jnp.where(strict, w, 0.0)
        outs.append((jnp.dot(w.astype(BF16), vv, preferred_element_type=F32),
                     jnp.sum(log_1mb, axis=1, keepdims=True)))
    return outs


def _sb_kernel(q_ref, k_ref, v_ref, o_ref, *, blk):
    n_pairs = q_ref.shape[1] // LANES
    n_sub = q_ref.shape[0] // blk

    def lanes(hp):
        return slice(hp * LANES, (hp + 1) * LANES)

    rows = lax.broadcasted_iota(jnp.int32, (2 * blk, 2 * blk), 0)
    q_off = jnp.where(rows >= blk, rows - blk, rows)
    col = lax.broadcasted_iota(jnp.int32, (2 * blk, 2 * blk), 1)
    later2 = _strict_lower(2 * blk)
    later1 = _strict_lower(blk)
    no_carry = jnp.zeros((2 * blk, 1), F32)

    first = []
    for sub in range(n_sub):
        r0 = pl.multiple_of((pl.program_id(1) * n_sub + sub) * blk, blk)
        p0 = pl.multiple_of(jnp.maximum(r0 - blk, 0), blk)
        strict = (col - blk < q_off) & (col + (r0 - blk) >= 0)
        for hp in range(n_pairs):
            qq = _head_pair_rows(q_ref[sub * blk:(sub + 1) * blk, lanes(hp)])
            kk = jnp.concatenate([k_ref[pl.ds(p0, blk), lanes(hp)], k_ref[pl.ds(r0, blk), lanes(hp)]], axis=0)
            vv = jnp.concatenate([v_ref[pl.ds(p0, blk), lanes(hp)], v_ref[pl.ds(r0, blk), lanes(hp)]], axis=0)
            first.append((qq, kk, vv, strict, no_carry))
    started = _sb_tiles(first, later2)

    def finish(sub):
        bi = pl.program_id(1) * n_sub + sub
        mine = slice(sub * n_pairs, (sub + 1) * n_pairs)
        qqs = [chain[0] for chain in first[mine]]
        accs = tuple(pv for pv, _ in started[mine])
        carries = tuple(dsum for _, dsum in started[mine])

        def cond(c):
            j, carries, _ = c
            return (j >= 0) & (jnp.max(functools.reduce(jnp.maximum, carries)) > SB_EXIT_LOG2_WEIGHT)

        def body(c):
            j, carries, accs = c
            start = pl.multiple_of(j * blk, blk)
            outs = _sb_tiles([(qqs[hp], k_ref[pl.ds(start, blk), lanes(hp)], v_ref[pl.ds(start, blk), lanes(hp)],
                               None, carries[hp]) for hp in range(n_pairs)], later1)
            return (j - 1, tuple(c + dsum for c, (_, dsum) in zip(carries, outs)),
                    tuple(a + pv for a, (pv, _) in zip(accs, outs)))

        _, _, accs = lax.while_loop(cond, body, (bi - 2, carries, accs))
        o_ref[sub * blk:(sub + 1) * blk, :] = jnp.concatenate(
            [_merge_head_pair(a, blk) for a in accs], axis=1).astype(o_ref.dtype)

    for sub in range(n_sub):
        finish(sub)


def _row_block_attention(kernel_fn, name, q, k, v, extra_inputs, extra_specs, blk):
    b, s, wq = q.shape
    q_spec = pl.BlockSpec((None, blk, wq), lambda bi, qi: (bi, qi, 0))
    kv_spec = pl.BlockSpec((None, s, k.shape[2]), lambda bi, qi: (bi, 0, 0))
    return pl.pallas_call(
        kernel_fn,
        grid=(b, s // blk),
        in_specs=list(extra_specs) + [q_spec, kv_spec, kv_spec],
        out_specs=q_spec,
        out_shape=jax.ShapeDtypeStruct(q.shape, BF16),
        compiler_params=pltpu.CompilerParams(
            dimension_semantics=("parallel", "arbitrary"),
            vmem_limit_bytes=VMEM_LIMIT_BYTES),
        name=name,
    )(*extra_inputs, q, k, v)


def _diff_kernel(dl_ref, subln_ref, q_ref, k_ref, v_ref, o_ref, vp_ref, sa_ref, sb_ref, sd_ref, acc_ref, *, blk, lambda_init):
    qi = pl.program_id(2)
    seq = v_ref.shape[0]

    @pl.when(qi == 0)
    def _():
        low = lax.broadcasted_iota(jnp.int32, (blk, LANES), 1) < HEAD_DIM

        def build(c, _):
            rows = pl.ds(pl.multiple_of(c * blk, blk), blk)
            vf = v_ref[rows, :].astype(F32)
            vp_ref[rows, :LANES] = jnp.where(low, vf, 1.0).astype(BF16)
            vp_ref[rows, LANES:] = jnp.where(low, pltpu.roll(vf, HEAD_DIM, 1), 1.0).astype(BF16)
            return 0

        lax.fori_loop(0, seq // blk, build, 0)

    q2 = q_ref[...]
    lane = lax.broadcasted_iota(jnp.int32, q2.shape, 1)
    zero = jnp.zeros_like(q2)
    n_maps = LANES // DIFF_QK_DIM
    qq = jnp.concatenate([jnp.where(lane // DIFF_QK_DIM == c, q2, zero) for c in range(n_maps)], axis=0)
    rows_n = n_maps * blk
    half = rows_n // 2

    wide = 2 * blk

    def wide_keys(block):
        return pl.ds(pl.multiple_of(block * wide, wide), wide)

    def narrow_keys(start):
        return pl.ds(pl.multiple_of(start, blk), blk)

    def scores(s_ref, block):
        s_ref[...] = _nt_dot(qq, k_ref[wide_keys(block), :])

    def update(s, keys, m, diagonal=False):
        if diagonal:
            q_off = lax.broadcasted_iota(jnp.int32, s.shape, 0) % blk
            k_off = lax.broadcasted_iota(jnp.int32, s.shape, 1)
            s = jnp.where(k_off <= q_off, s, NEG_BIG)
        m_new = jnp.maximum(m, jnp.max(s, axis=1, keepdims=True))
        p = jnp.exp2(s - m_new).astype(BF16)
        alpha = jnp.exp2(m - m_new)
        for h, v_lanes in enumerate((slice(0, LANES), slice(LANES, 2 * LANES))):
            rows = slice(h * half, (h + 1) * half)
            acc_ref[rows, :] = alpha[rows] * acc_ref[rows, :] + jnp.dot(
                p[rows], vp_ref[keys, v_lanes], preferred_element_type=F32)
        return m_new

    def finish():
        acc = acc_ref[...]
        dl = dl_ref[...]
        lam = (jnp.exp(jnp.sum(dl[0:1] * dl[1:2], axis=1, keepdims=True))
               - jnp.exp(jnp.sum(dl[2:3] * dl[3:4], axis=1, keepdims=True)) + lambda_init)
        low = lax.broadcasted_iota(jnp.int32, (blk, LANES), 1) < HEAD_DIM
        ones = jnp.ones((LANES, LANES), BF16)
        normed = []
        for h in range(2):
            n0 = acc[2 * h * blk:(2 * h + 1) * blk]
            n1 = acc[(2 * h + 1) * blk:(2 * h + 2) * blk]
            l0 = pltpu.roll(n0, HEAD_DIM, 1)
            l1 = pltpu.roll(n1, HEAD_DIM, 1)
            d = n0 - (lam * l0 / l1) * n1
            dsq = jnp.where(low, d * d, 0.0)
            hi = dsq.astype(BF16)
            lo = (dsq - hi.astype(F32)).astype(BF16)
            ms = (jnp.dot(hi, ones, preferred_element_type=F32)
                  + jnp.dot(lo, ones, preferred_element_type=F32)) * (1.0 / HEAD_DIM)
            normed.append(d * lax.rsqrt(ms + LN_EPS * l0 * l0))
        y = jnp.where(low, normed[0], pltpu.roll(normed[1], HEAD_DIM, 1))
        o_ref[...] = (y * subln_ref[...] * (1.0 - lambda_init)).astype(o_ref.dtype)

    n_wide = qi // 2
    acc_ref[...] = jnp.zeros(acc_ref.shape, F32)
    sd_ref[...] = _nt_dot(qq, k_ref[narrow_keys(qi * blk), :])
    scores(sa_ref, 0)
    m = update(sd_ref[...], narrow_keys(qi * blk), jnp.full((rows_n, 1), NEG_BIG, F32), diagonal=True)

    def wide_update(s_ref, block, m):
        return update(s_ref[...], wide_keys(block), m)

    def leftover_update(s_ref, block, m):
        return update(s_ref[:, :blk], narrow_keys(block * wide), m)

    def body(i, m):
        scores(sb_ref, 2 * i + 1)
        m = wide_update(sa_ref, 2 * i, m)
        scores(sa_ref, 2 * i + 2)
        return wide_update(sb_ref, 2 * i + 1, m)

    n_loop = jnp.maximum(n_wide - 1, 0) // 2
    m = lax.fori_loop(0, n_loop, body, m)
    first = 2 * n_loop
    left = n_wide - first
    odd = qi % 2

    def tail(n_left, is_odd):
        def run():
            mm = m
            if n_left >= 1:
                if n_left == 2 or is_odd:
                    scores(sb_ref, first + 1)
                mm = wide_update(sa_ref, first, mm)
            if n_left == 2:
                if is_odd:
                    scores(sa_ref, first + 2)
                mm = wide_update(sb_ref, first + 1, mm)
            if is_odd:
                leftover_update(sa_ref if n_left in (0, 2) else sb_ref, n_wide, mm)
            finish()
        pl.when(jnp.logical_and(left == n_left, odd == int(is_odd)))(run)

    for n_left in range(3):
        for is_odd in (False, True):
            tail(n_left, is_odd)


def _diff_attention(q, k, v, diff_lambda, subln_lanes, blk, lambda_init):
    b, s, wq = q.shape
    assert s % (2 * blk) == 0
    rows_n = LANES // DIFF_QK_DIM * blk
    q_spec = pl.BlockSpec((None, blk, LANES), lambda bi, hp, qi: (bi, qi, hp))
    kv_spec = pl.BlockSpec((None, s, LANES), lambda bi, hp, qi: (bi, 0, hp))
    return pl.pallas_call(
        functools.partial(_diff_kernel, blk=blk, lambda_init=lambda_init),
        grid=(b, wq // LANES, s // blk),
        in_specs=[pl.BlockSpec(diff_lambda.shape, lambda bi, hp, qi: (0, 0)),
                  pl.BlockSpec((1, LANES), lambda bi, hp, qi: (0, 0)), q_spec, kv_spec, kv_spec],
        out_specs=q_spec,
        out_shape=jax.ShapeDtypeStruct(q.shape, BF16),
        scratch_shapes=[pltpu.VMEM((s, 2 * LANES), BF16),
                        pltpu.VMEM((rows_n, 2 * blk), F32),
                        pltpu.VMEM((rows_n, 2 * blk), F32),
                        pltpu.VMEM((rows_n, blk), F32),
                        pltpu.VMEM((rows_n, LANES), F32)],
        compiler_params=pltpu.CompilerParams(
            dimension_semantics=("parallel", "parallel", "arbitrary"),
            vmem_limit_bytes=VMEM_LIMIT_BYTES),
        name="differential",
    )(diff_lambda, subln_lanes, q, k, v)


def _swa_kernel(sinks_ref, q_ref, k_ref, v_ref, o_ref, *, blk, pairs_per_kv_block):
    n_sub = q_ref.shape[0] // blk

    def lanes(hb):
        return slice(hb * LANES, (hb + 1) * LANES)

    shape = (2 * blk, WINDOW + blk)
    rows = lax.broadcasted_iota(jnp.int32, shape, 0)
    q_off = jnp.where(rows >= blk, rows - blk, rows)
    col = lax.broadcasted_iota(jnp.int32, shape, 1)
    rel = (col - WINDOW) - q_off
    in_window = (rel <= 0) & (rel > -WINDOW)
    second_head = lax.broadcasted_iota(jnp.int32, (2 * blk, 1), 0) >= blk
    n_blocks = q_ref.shape[1] // LANES
    chains = []
    for sub in range(n_sub):
        r0 = pl.multiple_of((pl.program_id(1) * n_sub + sub) * blk, blk)
        p0 = pl.multiple_of(jnp.maximum(r0 - WINDOW, 0), WINDOW)
        band = in_window & (col + (r0 - WINDOW) >= 0)
        for hb in range(n_blocks):
            kvb = hb // pairs_per_kv_block
            kk = jnp.concatenate([k_ref[pl.ds(p0, WINDOW), lanes(kvb)], k_ref[pl.ds(r0, blk), lanes(kvb)]], axis=0)
            vv = jnp.concatenate([v_ref[pl.ds(p0, WINDOW), lanes(kvb)], v_ref[pl.ds(r0, blk), lanes(kvb)]], axis=0)
            qq = _head_pair_rows(q_ref[sub * blk:(sub + 1) * blk, lanes(hb)])
            sink = jnp.where(second_head, sinks_ref[SW_HEAD_ORDER[2 * hb + 1]],
                             sinks_ref[SW_HEAD_ORDER[2 * hb]]) * LOG2_E
            chains.append((qq, kk, vv, band, sink))
    scores = [jnp.where(band, _nt_dot(qq, kk), NEG_BIG) for qq, kk, _, band, _ in chains]
    probs = []
    for s, (_, _, _, _, sink) in zip(scores, chains):
        m = jnp.maximum(jnp.max(s, axis=1, keepdims=True), sink)
        p = jnp.exp2(s - m)
        probs.append((p.astype(BF16), jnp.sum(p, axis=1, keepdims=True) + jnp.exp2(sink - m)))
    outs = [_merge_head_pair(jnp.dot(p, vv, preferred_element_type=F32) / denom, blk)
            for (p, denom), (_, _, vv, _, _) in zip(probs, chains)]
    for sub in range(n_sub):
        o_ref[sub * blk:(sub + 1) * blk, :] = jnp.concatenate(
            outs[sub * n_blocks:(sub + 1) * n_blocks], axis=1).astype(o_ref.dtype)


def _out_kernel(*refs, n_attn, alpha):
    x_ref = refs[0]
    attn_refs = refs[1:1 + n_attn]
    mq_ref, gate_ref, mk_ref, mv_ref, w_ref, g_ref, b_ref, o_ref = refs[1 + n_attn:]
    tm = x_ref.shape[0]

    parts = [r[...].astype(F32) for r in attn_refs]
    blocks = [slice(hb * LANES, (hb + 1) * LANES) for hb in range(mq_ref.shape[1] // LANES)]
    scores = [_nt_dot(_head_pair_rows(mq_ref[:, c]), mk_ref[:, c]) for c in blocks]
    probs = [jnp.exp2(s - jnp.max(s, axis=1, keepdims=True)) for s in scores]
    for p, c in zip(probs, blocks):
        o = jnp.dot(p.astype(BF16), mv_ref[:, c], preferred_element_type=F32) / jnp.sum(p, axis=1, keepdims=True)
        parts.append(_merge_head_pair(o, tm))
    mixed = jnp.concatenate(parts, axis=1)
    gate = gate_ref[...].astype(F32)
    gated = mixed * (gate / (1.0 + jnp.exp(-gate)))
    y = jnp.dot(gated.astype(BF16), w_ref[...], preferred_element_type=F32)
    r = alpha * x_ref[...] + y
    mu = jnp.mean(r, axis=1, keepdims=True)
    c = r - mu
    var = jnp.mean(c * c, axis=1, keepdims=True)
    o_ref[...] = c * lax.rsqrt(var + LN_EPS) * g_ref[...] + b_ref[...]


def _out_layer(x2, attn_outs, mq, gate, mk, mv, w_out, ln_g, ln_b, seq, tm, alpha, name):
    m, d = x2.shape
    mem_len = mk.shape[0] // (m // seq)
    steps_per_batch = seq // tm

    def row(wd):
        return pl.BlockSpec((tm, wd), lambda i: (i, 0))

    def whole(a):
        return pl.BlockSpec(a.shape, lambda i: (0, 0))

    mem_spec = pl.BlockSpec((mem_len, mk.shape[1]), lambda i: (i // steps_per_batch, 0))
    in_specs = ([row(d)] + [row(a.shape[1]) for a in attn_outs]
                + [row(mq.shape[1]), row(gate.shape[1]), mem_spec, mem_spec,
                   whole(w_out), whole(ln_g), whole(ln_b)])
    return pl.pallas_call(
        functools.partial(_out_kernel, n_attn=len(attn_outs), alpha=alpha),
        grid=(m // tm,),
        in_specs=in_specs,
        out_specs=row(d),
        out_shape=jax.ShapeDtypeStruct((m, d), F32),
        compiler_params=pltpu.CompilerParams(dimension_semantics=("parallel",),
                                             vmem_limit_bytes=VMEM_LIMIT_BYTES),
        name=name,
    )(x2, *attn_outs, mq, gate, mk, mv, w_out, ln_g, ln_b)


ROW_TILE = 512
SB_BLOCK = 128
SB_ROWS_PER_STEP = 512
DIFF_BLOCK = 512
SWA_BLOCK = 128
SWA_ROWS_PER_STEP = 512


def _even_layer(x2, mem2, pos2, w_in, w_memkv, diff_lambda, diff_subln, w_out, ln_g, ln_b,
                layer_idx, batch, seq, alpha):
    lambda_init = 0.8 - 0.6 * math.exp(-0.3 * layer_idx)
    w_sb = SB_HEADS * HEAD_DIM
    w_df = DIFF_HEADS * HEAD_DIM
    sbq, sbk, sbv, dfq, dfk, dfv, mq, gate = _inproj(
        _inproj_even_kernel, "inproj_even", x2, pos2, _inv_freq_lanes(DIFF_QK_DIM), w_in.astype(BF16),
        (w_sb, w_sb, w_sb, w_df, w_df, w_df, MEM_HEADS * HEAD_DIM, w_out.shape[0]), ROW_TILE)
    mk, mv = _memkv(mem2, w_memkv.astype(BF16))

    def b3(a):
        return a.reshape(batch, seq, a.shape[1])

    sb_o = _row_block_attention(functools.partial(_sb_kernel, blk=SB_BLOCK), "stick_breaking",
                                b3(sbq), b3(sbk), b3(sbv), (), (), SB_ROWS_PER_STEP)
    subln_lanes = jnp.tile(diff_subln.astype(F32), LANES // HEAD_DIM)[None, :]
    df_o = _diff_attention(b3(dfq), b3(dfk), b3(dfv), diff_lambda.astype(F32), subln_lanes,
                           DIFF_BLOCK, lambda_init)
    return _out_layer(x2, [sb_o.reshape(x2.shape[0], -1), df_o.reshape(x2.shape[0], -1)], mq, gate, mk, mv,
                      w_out.astype(BF16), ln_g[None, :], ln_b[None, :], seq, ROW_TILE, alpha, "out_even")


def _odd_layer(x2, mem2, pos2, w_in, w_memkv, sinks, w_out, ln_g, ln_b, batch, seq, alpha):
    w_q = SWA_Q_HEADS * HEAD_DIM
    w_kv = SWA_KV_HEADS * HEAD_DIM
    w_m = MEM_HEADS * HEAD_DIM
    def heads_in_order(a, axis, lo):
        return [lax.slice_in_dim(a, lo + h * HEAD_DIM, lo + (h + 1) * HEAD_DIM, axis=axis) for h in SW_HEAD_ORDER]

    gate_lo = w_q + 2 * w_kv + w_m
    last_kv = slice((SWA_KV_HEADS - 1) * HEAD_DIM, w_kv)
    w_in = w_in.astype(BF16)
    w_out = w_out.astype(BF16)
    k_w = w_in[:, w_q:w_q + w_kv]
    v_w = w_in[:, w_q + w_kv:w_q + 2 * w_kv]
    w_perm = jnp.concatenate(
        heads_in_order(w_in, 1, 0) + [k_w, k_w[:, last_kv], v_w, v_w[:, last_kv], w_in[:, w_q + 2 * w_kv:gate_lo]]
        + heads_in_order(w_in, 1, gate_lo) + [w_in[:, gate_lo + w_q:]], axis=1)
    w_out_perm = jnp.concatenate(heads_in_order(w_out, 0, 0) + [w_out[w_q:]], axis=0)

    cq, ck, cv, mq, gate = _inproj(
        _inproj_odd_kernel, "inproj_odd", x2, pos2, _inv_freq_lanes(HEAD_DIM), w_perm,
        (w_q, 2 * LANES, 2 * LANES, w_m, w_out.shape[0]), ROW_TILE)
    mk, mv = _memkv(mem2, w_memkv.astype(BF16))

    def b3(a):
        return a.reshape(batch, seq, a.shape[1])

    pairs_per_kv_block = LANES // HEAD_DIM * SWA_GROUP // 2
    sink_spec = [pl.BlockSpec(memory_space=pltpu.SMEM)]
    c_o = _row_block_attention(
        functools.partial(_swa_kernel, blk=SWA_BLOCK, pairs_per_kv_block=pairs_per_kv_block),
        "sliding_window", b3(cq), b3(ck), b3(cv), (sinks.astype(F32),), sink_spec,
        SWA_ROWS_PER_STEP)
    return _out_layer(x2, [c_o.reshape(x2.shape[0], -1)], mq, gate, mk, mv, w_out_perm,
                      ln_g[None, :], ln_b[None, :], seq, ROW_TILE, alpha, "out_odd")


def kernel(x, mem, positions, w_in_even, w_memkv_even, diff_lambda_even, diff_subln_even, w_out_even,
           ln_g_even, ln_b_even, w_in_odd, w_memkv_odd, sinks_odd, w_out_odd, ln_g_odd, ln_b_odd):
    batch, seq, d = x.shape
    depth = w_in_even.shape[0] + w_in_odd.shape[0]
    alpha = (2 * depth) ** 0.25
    x2 = x.reshape(batch * seq, d)
    mem2 = mem.reshape(batch * mem.shape[1], d)
    pos2 = positions.reshape(batch * seq, 1)
    for i in range(depth):
        j = i // 2
        if i % 2 == 0:
            x2 = _even_layer(x2, mem2, pos2, w_in_even[j], w_memkv_even[j], diff_lambda_even[j],
                             diff_subln_even[j], w_out_even[j], ln_g_even[j], ln_b_even[j], i,
                             batch, seq, alpha)
        else:
            x2 = _odd_layer(x2, mem2, pos2, w_in_odd[j], w_memkv_odd[j], sinks_odd[j], w_out_odd[j],
                            ln_g_odd[j], ln_b_odd[j], batch, seq, alpha)
    return x2.reshape(batch, seq, d)
```

```python
import functools
import math

import jax
import jax.numpy as jnp
import numpy as np
from jax import lax
from jax.experimental import pallas as pl
from jax.experimental.pallas import tpu as pltpu

F32 = jnp.float32
BF16 = jnp.bfloat16

HEAD_DIM = 64
LANES = 128
MEM_HEADS = 4
SB_HEADS = 6
DIFF_HEADS = 6
DIFF_QK_DIM = 32
SWA_Q_HEADS = 12
SWA_KV_HEADS = 3
SWA_GROUP = 4
WINDOW = 128
ROPE_THETA = 500000.0
ROPE_FRACTION = 4
LN_EPS = 1e-5
NEG_BIG = -1e30
VMEM_LIMIT_BYTES = 48 * 1024 * 1024

SB_EXIT_LOG2_WEIGHT = -92.0
LOG2_E = math.log2(math.e)

SW_HEAD_ORDER = (0, 4, 1, 5, 2, 6, 3, 7, 8, 9, 10, 11)


def _nt_dot(a, b):
    return lax.dot_general(a, b, (((1,), (1,)), ((), ())), preferred_element_type=F32)


def _head_pair_rows(q2):
    lane = lax.broadcasted_iota(jnp.int32, q2.shape, 1)
    low = lane < HEAD_DIM
    zero = jnp.zeros_like(q2)
    return jnp.concatenate([jnp.where(low, q2, zero), jnp.where(low, zero, q2)], axis=0)


def _merge_head_pair(o, rows):
    lane = lax.broadcasted_iota(jnp.int32, (rows, LANES), 1)
    return jnp.where(lane < HEAD_DIM, o[:rows], o[rows:])


def _rope_lane_tables(group, pos_f32, inv_freq_row):
    rot = group // ROPE_FRACTION
    half = rot // 2
    lane = lax.broadcasted_iota(jnp.int32, (1, LANES), 1)
    r = lane % group
    ang = pos_f32 * inv_freq_row
    cos = jnp.cos(ang)
    sin = jnp.sin(ang)
    cos_t = jnp.where(r < rot, cos, 1.0)
    sin_first = jnp.where(r < half, -sin, 0.0)
    sin_second = jnp.where((r >= half) & (r < rot), sin, 0.0)
    return cos_t, sin_first, sin_second, half


def _apply_rope(h, tables):
    cos_t, sin_first, sin_second, half = tables
    outs = []
    for c in range(h.shape[1] // LANES):
        blk = h[:, c * LANES:(c + 1) * LANES]
        outs.append(blk * cos_t + pltpu.roll(blk, LANES - half, 1) * sin_first
                    + pltpu.roll(blk, half, 1) * sin_second)
    return jnp.concatenate(outs, axis=1)


def _inv_freq_lanes(group):
    half = group // ROPE_FRACTION // 2
    inv_freq = jnp.exp(-(jnp.arange(half, dtype=F32) / half) * math.log(ROPE_THETA))
    lane = np.arange(LANES)
    return inv_freq[(lane % group) % half][None, :]


def _inproj_even_kernel(x_ref, pos_ref, invf_ref, w_ref, sbq_ref, sbk_ref, sbv_ref,
                        dfq_ref, dfk_ref, dfv_ref):
    h = jnp.dot(x_ref[...].astype(BF16), w_ref[...], preferred_element_type=F32)
    edges = [0]

    def sec(width):
        edges.append(edges[-1] + width)
        return h[:, edges[-2]:edges[-1]]

    w_sb = SB_HEADS * HEAD_DIM
    w_df = DIFF_HEADS * 2 * DIFF_QK_DIM
    tables = _rope_lane_tables(DIFF_QK_DIM, pos_ref[...].astype(F32), invf_ref[...])
    sbq_ref[...] = (sec(w_sb) * (HEAD_DIM ** -0.5 * LOG2_E)).astype(BF16)
    sbk_ref[...] = sec(w_sb).astype(BF16)
    sbv_ref[...] = sec(w_sb).astype(BF16)
    q_scale = DIFF_QK_DIM ** -0.5 * math.log2(math.e)
    dfq_ref[...] = (_apply_rope(sec(w_df), tables) * q_scale).astype(BF16)
    dfk_ref[...] = _apply_rope(sec(w_df), tables).astype(BF16)
    dfv_ref[...] = sec(w_df).astype(BF16)


def _inproj_odd_kernel(x_ref, pos_ref, invf_ref, w_ref, cq_ref, ck_ref, cv_ref):
    h = jnp.dot(x_ref[...].astype(BF16), w_ref[...], preferred_element_type=F32)
    edges = [0]

    def sec(width):
        edges.append(edges[-1] + width)
        return h[:, edges[-2]:edges[-1]]

    w_kv = ck_ref.shape[1]
    tables = _rope_lane_tables(HEAD_DIM, pos_ref[...].astype(F32), invf_ref[...])
    cq_ref[...] = (_apply_rope(sec(SWA_Q_HEADS * HEAD_DIM), tables)
                   * (HEAD_DIM ** -0.5 * LOG2_E)).astype(BF16)
    ck_ref[...] = _apply_rope(sec(w_kv), tables).astype(BF16)
    cv_ref[...] = sec(w_kv).astype(BF16)


def _inproj(kernel_fn, name, x2, pos2, invf, w, out_widths, tm):
    m, d = x2.shape
    return pl.pallas_call(
        kernel_fn,
        grid=(m // tm,),
        in_specs=[pl.BlockSpec((tm, d), lambda i: (i, 0)),
                  pl.BlockSpec((tm, 1), lambda i: (i, 0)),
                  pl.BlockSpec((1, LANES), lambda i: (0, 0)),
                  pl.BlockSpec(w.shape, lambda i: (0, 0))],
        out_specs=[pl.BlockSpec((tm, wd), lambda i: (i, 0)) for wd in out_widths],
        out_shape=[jax.ShapeDtypeStruct((m, wd), BF16) for wd in out_widths],
        compiler_params=pltpu.CompilerParams(dimension_semantics=("parallel",),
                                             vmem_limit_bytes=VMEM_LIMIT_BYTES),
        name=name,
    )(x2, pos2, invf, w)


def _memkv_kernel(mem_ref, w_ref, mk_ref, mv_ref):
    kv = jnp.dot(mem_ref[...].astype(BF16), w_ref[...], preferred_element_type=F32)
    half = mk_ref.shape[1]
    mk_ref[...] = kv[:, :half].astype(BF16)
    mv_ref[...] = kv[:, half:].astype(BF16)


def _memkv(mem2, w):
    rows = mem2.shape[0]
    half = w.shape[1] // 2
    return pl.pallas_call(
        _memkv_kernel,
        out_shape=[jax.ShapeDtypeStruct((rows, half), BF16)] * 2,
        compiler_params=pltpu.CompilerParams(vmem_limit_bytes=VMEM_LIMIT_BYTES),
        name="memkv",
    )(mem2, w)


def _strict_lower(n):
    return (lax.broadcasted_iota(jnp.int32, (n, n), 0)
            > lax.broadcasted_iota(jnp.int32, (n, n), 1)).astype(BF16)


def _sb_tiles(chains, later):
    zs = [_nt_dot(qq, kk) for qq, kk, _, _, _ in chains]
    logs = []
    for z, (_, _, _, strict, _) in zip(zs, chains):
        soft = jnp.log(1.0 + jnp.exp2(-jnp.abs(z))) * LOG2_E
        log_1mb = -jnp.maximum(z, 0.0) - soft
        if strict is not None:
            log_1mb = jnp.where(strict, log_1mb, 0.0)
        hi = log_1mb.astype(BF16)
        lo = (log_1mb - hi.astype(F32)).astype(BF16)
        logs.append((jnp.minimum(z, 0.0) - soft, log_1mb, hi, lo))
    afters = [jnp.dot(hi, later, preferred_element_type=F32) + jnp.dot(lo, later, preferred_element_type=F32)
              for _, _, hi, lo in logs]
    outs = []
    for (log_b, log_1mb, _, _), after, (_, _, vv, strict, carry) in zip(logs, afters, chains):
        w = jnp.exp2(log_b + after + carry)
        if strict is not None:
            w = jnp.where(strict, w, 0.0)
        outs.append((jnp.dot(w.astype(BF16), vv, preferred_element_type=F32),
                     jnp.sum(log_1mb, axis=1, keepdims=True)))
    return outs


def _sb_kernel(q_ref, k_ref, v_ref, o_ref, *, blk):
    n_pairs = q_ref.shape[1] // LANES
    n_sub = q_ref.shape[0] // blk

    def lanes(hp):
        return slice(hp * LANES, (hp + 1) * LANES)

    rows = lax.broadcasted_iota(jnp.int32, (2 * blk, 2 * blk), 0)
    q_off = jnp.where(rows >= blk, rows - blk, rows)
    col = lax.broadcasted_iota(jnp.int32, (2 * blk, 2 * blk), 1)
    later2 = _strict_lower(2 * blk)
    later1 = _strict_lower(blk)
    no_carry = jnp.zeros((2 * blk, 1), F32)

    first = []
    for sub in range(n_sub):
        r0 = pl.multiple_of((pl.program_id(1) * n_sub + sub) * blk, blk)
        p0 = pl.multiple_of(jnp.maximum(r0 - blk, 0), blk)
        strict = (col - blk < q_off) & (col + (r0 - blk) >= 0)
        for hp in range(n_pairs):
            qq = _head_pair_rows(q_ref[sub * blk:(sub + 1) * blk, lanes(hp)])
            kk = jnp.concatenate([k_ref[pl.ds(p0, blk), lanes(hp)], k_ref[pl.ds(r0, blk), lanes(hp)]], axis=0)
            vv = jnp.concatenate([v_ref[pl.ds(p0, blk), lanes(hp)], v_ref[pl.ds(r0, blk), lanes(hp)]], axis=0)
            first.append((qq, kk, vv, strict, no_carry))
    started = _sb_tiles(first, later2)

    def finish(sub):
        bi = pl.program_id(1) * n_sub + sub
        mine = slice(sub * n_pairs, (sub + 1) * n_pairs)
        qqs = [chain[0] for chain in first[mine]]
        accs = tuple(pv for pv, _ in started[mine])
        carries = tuple(dsum for _, dsum in started[mine])

        def cond(c):
            j, carries, _ = c
            return (j >= 0) & (jnp.max(functools.reduce(jnp.maximum, carries)) > SB_EXIT_LOG2_WEIGHT)

        def body(c):
            j, carries, accs = c
            start = pl.multiple_of(j * blk, blk)
            outs = _sb_tiles([(qqs[hp], k_ref[pl.ds(start, blk), lanes(hp)], v_ref[pl.ds(start, blk), lanes(hp)],
                               None, carries[hp]) for hp in range(n_pairs)], later1)
            return (j - 1, tuple(c + dsum for c, (_, dsum) in zip(carries, outs)),
                    tuple(a + pv for a, (pv, _) in zip(accs, outs)))

        _, _, accs = lax.while_loop(cond, body, (bi - 2, carries, accs))
        o_ref[sub * blk:(sub + 1) * blk, :] = jnp.concatenate(
            [_merge_head_pair(a, blk) for a in accs], axis=1).astype(o_ref.dtype)

    for sub in range(n_sub):
        finish(sub)


def _row_block_attention(kernel_fn, name, q, k, v, extra_inputs, extra_specs, blk):
    b, s, wq = q.shape
    q_spec = pl.BlockSpec((None, blk, wq), lambda bi, qi: (bi, qi, 0))
    kv_spec = pl.BlockSpec((None, s, k.shape[2]), lambda bi, qi: (bi, 0, 0))
    return pl.pallas_call(
        kernel_fn,
        grid=(b, s // blk),
        in_specs=list(extra_specs) + [q_spec, kv_spec, kv_spec],
        out_specs=q_spec,
        out_shape=jax.ShapeDtypeStruct(q.shape, BF16),
        compiler_params=pltpu.CompilerParams(
            dimension_semantics=("parallel", "arbitrary"),
            vmem_limit_bytes=VMEM_LIMIT_BYTES),
        name=name,
    )(*extra_inputs, q, k, v)


def _diff_kernel(dl_ref, subln_ref, q_ref, k_ref, v_ref, o_ref, vp_ref, sa_ref, sb_ref, sd_ref, acc_ref, *, blk, lambda_init):
    qi = pl.program_id(2)
    seq = v_ref.shape[0]

    @pl.when(qi == 0)
    def _():
        low = lax.broadcasted_iota(jnp.int32, (blk, LANES), 1) < HEAD_DIM

        def build(c, _):
            rows = pl.ds(pl.multiple_of(c * blk, blk), blk)
            vf = v_ref[rows, :].astype(F32)
            vp_ref[rows, :LANES] = jnp.where(low, vf, 1.0).astype(BF16)
            vp_ref[rows, LANES:] = jnp.where(low, pltpu.roll(vf, HEAD_DIM, 1), 1.0).astype(BF16)
            return 0

        lax.fori_loop(0, seq // blk, build, 0)

    q2 = q_ref[...]
    lane = lax.broadcasted_iota(jnp.int32, q2.shape, 1)
    zero = jnp.zeros_like(q2)
    n_maps = LANES // DIFF_QK_DIM
    qq = jnp.concatenate([jnp.where(lane // DIFF_QK_DIM == c, q2, zero) for c in range(n_maps)], axis=0)
    rows_n = n_maps * blk
    half = rows_n // 2

    wide = 2 * blk

    def wide_keys(block):
        return pl.ds(pl.multiple_of(block * wide, wide), wide)

    def narrow_keys(start):
        return pl.ds(pl.multiple_of(start, blk), blk)

    def scores(s_ref, block):
        s_ref[...] = _nt_dot(qq, k_ref[wide_keys(block), :])

    def update(s, keys, m, diagonal=False):
        if diagonal:
            q_off = lax.broadcasted_iota(jnp.int32, s.shape, 0) % blk
            k_off = lax.broadcasted_iota(jnp.int32, s.shape, 1)
            s = jnp.where(k_off <= q_off, s, NEG_BIG)
        m_new = jnp.maximum(m, jnp.max(s, axis=1, keepdims=True))
        p = jnp.exp2(s - m_new).astype(BF16)
        alpha = jnp.exp2(m - m_new)
        for h, v_lanes in enumerate((slice(0, LANES), slice(LANES, 2 * LANES))):
            rows = slice(h * half, (h + 1) * half)
            acc_ref[rows, :] = alpha[rows] * acc_ref[rows, :] + jnp.dot(
                p[rows], vp_ref[keys, v_lanes], preferred_element_type=F32)
        return m_new

    def finish():
        acc = acc_ref[...]
        dl = dl_ref[...]
        lam = (jnp.exp(jnp.sum(dl[0:1] * dl[1:2], axis=1, keepdims=True))
               - jnp.exp(jnp.sum(dl[2:3] * dl[3:4], axis=1, keepdims=True)) + lambda_init)
        low = lax.broadcasted_iota(jnp.int32, (blk, LANES), 1) < HEAD_DIM
        ones = jnp.ones((LANES, LANES), BF16)
        normed = []
        for h in range(2):
            n0 = acc[2 * h * blk:(2 * h + 1) * blk]
            n1 = acc[(2 * h + 1) * blk:(2 * h + 2) * blk]
            l0 = pltpu.roll(n0, HEAD_DIM, 1)
            l1 = pltpu.roll(n1, HEAD_DIM, 1)
            d = n0 - (lam * l0 / l1) * n1
            dsq = jnp.where(low, d * d, 0.0)
            hi = dsq.astype(BF16)
            lo = (dsq - hi.astype(F32)).astype(BF16)
            ms = (jnp.dot(hi, ones, preferred_element_type=F32)
                  + jnp.dot(lo, ones, preferred_element_type=F32)) * (1.0 / HEAD_DIM)
            normed.append(d * lax.rsqrt(ms + LN_EPS * l0 * l0))
        y = jnp.where(low, normed[0], pltpu.roll(normed[1], HEAD_DIM, 1))
        o_ref[...] = (y * subln_ref[...] * (1.0 - lambda_init)).astype(o_ref.dtype)

    n_wide = qi // 2
    acc_ref[...] = jnp.zeros(acc_ref.shape, F32)
    sd_ref[...] = _nt_dot(qq, k_ref[narrow_keys(qi * blk), :])
    scores(sa_ref, 0)
    m = update(sd_ref[...], narrow_keys(qi * blk), jnp.full((rows_n, 1), NEG_BIG, F32), diagonal=True)

    def wide_update(s_ref, block, m):
        return update(s_ref[...], wide_keys(block), m)

    def leftover_update(s_ref, block, m):
        return update(s_ref[:, :blk], narrow_keys(block * wide), m)

    def body(i, m):
        scores(sb_ref, 2 * i + 1)
        m = wide_update(sa_ref, 2 * i, m)
        scores(sa_ref, 2 * i + 2)
        return wide_update(sb_ref, 2 * i + 1, m)

    n_loop = jnp.maximum(n_wide - 1, 0) // 2
    m = lax.fori_loop(0, n_loop, body, m)
    first = 2 * n_loop
    left = n_wide - first
    odd = qi % 2

    def tail(n_left, is_odd):
        def run():
            mm = m
            if n_left >= 1:
                if n_left == 2 or is_odd:
                    scores(sb_ref, first + 1)
                mm = wide_update(sa_ref, first, mm)
            if n_left == 2:
                if is_odd:
                    scores(sa_ref, first + 2)
                mm = wide_update(sb_ref, first + 1, mm)
            if is_odd:
                leftover_update(sa_ref if n_left in (0, 2) else sb_ref, n_wide, mm)
            finish()
        pl.when(jnp.logical_and(left == n_left, odd == int(is_odd)))(run)

    for n_left in range(3):
        for is_odd in (False, True):
            tail(n_left, is_odd)


def _diff_attention(q, k, v, diff_lambda, subln_lanes, blk, lambda_init):
    b, s, wq = q.shape
    assert s % (2 * blk) == 0
    rows_n = LANES // DIFF_QK_DIM * blk
    q_spec = pl.BlockSpec((None, blk, LANES), lambda bi, hp, qi: (bi, qi, hp))
    kv_spec = pl.BlockSpec((None, s, LANES), lambda bi, hp, qi: (bi, 0, hp))
    return pl.pallas_call(
        functools.partial(_diff_kernel, blk=blk, lambda_init=lambda_init),
        grid=(b, wq // LANES, s // blk),
        in_specs=[pl.BlockSpec(diff_lambda.shape, lambda bi, hp, qi: (0, 0)),
                  pl.BlockSpec((1, LANES), lambda bi, hp, qi: (0, 0)), q_spec, kv_spec, kv_spec],
        out_specs=q_spec,
        out_shape=jax.ShapeDtypeStruct(q.shape, BF16),
        scratch_shapes=[pltpu.VMEM((s, 2 * LANES), BF16),
                        pltpu.VMEM((rows_n, 2 * blk), F32),
                        pltpu.VMEM((rows_n, 2 * blk), F32),
                        pltpu.VMEM((rows_n, blk), F32),
                        pltpu.VMEM((rows_n, LANES), F32)],
        compiler_params=pltpu.CompilerParams(
            dimension_semantics=("parallel", "parallel", "arbitrary"),
            vmem_limit_bytes=VMEM_LIMIT_BYTES),
        name="differential",
    )(diff_lambda, subln_lanes, q, k, v)


def _swa_kernel(sinks_ref, q_ref, k_ref, v_ref, o_ref, *, blk, pairs_per_kv_block):
    n_sub = q_ref.shape[0] // blk

    def lanes(hb):
        return slice(hb * LANES, (hb + 1) * LANES)

    shape = (2 * blk, WINDOW + blk)
    rows = lax.broadcasted_iota(jnp.int32, shape, 0)
    q_off = jnp.where(rows >= blk, rows - blk, rows)
    col = lax.broadcasted_iota(jnp.int32, shape, 1)
    rel = (col - WINDOW) - q_off
    in_window = (rel <= 0) & (rel > -WINDOW)
    second_head = lax.broadcasted_iota(jnp.int32, (2 * blk, 1), 0) >= blk
    n_blocks = q_ref.shape[1] // LANES
    chains = []
    for sub in range(n_sub):
        r0 = pl.multiple_of((pl.program_id(1) * n_sub + sub) * blk, blk)
        p0 = pl.multiple_of(jnp.maximum(r0 - WINDOW, 0), WINDOW)
        band = in_window & (col + (r0 - WINDOW) >= 0)
        for hb in range(n_blocks):
            kvb = hb // pairs_per_kv_block
            kk = jnp.concatenate([k_ref[pl.ds(p0, WINDOW), lanes(kvb)], k_ref[pl.ds(r0, blk), lanes(kvb)]], axis=0)
            vv = jnp.concatenate([v_ref[pl.ds(p0, WINDOW), lanes(kvb)], v_ref[pl.ds(r0, blk), lanes(kvb)]], axis=0)
            qq = _head_pair_rows(q_ref[sub * blk:(sub + 1) * blk, lanes(hb)])
            sink = jnp.where(second_head, sinks_ref[SW_HEAD_ORDER[2 * hb + 1]],
                             sinks_ref[SW_HEAD_ORDER[2 * hb]]) * LOG2_E
            chains.append((qq, kk, vv, band, sink))
    scores = [jnp.where(band, _nt_dot(qq, kk), NEG_BIG) for qq, kk, _, band, _ in chains]
    probs = []
    for s, (_, _, _, _, sink) in zip(scores, chains):
        m = jnp.maximum(jnp.max(s, axis=1, keepdims=True), sink)
        p = jnp.exp2(s - m)
        probs.append((p.astype(BF16), jnp.sum(p, axis=1, keepdims=True) + jnp.exp2(sink - m)))
    outs = [_merge_head_pair(jnp.dot(p, vv, preferred_element_type=F32) / denom, blk)
            for (p, denom), (_, _, vv, _, _) in zip(probs, chains)]
    for sub in range(n_sub):
        o_ref[sub * blk:(sub + 1) * blk, :] = jnp.concatenate(
            outs[sub * n_blocks:(sub + 1) * n_blocks], axis=1).astype(o_ref.dtype)


def _out_kernel(*refs, n_attn, alpha):
    x_ref = refs[0]
    attn_refs = refs[1:1 + n_attn]
    wgm_ref, mk_ref, mv_ref, w_ref, g_ref, b_ref, o_ref = refs[1 + n_attn:]
    tm = x_ref.shape[0]
    w_m = MEM_HEADS * HEAD_DIM

    h = jnp.dot(x_ref[...].astype(BF16), wgm_ref[...], preferred_element_type=F32)
    mq = (h[:, :w_m] * (HEAD_DIM ** -0.5 * LOG2_E)).astype(BF16)
    gate = h[:, w_m:]
    parts = [r[...].astype(F32) for r in attn_refs]
    blocks = [slice(hb * LANES, (hb + 1) * LANES) for hb in range(w_m // LANES)]
    scores = [_nt_dot(_head_pair_rows(mq[:, c]), mk_ref[:, c]) for c in blocks]
    probs = [jnp.exp2(s - jnp.max(s, axis=1, keepdims=True)) for s in scores]
    for p, c in zip(probs, blocks):
        o = jnp.dot(p.astype(BF16), mv_ref[:, c], preferred_element_type=F32) / jnp.sum(p, axis=1, keepdims=True)
        parts.append(_merge_head_pair(o, tm))
    mixed = jnp.concatenate(parts, axis=1)
    gated = mixed * (gate / (1.0 + jnp.exp(-gate)))
    y = jnp.dot(gated.astype(BF16), w_ref[...], preferred_element_type=F32)
    r = alpha * x_ref[...] + y
    mu = jnp.mean(r, axis=1, keepdims=True)
    c = r - mu
    var = jnp.mean(c * c, axis=1, keepdims=True)
    o_ref[...] = c * lax.rsqrt(var + LN_EPS) * g_ref[...] + b_ref[...]


def _out_layer(x2, attn_outs, w_gm, mk, mv, w_out, ln_g, ln_b, seq, tm, alpha, name):
    m, d = x2.shape
    mem_len = mk.shape[0] // (m // seq)
    steps_per_batch = seq // tm

    def row(wd):
        return pl.BlockSpec((tm, wd), lambda i: (i, 0))

    def whole(a):
        return pl.BlockSpec(a.shape, lambda i: (0, 0))

    mem_spec = pl.BlockSpec((mem_len, mk.shape[1]), lambda i: (i // steps_per_batch, 0))
    in_specs = ([row(d)] + [row(a.shape[1]) for a in attn_outs]
                + [whole(w_gm), mem_spec, mem_spec, whole(w_out), whole(ln_g), whole(ln_b)])
    return pl.pallas_call(
        functools.partial(_out_kernel, n_attn=len(attn_outs), alpha=alpha),
        grid=(m // tm,),
        in_specs=in_specs,
        out_specs=row(d),
        out_shape=jax.ShapeDtypeStruct((m, d), F32),
        compiler_params=pltpu.CompilerParams(dimension_semantics=("parallel",),
                                             vmem_limit_bytes=VMEM_LIMIT_BYTES),
        name=name,
    )(x2, *attn_outs, w_gm, mk, mv, w_out, ln_g, ln_b)


ROW_TILE = 512
SB_BLOCK = 128
SB_ROWS_PER_STEP = 512
DIFF_BLOCK = 512
SWA_BLOCK = 128
SWA_ROWS_PER_STEP = 512


def _even_layer(x2, mem2, pos2, w_in, w_memkv, diff_lambda, diff_subln, w_out, ln_g, ln_b,
                layer_idx, batch, seq, alpha):
    lambda_init = 0.8 - 0.6 * math.exp(-0.3 * layer_idx)
    w_sb = SB_HEADS * HEAD_DIM
    w_df = DIFF_HEADS * HEAD_DIM
    w_attn = 3 * w_sb + 3 * w_df
    w_in = w_in.astype(BF16)
    sbq, sbk, sbv, dfq, dfk, dfv = _inproj(
        _inproj_even_kernel, "inproj_even", x2, pos2, _inv_freq_lanes(DIFF_QK_DIM), w_in[:, :w_attn],
        (w_sb, w_sb, w_sb, w_df, w_df, w_df), ROW_TILE)
    mk, mv = _memkv(mem2, w_memkv.astype(BF16))

    def b3(a):
        return a.reshape(batch, seq, a.shape[1])

    sb_o = _row_block_attention(functools.partial(_sb_kernel, blk=SB_BLOCK), "stick_breaking",
                                b3(sbq), b3(sbk), b3(sbv), (), (), SB_ROWS_PER_STEP)
    subln_lanes = jnp.tile(diff_subln.astype(F32), LANES // HEAD_DIM)[None, :]
    df_o = _diff_attention(b3(dfq), b3(dfk), b3(dfv), diff_lambda.astype(F32), subln_lanes,
                           DIFF_BLOCK, lambda_init)
    return _out_layer(x2, [sb_o.reshape(x2.shape[0], -1), df_o.reshape(x2.shape[0], -1)], w_in[:, w_attn:],
                      mk, mv, w_out.astype(BF16), ln_g[None, :], ln_b[None, :], seq, ROW_TILE, alpha, "out_even")


def _odd_layer(x2, mem2, pos2, w_in, w_memkv, sinks, w_out, ln_g, ln_b, batch, seq, alpha):
    w_q = SWA_Q_HEADS * HEAD_DIM
    w_kv = SWA_KV_HEADS * HEAD_DIM
    w_m = MEM_HEADS * HEAD_DIM
    def heads_in_order(a, axis, lo):
        return [lax.slice_in_dim(a, lo + h * HEAD_DIM, lo + (h + 1) * HEAD_DIM, axis=axis) for h in SW_HEAD_ORDER]

    gate_lo = w_q + 2 * w_kv + w_m
    last_kv = slice((SWA_KV_HEADS - 1) * HEAD_DIM, w_kv)
    w_in = w_in.astype(BF16)
    w_out = w_out.astype(BF16)
    k_w = w_in[:, w_q:w_q + w_kv]
    v_w = w_in[:, w_q + w_kv:w_q + 2 * w_kv]
    w_attn = jnp.concatenate(heads_in_order(w_in, 1, 0) + [k_w, k_w[:, last_kv], v_w, v_w[:, last_kv]], axis=1)
    w_gm = jnp.concatenate([w_in[:, w_q + 2 * w_kv:gate_lo]] + heads_in_order(w_in, 1, gate_lo)
                           + [w_in[:, gate_lo + w_q:]], axis=1)
    w_out_perm = jnp.concatenate(heads_in_order(w_out, 0, 0) + [w_out[w_q:]], axis=0)

    cq, ck, cv = _inproj(
        _inproj_odd_kernel, "inproj_odd", x2, pos2, _inv_freq_lanes(HEAD_DIM), w_attn,
        (w_q, 2 * LANES, 2 * LANES), ROW_TILE)
    mk, mv = _memkv(mem2, w_memkv.astype(BF16))

    def b3(a):
        return a.reshape(batch, seq, a.shape[1])

    pairs_per_kv_block = LANES // HEAD_DIM * SWA_GROUP // 2
    sink_spec = [pl.BlockSpec(memory_space=pltpu.SMEM)]
    c_o = _row_block_attention(
        functools.partial(_swa_kernel, blk=SWA_BLOCK, pairs_per_kv_block=pairs_per_kv_block),
        "sliding_window", b3(cq), b3(ck), b3(cv), (sinks.astype(F32),), sink_spec,
        SWA_ROWS_PER_STEP)
    return _out_layer(x2, [c_o.reshape(x2.shape[0], -1)], w_gm, mk, mv, w_out_perm,
                      ln_g[None, :], ln_b[None, :], seq, ROW_TILE, alpha, "out_odd")


def kernel(x, mem, positions, w_in_even, w_memkv_even, diff_lambda_even, diff_subln_even, w_out_even,
           ln_g_even, ln_b_even, w_in_odd, w_memkv_odd, sinks_odd, w_out_odd, ln_g_odd, ln_b_odd):
    batch, seq, d = x.shape
    depth = w_in_even.shape[0] + w_in_odd.shape[0]
    alpha = (2 * depth) ** 0.25
    x2 = x.reshape(batch * seq, d)
    mem2 = mem.reshape(batch * mem.shape[1], d)
    pos2 = positions.reshape(batch * seq, 1)
    for i in range(depth):
        j = i // 2
        if i % 2 == 0:
            x2 = _even_layer(x2, mem2, pos2, w_in_even[j], w_memkv_even[j], diff_lambda_even[j],
                             diff_subln_even[j], w_out_even[j], ln_g_even[j], ln_b_even[j], i,
                             batch, seq, alpha)
        else:
            x2 = _odd_layer(x2, mem2, pos2, w_in_odd[j], w_memkv_odd[j], sinks_odd[j], w_out_odd[j],
                            ln_g_odd[j], ln_b_odd[j], batch, seq, alpha)
    return x2.reshape(batch, seq, d)
```

```python
import functools
import math

import jax
import jax.numpy as jnp
import numpy as np
from jax import lax
from jax.experimental import pallas as pl
from jax.experimental.pallas import tpu as pltpu

F32 = jnp.float32
BF16 = jnp.bfloat16

HEAD_DIM = 64
LANES = 128
MEM_HEADS = 4
SB_HEADS = 6
DIFF_HEADS = 6
DIFF_QK_DIM = 32
SWA_Q_HEADS = 12
SWA_KV_HEADS = 3
SWA_GROUP = 4
WINDOW = 128
ROPE_THETA = 500000.0
ROPE_FRACTION = 4
LN_EPS = 1e-5
NEG_BIG = -1e30
VMEM_LIMIT_BYTES = 48 * 1024 * 1024

SB_EXIT_COST = 92.0
LOG2_E = math.log2(math.e)

SW_HEAD_ORDER = (0, 4, 1, 5, 2, 6, 3, 7, 8, 9, 10, 11)


def _nt_dot(a, b):
    return lax.dot_general(a, b, (((1,), (1,)), ((), ())), preferred_element_type=F32)


def _head_pair_rows(q2):
    lane = lax.broadcasted_iota(jnp.int32, q2.shape, 1)
    low = lane < HEAD_DIM
    zero = jnp.zeros_like(q2)
    return jnp.concatenate([jnp.where(low, q2, zero), jnp.where(low, zero, q2)], axis=0)


def _merge_head_pair(o, rows):
    lane = lax.broadcasted_iota(jnp.int32, (rows, LANES), 1)
    return jnp.where(lane < HEAD_DIM, o[:rows], o[rows:])


def _rope_lane_tables(group, pos_f32, inv_freq_row):
    rot = group // ROPE_FRACTION
    half = rot // 2
    lane = lax.broadcasted_iota(jnp.int32, (1, LANES), 1)
    r = lane % group
    ang = pos_f32 * inv_freq_row
    cos = jnp.cos(ang)
    sin = jnp.sin(ang)
    cos_t = jnp.where(r < rot, cos, 1.0)
    sin_first = jnp.where(r < half, -sin, 0.0)
    sin_second = jnp.where((r >= half) & (r < rot), sin, 0.0)
    return cos_t, sin_first, sin_second, half


def _apply_rope(h, tables):
    cos_t, sin_first, sin_second, half = tables
    outs = []
    for c in range(h.shape[1] // LANES):
        blk = h[:, c * LANES:(c + 1) * LANES]
        outs.append(blk * cos_t + pltpu.roll(blk, LANES - half, 1) * sin_first
                    + pltpu.roll(blk, half, 1) * sin_second)
    return jnp.concatenate(outs, axis=1)


def _inv_freq_lanes(group):
    half = group // ROPE_FRACTION // 2
    inv_freq = jnp.exp(-(jnp.arange(half, dtype=F32) / half) * math.log(ROPE_THETA))
    lane = np.arange(LANES)
    return inv_freq[(lane % group) % half][None, :]


def _inproj_even_kernel(x_ref, pos_ref, invf_ref, w_ref, sbq_ref, sbk_ref, sbv_ref,
                        dfq_ref, dfk_ref, dfv_ref):
    h = jnp.dot(x_ref[...].astype(BF16), w_ref[...], preferred_element_type=F32)
    edges = [0]

    def sec(width):
        edges.append(edges[-1] + width)
        return h[:, edges[-2]:edges[-1]]

    w_sb = SB_HEADS * HEAD_DIM
    w_df = DIFF_HEADS * 2 * DIFF_QK_DIM
    tables = _rope_lane_tables(DIFF_QK_DIM, pos_ref[...].astype(F32), invf_ref[...])
    sbq_ref[...] = (sec(w_sb) * (HEAD_DIM ** -0.5 * LOG2_E)).astype(BF16)
    sbk_ref[...] = sec(w_sb).astype(BF16)
    sbv_ref[...] = sec(w_sb).astype(BF16)
    q_scale = DIFF_QK_DIM ** -0.5 * math.log2(math.e)
    dfq_ref[...] = (_apply_rope(sec(w_df), tables) * q_scale).astype(BF16)
    dfk_ref[...] = _apply_rope(sec(w_df), tables).astype(BF16)
    dfv_ref[...] = sec(w_df).astype(BF16)


def _inproj_odd_kernel(x_ref, pos_ref, invf_ref, w_ref, cq_ref, ck_ref, cv_ref):
    h = jnp.dot(x_ref[...].astype(BF16), w_ref[...], preferred_element_type=F32)
    edges = [0]

    def sec(width):
        edges.append(edges[-1] + width)
        return h[:, edges[-2]:edges[-1]]

    w_kv = ck_ref.shape[1]
    tables = _rope_lane_tables(HEAD_DIM, pos_ref[...].astype(F32), invf_ref[...])
    cq_ref[...] = (_apply_rope(sec(SWA_Q_HEADS * HEAD_DIM), tables)
                   * (HEAD_DIM ** -0.5 * LOG2_E)).astype(BF16)
    ck_ref[...] = _apply_rope(sec(w_kv), tables).astype(BF16)
    cv_ref[...] = sec(w_kv).astype(BF16)


def _inproj(kernel_fn, name, x2, pos2, invf, w, out_widths, tm):
    m, d = x2.shape
    return pl.pallas_call(
        kernel_fn,
        grid=(m // tm,),
        in_specs=[pl.BlockSpec((tm, d), lambda i: (i, 0)),
                  pl.BlockSpec((tm, 1), lambda i: (i, 0)),
                  pl.BlockSpec((1, LANES), lambda i: (0, 0)),
                  pl.BlockSpec(w.shape, lambda i: (0, 0))],
        out_specs=[pl.BlockSpec((tm, wd), lambda i: (i, 0)) for wd in out_widths],
        out_shape=[jax.ShapeDtypeStruct((m, wd), BF16) for wd in out_widths],
        compiler_params=pltpu.CompilerParams(dimension_semantics=("parallel",),
                                             vmem_limit_bytes=VMEM_LIMIT_BYTES),
        name=name,
    )(x2, pos2, invf, w)


def _memkv_kernel(mem_ref, w_ref, mk_ref, mv_ref):
    kv = jnp.dot(mem_ref[...].astype(BF16), w_ref[...], preferred_element_type=F32)
    half = mk_ref.shape[1]
    mk_ref[...] = kv[:, :half].astype(BF16)
    mv_ref[...] = kv[:, half:].astype(BF16)


def _memkv(mem2, w):
    rows = mem2.shape[0]
    half = w.shape[1] // 2
    return pl.pallas_call(
        _memkv_kernel,
        out_shape=[jax.ShapeDtypeStruct((rows, half), BF16)] * 2,
        compiler_params=pltpu.CompilerParams(vmem_limit_bytes=VMEM_LIMIT_BYTES),
        name="memkv",
    )(mem2, w)


def _strict_lower(n):
    return (lax.broadcasted_iota(jnp.int32, (n, n), 0)
            > lax.broadcasted_iota(jnp.int32, (n, n), 1)).astype(BF16)


def _sb_tiles(chains, later):
    zs = [_nt_dot(qq, kk) for qq, kk, _, _, _ in chains]
    costs = []
    for z, (_, _, _, strict, _) in zip(zs, chains):
        cost = jnp.maximum(z, 0.0) + jnp.log(1.0 + jnp.exp2(-jnp.abs(z))) * LOG2_E
        if strict is not None:
            cost = jnp.where(strict, cost, 0.0)
        hi = cost.astype(BF16)
        lo = (cost - hi.astype(F32)).astype(BF16)
        costs.append((cost, hi, lo))
    afters = [jnp.dot(hi, later, preferred_element_type=F32) + jnp.dot(lo, later, preferred_element_type=F32)
              for _, hi, lo in costs]
    outs = []
    for z, (cost, _, _), after, (_, _, vv, strict, carry) in zip(zs, costs, afters, chains):
        w = jnp.exp2(z - (cost + after + carry))
        if strict is not None:
            w = jnp.where(strict, w, 0.0)
        outs.append((jnp.dot(w.astype(BF16), vv, preferred_element_type=F32),
                     jnp.sum(cost, axis=1, keepdims=True)))
    return outs


def _sb_kernel(q_ref, k_ref, v_ref, o_ref, *, blk):
    n_pairs = q_ref.shape[1] // LANES
    n_sub = q_ref.shape[0] // blk

    def lanes(hp):
        return slice(hp * LANES, (hp + 1) * LANES)

    rows = lax.broadcasted_iota(jnp.int32, (2 * blk, 2 * blk), 0)
    q_off = jnp.where(rows >= blk, rows - blk, rows)
    col = lax.broadcasted_iota(jnp.int32, (2 * blk, 2 * blk), 1)
    later2 = _strict_lower(2 * blk)
    later1 = _strict_lower(blk)
    no_carry = jnp.zeros((2 * blk, 1), F32)

    first = []
    for sub in range(n_sub):
        r0 = pl.multiple_of((pl.program_id(1) * n_sub + sub) * blk, blk)
        p0 = pl.multiple_of(jnp.maximum(r0 - blk, 0), blk)
        strict = (col - blk < q_off) & (col + (r0 - blk) >= 0)
        for hp in range(n_pairs):
            qq = _head_pair_rows(q_ref[sub * blk:(sub + 1) * blk, lanes(hp)])
            kk = jnp.concatenate([k_ref[pl.ds(p0, blk), lanes(hp)], k_ref[pl.ds(r0, blk), lanes(hp)]], axis=0)
            vv = jnp.concatenate([v_ref[pl.ds(p0, blk), lanes(hp)], v_ref[pl.ds(r0, blk), lanes(hp)]], axis=0)
            first.append((qq, kk, vv, strict, no_carry))
    started = _sb_tiles(first, later2)

    def finish(sub):
        bi = pl.program_id(1) * n_sub + sub
        mine = slice(sub * n_pairs, (sub + 1) * n_pairs)
        qqs = [chain[0] for chain in first[mine]]
        accs = tuple(pv for pv, _ in started[mine])
        carries = tuple(dsum for _, dsum in started[mine])

        def cond(c):
            j, carries, _ = c
            return (j >= 0) & (jnp.min(functools.reduce(jnp.minimum, carries)) < SB_EXIT_COST)

        def body(c):
            j, carries, accs = c
            start = pl.multiple_of(j * blk, blk)
            outs = _sb_tiles([(qqs[hp], k_ref[pl.ds(start, blk), lanes(hp)], v_ref[pl.ds(start, blk), lanes(hp)],
                               None, carries[hp]) for hp in range(n_pairs)], later1)
            return (j - 1, tuple(c + dsum for c, (_, dsum) in zip(carries, outs)),
                    tuple(a + pv for a, (pv, _) in zip(accs, outs)))

        _, _, accs = lax.while_loop(cond, body, (bi - 2, carries, accs))
        o_ref[sub * blk:(sub + 1) * blk, :] = jnp.concatenate(
            [_merge_head_pair(a, blk) for a in accs], axis=1).astype(o_ref.dtype)

    for sub in range(n_sub):
        finish(sub)


def _row_block_attention(kernel_fn, name, q, k, v, extra_inputs, extra_specs, blk):
    b, s, wq = q.shape
    q_spec = pl.BlockSpec((None, blk, wq), lambda bi, qi: (bi, qi, 0))
    kv_spec = pl.BlockSpec((None, s, k.shape[2]), lambda bi, qi: (bi, 0, 0))
    return pl.pallas_call(
        kernel_fn,
        grid=(b, s // blk),
        in_specs=list(extra_specs) + [q_spec, kv_spec, kv_spec],
        out_specs=q_spec,
        out_shape=jax.ShapeDtypeStruct(q.shape, BF16),
        compiler_params=pltpu.CompilerParams(
            dimension_semantics=("parallel", "arbitrary"),
            vmem_limit_bytes=VMEM_LIMIT_BYTES),
        name=name,
    )(*extra_inputs, q, k, v)


def _diff_kernel(dl_ref, subln_ref, q_ref, k_ref, v_ref, o_ref, vp_ref, sa_ref, sb_ref, sd_ref, acc_ref, *, blk, lambda_init):
    qi = pl.program_id(2)
    seq = v_ref.shape[0]

    @pl.when(qi == 0)
    def _():
        low = lax.broadcasted_iota(jnp.int32, (blk, LANES), 1) < HEAD_DIM

        def build(c, _):
            rows = pl.ds(pl.multiple_of(c * blk, blk), blk)
            vf = v_ref[rows, :].astype(F32)
            vp_ref[rows, :LANES] = jnp.where(low, vf, 1.0).astype(BF16)
            vp_ref[rows, LANES:] = jnp.where(low, pltpu.roll(vf, HEAD_DIM, 1), 1.0).astype(BF16)
            return 0

        lax.fori_loop(0, seq // blk, build, 0)

    q2 = q_ref[...]
    lane = lax.broadcasted_iota(jnp.int32, q2.shape, 1)
    zero = jnp.zeros_like(q2)
    n_maps = LANES // DIFF_QK_DIM
    qq = jnp.concatenate([jnp.where(lane // DIFF_QK_DIM == c, q2, zero) for c in range(n_maps)], axis=0)
    rows_n = n_maps * blk
    half = rows_n // 2

    wide = 2 * blk

    def wide_keys(block):
        return pl.ds(pl.multiple_of(block * wide, wide), wide)

    def narrow_keys(start):
        return pl.ds(pl.multiple_of(start, blk), blk)

    def scores(s_ref, block):
        s_ref[...] = _nt_dot(qq, k_ref[wide_keys(block), :])

    def update(s, keys, m, diagonal=False):
        if diagonal:
            q_off = lax.broadcasted_iota(jnp.int32, s.shape, 0) % blk
            k_off = lax.broadcasted_iota(jnp.int32, s.shape, 1)
            s = jnp.where(k_off <= q_off, s, NEG_BIG)
        m_new = jnp.maximum(m, jnp.max(s, axis=1, keepdims=True))
        p = jnp.exp2(s - m_new).astype(BF16)
        alpha = jnp.exp2(m - m_new)
        for h, v_lanes in enumerate((slice(0, LANES), slice(LANES, 2 * LANES))):
            rows = slice(h * half, (h + 1) * half)
            acc_ref[rows, :] = alpha[rows] * acc_ref[rows, :] + jnp.dot(
                p[rows], vp_ref[keys, v_lanes], preferred_element_type=F32)
        return m_new

    def finish():
        acc = acc_ref[...]
        dl = dl_ref[...]
        lam = (jnp.exp(jnp.sum(dl[0:1] * dl[1:2], axis=1, keepdims=True))
               - jnp.exp(jnp.sum(dl[2:3] * dl[3:4], axis=1, keepdims=True)) + lambda_init)
        low = lax.broadcasted_iota(jnp.int32, (blk, LANES), 1) < HEAD_DIM
        ones = jnp.ones((LANES, LANES), BF16)
        normed = []
        for h in range(2):
            n0 = acc[2 * h * blk:(2 * h + 1) * blk]
            n1 = acc[(2 * h + 1) * blk:(2 * h + 2) * blk]
            l0 = pltpu.roll(n0, HEAD_DIM, 1)
            l1 = pltpu.roll(n1, HEAD_DIM, 1)
            d = n0 - (lam * l0 / l1) * n1
            dsq = jnp.where(low, d * d, 0.0)
            hi = dsq.astype(BF16)
            lo = (dsq - hi.astype(F32)).astype(BF16)
            ms = (jnp.dot(hi, ones, preferred_element_type=F32)
                  + jnp.dot(lo, ones, preferred_element_type=F32)) * (1.0 / HEAD_DIM)
            normed.append(d * lax.rsqrt(ms + LN_EPS * l0 * l0))
        y = jnp.where(low, normed[0], pltpu.roll(normed[1], HEAD_DIM, 1))
        o_ref[...] = (y * subln_ref[...] * (1.0 - lambda_init)).astype(o_ref.dtype)

    n_wide = qi // 2
    acc_ref[...] = jnp.zeros(acc_ref.shape, F32)
    sd_ref[...] = _nt_dot(qq, k_ref[narrow_keys(qi * blk), :])
    scores(sa_ref, 0)
    m = update(sd_ref[...], narrow_keys(qi * blk), jnp.full((rows_n, 1), NEG_BIG, F32), diagonal=True)

    def wide_update(s_ref, block, m):
        return update(s_ref[...], wide_keys(block), m)

    def leftover_update(s_ref, block, m):
        return update(s_ref[:, :blk], narrow_keys(block * wide), m)

    def body(i, m):
        scores(sb_ref, 2 * i + 1)
        m = wide_update(sa_ref, 2 * i, m)
        scores(sa_ref, 2 * i + 2)
        return wide_update(sb_ref, 2 * i + 1, m)

    n_loop = jnp.maximum(n_wide - 1, 0) // 2
    m = lax.fori_loop(0, n_loop, body, m)
    first = 2 * n_loop
    left = n_wide - first
    odd = qi % 2

    def tail(n_left, is_odd):
        def run():
            mm = m
            if n_left >= 1:
                if n_left == 2 or is_odd:
                    scores(sb_ref, first + 1)
                mm = wide_update(sa_ref, first, mm)
            if n_left == 2:
                if is_odd:
                    scores(sa_ref, first + 2)
                mm = wide_update(sb_ref, first + 1, mm)
            if is_odd:
                leftover_update(sa_ref if n_left in (0, 2) else sb_ref, n_wide, mm)
            finish()
        pl.when(jnp.logical_and(left == n_left, odd == int(is_odd)))(run)

    for n_left in range(3):
        for is_odd in (False, True):
            tail(n_left, is_odd)


def _diff_attention(q, k, v, diff_lambda, subln_lanes, blk, lambda_init):
    b, s, wq = q.shape
    assert s % (2 * blk) == 0
    rows_n = LANES // DIFF_QK_DIM * blk
    q_spec = pl.BlockSpec((None, blk, LANES), lambda bi, hp, qi: (bi, qi, hp))
    kv_spec = pl.BlockSpec((None, s, LANES), lambda bi, hp, qi: (bi, 0, hp))
    return pl.pallas_call(
        functools.partial(_diff_kernel, blk=blk, lambda_init=lambda_init),
        grid=(b, wq // LANES, s // blk),
        in_specs=[pl.BlockSpec(diff_lambda.shape, lambda bi, hp, qi: (0, 0)),
                  pl.BlockSpec((1, LANES), lambda bi, hp, qi: (0, 0)), q_spec, kv_spec, kv_spec],
        out_specs=q_spec,
        out_shape=jax.ShapeDtypeStruct(q.shape, BF16),
        scratch_shapes=[pltpu.VMEM((s, 2 * LANES), BF16),
                        pltpu.VMEM((rows_n, 2 * blk), F32),
                        pltpu.VMEM((rows_n, 2 * blk), F32),
                        pltpu.VMEM((rows_n, blk), F32),
                        pltpu.VMEM((rows_n, LANES), F32)],
        compiler_params=pltpu.CompilerParams(
            dimension_semantics=("parallel", "parallel", "arbitrary"),
            vmem_limit_bytes=VMEM_LIMIT_BYTES),
        name="differential",
    )(diff_lambda, subln_lanes, q, k, v)


def _swa_kernel(sinks_ref, q_ref, k_ref, v_ref, o_ref, *, blk, pairs_per_kv_block):
    n_sub = q_ref.shape[0] // blk

    def lanes(hb):
        return slice(hb * LANES, (hb + 1) * LANES)

    shape = (2 * blk, WINDOW + blk)
    rows = lax.broadcasted_iota(jnp.int32, shape, 0)
    q_off = jnp.where(rows >= blk, rows - blk, rows)
    col = lax.broadcasted_iota(jnp.int32, shape, 1)
    rel = (col - WINDOW) - q_off
    in_window = (rel <= 0) & (rel > -WINDOW)
    second_head = lax.broadcasted_iota(jnp.int32, (2 * blk, 1), 0) >= blk
    n_blocks = q_ref.shape[1] // LANES
    chains = []
    for sub in range(n_sub):
        r0 = pl.multiple_of((pl.program_id(1) * n_sub + sub) * blk, blk)
        p0 = pl.multiple_of(jnp.maximum(r0 - WINDOW, 0), WINDOW)
        band = in_window & (col + (r0 - WINDOW) >= 0)
        for hb in range(n_blocks):
            kvb = hb // pairs_per_kv_block
            kk = jnp.concatenate([k_ref[pl.ds(p0, WINDOW), lanes(kvb)], k_ref[pl.ds(r0, blk), lanes(kvb)]], axis=0)
            vv = jnp.concatenate([v_ref[pl.ds(p0, WINDOW), lanes(kvb)], v_ref[pl.ds(r0, blk), lanes(kvb)]], axis=0)
            qq = _head_pair_rows(q_ref[sub * blk:(sub + 1) * blk, lanes(hb)])
            sink = jnp.where(second_head, sinks_ref[SW_HEAD_ORDER[2 * hb + 1]],
                             sinks_ref[SW_HEAD_ORDER[2 * hb]]) * LOG2_E
            chains.append((qq, kk, vv, band, sink))
    scores = [jnp.where(band, _nt_dot(qq, kk), NEG_BIG) for qq, kk, _, band, _ in chains]
    probs = []
    for s, (_, _, _, _, sink) in zip(scores, chains):
        m = jnp.maximum(jnp.max(s, axis=1, keepdims=True), sink)
        p = jnp.exp2(s - m)
        probs.append((p.astype(BF16), jnp.sum(p, axis=1, keepdims=True) + jnp.exp2(sink - m)))
    outs = [_merge_head_pair(jnp.dot(p, vv, preferred_element_type=F32) / denom, blk)
            for (p, denom), (_, _, vv, _, _) in zip(probs, chains)]
    for sub in range(n_sub):
        o_ref[sub * blk:(sub + 1) * blk, :] = jnp.concatenate(
            outs[sub * n_blocks:(sub + 1) * n_blocks], axis=1).astype(o_ref.dtype)


def _out_kernel(*refs, n_attn, alpha):
    x_ref = refs[0]
    attn_refs = refs[1:1 + n_attn]
    wgm_ref, mk_ref, mv_ref, w_ref, g_ref, b_ref, o_ref = refs[1 + n_attn:]
    tm = x_ref.shape[0]
    w_m = MEM_HEADS * HEAD_DIM

    h = jnp.dot(x_ref[...].astype(BF16), wgm_ref[...], preferred_element_type=F32)
    mq = (h[:, :w_m] * (HEAD_DIM ** -0.5 * LOG2_E)).astype(BF16)
    gate = h[:, w_m:]
    parts = [r[...].astype(F32) for r in attn_refs]
    blocks = [slice(hb * LANES, (hb + 1) * LANES) for hb in range(w_m // LANES)]
    scores = [_nt_dot(_head_pair_rows(mq[:, c]), mk_ref[:, c]) for c in blocks]
    probs = [jnp.exp2(s - jnp.max(s, axis=1, keepdims=True)) for s in scores]
    for p, c in zip(probs, blocks):
        o = jnp.dot(p.astype(BF16), mv_ref[:, c], preferred_element_type=F32) / jnp.sum(p, axis=1, keepdims=True)
        parts.append(_merge_head_pair(o, tm))
    mixed = jnp.concatenate(parts, axis=1)
    gated = mixed * (gate / (1.0 + jnp.exp(-gate)))
    y = jnp.dot(gated.astype(BF16), w_ref[...], preferred_element_type=F32)
    r = alpha * x_ref[...] + y
    mu = jnp.mean(r, axis=1, keepdims=True)
    c = r - mu
    var = jnp.mean(c * c, axis=1, keepdims=True)
    o_ref[...] = c * lax.rsqrt(var + LN_EPS) * g_ref[...] + b_ref[...]


def _out_layer(x2, attn_outs, w_gm, mk, mv, w_out, ln_g, ln_b, seq, tm, alpha, name):
    m, d = x2.shape
    mem_len = mk.shape[0] // (m // seq)
    steps_per_batch = seq // tm

    def row(wd):
        return pl.BlockSpec((tm, wd), lambda i: (i, 0))

    def whole(a):
        return pl.BlockSpec(a.shape, lambda i: (0, 0))

    mem_spec = pl.BlockSpec((mem_len, mk.shape[1]), lambda i: (i // steps_per_batch, 0))
    in_specs = ([row(d)] + [row(a.shape[1]) for a in attn_outs]
                + [whole(w_gm), mem_spec, mem_spec, whole(w_out), whole(ln_g), whole(ln_b)])
    return pl.pallas_call(
        functools.partial(_out_kernel, n_attn=len(attn_outs), alpha=alpha),
        grid=(m // tm,),
        in_specs=in_specs,
        out_specs=row(d),
        out_shape=jax.ShapeDtypeStruct((m, d), F32),
        compiler_params=pltpu.CompilerParams(dimension_semantics=("parallel",),
                                             vmem_limit_bytes=VMEM_LIMIT_BYTES),
        name=name,
    )(x2, *attn_outs, w_gm, mk, mv, w_out, ln_g, ln_b)


ROW_TILE = 512
IN_ROW_TILE = 1024
SB_BLOCK = 128
SB_ROWS_PER_STEP = 512
DIFF_BLOCK = 512
SWA_BLOCK = 128
SWA_ROWS_PER_STEP = 512


def _even_layer(x2, mem2, pos2, w_in, w_memkv, diff_lambda, diff_subln, w_out, ln_g, ln_b,
                layer_idx, batch, seq, alpha):
    lambda_init = 0.8 - 0.6 * math.exp(-0.3 * layer_idx)
    w_sb = SB_HEADS * HEAD_DIM
    w_df = DIFF_HEADS * HEAD_DIM
    w_attn = 3 * w_sb + 3 * w_df
    w_in = w_in.astype(BF16)
    sbq, sbk, sbv, dfq, dfk, dfv = _inproj(
        _inproj_even_kernel, "inproj_even", x2, pos2, _inv_freq_lanes(DIFF_QK_DIM), w_in[:, :w_attn],
        (w_sb, w_sb, w_sb, w_df, w_df, w_df), IN_ROW_TILE)
    mk, mv = _memkv(mem2, w_memkv.astype(BF16))

    def b3(a):
        return a.reshape(batch, seq, a.shape[1])

    sb_o = _row_block_attention(functools.partial(_sb_kernel, blk=SB_BLOCK), "stick_breaking",
                                b3(sbq), b3(sbk), b3(sbv), (), (), SB_ROWS_PER_STEP)
    subln_lanes = jnp.tile(diff_subln.astype(F32), LANES // HEAD_DIM)[None, :]
    df_o = _diff_attention(b3(dfq), b3(dfk), b3(dfv), diff_lambda.astype(F32), subln_lanes,
                           DIFF_BLOCK, lambda_init)
    return _out_layer(x2, [sb_o.reshape(x2.shape[0], -1), df_o.reshape(x2.shape[0], -1)], w_in[:, w_attn:],
                      mk, mv, w_out.astype(BF16), ln_g[None, :], ln_b[None, :], seq, ROW_TILE, alpha, "out_even")


def _odd_layer(x2, mem2, pos2, w_in, w_memkv, sinks, w_out, ln_g, ln_b, batch, seq, alpha):
    w_q = SWA_Q_HEADS * HEAD_DIM
    w_kv = SWA_KV_HEADS * HEAD_DIM
    w_m = MEM_HEADS * HEAD_DIM
    def heads_in_order(a, axis, lo):
        return [lax.slice_in_dim(a, lo + h * HEAD_DIM, lo + (h + 1) * HEAD_DIM, axis=axis) for h in SW_HEAD_ORDER]

    gate_lo = w_q + 2 * w_kv + w_m
    last_kv = slice((SWA_KV_HEADS - 1) * HEAD_DIM, w_kv)
    w_in = w_in.astype(BF16)
    w_out = w_out.astype(BF16)
    k_w = w_in[:, w_q:w_q + w_kv]
    v_w = w_in[:, w_q + w_kv:w_q + 2 * w_kv]
    w_attn = jnp.concatenate(heads_in_order(w_in, 1, 0) + [k_w, k_w[:, last_kv], v_w, v_w[:, last_kv]], axis=1)
    w_gm = jnp.concatenate([w_in[:, w_q + 2 * w_kv:gate_lo]] + heads_in_order(w_in, 1, gate_lo)
                           + [w_in[:, gate_lo + w_q:]], axis=1)
    w_out_perm = jnp.concatenate(heads_in_order(w_out, 0, 0) + [w_out[w_q:]], axis=0)

    cq, ck, cv = _inproj(
        _inproj_odd_kernel, "inproj_odd", x2, pos2, _inv_freq_lanes(HEAD_DIM), w_attn,
        (w_q, 2 * LANES, 2 * LANES), IN_ROW_TILE)
    mk, mv = _memkv(mem2, w_memkv.astype(BF16))

    def b3(a):
        return a.reshape(batch, seq, a.shape[1])

    pairs_per_kv_block = LANES // HEAD_DIM * SWA_GROUP // 2
    sink_spec = [pl.BlockSpec(memory_space=pltpu.SMEM)]
    c_o = _row_block_attention(
        functools.partial(_swa_kernel, blk=SWA_BLOCK, pairs_per_kv_block=pairs_per_kv_block),
        "sliding_window", b3(cq), b3(ck), b3(cv), (sinks.astype(F32),), sink_spec,
        SWA_ROWS_PER_STEP)
    return _out_layer(x2, [c_o.reshape(x2.shape[0], -1)], w_gm, mk, mv, w_out_perm,
                      ln_g[None, :], ln_b[None, :], seq, ROW_TILE, alpha, "out_odd")


def kernel(x, mem, positions, w_in_even, w_memkv_even, diff_lambda_even, diff_subln_even, w_out_even,
           ln_g_even, ln_b_even, w_in_odd, w_memkv_odd, sinks_odd, w_out_odd, ln_g_odd, ln_b_odd):
    batch, seq, d = x.shape
    depth = w_in_even.shape[0] + w_in_odd.shape[0]
    alpha = (2 * depth) ** 0.25
    x2 = x.reshape(batch * seq, d)
    mem2 = mem.reshape(batch * mem.shape[1], d)
    pos2 = positions.reshape(batch * seq, 1)
    for i in range(depth):
        j = i // 2
        if i % 2 == 0:
            x2 = _even_layer(x2, mem2, pos2, w_in_even[j], w_memkv_even[j], diff_lambda_even[j],
                             diff_subln_even[j], w_out_even[j], ln_g_even[j], ln_b_even[j], i,
                             batch, seq, alpha)
        else:
            x2 = _odd_layer(x2, mem2, pos2, w_in_odd[j], w_memkv_odd[j], sinks_odd[j], w_out_odd[j],
                            ln_g_odd[j], ln_b_odd[j], batch, seq, alpha)
    return x2.reshape(batch, seq, d)
```

```python
import functools
import math

import jax
import jax.numpy as jnp
import numpy as np
from jax import lax
from jax.experimental import pallas as pl
from jax.experimental.pallas import tpu as pltpu

F32 = jnp.float32
BF16 = jnp.bfloat16

HEAD_DIM = 64
LANES = 128
MEM_HEADS = 4
SB_HEADS = 6
DIFF_HEADS = 6
DIFF_QK_DIM = 32
SWA_Q_HEADS = 12
SWA_KV_HEADS = 3
SWA_GROUP = 4
WINDOW = 128
ROPE_THETA = 500000.0
ROPE_FRACTION = 4
LN_EPS = 1e-5
NEG_BIG = -1e30
VMEM_LIMIT_BYTES = 48 * 1024 * 1024

SB_EXIT_COST = 92.0
LOG2_E = math.log2(math.e)

SW_HEAD_ORDER = (0, 4, 1, 5, 2, 6, 3, 7, 8, 9, 10, 11)


def _nt_dot(a, b):
    return lax.dot_general(a, b, (((1,), (1,)), ((), ())), preferred_element_type=F32)


def _head_pair_rows(q2):
    lane = lax.broadcasted_iota(jnp.int32, q2.shape, 1)
    low = lane < HEAD_DIM
    zero = jnp.zeros_like(q2)
    return jnp.concatenate([jnp.where(low, q2, zero), jnp.where(low, zero, q2)], axis=0)


def _merge_head_pair(o, rows):
    lane = lax.broadcasted_iota(jnp.int32, (rows, LANES), 1)
    return jnp.where(lane < HEAD_DIM, o[:rows], o[rows:])


def _rope_lane_tables(group, pos_f32, inv_freq_row):
    rot = group // ROPE_FRACTION
    half = rot // 2
    lane = lax.broadcasted_iota(jnp.int32, (1, LANES), 1)
    r = lane % group
    ang = pos_f32 * inv_freq_row
    cos = jnp.cos(ang)
    sin = jnp.sin(ang)
    cos_t = jnp.where(r < rot, cos, 1.0)
    sin_first = jnp.where(r < half, -sin, 0.0)
    sin_second = jnp.where((r >= half) & (r < rot), sin, 0.0)
    return cos_t, sin_first, sin_second, half


def _apply_rope(h, tables):
    cos_t, sin_first, sin_second, half = tables
    outs = []
    for c in range(h.shape[1] // LANES):
        blk = h[:, c * LANES:(c + 1) * LANES]
        outs.append(blk * cos_t + pltpu.roll(blk, LANES - half, 1) * sin_first
                    + pltpu.roll(blk, half, 1) * sin_second)
    return jnp.concatenate(outs, axis=1)


def _inv_freq_lanes(group):
    half = group // ROPE_FRACTION // 2
    inv_freq = jnp.exp(-(jnp.arange(half, dtype=F32) / half) * math.log(ROPE_THETA))
    lane = np.arange(LANES)
    return inv_freq[(lane % group) % half][None, :]


def _projected_sub_tiles(x_ref, w_ref, n_sub):
    ts = x_ref.shape[0] // n_sub
    subs = [slice(i * ts, (i + 1) * ts) for i in range(n_sub)]
    return [(r, jnp.dot(x_ref[r, :].astype(BF16), w_ref[...], preferred_element_type=F32)) for r in subs]


def _sections(h):
    edges = [0]

    def sec(width):
        edges.append(edges[-1] + width)
        return h[:, edges[-2]:edges[-1]]

    return sec


def _inproj_even_kernel(x_ref, pos_ref, invf_ref, w_ref, sbq_ref, sbk_ref, sbv_ref,
                        dfq_ref, dfk_ref, dfv_ref, *, n_sub):
    w_sb = SB_HEADS * HEAD_DIM
    w_df = DIFF_HEADS * 2 * DIFF_QK_DIM
    q_scale = DIFF_QK_DIM ** -0.5 * LOG2_E
    for r, h in _projected_sub_tiles(x_ref, w_ref, n_sub):
        sec = _sections(h)
        tables = _rope_lane_tables(DIFF_QK_DIM, pos_ref[r, :].astype(F32), invf_ref[...])
        sbq_ref[r, :] = (sec(w_sb) * (HEAD_DIM ** -0.5 * LOG2_E)).astype(BF16)
        sbk_ref[r, :] = sec(w_sb).astype(BF16)
        sbv_ref[r, :] = sec(w_sb).astype(BF16)
        dfq_ref[r, :] = (_apply_rope(sec(w_df), tables) * q_scale).astype(BF16)
        dfk_ref[r, :] = _apply_rope(sec(w_df), tables).astype(BF16)
        dfv_ref[r, :] = sec(w_df).astype(BF16)


def _inproj_odd_kernel(x_ref, pos_ref, invf_ref, w_ref, cq_ref, ck_ref, cv_ref, *, n_sub):
    w_kv = ck_ref.shape[1]
    for r, h in _projected_sub_tiles(x_ref, w_ref, n_sub):
        sec = _sections(h)
        tables = _rope_lane_tables(HEAD_DIM, pos_ref[r, :].astype(F32), invf_ref[...])
        cq_ref[r, :] = (_apply_rope(sec(SWA_Q_HEADS * HEAD_DIM), tables)
                        * (HEAD_DIM ** -0.5 * LOG2_E)).astype(BF16)
        ck_ref[r, :] = _apply_rope(sec(w_kv), tables).astype(BF16)
        cv_ref[r, :] = sec(w_kv).astype(BF16)


def _inproj(kernel_fn, name, x2, pos2, invf, w, out_widths, tm):
    m, d = x2.shape
    return pl.pallas_call(
        kernel_fn,
        grid=(m // tm,),
        in_specs=[pl.BlockSpec((tm, d), lambda i: (i, 0)),
                  pl.BlockSpec((tm, 1), lambda i: (i, 0)),
                  pl.BlockSpec((1, LANES), lambda i: (0, 0)),
                  pl.BlockSpec(w.shape, lambda i: (0, 0))],
        out_specs=[pl.BlockSpec((tm, wd), lambda i: (i, 0)) for wd in out_widths],
        out_shape=[jax.ShapeDtypeStruct((m, wd), BF16) for wd in out_widths],
        compiler_params=pltpu.CompilerParams(dimension_semantics=("parallel",),
                                             vmem_limit_bytes=VMEM_LIMIT_BYTES),
        name=name,
    )(x2, pos2, invf, w)


def _memkv_kernel(mem_ref, w_ref, mk_ref, mv_ref):
    kv = jnp.dot(mem_ref[...].astype(BF16), w_ref[...], preferred_element_type=F32)
    half = mk_ref.shape[1]
    mk_ref[...] = kv[:, :half].astype(BF16)
    mv_ref[...] = kv[:, half:].astype(BF16)


def _memkv(mem2, w):
    rows = mem2.shape[0]
    half = w.shape[1] // 2
    return pl.pallas_call(
        _memkv_kernel,
        out_shape=[jax.ShapeDtypeStruct((rows, half), BF16)] * 2,
        compiler_params=pltpu.CompilerParams(vmem_limit_bytes=VMEM_LIMIT_BYTES),
        name="memkv",
    )(mem2, w)


def _strict_lower(n):
    return (lax.broadcasted_iota(jnp.int32, (n, n), 0)
            > lax.broadcasted_iota(jnp.int32, (n, n), 1)).astype(BF16)


def _sb_tiles(chains, later):
    zs = [_nt_dot(qq, kk) for qq, kk, _, _, _ in chains]
    costs = []
    for z, (_, _, _, strict, _) in zip(zs, chains):
        cost = jnp.maximum(z, 0.0) + jnp.log(1.0 + jnp.exp2(-jnp.abs(z))) * LOG2_E
        if strict is not None:
            cost = jnp.where(strict, cost, 0.0)
        hi = cost.astype(BF16)
        lo = (cost - hi.astype(F32)).astype(BF16)
        costs.append((cost, hi, lo))
    afters = [jnp.dot(hi, later, preferred_element_type=F32) + jnp.dot(lo, later, preferred_element_type=F32)
              for _, hi, lo in costs]
    outs = []
    for z, (cost, _, _), after, (_, _, vv, strict, carry) in zip(zs, costs, afters, chains):
        w = jnp.exp2(z - (cost + after + carry))
        if strict is not None:
            w = jnp.where(strict, w, 0.0)
        outs.append((jnp.dot(w.astype(BF16), vv, preferred_element_type=F32),
                     jnp.sum(cost, axis=1, keepdims=True)))
    return outs


def _sb_kernel(q_ref, k_ref, v_ref, o_ref, *, blk):
    n_pairs = q_ref.shape[1] // LANES
    n_sub = q_ref.shape[0] // blk

    def lanes(hp):
        return slice(hp * LANES, (hp + 1) * LANES)

    rows = lax.broadcasted_iota(jnp.int32, (2 * blk, 2 * blk), 0)
    q_off = jnp.where(rows >= blk, rows - blk, rows)
    col = lax.broadcasted_iota(jnp.int32, (2 * blk, 2 * blk), 1)
    later2 = _strict_lower(2 * blk)
    later1 = _strict_lower(blk)
    no_carry = jnp.zeros((2 * blk, 1), F32)

    first = []
    for sub in range(n_sub):
        r0 = pl.multiple_of((pl.program_id(1) * n_sub + sub) * blk, blk)
        p0 = pl.multiple_of(jnp.maximum(r0 - blk, 0), blk)
        strict = (col - blk < q_off) & (col + (r0 - blk) >= 0)
        for hp in range(n_pairs):
            qq = _head_pair_rows(q_ref[sub * blk:(sub + 1) * blk, lanes(hp)])
            kk = jnp.concatenate([k_ref[pl.ds(p0, blk), lanes(hp)], k_ref[pl.ds(r0, blk), lanes(hp)]], axis=0)
            vv = jnp.concatenate([v_ref[pl.ds(p0, blk), lanes(hp)], v_ref[pl.ds(r0, blk), lanes(hp)]], axis=0)
            first.append((qq, kk, vv, strict, no_carry))
    started = _sb_tiles(first, later2)

    def finish(sub):
        bi = pl.program_id(1) * n_sub + sub
        mine = slice(sub * n_pairs, (sub + 1) * n_pairs)
        qqs = [chain[0] for chain in first[mine]]
        accs = tuple(pv for pv, _ in started[mine])
        carries = tuple(dsum for _, dsum in started[mine])

        def cond(c):
            j, carries, _ = c
            return (j >= 0) & (jnp.min(functools.reduce(jnp.minimum, carries)) < SB_EXIT_COST)

        def body(c):
            j, carries, accs = c
            start = pl.multiple_of(j * blk, blk)
            outs = _sb_tiles([(qqs[hp], k_ref[pl.ds(start, blk), lanes(hp)], v_ref[pl.ds(start, blk), lanes(hp)],
                               None, carries[hp]) for hp in range(n_pairs)], later1)
            return (j - 1, tuple(c + dsum for c, (_, dsum) in zip(carries, outs)),
                    tuple(a + pv for a, (pv, _) in zip(accs, outs)))

        _, _, accs = lax.while_loop(cond, body, (bi - 2, carries, accs))
        o_ref[sub * blk:(sub + 1) * blk, :] = jnp.concatenate(
            [_merge_head_pair(a, blk) for a in accs], axis=1).astype(o_ref.dtype)

    for sub in range(n_sub):
        finish(sub)


def _row_block_attention(kernel_fn, name, q, k, v, extra_inputs, extra_specs, blk):
    b, s, wq = q.shape
    q_spec = pl.BlockSpec((None, blk, wq), lambda bi, qi: (bi, qi, 0))
    kv_spec = pl.BlockSpec((None, s, k.shape[2]), lambda bi, qi: (bi, 0, 0))
    return pl.pallas_call(
        kernel_fn,
        grid=(b, s // blk),
        in_specs=list(extra_specs) + [q_spec, kv_spec, kv_spec],
        out_specs=q_spec,
        out_shape=jax.ShapeDtypeStruct(q.shape, BF16),
        compiler_params=pltpu.CompilerParams(
            dimension_semantics=("parallel", "arbitrary"),
            vmem_limit_bytes=VMEM_LIMIT_BYTES),
        name=name,
    )(*extra_inputs, q, k, v)


def _diff_kernel(dl_ref, subln_ref, q_ref, k_ref, v_ref, o_ref, vp_ref, sa_ref, sb_ref, sd_ref, acc_ref, *, blk, lambda_init):
    qi = pl.program_id(2)
    seq = v_ref.shape[0]

    @pl.when(qi == 0)
    def _():
        low = lax.broadcasted_iota(jnp.int32, (blk, LANES), 1) < HEAD_DIM

        def build(c, _):
            rows = pl.ds(pl.multiple_of(c * blk, blk), blk)
            vf = v_ref[rows, :].astype(F32)
            vp_ref[rows, :LANES] = jnp.where(low, vf, 1.0).astype(BF16)
            vp_ref[rows, LANES:] = jnp.where(low, pltpu.roll(vf, HEAD_DIM, 1), 1.0).astype(BF16)
            return 0

        lax.fori_loop(0, seq // blk, build, 0)

    q2 = q_ref[...]
    lane = lax.broadcasted_iota(jnp.int32, q2.shape, 1)
    zero = jnp.zeros_like(q2)
    n_maps = LANES // DIFF_QK_DIM
    qq = jnp.concatenate([jnp.where(lane // DIFF_QK_DIM == c, q2, zero) for c in range(n_maps)], axis=0)
    rows_n = n_maps * blk
    half = rows_n // 2

    wide = 2 * blk

    def wide_keys(block):
        return pl.ds(pl.multiple_of(block * wide, wide), wide)

    def narrow_keys(start):
        return pl.ds(pl.multiple_of(start, blk), blk)

    def scores(s_ref, block):
        s_ref[...] = _nt_dot(qq, k_ref[wide_keys(block), :])

    def update(s, keys, m, diagonal=False):
        if diagonal:
            q_off = lax.broadcasted_iota(jnp.int32, s.shape, 0) % blk
            k_off = lax.broadcasted_iota(jnp.int32, s.shape, 1)
            s = jnp.where(k_off <= q_off, s, NEG_BIG)
        m_new = jnp.maximum(m, jnp.max(s, axis=1, keepdims=True))
        p = jnp.exp2(s - m_new).astype(BF16)
        alpha = jnp.exp2(m - m_new)
        for h, v_lanes in enumerate((slice(0, LANES), slice(LANES, 2 * LANES))):
            rows = slice(h * half, (h + 1) * half)
            acc_ref[rows, :] = alpha[rows] * acc_ref[rows, :] + jnp.dot(
                p[rows], vp_ref[keys, v_lanes], preferred_element_type=F32)
        return m_new

    def finish():
        acc = acc_ref[...]
        dl = dl_ref[...]
        lam = (jnp.exp(jnp.sum(dl[0:1] * dl[1:2], axis=1, keepdims=True))
               - jnp.exp(jnp.sum(dl[2:3] * dl[3:4], axis=1, keepdims=True)) + lambda_init)
        low = lax.broadcasted_iota(jnp.int32, (blk, LANES), 1) < HEAD_DIM
        ones = jnp.ones((LANES, LANES), BF16)
        normed = []
        for h in range(2):
            n0 = acc[2 * h * blk:(2 * h + 1) * blk]
            n1 = acc[(2 * h + 1) * blk:(2 * h + 2) * blk]
            l0 = pltpu.roll(n0, HEAD_DIM, 1)
            l1 = pltpu.roll(n1, HEAD_DIM, 1)
            d = n0 - (lam * l0 / l1) * n1
            dsq = jnp.where(low, d * d, 0.0)
            hi = dsq.astype(BF16)
            lo = (dsq - hi.astype(F32)).astype(BF16)
            ms = (jnp.dot(hi, ones, preferred_element_type=F32)
                  + jnp.dot(lo, ones, preferred_element_type=F32)) * (1.0 / HEAD_DIM)
            normed.append(d * lax.rsqrt(ms + LN_EPS * l0 * l0))
        y = jnp.where(low, normed[0], pltpu.roll(normed[1], HEAD_DIM, 1))
        o_ref[...] = (y * subln_ref[...] * (1.0 - lambda_init)).astype(o_ref.dtype)

    n_wide = qi // 2
    acc_ref[...] = jnp.zeros(acc_ref.shape, F32)
    sd_ref[...] = _nt_dot(qq, k_ref[narrow_keys(qi * blk), :])
    scores(sa_ref, 0)
    m = update(sd_ref[...], narrow_keys(qi * blk), jnp.full((rows_n, 1), NEG_BIG, F32), diagonal=True)

    def wide_update(s_ref, block, m):
        return update(s_ref[...], wide_keys(block), m)

    def leftover_update(s_ref, block, m):
        return update(s_ref[:, :blk], narrow_keys(block * wide), m)

    def body(i, m):
        scores(sb_ref, 2 * i + 1)
        m = wide_update(sa_ref, 2 * i, m)
        scores(sa_ref, 2 * i + 2)
        return wide_update(sb_ref, 2 * i + 1, m)

    n_loop = jnp.maximum(n_wide - 1, 0) // 2
    m = lax.fori_loop(0, n_loop, body, m)
    first = 2 * n_loop
    left = n_wide - first
    odd = qi % 2

    def tail(n_left, is_odd):
        def run():
            mm = m
            if n_left >= 1:
                if n_left == 2 or is_odd:
                    scores(sb_ref, first + 1)
                mm = wide_update(sa_ref, first, mm)
            if n_left == 2:
                if is_odd:
                    scores(sa_ref, first + 2)
                mm = wide_update(sb_ref, first + 1, mm)
            if is_odd:
                leftover_update(sa_ref if n_left in (0, 2) else sb_ref, n_wide, mm)
            finish()
        pl.when(jnp.logical_and(left == n_left, odd == int(is_odd)))(run)

    for n_left in range(3):
        for is_odd in (False, True):
            tail(n_left, is_odd)


def _diff_attention(q, k, v, diff_lambda, subln_lanes, blk, lambda_init):
    b, s, wq = q.shape
    assert s % (2 * blk) == 0
    rows_n = LANES // DIFF_QK_DIM * blk
    q_spec = pl.BlockSpec((None, blk, LANES), lambda bi, hp, qi: (bi, qi, hp))
    kv_spec = pl.BlockSpec((None, s, LANES), lambda bi, hp, qi: (bi, 0, hp))
    return pl.pallas_call(
        functools.partial(_diff_kernel, blk=blk, lambda_init=lambda_init),
        grid=(b, wq // LANES, s // blk),
        in_specs=[pl.BlockSpec(diff_lambda.shape, lambda bi, hp, qi: (0, 0)),
                  pl.BlockSpec((1, LANES), lambda bi, hp, qi: (0, 0)), q_spec, kv_spec, kv_spec],
        out_specs=q_spec,
        out_shape=jax.ShapeDtypeStruct(q.shape, BF16),
        scratch_shapes=[pltpu.VMEM((s, 2 * LANES), BF16),
                        pltpu.VMEM((rows_n, 2 * blk), F32),
                        pltpu.VMEM((rows_n, 2 * blk), F32),
                        pltpu.VMEM((rows_n, blk), F32),
                        pltpu.VMEM((rows_n, LANES), F32)],
        compiler_params=pltpu.CompilerParams(
            dimension_semantics=("parallel", "parallel", "arbitrary"),
            vmem_limit_bytes=VMEM_LIMIT_BYTES),
        name="differential",
    )(diff_lambda, subln_lanes, q, k, v)


def _swa_kernel(sinks_ref, q_ref, k_ref, v_ref, o_ref, *, blk, pairs_per_kv_block):
    n_sub = q_ref.shape[0] // blk

    def lanes(hb):
        return slice(hb * LANES, (hb + 1) * LANES)

    shape = (2 * blk, WINDOW + blk)
    rows = lax.broadcasted_iota(jnp.int32, shape, 0)
    q_off = jnp.where(rows >= blk, rows - blk, rows)
    col = lax.broadcasted_iota(jnp.int32, shape, 1)
    rel = (col - WINDOW) - q_off
    in_window = (rel <= 0) & (rel > -WINDOW)
    second_head = lax.broadcasted_iota(jnp.int32, (2 * blk, 1), 0) >= blk
    n_blocks = q_ref.shape[1] // LANES
    chains = []
    for sub in range(n_sub):
        r0 = pl.multiple_of((pl.program_id(1) * n_sub + sub) * blk, blk)
        p0 = pl.multiple_of(jnp.maximum(r0 - WINDOW, 0), WINDOW)
        band = in_window & (col + (r0 - WINDOW) >= 0)
        for hb in range(n_blocks):
            kvb = hb // pairs_per_kv_block
            kk = jnp.concatenate([k_ref[pl.ds(p0, WINDOW), lanes(kvb)], k_ref[pl.ds(r0, blk), lanes(kvb)]], axis=0)
            vv = jnp.concatenate([v_ref[pl.ds(p0, WINDOW), lanes(kvb)], v_ref[pl.ds(r0, blk), lanes(kvb)]], axis=0)
            qq = _head_pair_rows(q_ref[sub * blk:(sub + 1) * blk, lanes(hb)])
            sink = jnp.where(second_head, sinks_ref[SW_HEAD_ORDER[2 * hb + 1]],
                             sinks_ref[SW_HEAD_ORDER[2 * hb]]) * LOG2_E
            chains.append((qq, kk, vv, band, sink))
    scores = [jnp.where(band, _nt_dot(qq, kk), NEG_BIG) for qq, kk, _, band, _ in chains]
    probs = []
    for s, (_, _, _, _, sink) in zip(scores, chains):
        m = jnp.maximum(jnp.max(s, axis=1, keepdims=True), sink)
        p = jnp.exp2(s - m)
        probs.append((p.astype(BF16), jnp.sum(p, axis=1, keepdims=True) + jnp.exp2(sink - m)))
    outs = [_merge_head_pair(jnp.dot(p, vv, preferred_element_type=F32) / denom, blk)
            for (p, denom), (_, _, vv, _, _) in zip(probs, chains)]
    for sub in range(n_sub):
        o_ref[sub * blk:(sub + 1) * blk, :] = jnp.concatenate(
            outs[sub * n_blocks:(sub + 1) * n_blocks], axis=1).astype(o_ref.dtype)


def _out_kernel(*refs, n_attn, alpha, n_sub):
    x_ref = refs[0]
    attn_refs = refs[1:1 + n_attn]
    wgm_ref, mk_ref, mv_ref, w_ref, g_ref, b_ref, o_ref = refs[1 + n_attn:]
    ts = x_ref.shape[0] // n_sub
    subs = [slice(i * ts, (i + 1) * ts) for i in range(n_sub)]
    w_m = MEM_HEADS * HEAD_DIM
    blocks = [slice(hb * LANES, (hb + 1) * LANES) for hb in range(w_m // LANES)]

    hs = [jnp.dot(x_ref[r, :].astype(BF16), wgm_ref[...], preferred_element_type=F32) for r in subs]
    mqs = [(h[:, :w_m] * (HEAD_DIM ** -0.5 * LOG2_E)).astype(BF16) for h in hs]
    scores = [[_nt_dot(_head_pair_rows(mq[:, c]), mk_ref[:, c]) for c in blocks] for mq in mqs]
    probs = [[jnp.exp2(s - jnp.max(s, axis=1, keepdims=True)) for s in row] for row in scores]
    mems = [[_merge_head_pair(jnp.dot(p.astype(BF16), mv_ref[:, c], preferred_element_type=F32)
                              / jnp.sum(p, axis=1, keepdims=True), ts) for p, c in zip(row, blocks)]
            for row in probs]
    gated = []
    for h, r, mem in zip(hs, subs, mems):
        gate = h[:, w_m:]
        mixed = jnp.concatenate([a[r, :].astype(F32) for a in attn_refs] + mem, axis=1)
        gated.append((mixed * (gate / (1.0 + jnp.exp(-gate)))).astype(BF16))
    ys = [jnp.dot(g, w_ref[...], preferred_element_type=F32) for g in gated]
    for y, r in zip(ys, subs):
        res = alpha * x_ref[r, :] + y
        c = res - jnp.mean(res, axis=1, keepdims=True)
        var = jnp.mean(c * c, axis=1, keepdims=True)
        o_ref[r, :] = c * lax.rsqrt(var + LN_EPS) * g_ref[...] + b_ref[...]


def _out_layer(x2, attn_outs, w_gm, mk, mv, w_out, ln_g, ln_b, seq, tm, alpha, name):
    m, d = x2.shape
    mem_len = mk.shape[0] // (m // seq)
    steps_per_batch = seq // tm

    def row(wd):
        return pl.BlockSpec((tm, wd), lambda i: (i, 0))

    def whole(a):
        return pl.BlockSpec(a.shape, lambda i: (0, 0))

    mem_spec = pl.BlockSpec((mem_len, mk.shape[1]), lambda i: (i // steps_per_batch, 0))
    in_specs = ([row(d)] + [row(a.shape[1]) for a in attn_outs]
                + [whole(w_gm), mem_spec, mem_spec, whole(w_out), whole(ln_g), whole(ln_b)])
    return pl.pallas_call(
        functools.partial(_out_kernel, n_attn=len(attn_outs), alpha=alpha, n_sub=OUT_SUB_TILES),
        grid=(m // tm,),
        in_specs=in_specs,
        out_specs=row(d),
        out_shape=jax.ShapeDtypeStruct((m, d), F32),
        compiler_params=pltpu.CompilerParams(dimension_semantics=("parallel",),
                                             vmem_limit_bytes=VMEM_LIMIT_BYTES),
        name=name,
    )(x2, *attn_outs, w_gm, mk, mv, w_out, ln_g, ln_b)


ROW_TILE = 1024
OUT_SUB_TILES = 4
IN_ROW_TILE = 1024
IN_EVEN_SUB_TILES = 1
IN_ODD_SUB_TILES = 2
SB_BLOCK = 128
SB_ROWS_PER_STEP = 512
DIFF_BLOCK = 512
SWA_BLOCK = 128
SWA_ROWS_PER_STEP = 512


def _even_layer(x2, mem2, pos2, w_in, w_memkv, diff_lambda, diff_subln, w_out, ln_g, ln_b,
                layer_idx, batch, seq, alpha):
    lambda_init = 0.8 - 0.6 * math.exp(-0.3 * layer_idx)
    w_sb = SB_HEADS * HEAD_DIM
    w_df = DIFF_HEADS * HEAD_DIM
    w_attn = 3 * w_sb + 3 * w_df
    w_in = w_in.astype(BF16)
    sbq, sbk, sbv, dfq, dfk, dfv = _inproj(
        functools.partial(_inproj_even_kernel, n_sub=IN_EVEN_SUB_TILES), "inproj_even", x2, pos2, _inv_freq_lanes(DIFF_QK_DIM), w_in[:, :w_attn],
        (w_sb, w_sb, w_sb, w_df, w_df, w_df), IN_ROW_TILE)
    mk, mv = _memkv(mem2, w_memkv.astype(BF16))

    def b3(a):
        return a.reshape(batch, seq, a.shape[1])

    sb_o = _row_block_attention(functools.partial(_sb_kernel, blk=SB_BLOCK), "stick_breaking",
                                b3(sbq), b3(sbk), b3(sbv), (), (), SB_ROWS_PER_STEP)
    subln_lanes = jnp.tile(diff_subln.astype(F32), LANES // HEAD_DIM)[None, :]
    df_o = _diff_attention(b3(dfq), b3(dfk), b3(dfv), diff_lambda.astype(F32), subln_lanes,
                           DIFF_BLOCK, lambda_init)
    return _out_layer(x2, [sb_o.reshape(x2.shape[0], -1), df_o.reshape(x2.shape[0], -1)], w_in[:, w_attn:],
                      mk, mv, w_out.astype(BF16), ln_g[None, :], ln_b[None, :], seq, ROW_TILE, alpha, "out_even")


def _odd_layer(x2, mem2, pos2, w_in, w_memkv, sinks, w_out, ln_g, ln_b, batch, seq, alpha):
    w_q = SWA_Q_HEADS * HEAD_DIM
    w_kv = SWA_KV_HEADS * HEAD_DIM
    w_m = MEM_HEADS * HEAD_DIM
    def heads_in_order(a, axis, lo):
        return [lax.slice_in_dim(a, lo + h * HEAD_DIM, lo + (h + 1) * HEAD_DIM, axis=axis) for h in SW_HEAD_ORDER]

    gate_lo = w_q + 2 * w_kv + w_m
    last_kv = slice((SWA_KV_HEADS - 1) * HEAD_DIM, w_kv)
    w_in = w_in.astype(BF16)
    w_out = w_out.astype(BF16)
    k_w = w_in[:, w_q:w_q + w_kv]
    v_w = w_in[:, w_q + w_kv:w_q + 2 * w_kv]
    w_attn = jnp.concatenate(heads_in_order(w_in, 1, 0) + [k_w, k_w[:, last_kv], v_w, v_w[:, last_kv]], axis=1)
    w_gm = jnp.concatenate([w_in[:, w_q + 2 * w_kv:gate_lo]] + heads_in_order(w_in, 1, gate_lo)
                           + [w_in[:, gate_lo + w_q:]], axis=1)
    w_out_perm = jnp.concatenate(heads_in_order(w_out, 0, 0) + [w_out[w_q:]], axis=0)

    cq, ck, cv = _inproj(
        functools.partial(_inproj_odd_kernel, n_sub=IN_ODD_SUB_TILES), "inproj_odd", x2, pos2, _inv_freq_lanes(HEAD_DIM), w_attn,
        (w_q, 2 * LANES, 2 * LANES), IN_ROW_TILE)
    mk, mv = _memkv(mem2, w_memkv.astype(BF16))

    def b3(a):
        return a.reshape(batch, seq, a.shape[1])

    pairs_per_kv_block = LANES // HEAD_DIM * SWA_GROUP // 2
    sink_spec = [pl.BlockSpec(memory_space=pltpu.SMEM)]
    c_o = _row_block_attention(
        functools.partial(_swa_kernel, blk=SWA_BLOCK, pairs_per_kv_block=pairs_per_kv_block),
        "sliding_window", b3(cq), b3(ck), b3(cv), (sinks.astype(F32),), sink_spec,
        SWA_ROWS_PER_STEP)
    return _out_layer(x2, [c_o.reshape(x2.shape[0], -1)], w_gm, mk, mv, w_out_perm,
                      ln_g[None, :], ln_b[None, :], seq, ROW_TILE, alpha, "out_odd")


def kernel(x, mem, positions, w_in_even, w_memkv_even, diff_lambda_even, diff_subln_even, w_out_even,
           ln_g_even, ln_b_even, w_in_odd, w_memkv_odd, sinks_odd, w_out_odd, ln_g_odd, ln_b_odd):
    batch, seq, d = x.shape
    depth = w_in_even.shape[0] + w_in_odd.shape[0]
    alpha = (2 * depth) ** 0.25
    x2 = x.reshape(batch * seq, d)
    mem2 = mem.reshape(batch * mem.shape[1], d)
    pos2 = positions.reshape(batch * seq, 1)
    for i in range(depth):
        j = i // 2
        if i % 2 == 0:
            x2 = _even_layer(x2, mem2, pos2, w_in_even[j], w_memkv_even[j], diff_lambda_even[j],
                             diff_subln_even[j], w_out_even[j], ln_g_even[j], ln_b_even[j], i,
                             batch, seq, alpha)
        else:
            x2 = _odd_layer(x2, mem2, pos2, w_in_odd[j], w_memkv_odd[j], sinks_odd[j], w_out_odd[j],
                            ln_g_odd[j], ln_b_odd[j], batch, seq, alpha)
    return x2.reshape(batch, seq, d)
```

```python
import functools
import math

import jax
import jax.numpy as jnp
import numpy as np
from jax import lax
from jax.experimental import pallas as pl
from jax.experimental.pallas import tpu as pltpu

F32 = jnp.float32
BF16 = jnp.bfloat16

HEAD_DIM = 64
LANES = 128
MEM_HEADS = 4
SB_HEADS = 6
DIFF_HEADS = 6
DIFF_QK_DIM = 32
SWA_Q_HEADS = 12
SWA_KV_HEADS = 3
SWA_GROUP = 4
WINDOW = 128
ROPE_THETA = 500000.0
ROPE_FRACTION = 4
LN_EPS = 1e-5
NEG_BIG = -1e30
VMEM_LIMIT_BYTES = 48 * 1024 * 1024

SB_EXIT_COST = 92.0
LOG2_E = math.log2(math.e)

SW_HEAD_ORDER = (0, 4, 1, 5, 2, 6, 3, 7, 8, 9, 10, 11)


def _nt_dot(a, b):
    return lax.dot_general(a, b, (((1,), (1,)), ((), ())), preferred_element_type=F32)


def _head_pair_rows(q2):
    lane = lax.broadcasted_iota(jnp.int32, q2.shape, 1)
    low = lane < HEAD_DIM
    zero = jnp.zeros_like(q2)
    return jnp.concatenate([jnp.where(low, q2, zero), jnp.where(low, zero, q2)], axis=0)


def _merge_head_pair(o, rows):
    lane = lax.broadcasted_iota(jnp.int32, (rows, LANES), 1)
    return jnp.where(lane < HEAD_DIM, o[:rows], o[rows:])


def _rope_lane_tables(group, pos_f32, inv_freq_row):
    rot = group // ROPE_FRACTION
    half = rot // 2
    lane = lax.broadcasted_iota(jnp.int32, (1, LANES), 1)
    r = lane % group
    ang = pos_f32 * inv_freq_row
    cos = jnp.cos(ang)
    sin = jnp.sin(ang)
    cos_t = jnp.where(r < rot, cos, 1.0)
    sin_first = jnp.where(r < half, -sin, 0.0)
    sin_second = jnp.where((r >= half) & (r < rot), sin, 0.0)
    return cos_t, sin_first, sin_second, half


def _apply_rope(h, tables):
    cos_t, sin_first, sin_second, half = tables
    outs = []
    for c in range(h.shape[1] // LANES):
        blk = h[:, c * LANES:(c + 1) * LANES]
        outs.append(blk * cos_t + pltpu.roll(blk, LANES - half, 1) * sin_first
                    + pltpu.roll(blk, half, 1) * sin_second)
    return jnp.concatenate(outs, axis=1)


def _inv_freq_lanes(group):
    half = group // ROPE_FRACTION // 2
    inv_freq = np.exp(-(np.arange(half, dtype=np.float32) / np.float32(half)) * np.float32(math.log(ROPE_THETA)))
    lane = np.arange(LANES)
    return jnp.asarray(inv_freq.astype(np.float32)[(lane % group) % half][None, :])


def _projected_sub_tiles(x_ref, w_ref, n_sub):
    ts = x_ref.shape[0] // n_sub
    subs = [slice(i * ts, (i + 1) * ts) for i in range(n_sub)]
    return [(r, jnp.dot(x_ref[r, :].astype(BF16), w_ref[...], preferred_element_type=F32)) for r in subs]


def _sections(h):
    edges = [0]

    def sec(width):
        edges.append(edges[-1] + width)
        return h[:, edges[-2]:edges[-1]]

    return sec


def _inproj_even_kernel(x_ref, pos_ref, invf_ref, w_ref, sbq_ref, sbk_ref, sbv_ref,
                        dfq_ref, dfk_ref, dfv_ref, *, n_sub):
    w_sb = SB_HEADS * HEAD_DIM
    w_df = DIFF_HEADS * 2 * DIFF_QK_DIM
    q_scale = DIFF_QK_DIM ** -0.5 * LOG2_E
    for r, h in _projected_sub_tiles(x_ref, w_ref, n_sub):
        sec = _sections(h)
        tables = _rope_lane_tables(DIFF_QK_DIM, pos_ref[r, :].astype(F32), invf_ref[...])
        sbq_ref[r, :] = (sec(w_sb) * (HEAD_DIM ** -0.5 * LOG2_E)).astype(BF16)
        sbk_ref[r, :] = sec(w_sb).astype(BF16)
        sbv_ref[r, :] = sec(w_sb).astype(BF16)
        dfq_ref[r, :] = (_apply_rope(sec(w_df), tables) * q_scale).astype(BF16)
        dfk_ref[r, :] = _apply_rope(sec(w_df), tables).astype(BF16)
        dfv_ref[r, :] = sec(w_df).astype(BF16)


def _inproj_odd_kernel(x_ref, pos_ref, invf_ref, w_ref, cq_ref, ck_ref, cv_ref, *, n_sub):
    w_kv = ck_ref.shape[1]
    for r, h in _projected_sub_tiles(x_ref, w_ref, n_sub):
        sec = _sections(h)
        tables = _rope_lane_tables(HEAD_DIM, pos_ref[r, :].astype(F32), invf_ref[...])
        cq_ref[r, :] = (_apply_rope(sec(SWA_Q_HEADS * HEAD_DIM), tables)
                        * (HEAD_DIM ** -0.5 * LOG2_E)).astype(BF16)
        ck_ref[r, :] = _apply_rope(sec(w_kv), tables).astype(BF16)
        cv_ref[r, :] = sec(w_kv).astype(BF16)


def _inproj(kernel_fn, name, x2, pos2, invf, w, out_widths, tm):
    m, d = x2.shape
    return pl.pallas_call(
        kernel_fn,
        grid=(m // tm,),
        in_specs=[pl.BlockSpec((tm, d), lambda i: (i, 0)),
                  pl.BlockSpec((tm, 1), lambda i: (i, 0)),
                  pl.BlockSpec((1, LANES), lambda i: (0, 0)),
                  pl.BlockSpec(w.shape, lambda i: (0, 0))],
        out_specs=[pl.BlockSpec((tm, wd), lambda i: (i, 0)) for wd in out_widths],
        out_shape=[jax.ShapeDtypeStruct((m, wd), BF16) for wd in out_widths],
        compiler_params=pltpu.CompilerParams(dimension_semantics=("parallel",),
                                             vmem_limit_bytes=VMEM_LIMIT_BYTES),
        name=name,
    )(x2, pos2, invf, w)


def _memkv_kernel(mem_ref, w_ref, mk_ref, mv_ref):
    kv = jnp.dot(mem_ref[...].astype(BF16), w_ref[...], preferred_element_type=F32)
    half = mk_ref.shape[1]
    mk_ref[...] = kv[:, :half].astype(BF16)
    mv_ref[...] = kv[:, half:].astype(BF16)


def _memkv(mem2, w):
    rows = mem2.shape[0]
    half = w.shape[1] // 2
    return pl.pallas_call(
        _memkv_kernel,
        out_shape=[jax.ShapeDtypeStruct((rows, half), BF16)] * 2,
        compiler_params=pltpu.CompilerParams(vmem_limit_bytes=VMEM_LIMIT_BYTES),
        name="memkv",
    )(mem2, w)


def _strict_lower(n):
    return (lax.broadcasted_iota(jnp.int32, (n, n), 0)
            > lax.broadcasted_iota(jnp.int32, (n, n), 1)).astype(BF16)


def _sb_tiles(chains, later):
    zs = [_nt_dot(qq, kk) if strict is None else jnp.where(strict, _nt_dot(qq, kk), NEG_BIG)
          for qq, kk, _, strict, _ in chains]
    costs = []
    for z in zs:
        cost = jnp.maximum(z, 0.0) + jnp.log(1.0 + jnp.exp2(-jnp.abs(z))) * LOG2_E
        hi = cost.astype(BF16)
        lo = (cost - hi.astype(F32)).astype(BF16)
        costs.append((cost, hi, lo))
    afters = [jnp.dot(hi, later, preferred_element_type=F32) + jnp.dot(lo, later, preferred_element_type=F32)
              for _, hi, lo in costs]
    outs = []
    for z, (cost, _, _), after, (_, _, vv, _, carry) in zip(zs, costs, afters, chains):
        w = jnp.exp2(z - (cost + after + carry))
        outs.append((jnp.dot(w.astype(BF16), vv, preferred_element_type=F32),
                     jnp.sum(cost, axis=1, keepdims=True)))
    return outs


def _sb_kernel(q_ref, k_ref, v_ref, o_ref, *, blk):
    n_pairs = q_ref.shape[1] // LANES
    n_sub = q_ref.shape[0] // blk

    def lanes(hp):
        return slice(hp * LANES, (hp + 1) * LANES)

    rows = lax.broadcasted_iota(jnp.int32, (2 * blk, 2 * blk), 0)
    q_off = jnp.where(rows >= blk, rows - blk, rows)
    col = lax.broadcasted_iota(jnp.int32, (2 * blk, 2 * blk), 1)
    later2 = _strict_lower(2 * blk)
    later1 = _strict_lower(blk)
    no_carry = jnp.zeros((2 * blk, 1), F32)

    first = []
    for sub in range(n_sub):
        r0 = pl.multiple_of((pl.program_id(1) * n_sub + sub) * blk, blk)
        p0 = pl.multiple_of(jnp.maximum(r0 - blk, 0), blk)
        strict = (col - blk < q_off) & (col + (r0 - blk) >= 0)
        for hp in range(n_pairs):
            qq = _head_pair_rows(q_ref[sub * blk:(sub + 1) * blk, lanes(hp)])
            kk = jnp.concatenate([k_ref[pl.ds(p0, blk), lanes(hp)], k_ref[pl.ds(r0, blk), lanes(hp)]], axis=0)
            vv = jnp.concatenate([v_ref[pl.ds(p0, blk), lanes(hp)], v_ref[pl.ds(r0, blk), lanes(hp)]], axis=0)
            first.append((qq, kk, vv, strict, no_carry))
    started = _sb_tiles(first, later2)

    def finish(sub):
        bi = pl.program_id(1) * n_sub + sub
        mine = slice(sub * n_pairs, (sub + 1) * n_pairs)
        qqs = [chain[0] for chain in first[mine]]
        accs = tuple(pv for pv, _ in started[mine])
        carries = tuple(dsum for _, dsum in started[mine])

        def cond(c):
            j, carries, _ = c
            return (j >= 0) & (jnp.min(functools.reduce(jnp.minimum, carries)) < SB_EXIT_COST)

        def body(c):
            j, carries, accs = c
            start = pl.multiple_of(j * blk, blk)
            outs = _sb_tiles([(qqs[hp], k_ref[pl.ds(start, blk), lanes(hp)], v_ref[pl.ds(start, blk), lanes(hp)],
                               None, carries[hp]) for hp in range(n_pairs)], later1)
            return (j - 1, tuple(c + dsum for c, (_, dsum) in zip(carries, outs)),
                    tuple(a + pv for a, (pv, _) in zip(accs, outs)))

        _, _, accs = lax.while_loop(cond, body, (bi - 2, carries, accs))
        o_ref[sub * blk:(sub + 1) * blk, :] = jnp.concatenate(
            [_merge_head_pair(a, blk) for a in accs], axis=1).astype(o_ref.dtype)

    for sub in range(n_sub):
        finish(sub)


def _row_block_attention(kernel_fn, name, q, k, v, extra_inputs, extra_specs, blk):
    b, s, wq = q.shape
    q_spec = pl.BlockSpec((None, blk, wq), lambda bi, qi: (bi, qi, 0))
    kv_spec = pl.BlockSpec((None, s, k.shape[2]), lambda bi, qi: (bi, 0, 0))
    return pl.pallas_call(
        kernel_fn,
        grid=(b, s // blk),
        in_specs=list(extra_specs) + [q_spec, kv_spec, kv_spec],
        out_specs=q_spec,
        out_shape=jax.ShapeDtypeStruct(q.shape, BF16),
        compiler_params=pltpu.CompilerParams(
            dimension_semantics=("parallel", "arbitrary"),
            vmem_limit_bytes=VMEM_LIMIT_BYTES),
        name=name,
    )(*extra_inputs, q, k, v)


def _diff_kernel(dl_ref, subln_ref, q_ref, k_ref, v_ref, o_ref, vp_ref, sa_ref, sb_ref, sd_ref, acc_ref, *, blk, lambda_init):
    qi = pl.program_id(2)
    seq = v_ref.shape[0]

    @pl.when(qi == 0)
    def _():
        low = lax.broadcasted_iota(jnp.int32, (blk, LANES), 1) < HEAD_DIM

        def build(c, _):
            rows = pl.ds(pl.multiple_of(c * blk, blk), blk)
            vf = v_ref[rows, :].astype(F32)
            vp_ref[rows, :LANES] = jnp.where(low, vf, 1.0).astype(BF16)
            vp_ref[rows, LANES:] = jnp.where(low, pltpu.roll(vf, HEAD_DIM, 1), 1.0).astype(BF16)
            return 0

        lax.fori_loop(0, seq // blk, build, 0)

    q2 = q_ref[...]
    lane = lax.broadcasted_iota(jnp.int32, q2.shape, 1)
    zero = jnp.zeros_like(q2)
    n_maps = LANES // DIFF_QK_DIM
    qq = jnp.concatenate([jnp.where(lane // DIFF_QK_DIM == c, q2, zero) for c in range(n_maps)], axis=0)
    rows_n = n_maps * blk
    half = rows_n // 2

    wide = 2 * blk

    def wide_keys(block):
        return pl.ds(pl.multiple_of(block * wide, wide), wide)

    def narrow_keys(start):
        return pl.ds(pl.multiple_of(start, blk), blk)

    def scores(s_ref, block):
        s_ref[...] = _nt_dot(qq, k_ref[wide_keys(block), :])

    def update(s, keys, m, diagonal=False):
        if diagonal:
            q_off = lax.broadcasted_iota(jnp.int32, s.shape, 0) % blk
            k_off = lax.broadcasted_iota(jnp.int32, s.shape, 1)
            s = jnp.where(k_off <= q_off, s, NEG_BIG)
        m_new = jnp.maximum(m, jnp.max(s, axis=1, keepdims=True))
        p = jnp.exp2(s - m_new).astype(BF16)
        alpha = jnp.exp2(m - m_new)
        for h, v_lanes in enumerate((slice(0, LANES), slice(LANES, 2 * LANES))):
            rows = slice(h * half, (h + 1) * half)
            acc_ref[rows, :] = alpha[rows] * acc_ref[rows, :] + jnp.dot(
                p[rows], vp_ref[keys, v_lanes], preferred_element_type=F32)
        return m_new

    def finish():
        acc = acc_ref[...]
        dl = dl_ref[...]
        lam = (jnp.exp(jnp.sum(dl[0:1] * dl[1:2], axis=1, keepdims=True))
               - jnp.exp(jnp.sum(dl[2:3] * dl[3:4], axis=1, keepdims=True)) + lambda_init)
        low = lax.broadcasted_iota(jnp.int32, (blk, LANES), 1) < HEAD_DIM
        ones = jnp.ones((LANES, LANES), BF16)
        normed = []
        for h in range(2):
            n0 = acc[2 * h * blk:(2 * h + 1) * blk]
            n1 = acc[(2 * h + 1) * blk:(2 * h + 2) * blk]
            l0 = pltpu.roll(n0, HEAD_DIM, 1)
            l1 = pltpu.roll(n1, HEAD_DIM, 1)
            d = n0 - (lam * l0 / l1) * n1
            dsq = jnp.where(low, d * d, 0.0)
            hi = dsq.astype(BF16)
            lo = (dsq - hi.astype(F32)).astype(BF16)
            ms = (jnp.dot(hi, ones, preferred_element_type=F32)
                  + jnp.dot(lo, ones, preferred_element_type=F32)) * (1.0 / HEAD_DIM)
            normed.append(d * lax.rsqrt(ms + LN_EPS * l0 * l0))
        y = jnp.where(low, normed[0], pltpu.roll(normed[1], HEAD_DIM, 1))
        o_ref[...] = (y * subln_ref[...] * (1.0 - lambda_init)).astype(o_ref.dtype)

    n_wide = qi // 2
    acc_ref[...] = jnp.zeros(acc_ref.shape, F32)
    sd_ref[...] = _nt_dot(qq, k_ref[narrow_keys(qi * blk), :])
    scores(sa_ref, 0)
    m = update(sd_ref[...], narrow_keys(qi * blk), jnp.full((rows_n, 1), NEG_BIG, F32), diagonal=True)

    def wide_update(s_ref, block, m):
        return update(s_ref[...], wide_keys(block), m)

    def leftover_update(s_ref, block, m):
        return update(s_ref[:, :blk], narrow_keys(block * wide), m)

    def body(i, m):
        scores(sb_ref, 2 * i + 1)
        m = wide_update(sa_ref, 2 * i, m)
        scores(sa_ref, 2 * i + 2)
        return wide_update(sb_ref, 2 * i + 1, m)

    n_loop = jnp.maximum(n_wide - 1, 0) // 2
    m = lax.fori_loop(0, n_loop, body, m)
    first = 2 * n_loop
    left = n_wide - first
    odd = qi % 2

    def tail(n_left, is_odd):
        def run():
            mm = m
            if n_left >= 1:
                if n_left == 2 or is_odd:
                    scores(sb_ref, first + 1)
                mm = wide_update(sa_ref, first, mm)
            if n_left == 2:
                if is_odd:
                    scores(sa_ref, first + 2)
                mm = wide_update(sb_ref, first + 1, mm)
            if is_odd:
                leftover_update(sa_ref if n_left in (0, 2) else sb_ref, n_wide, mm)
            finish()
        pl.when(jnp.logical_and(left == n_left, odd == int(is_odd)))(run)

    for n_left in range(3):
        for is_odd in (False, True):
            tail(n_left, is_odd)


def _diff_attention(q, k, v, diff_lambda, subln_lanes, blk, lambda_init):
    b, s, wq = q.shape
    assert s % (2 * blk) == 0
    rows_n = LANES // DIFF_QK_DIM * blk
    q_spec = pl.BlockSpec((None, blk, LANES), lambda bi, hp, qi: (bi, qi, hp))
    kv_spec = pl.BlockSpec((None, s, LANES), lambda bi, hp, qi: (bi, 0, hp))
    return pl.pallas_call(
        functools.partial(_diff_kernel, blk=blk, lambda_init=lambda_init),
        grid=(b, wq // LANES, s // blk),
        in_specs=[pl.BlockSpec(diff_lambda.shape, lambda bi, hp, qi: (0, 0)),
                  pl.BlockSpec((1, LANES), lambda bi, hp, qi: (0, 0)), q_spec, kv_spec, kv_spec],
        out_specs=q_spec,
        out_shape=jax.ShapeDtypeStruct(q.shape, BF16),
        scratch_shapes=[pltpu.VMEM((s, 2 * LANES), BF16),
                        pltpu.VMEM((rows_n, 2 * blk), F32),
                        pltpu.VMEM((rows_n, 2 * blk), F32),
                        pltpu.VMEM((rows_n, blk), F32),
                        pltpu.VMEM((rows_n, LANES), F32)],
        compiler_params=pltpu.CompilerParams(
            dimension_semantics=("parallel", "parallel", "arbitrary"),
            vmem_limit_bytes=VMEM_LIMIT_BYTES),
        name="differential",
    )(diff_lambda, subln_lanes, q, k, v)


def _swa_kernel(sinks_ref, q_ref, k_ref, v_ref, o_ref, *, blk, pairs_per_kv_block):
    n_sub = q_ref.shape[0] // blk

    def lanes(hb):
        return slice(hb * LANES, (hb + 1) * LANES)

    shape = (2 * blk, WINDOW + blk)
    rows = lax.broadcasted_iota(jnp.int32, shape, 0)
    q_off = jnp.where(rows >= blk, rows - blk, rows)
    col = lax.broadcasted_iota(jnp.int32, shape, 1)
    rel = (col - WINDOW) - q_off
    in_window = (rel <= 0) & (rel > -WINDOW)
    second_head = lax.broadcasted_iota(jnp.int32, (2 * blk, 1), 0) >= blk
    n_blocks = q_ref.shape[1] // LANES
    chains = []
    for sub in range(n_sub):
        r0 = pl.multiple_of((pl.program_id(1) * n_sub + sub) * blk, blk)
        p0 = pl.multiple_of(jnp.maximum(r0 - WINDOW, 0), WINDOW)
        band = in_window & (col + (r0 - WINDOW) >= 0)
        for hb in range(n_blocks):
            kvb = hb // pairs_per_kv_block
            kk = jnp.concatenate([k_ref[pl.ds(p0, WINDOW), lanes(kvb)], k_ref[pl.ds(r0, blk), lanes(kvb)]], axis=0)
            vv = jnp.concatenate([v_ref[pl.ds(p0, WINDOW), lanes(kvb)], v_ref[pl.ds(r0, blk), lanes(kvb)]], axis=0)
            qq = _head_pair_rows(q_ref[sub * blk:(sub + 1) * blk, lanes(hb)])
            sink = jnp.where(second_head, sinks_ref[SW_HEAD_ORDER[2 * hb + 1]],
                             sinks_ref[SW_HEAD_ORDER[2 * hb]]) * LOG2_E
            chains.append((qq, kk, vv, band, sink))
    scores = [jnp.where(band, _nt_dot(qq, kk), NEG_BIG) for qq, kk, _, band, _ in chains]
    probs = []
    for s, (_, _, _, _, sink) in zip(scores, chains):
        m = jnp.maximum(jnp.max(s, axis=1, keepdims=True), sink)
        p = jnp.exp2(s - m)
        probs.append((p.astype(BF16), jnp.sum(p, axis=1, keepdims=True) + jnp.exp2(sink - m)))
    outs = [_merge_head_pair(jnp.dot(p, vv, preferred_element_type=F32) / denom, blk)
            for (p, denom), (_, _, vv, _, _) in zip(probs, chains)]
    for sub in range(n_sub):
        o_ref[sub * blk:(sub + 1) * blk, :] = jnp.concatenate(
            outs[sub * n_blocks:(sub + 1) * n_blocks], axis=1).astype(o_ref.dtype)


def _out_kernel(*refs, n_attn, alpha, n_sub):
    x_ref = refs[0]
    attn_refs = refs[1:1 + n_attn]
    wgm_ref, mk_ref, mv_ref, w_ref, g_ref, b_ref, o_ref = refs[1 + n_attn:]
    ts = x_ref.shape[0] // n_sub
    subs = [slice(i * ts, (i + 1) * ts) for i in range(n_sub)]
    w_m = MEM_HEADS * HEAD_DIM
    blocks = [slice(hb * LANES, (hb + 1) * LANES) for hb in range(w_m // LANES)]

    hs = [jnp.dot(x_ref[r, :].astype(BF16), wgm_ref[...], preferred_element_type=F32) for r in subs]
    mqs = [(h[:, :w_m] * (HEAD_DIM ** -0.5 * LOG2_E)).astype(BF16) for h in hs]
    scores = [[_nt_dot(_head_pair_rows(mq[:, c]), mk_ref[:, c]) for c in blocks] for mq in mqs]
    probs = [[jnp.exp2(s - jnp.max(s, axis=1, keepdims=True)) for s in row] for row in scores]
    mems = [[_merge_head_pair(jnp.dot(p.astype(BF16), mv_ref[:, c], preferred_element_type=F32)
                              / jnp.sum(p, axis=1, keepdims=True), ts) for p, c in zip(row, blocks)]
            for row in probs]
    gated = []
    for h, r, mem in zip(hs, subs, mems):
        gate = h[:, w_m:]
        mixed = jnp.concatenate([a[r, :].astype(F32) for a in attn_refs] + mem, axis=1)
        gated.append((mixed * (gate / (1.0 + jnp.exp(-gate)))).astype(BF16))
    ys = [jnp.dot(g, w_ref[...], preferred_element_type=F32) for g in gated]
    for y, r in zip(ys, subs):
        res = alpha * x_ref[r, :] + y
        c = res - jnp.mean(res, axis=1, keepdims=True)
        var = jnp.mean(c * c, axis=1, keepdims=True)
        o_ref[r, :] = c * lax.rsqrt(var + LN_EPS) * g_ref[...] + b_ref[...]


def _out_layer(x2, attn_outs, w_gm, mk, mv, w_out, ln_g, ln_b, seq, tm, alpha, name):
    m, d = x2.shape
    mem_len = mk.shape[0] // (m // seq)
    steps_per_batch = seq // tm

    def row(wd):
        return pl.BlockSpec((tm, wd), lambda i: (i, 0))

    def whole(a):
        return pl.BlockSpec(a.shape, lambda i: (0, 0))

    mem_spec = pl.BlockSpec((mem_len, mk.shape[1]), lambda i: (i // steps_per_batch, 0))
    in_specs = ([row(d)] + [row(a.shape[1]) for a in attn_outs]
                + [whole(w_gm), mem_spec, mem_spec, whole(w_out), whole(ln_g), whole(ln_b)])
    return pl.pallas_call(
        functools.partial(_out_kernel, n_attn=len(attn_outs), alpha=alpha, n_sub=OUT_SUB_TILES),
        grid=(m // tm,),
        in_specs=in_specs,
        out_specs=row(d),
        out_shape=jax.ShapeDtypeStruct((m, d), F32),
        compiler_params=pltpu.CompilerParams(dimension_semantics=("parallel",),
                                             vmem_limit_bytes=VMEM_LIMIT_BYTES),
        name=name,
    )(x2, *attn_outs, w_gm, mk, mv, w_out, ln_g, ln_b)


ROW_TILE = 1024
OUT_SUB_TILES = 4
IN_ROW_TILE = 1024
IN_EVEN_SUB_TILES = 1
IN_ODD_SUB_TILES = 2
SB_BLOCK = 128
SB_ROWS_PER_STEP = 512
DIFF_BLOCK = 512
SWA_BLOCK = 128
SWA_ROWS_PER_STEP = 512


def _even_layer(x2, mem2, pos2, w_in, w_memkv, diff_lambda, diff_subln, w_out, ln_g, ln_b,
                layer_idx, batch, seq, alpha):
    lambda_init = 0.8 - 0.6 * math.exp(-0.3 * layer_idx)
    w_sb = SB_HEADS * HEAD_DIM
    w_df = DIFF_HEADS * HEAD_DIM
    w_attn = 3 * w_sb + 3 * w_df
    w_in = w_in.astype(BF16)
    sbq, sbk, sbv, dfq, dfk, dfv = _inproj(
        functools.partial(_inproj_even_kernel, n_sub=IN_EVEN_SUB_TILES), "inproj_even", x2, pos2, _inv_freq_lanes(DIFF_QK_DIM), w_in[:, :w_attn],
        (w_sb, w_sb, w_sb, w_df, w_df, w_df), IN_ROW_TILE)
    mk, mv = _memkv(mem2, w_memkv.astype(BF16))

    def b3(a):
        return a.reshape(batch, seq, a.shape[1])

    sb_o = _row_block_attention(functools.partial(_sb_kernel, blk=SB_BLOCK), "stick_breaking",
                                b3(sbq), b3(sbk), b3(sbv), (), (), SB_ROWS_PER_STEP)
    subln_lanes = jnp.tile(diff_subln.astype(F32), LANES // HEAD_DIM)[None, :]
    df_o = _diff_attention(b3(dfq), b3(dfk), b3(dfv), diff_lambda.astype(F32), subln_lanes,
                           DIFF_BLOCK, lambda_init)
    return _out_layer(x2, [sb_o.reshape(x2.shape[0], -1), df_o.reshape(x2.shape[0], -1)], w_in[:, w_attn:],
                      mk, mv, w_out.astype(BF16), ln_g[None, :], ln_b[None, :], seq, ROW_TILE, alpha, "out_even")


def _odd_layer(x2, mem2, pos2, w_in, w_memkv, sinks, w_out, ln_g, ln_b, batch, seq, alpha):
    w_q = SWA_Q_HEADS * HEAD_DIM
    w_kv = SWA_KV_HEADS * HEAD_DIM
    w_m = MEM_HEADS * HEAD_DIM
    def heads_in_order(a, axis, lo):
        return [lax.slice_in_dim(a, lo + h * HEAD_DIM, lo + (h + 1) * HEAD_DIM, axis=axis) for h in SW_HEAD_ORDER]

    gate_lo = w_q + 2 * w_kv + w_m
    last_kv = slice((SWA_KV_HEADS - 1) * HEAD_DIM, w_kv)
    w_in = w_in.astype(BF16)
    w_out = w_out.astype(BF16)
    k_w = w_in[:, w_q:w_q + w_kv]
    v_w = w_in[:, w_q + w_kv:w_q + 2 * w_kv]
    w_attn = jnp.concatenate(heads_in_order(w_in, 1, 0) + [k_w, k_w[:, last_kv], v_w, v_w[:, last_kv]], axis=1)
    w_gm = jnp.concatenate([w_in[:, w_q + 2 * w_kv:gate_lo]] + heads_in_order(w_in, 1, gate_lo)
                           + [w_in[:, gate_lo + w_q:]], axis=1)
    w_out_perm = jnp.concatenate(heads_in_order(w_out, 0, 0) + [w_out[w_q:]], axis=0)

    cq, ck, cv = _inproj(
        functools.partial(_inproj_odd_kernel, n_sub=IN_ODD_SUB_TILES), "inproj_odd", x2, pos2, _inv_freq_lanes(HEAD_DIM), w_attn,
        (w_q, 2 * LANES, 2 * LANES), IN_ROW_TILE)
    mk, mv = _memkv(mem2, w_memkv.astype(BF16))

    def b3(a):
        return a.reshape(batch, seq, a.shape[1])

    pairs_per_kv_block = LANES // HEAD_DIM * SWA_GROUP // 2
    sink_spec = [pl.BlockSpec(memory_space=pltpu.SMEM)]
    c_o = _row_block_attention(
        functools.partial(_swa_kernel, blk=SWA_BLOCK, pairs_per_kv_block=pairs_per_kv_block),
        "sliding_window", b3(cq), b3(ck), b3(cv), (sinks.astype(F32),), sink_spec,
        SWA_ROWS_PER_STEP)
    return _out_layer(x2, [c_o.reshape(x2.shape[0], -1)], w_gm, mk, mv, w_out_perm,
                      ln_g[None, :], ln_b[None, :], seq, ROW_TILE, alpha, "out_odd")


def kernel(x, mem, positions, w_in_even, w_memkv_even, diff_lambda_even, diff_subln_even, w_out_even,
           ln_g_even, ln_b_even, w_in_odd, w_memkv_odd, sinks_odd, w_out_odd, ln_g_odd, ln_b_odd):
    batch, seq, d = x.shape
    depth = w_in_even.shape[0] + w_in_odd.shape[0]
    alpha = (2 * depth) ** 0.25
    x2 = x.reshape(batch * seq, d)
    mem2 = mem.reshape(batch * mem.shape[1], d)
    pos2 = positions.reshape(batch * seq, 1)
    for i in range(depth):
        j = i // 2
        if i % 2 == 0:
            x2 = _even_layer(x2, mem2, pos2, w_in_even[j], w_memkv_even[j], diff_lambda_even[j],
                             diff_subln_even[j], w_out_even[j], ln_g_even[j], ln_b_even[j], i,
                             batch, seq, alpha)
        else:
            x2 = _odd_layer(x2, mem2, pos2, w_in_odd[j], w_memkv_odd[j], sinks_odd[j], w_out_odd[j],
                            ln_g_odd[j], ln_b_odd[j], batch, seq, alpha)
    return x2.reshape(batch, seq, d)
```

```python
import functools
import math

import jax
import jax.numpy as jnp
import numpy as np
from jax import lax
from jax.experimental import pallas as pl
from jax.experimental.pallas import tpu as pltpu

F32 = jnp.float32
BF16 = jnp.bfloat16

HEAD_DIM = 64
LANES = 128
MEM_HEADS = 4
SB_HEADS = 6
DIFF_HEADS = 6
DIFF_QK_DIM = 32
SWA_Q_HEADS = 12
SWA_KV_HEADS = 3
SWA_GROUP = 4
WINDOW = 128
ROPE_THETA = 500000.0
ROPE_FRACTION = 4
LN_EPS = 1e-5
NEG_BIG = -1e30
VMEM_LIMIT_BYTES = 48 * 1024 * 1024

SB_EXIT_COST = 92.0
LOG2_E = math.log2(math.e)

SW_HEAD_ORDER = (0, 4, 1, 5, 2, 6, 3, 7, 8, 9, 10, 11)


def _nt_dot(a, b):
    return lax.dot_general(a, b, (((1,), (1,)), ((), ())), preferred_element_type=F32)


def _head_pair_rows(q2):
    lane = lax.broadcasted_iota(jnp.int32, q2.shape, 1)
    low = lane < HEAD_DIM
    zero = jnp.zeros_like(q2)
    return jnp.concatenate([jnp.where(low, q2, zero), jnp.where(low, zero, q2)], axis=0)


def _merge_head_pair(o, rows):
    lane = lax.broadcasted_iota(jnp.int32, (rows, LANES), 1)
    return jnp.where(lane < HEAD_DIM, o[:rows], o[rows:])


def _rope_lane_tables(group, pos_f32, inv_freq_row):
    rot = group // ROPE_FRACTION
    half = rot // 2
    lane = lax.broadcasted_iota(jnp.int32, (1, LANES), 1)
    r = lane % group
    ang = pos_f32 * inv_freq_row
    cos = jnp.cos(ang)
    sin = jnp.sin(ang)
    cos_t = jnp.where(r < rot, cos, 1.0)
    sin_first = jnp.where(r < half, -sin, 0.0)
    sin_second = jnp.where((r >= half) & (r < rot), sin, 0.0)
    return cos_t, sin_first, sin_second, half


def _apply_rope(h, tables):
    cos_t, sin_first, sin_second, half = tables
    outs = []
    for c in range(h.shape[1] // LANES):
        blk = h[:, c * LANES:(c + 1) * LANES]
        outs.append(blk * cos_t + pltpu.roll(blk, LANES - half, 1) * sin_first
                    + pltpu.roll(blk, half, 1) * sin_second)
    return jnp.concatenate(outs, axis=1)


def _inv_freq_lanes(group):
    half = group // ROPE_FRACTION // 2
    inv_freq = np.exp(-(np.arange(half, dtype=np.float32) / np.float32(half)) * np.float32(math.log(ROPE_THETA)))
    lane = np.arange(LANES)
    return jnp.asarray(inv_freq.astype(np.float32)[(lane % group) % half][None, :])


def _projected_sub_tiles(x_ref, w_ref, n_sub):
    ts = x_ref.shape[0] // n_sub
    subs = [slice(i * ts, (i + 1) * ts) for i in range(n_sub)]
    return [(r, jnp.dot(x_ref[r, :].astype(BF16), w_ref[...], preferred_element_type=F32)) for r in subs]


def _sections(h):
    edges = [0]

    def sec(width):
        edges.append(edges[-1] + width)
        return h[:, edges[-2]:edges[-1]]

    return sec


def _inproj_even_kernel(x_ref, pos_ref, invf_ref, w_ref, sbq_ref, sbk_ref, sbv_ref,
                        dfq_ref, dfk_ref, dfv_ref, *, n_sub):
    w_sb = SB_HEADS * HEAD_DIM
    w_df = DIFF_HEADS * 2 * DIFF_QK_DIM
    q_scale = DIFF_QK_DIM ** -0.5 * LOG2_E
    for r, h in _projected_sub_tiles(x_ref, w_ref, n_sub):
        sec = _sections(h)
        tables = _rope_lane_tables(DIFF_QK_DIM, pos_ref[r, :].astype(F32), invf_ref[...])
        sbq_ref[r, :] = (sec(w_sb) * (HEAD_DIM ** -0.5 * LOG2_E)).astype(BF16)
        sbk_ref[r, :] = sec(w_sb).astype(BF16)
        sbv_ref[r, :] = sec(w_sb).astype(BF16)
        dfq_ref[r, :] = (_apply_rope(sec(w_df), tables) * q_scale).astype(BF16)
        dfk_ref[r, :] = _apply_rope(sec(w_df), tables).astype(BF16)
        dfv_ref[r, :] = sec(w_df).astype(BF16)


def _inproj_odd_kernel(x_ref, pos_ref, invf_ref, w_ref, cq_ref, ck_ref, cv_ref, *, n_sub):
    w_kv = ck_ref.shape[1]
    for r, h in _projected_sub_tiles(x_ref, w_ref, n_sub):
        sec = _sections(h)
        tables = _rope_lane_tables(HEAD_DIM, pos_ref[r, :].astype(F32), invf_ref[...])
        cq_ref[r, :] = (_apply_rope(sec(SWA_Q_HEADS * HEAD_DIM), tables)
                        * (HEAD_DIM ** -0.5 * LOG2_E)).astype(BF16)
        ck_ref[r, :] = _apply_rope(sec(w_kv), tables).astype(BF16)
        cv_ref[r, :] = sec(w_kv).astype(BF16)


def _inproj(kernel_fn, name, x2, pos2, invf, w, out_widths, tm):
    m, d = x2.shape
    return pl.pallas_call(
        kernel_fn,
        grid=(m // tm,),
        in_specs=[pl.BlockSpec((tm, d), lambda i: (i, 0)),
                  pl.BlockSpec((tm, 1), lambda i: (i, 0)),
                  pl.BlockSpec((1, LANES), lambda i: (0, 0)),
                  pl.BlockSpec(w.shape, lambda i: (0, 0))],
        out_specs=[pl.BlockSpec((tm, wd), lambda i: (i, 0)) for wd in out_widths],
        out_shape=[jax.ShapeDtypeStruct((m, wd), BF16) for wd in out_widths],
        compiler_params=pltpu.CompilerParams(dimension_semantics=("parallel",),
                                             vmem_limit_bytes=VMEM_LIMIT_BYTES),
        name=name,
    )(x2, pos2, invf, w)


def _memkv_kernel(mem_ref, w_ref, mk_ref, mv_ref):
    kv = jnp.dot(mem_ref[...].astype(BF16), w_ref[...], preferred_element_type=F32)
    half = mk_ref.shape[1]
    mk_ref[...] = kv[:, :half].astype(BF16)
    mv_ref[...] = kv[:, half:].astype(BF16)


def _memkv(mem2, w):
    rows = mem2.shape[0]
    half = w.shape[1] // 2
    return pl.pallas_call(
        _memkv_kernel,
        out_shape=[jax.ShapeDtypeStruct((rows, half), BF16)] * 2,
        compiler_params=pltpu.CompilerParams(vmem_limit_bytes=VMEM_LIMIT_BYTES),
        name="memkv",
    )(mem2, w)


def _strict_lower(n):
    return (lax.broadcasted_iota(jnp.int32, (n, n), 0)
            > lax.broadcasted_iota(jnp.int32, (n, n), 1)).astype(BF16)


def _sb_tiles(chains, later):
    zs = [_nt_dot(qq, kk) if strict is None else jnp.where(strict, _nt_dot(qq, kk), NEG_BIG)
          for qq, kk, _, strict, _ in chains]
    costs = []
    for z in zs:
        cost = jnp.maximum(z, 0.0) + jnp.log(1.0 + jnp.exp2(-jnp.abs(z))) * LOG2_E
        hi = cost.astype(BF16)
        lo = (cost - hi.astype(F32)).astype(BF16)
        costs.append((cost, hi, lo))
    afters = [jnp.dot(hi, later, preferred_element_type=F32) + jnp.dot(lo, later, preferred_element_type=F32)
              for _, hi, lo in costs]
    outs = []
    for z, (cost, _, _), after, (_, _, vv, _, carry) in zip(zs, costs, afters, chains):
        w = jnp.exp2(z - (cost + after + carry))
        outs.append((jnp.dot(w.astype(BF16), vv, preferred_element_type=F32),
                     jnp.sum(cost, axis=1, keepdims=True)))
    return outs


def _sb_kernel(q_ref, k_ref, v_ref, o_ref, *, blk):
    n_pairs = q_ref.shape[1] // LANES
    n_sub = q_ref.shape[0] // blk

    def lanes(hp):
        return slice(hp * LANES, (hp + 1) * LANES)

    rows = lax.broadcasted_iota(jnp.int32, (2 * blk, 2 * blk), 0)
    q_off = jnp.where(rows >= blk, rows - blk, rows)
    col = lax.broadcasted_iota(jnp.int32, (2 * blk, 2 * blk), 1)
    later2 = _strict_lower(2 * blk)
    later1 = _strict_lower(blk)
    no_carry = jnp.zeros((2 * blk, 1), F32)

    first = []
    for sub in range(n_sub):
        r0 = pl.multiple_of((pl.program_id(1) * n_sub + sub) * blk, blk)
        p0 = pl.multiple_of(jnp.maximum(r0 - blk, 0), blk)
        strict = (col - blk < q_off) & (col + (r0 - blk) >= 0)
        for hp in range(n_pairs):
            qq = _head_pair_rows(q_ref[sub * blk:(sub + 1) * blk, lanes(hp)])
            kk = jnp.concatenate([k_ref[pl.ds(p0, blk), lanes(hp)], k_ref[pl.ds(r0, blk), lanes(hp)]], axis=0)
            vv = jnp.concatenate([v_ref[pl.ds(p0, blk), lanes(hp)], v_ref[pl.ds(r0, blk), lanes(hp)]], axis=0)
            first.append((qq, kk, vv, strict, no_carry))
    started = _sb_tiles(first, later2)

    def finish(sub):
        bi = pl.program_id(1) * n_sub + sub
        mine = slice(sub * n_pairs, (sub + 1) * n_pairs)
        qqs = [chain[0] for chain in first[mine]]
        accs = tuple(pv for pv, _ in started[mine])
        carries = tuple(dsum for _, dsum in started[mine])

        def cond(c):
            j, carries, _ = c
            return (j >= 0) & (jnp.min(functools.reduce(jnp.minimum, carries)) < SB_EXIT_COST)

        def body(c):
            j, carries, accs = c
            start = pl.multiple_of(j * blk, blk)
            outs = _sb_tiles([(qqs[hp], k_ref[pl.ds(start, blk), lanes(hp)], v_ref[pl.ds(start, blk), lanes(hp)],
                               None, carries[hp]) for hp in range(n_pairs)], later1)
            return (j - 1, tuple(c + dsum for c, (_, dsum) in zip(carries, outs)),
                    tuple(a + pv for a, (pv, _) in zip(accs, outs)))

        _, _, accs = lax.while_loop(cond, body, (bi - 2, carries, accs))
        o_ref[sub * blk:(sub + 1) * blk, :] = jnp.concatenate(
            [_merge_head_pair(a, blk) for a in accs], axis=1).astype(o_ref.dtype)

    for sub in range(n_sub):
        finish(sub)


def _row_block_attention(kernel_fn, name, q, k, v, extra_inputs, extra_specs, blk):
    b, s, wq = q.shape
    q_spec = pl.BlockSpec((None, blk, wq), lambda bi, qi: (bi, qi, 0))
    kv_spec = pl.BlockSpec((None, s, k.shape[2]), lambda bi, qi: (bi, 0, 0))
    return pl.pallas_call(
        kernel_fn,
        grid=(b, s // blk),
        in_specs=list(extra_specs) + [q_spec, kv_spec, kv_spec],
        out_specs=q_spec,
        out_shape=jax.ShapeDtypeStruct(q.shape, BF16),
        compiler_params=pltpu.CompilerParams(
            dimension_semantics=("parallel", "arbitrary"),
            vmem_limit_bytes=VMEM_LIMIT_BYTES),
        name=name,
    )(*extra_inputs, q, k, v)


def _diff_kernel(dl_ref, subln_ref, q_ref, k_ref, v_ref, o_ref, vp_ref, sa_ref, sb_ref, sd_ref, acc_ref, *, blk, lambda_init):
    qi = pl.program_id(2)
    seq = v_ref.shape[0]

    @pl.when(qi == 0)
    def _():
        low = lax.broadcasted_iota(jnp.int32, (blk, LANES), 1) < HEAD_DIM

        def build(c, _):
            rows = pl.ds(pl.multiple_of(c * blk, blk), blk)
            vf = v_ref[rows, :].astype(F32)
            vp_ref[rows, :LANES] = jnp.where(low, vf, 1.0).astype(BF16)
            vp_ref[rows, LANES:] = jnp.where(low, pltpu.roll(vf, HEAD_DIM, 1), 1.0).astype(BF16)
            return 0

        lax.fori_loop(0, seq // blk, build, 0)

    q2 = q_ref[...]
    lane = lax.broadcasted_iota(jnp.int32, q2.shape, 1)
    zero = jnp.zeros_like(q2)
    n_maps = LANES // DIFF_QK_DIM
    qq = jnp.concatenate([jnp.where(lane // DIFF_QK_DIM == c, q2, zero) for c in range(n_maps)], axis=0)
    rows_n = n_maps * blk
    half = rows_n // 2

    wide = 2 * blk

    def wide_keys(block):
        return pl.ds(pl.multiple_of(block * wide, wide), wide)

    def narrow_keys(start):
        return pl.ds(pl.multiple_of(start, blk), blk)

    def scores(s_ref, block):
        s_ref[...] = _nt_dot(qq, k_ref[wide_keys(block), :])

    def update(s, keys, m, diagonal=False):
        if diagonal:
            q_off = lax.broadcasted_iota(jnp.int32, s.shape, 0) % blk
            k_off = lax.broadcasted_iota(jnp.int32, s.shape, 1)
            s = jnp.where(k_off <= q_off, s, NEG_BIG)
        m_new = jnp.maximum(m, jnp.max(s, axis=1, keepdims=True))
        p = jnp.exp2(s - m_new).astype(BF16)
        alpha = jnp.exp2(m - m_new)
        for h, v_lanes in enumerate((slice(0, LANES), slice(LANES, 2 * LANES))):
            rows = slice(h * half, (h + 1) * half)
            acc_ref[rows, :] = alpha[rows] * acc_ref[rows, :] + jnp.dot(
                p[rows], vp_ref[keys, v_lanes], preferred_element_type=F32)
        return m_new

    def finish():
        acc = acc_ref[...]
        dl = dl_ref[...]
        lam = (jnp.exp(jnp.sum(dl[0:1] * dl[1:2], axis=1, keepdims=True))
               - jnp.exp(jnp.sum(dl[2:3] * dl[3:4], axis=1, keepdims=True)) + lambda_init)
        low = lax.broadcasted_iota(jnp.int32, (blk, LANES), 1) < HEAD_DIM
        ones = jnp.ones((LANES, LANES), BF16)
        normed = []
        for h in range(2):
            n0 = acc[2 * h * blk:(2 * h + 1) * blk]
            n1 = acc[(2 * h + 1) * blk:(2 * h + 2) * blk]
            l0 = pltpu.roll(n0, HEAD_DIM, 1)
            l1 = pltpu.roll(n1, HEAD_DIM, 1)
            d = n0 - (lam * l0 / l1) * n1
            dsq = jnp.where(low, d * d, 0.0)
            hi = dsq.astype(BF16)
            lo = (dsq - hi.astype(F32)).astype(BF16)
            ms = (jnp.dot(hi, ones, preferred_element_type=F32)
                  + jnp.dot(lo, ones, preferred_element_type=F32)) * (1.0 / HEAD_DIM)
            normed.append(d * lax.rsqrt(ms + LN_EPS * l0 * l0))
        y = jnp.where(low, normed[0], pltpu.roll(normed[1], HEAD_DIM, 1))
        o_ref[...] = (y * subln_ref[...] * (1.0 - lambda_init)).astype(o_ref.dtype)

    n_wide = qi // 2
    acc_ref[...] = jnp.zeros(acc_ref.shape, F32)
    sd_ref[...] = _nt_dot(qq, k_ref[narrow_keys(qi * blk), :])
    scores(sa_ref, 0)
    m = update(sd_ref[...], narrow_keys(qi * blk), jnp.full((rows_n, 1), NEG_BIG, F32), diagonal=True)

    def wide_update(s_ref, block, m):
        return update(s_ref[...], wide_keys(block), m)

    def leftover_update(s_ref, block, m):
        return update(s_ref[:, :blk], narrow_keys(block * wide), m)

    def body(i, m):
        scores(sb_ref, 2 * i + 1)
        m = wide_update(sa_ref, 2 * i, m)
        scores(sa_ref, 2 * i + 2)
        return wide_update(sb_ref, 2 * i + 1, m)

    n_loop = jnp.maximum(n_wide - 1, 0) // 2
    m = lax.fori_loop(0, n_loop, body, m)
    first = 2 * n_loop
    left = n_wide - first
    odd = qi % 2

    def tail(n_left, is_odd):
        def run():
            mm = m
            if n_left >= 1:
                if n_left == 2 or is_odd:
                    scores(sb_ref, first + 1)
                mm = wide_update(sa_ref, first, mm)
            if n_left == 2:
                if is_odd:
                    scores(sa_ref, first + 2)
                mm = wide_update(sb_ref, first + 1, mm)
            if is_odd:
                leftover_update(sa_ref if n_left in (0, 2) else sb_ref, n_wide, mm)
            finish()
        pl.when(jnp.logical_and(left == n_left, odd == int(is_odd)))(run)

    for n_left in range(3):
        for is_odd in (False, True):
            tail(n_left, is_odd)


def _diff_attention(q, k, v, diff_lambda, subln_lanes, blk, lambda_init):
    b, s, wq = q.shape
    assert s % (2 * blk) == 0
    rows_n = LANES // DIFF_QK_DIM * blk
    q_spec = pl.BlockSpec((None, blk, LANES), lambda bi, hp, qi: (bi, qi, hp))
    kv_spec = pl.BlockSpec((None, s, LANES), lambda bi, hp, qi: (bi, 0, hp))
    return pl.pallas_call(
        functools.partial(_diff_kernel, blk=blk, lambda_init=lambda_init),
        grid=(b, wq // LANES, s // blk),
        in_specs=[pl.BlockSpec(diff_lambda.shape, lambda bi, hp, qi: (0, 0)),
                  pl.BlockSpec((1, LANES), lambda bi, hp, qi: (0, 0)), q_spec, kv_spec, kv_spec],
        out_specs=q_spec,
        out_shape=jax.ShapeDtypeStruct(q.shape, BF16),
        scratch_shapes=[pltpu.VMEM((s, 2 * LANES), BF16),
                        pltpu.VMEM((rows_n, 2 * blk), F32),
                        pltpu.VMEM((rows_n, 2 * blk), F32),
                        pltpu.VMEM((rows_n, blk), F32),
                        pltpu.VMEM((rows_n, LANES), F32)],
        compiler_params=pltpu.CompilerParams(
            dimension_semantics=("parallel", "parallel", "arbitrary"),
            vmem_limit_bytes=VMEM_LIMIT_BYTES),
        name="differential",
    )(diff_lambda, subln_lanes, q, k, v)


def _swa_kernel(sinks_ref, q_ref, k_ref, v_ref, o_ref, *, blk, pairs_per_kv_block):
    n_sub = q_ref.shape[0] // blk

    def lanes(hb):
        return slice(hb * LANES, (hb + 1) * LANES)

    shape = (2 * blk, WINDOW + blk)
    rows = lax.broadcasted_iota(jnp.int32, shape, 0)
    q_off = jnp.where(rows >= blk, rows - blk, rows)
    col = lax.broadcasted_iota(jnp.int32, shape, 1)
    rel = (col - WINDOW) - q_off
    in_window = (rel <= 0) & (rel > -WINDOW)
    second_head = lax.broadcasted_iota(jnp.int32, (2 * blk, 1), 0) >= blk
    sink_lane = lax.broadcasted_iota(jnp.int32, (2 * blk, WINDOW), 1) == 0
    sink_row = lax.broadcasted_iota(jnp.int32, (WINDOW, LANES), 0) == 0
    masked_cur = jnp.full((2 * blk, blk), NEG_BIG, F32)
    n_blocks = q_ref.shape[1] // LANES
    chains = []
    for sub in range(n_sub):
        r0 = pl.multiple_of((pl.program_id(1) * n_sub + sub) * blk, blk)
        p0 = pl.multiple_of(jnp.maximum(r0 - WINDOW, 0), WINDOW)
        band = in_window & (col + (r0 - WINDOW) >= 0)
        keys_values = {}
        for hb in range(n_blocks):
            kvb = hb // pairs_per_kv_block
            if kvb not in keys_values:
                v_prev = v_ref[pl.ds(p0, WINDOW), lanes(kvb)]
                keys_values[kvb] = (
                    jnp.concatenate([k_ref[pl.ds(p0, WINDOW), lanes(kvb)], k_ref[pl.ds(r0, blk), lanes(kvb)]], axis=0),
                    jnp.concatenate([jnp.where(sink_row, jnp.zeros_like(v_prev), v_prev),
                                     v_ref[pl.ds(r0, blk), lanes(kvb)]], axis=0))
            kk, vv = keys_values[kvb]
            qq = _head_pair_rows(q_ref[sub * blk:(sub + 1) * blk, lanes(hb)])
            sink = jnp.where(second_head, sinks_ref[SW_HEAD_ORDER[2 * hb + 1]],
                             sinks_ref[SW_HEAD_ORDER[2 * hb]]) * LOG2_E
            fill = jnp.concatenate([jnp.where(sink_lane, sink, NEG_BIG), masked_cur], axis=1)
            chains.append((qq, kk, vv, band, fill))
    scores = [jnp.where(band, _nt_dot(qq, kk), fill) for qq, kk, _, band, fill in chains]
    probs = []
    for s in scores:
        p = jnp.exp2(s - jnp.max(s, axis=1, keepdims=True))
        probs.append((p.astype(BF16), jnp.sum(p, axis=1, keepdims=True)))
    outs = [_merge_head_pair(jnp.dot(p, vv, preferred_element_type=F32) / denom, blk)
            for (p, denom), (_, _, vv, _, _) in zip(probs, chains)]
    for sub in range(n_sub):
        o_ref[sub * blk:(sub + 1) * blk, :] = jnp.concatenate(
            outs[sub * n_blocks:(sub + 1) * n_blocks], axis=1).astype(o_ref.dtype)


def _out_kernel(*refs, n_attn, alpha, n_sub):
    x_ref = refs[0]
    attn_refs = refs[1:1 + n_attn]
    wgm_ref, mk_ref, mv_ref, w_ref, g_ref, b_ref, o_ref = refs[1 + n_attn:]
    ts = x_ref.shape[0] // n_sub
    subs = [slice(i * ts, (i + 1) * ts) for i in range(n_sub)]
    w_m = MEM_HEADS * HEAD_DIM
    blocks = [slice(hb * LANES, (hb + 1) * LANES) for hb in range(w_m // LANES)]

    hs = [jnp.dot(x_ref[r, :].astype(BF16), wgm_ref[...], preferred_element_type=F32) for r in subs]
    mqs = [(h[:, :w_m] * (HEAD_DIM ** -0.5 * LOG2_E)).astype(BF16) for h in hs]
    scores = [[_nt_dot(_head_pair_rows(mq[:, c]), mk_ref[:, c]) for c in blocks] for mq in mqs]
    probs = [[jnp.exp2(s - jnp.max(s, axis=1, keepdims=True)) for s in row] for row in scores]
    mems = [[_merge_head_pair(jnp.dot(p.astype(BF16), mv_ref[:, c], preferred_element_type=F32)
                              / jnp.sum(p, axis=1, keepdims=True), ts) for p, c in zip(row, blocks)]
            for row in probs]
    gated = []
    for h, r, mem in zip(hs, subs, mems):
        gate = h[:, w_m:]
        mixed = jnp.concatenate([a[r, :].astype(F32) for a in attn_refs] + mem, axis=1)
        gated.append((mixed * (gate / (1.0 + jnp.exp(-gate)))).astype(BF16))
    ys = [jnp.dot(g, w_ref[...], preferred_element_type=F32) for g in gated]
    for y, r in zip(ys, subs):
        res = alpha * x_ref[r, :] + y
        c = res - jnp.mean(res, axis=1, keepdims=True)
        var = jnp.mean(c * c, axis=1, keepdims=True)
        o_ref[r, :] = c * lax.rsqrt(var + LN_EPS) * g_ref[...] + b_ref[...]


def _out_layer(x2, attn_outs, w_gm, mk, mv, w_out, ln_g, ln_b, seq, tm, alpha, name):
    m, d = x2.shape
    mem_len = mk.shape[0] // (m // seq)
    steps_per_batch = seq // tm

    def row(wd):
        return pl.BlockSpec((tm, wd), lambda i: (i, 0))

    def whole(a):
        return pl.BlockSpec(a.shape, lambda i: (0, 0))

    mem_spec = pl.BlockSpec((mem_len, mk.shape[1]), lambda i: (i // steps_per_batch, 0))
    in_specs = ([row(d)] + [row(a.shape[1]) for a in attn_outs]
                + [whole(w_gm), mem_spec, mem_spec, whole(w_out), whole(ln_g), whole(ln_b)])
    return pl.pallas_call(
        functools.partial(_out_kernel, n_attn=len(attn_outs), alpha=alpha, n_sub=OUT_SUB_TILES),
        grid=(m // tm,),
        in_specs=in_specs,
        out_specs=row(d),
        out_shape=jax.ShapeDtypeStruct((m, d), F32),
        compiler_params=pltpu.CompilerParams(dimension_semantics=("parallel",),
                                             vmem_limit_bytes=VMEM_LIMIT_BYTES),
        name=name,
    )(x2, *attn_outs, w_gm, mk, mv, w_out, ln_g, ln_b)


ROW_TILE = 1024
OUT_SUB_TILES = 4
IN_ROW_TILE = 1024
IN_EVEN_SUB_TILES = 1
IN_ODD_SUB_TILES = 2
SB_BLOCK = 128
SB_ROWS_PER_STEP = 512
DIFF_BLOCK = 512
SWA_BLOCK = 128
SWA_ROWS_PER_STEP = 512


def _even_layer(x2, mem2, pos2, w_in, w_memkv, diff_lambda, diff_subln, w_out, ln_g, ln_b,
                layer_idx, batch, seq, alpha):
    lambda_init = 0.8 - 0.6 * math.exp(-0.3 * layer_idx)
    w_sb = SB_HEADS * HEAD_DIM
    w_df = DIFF_HEADS * HEAD_DIM
    w_attn = 3 * w_sb + 3 * w_df
    w_in = w_in.astype(BF16)
    sbq, sbk, sbv, dfq, dfk, dfv = _inproj(
        functools.partial(_inproj_even_kernel, n_sub=IN_EVEN_SUB_TILES), "inproj_even", x2, pos2, _inv_freq_lanes(DIFF_QK_DIM), w_in[:, :w_attn],
        (w_sb, w_sb, w_sb, w_df, w_df, w_df), IN_ROW_TILE)
    mk, mv = _memkv(mem2, w_memkv.astype(BF16))

    def b3(a):
        return a.reshape(batch, seq, a.shape[1])

    sb_o = _row_block_attention(functools.partial(_sb_kernel, blk=SB_BLOCK), "stick_breaking",
                                b3(sbq), b3(sbk), b3(sbv), (), (), SB_ROWS_PER_STEP)
    subln_lanes = jnp.tile(diff_subln.astype(F32), LANES // HEAD_DIM)[None, :]
    df_o = _diff_attention(b3(dfq), b3(dfk), b3(dfv), diff_lambda.astype(F32), subln_lanes,
                           DIFF_BLOCK, lambda_init)
    return _out_layer(x2, [sb_o.reshape(x2.shape[0], -1), df_o.reshape(x2.shape[0], -1)], w_in[:, w_attn:],
                      mk, mv, w_out.astype(BF16), ln_g[None, :], ln_b[None, :], seq, ROW_TILE, alpha, "out_even")


def _odd_layer(x2, mem2, pos2, w_in, w_memkv, sinks, w_out, ln_g, ln_b, batch, seq, alpha):
    w_q = SWA_Q_HEADS * HEAD_DIM
    w_kv = SWA_KV_HEADS * HEAD_DIM
    w_m = MEM_HEADS * HEAD_DIM
    def heads_in_order(a, axis, lo):
        return [lax.slice_in_dim(a, lo + h * HEAD_DIM, lo + (h + 1) * HEAD_DIM, axis=axis) for h in SW_HEAD_ORDER]

    gate_lo = w_q + 2 * w_kv + w_m
    last_kv = slice((SWA_KV_HEADS - 1) * HEAD_DIM, w_kv)
    w_in = w_in.astype(BF16)
    w_out = w_out.astype(BF16)
    k_w = w_in[:, w_q:w_q + w_kv]
    v_w = w_in[:, w_q + w_kv:w_q + 2 * w_kv]
    w_attn = jnp.concatenate(heads_in_order(w_in, 1, 0) + [k_w, k_w[:, last_kv], v_w, v_w[:, last_kv]], axis=1)
    w_gm = jnp.concatenate([w_in[:, w_q + 2 * w_kv:gate_lo]] + heads_in_order(w_in, 1, gate_lo)
                           + [w_in[:, gate_lo + w_q:]], axis=1)
    w_out_perm = jnp.concatenate(heads_in_order(w_out, 0, 0) + [w_out[w_q:]], axis=0)

    cq, ck, cv = _inproj(
        functools.partial(_inproj_odd_kernel, n_sub=IN_ODD_SUB_TILES), "inproj_odd", x2, pos2, _inv_freq_lanes(HEAD_DIM), w_attn,
        (w_q, 2 * LANES, 2 * LANES), IN_ROW_TILE)
    mk, mv = _memkv(mem2, w_memkv.astype(BF16))

    def b3(a):
        return a.reshape(batch, seq, a.shape[1])

    pairs_per_kv_block = LANES // HEAD_DIM * SWA_GROUP // 2
    sink_spec = [pl.BlockSpec(memory_space=pltpu.SMEM)]
    c_o = _row_block_attention(
        functools.partial(_swa_kernel, blk=SWA_BLOCK, pairs_per_kv_block=pairs_per_kv_block),
        "sliding_window", b3(cq), b3(ck), b3(cv), (sinks.astype(F32),), sink_spec,
        SWA_ROWS_PER_STEP)
    return _out_layer(x2, [c_o.reshape(x2.shape[0], -1)], w_gm, mk, mv, w_out_perm,
                      ln_g[None, :], ln_b[None, :], seq, ROW_TILE, alpha, "out_odd")


def kernel(x, mem, positions, w_in_even, w_memkv_even, diff_lambda_even, diff_subln_even, w_out_even,
           ln_g_even, ln_b_even, w_in_odd, w_memkv_odd, sinks_odd, w_out_odd, ln_g_odd, ln_b_odd):
    batch, seq, d = x.shape
    depth = w_in_even.shape[0] + w_in_odd.shape[0]
    alpha = (2 * depth) ** 0.25
    x2 = x.reshape(batch * seq, d)
    mem2 = mem.reshape(batch * mem.shape[1], d)
    pos2 = positions.reshape(batch * seq, 1)
    for i in range(depth):
        j = i // 2
        if i % 2 == 0:
            x2 = _even_layer(x2, mem2, pos2, w_in_even[j], w_memkv_even[j], diff_lambda_even[j],
                             diff_subln_even[j], w_out_even[j], ln_g_even[j], ln_b_even[j], i,
                             batch, seq, alpha)
        else:
            x2 = _odd_layer(x2, mem2, pos2, w_in_odd[j], w_memkv_odd[j], sinks_odd[j], w_out_odd[j],
                            ln_g_odd[j], ln_b_odd[j], batch, seq, alpha)
    return x2.reshape(batch, seq, d)
```

```python
import functools
import math

import jax
import jax.numpy as jnp
import numpy as np
from jax import lax
from jax.experimental import pallas as pl
from jax.experimental.pallas import tpu as pltpu

F32 = jnp.float32
BF16 = jnp.bfloat16

HEAD_DIM = 64
LANES = 128
MEM_HEADS = 4
SB_HEADS = 6
DIFF_HEADS = 6
DIFF_QK_DIM = 32
SWA_Q_HEADS = 12
SWA_KV_HEADS = 3
SWA_GROUP = 4
WINDOW = 128
ROPE_THETA = 500000.0
ROPE_FRACTION = 4
LN_EPS = 1e-5
NEG_BIG = -1e30
VMEM_LIMIT_BYTES = 48 * 1024 * 1024
EVEN_ATTENTION_VMEM_LIMIT_BYTES = 56 * 1024 * 1024

SB_EXIT_COST = 92.0
LOG2_E = math.log2(math.e)

SW_HEAD_ORDER = (0, 4, 1, 5, 2, 6, 3, 7, 8, 9, 10, 11)


def _nt_dot(a, b):
    return lax.dot_general(a, b, (((1,), (1,)), ((), ())), preferred_element_type=F32)


def _head_pair_rows(q2):
    lane = lax.broadcasted_iota(jnp.int32, q2.shape, 1)
    low = lane < HEAD_DIM
    zero = jnp.zeros_like(q2)
    return jnp.concatenate([jnp.where(low, q2, zero), jnp.where(low, zero, q2)], axis=0)


def _merge_head_pair(o, rows):
    lane = lax.broadcasted_iota(jnp.int32, (rows, LANES), 1)
    return jnp.where(lane < HEAD_DIM, o[:rows], o[rows:])


def _rope_lane_tables(group, pos_f32, inv_freq_row):
    rot = group // ROPE_FRACTION
    half = rot // 2
    lane = lax.broadcasted_iota(jnp.int32, (1, LANES), 1)
    r = lane % group
    ang = pos_f32 * inv_freq_row
    cos = jnp.cos(ang)
    sin = jnp.sin(ang)
    cos_t = jnp.where(r < rot, cos, 1.0)
    sin_first = jnp.where(r < half, -sin, 0.0)
    sin_second = jnp.where((r >= half) & (r < rot), sin, 0.0)
    return cos_t, sin_first, sin_second, half


def _apply_rope(h, tables):
    cos_t, sin_first, sin_second, half = tables
    outs = []
    for c in range(h.shape[1] // LANES):
        blk = h[:, c * LANES:(c + 1) * LANES]
        outs.append(blk * cos_t + pltpu.roll(blk, LANES - half, 1) * sin_first
                    + pltpu.roll(blk, half, 1) * sin_second)
    return jnp.concatenate(outs, axis=1)


def _inv_freq_lanes(group):
    half = group // ROPE_FRACTION // 2
    inv_freq = np.exp(-(np.arange(half, dtype=np.float32) / np.float32(half)) * np.float32(math.log(ROPE_THETA)))
    lane = np.arange(LANES)
    return jnp.asarray(inv_freq.astype(np.float32)[(lane % group) % half][None, :])


def _projected_sub_tiles(x_ref, w_ref, n_sub):
    ts = x_ref.shape[0] // n_sub
    subs = [slice(i * ts, (i + 1) * ts) for i in range(n_sub)]
    return [(r, jnp.dot(x_ref[r, :].astype(BF16), w_ref[...], preferred_element_type=F32)) for r in subs]


def _sections(h):
    edges = [0]

    def sec(width):
        edges.append(edges[-1] + width)
        return h[:, edges[-2]:edges[-1]]

    return sec


def _inproj_even_kernel(x_ref, pos_ref, invf_ref, w_ref, sbq_ref, sbk_ref, sbv_ref,
                        dfq_ref, dfk_ref, dfv_ref, *, n_sub):
    w_sb = SB_HEADS * HEAD_DIM
    w_df = DIFF_HEADS * 2 * DIFF_QK_DIM
    q_scale = DIFF_QK_DIM ** -0.5 * LOG2_E
    for r, h in _projected_sub_tiles(x_ref, w_ref, n_sub):
        sec = _sections(h)
        tables = _rope_lane_tables(DIFF_QK_DIM, pos_ref[r, :].astype(F32), invf_ref[...])
        sbq_ref[r, :] = (sec(w_sb) * (HEAD_DIM ** -0.5 * LOG2_E)).astype(BF16)
        sbk_ref[r, :] = sec(w_sb).astype(BF16)
        sbv_ref[r, :] = sec(w_sb).astype(BF16)
        dfq_ref[r, :] = (_apply_rope(sec(w_df), tables) * q_scale).astype(BF16)
        dfk_ref[r, :] = _apply_rope(sec(w_df), tables).astype(BF16)
        dfv_ref[r, :] = sec(w_df).astype(BF16)


def _inproj_odd_kernel(x_ref, pos_ref, invf_ref, w_ref, cq_ref, ck_ref, cv_ref, *, n_sub):
    w_kv = ck_ref.shape[1]
    for r, h in _projected_sub_tiles(x_ref, w_ref, n_sub):
        sec = _sections(h)
        tables = _rope_lane_tables(HEAD_DIM, pos_ref[r, :].astype(F32), invf_ref[...])
        cq_ref[r, :] = (_apply_rope(sec(SWA_Q_HEADS * HEAD_DIM), tables)
                        * (HEAD_DIM ** -0.5 * LOG2_E)).astype(BF16)
        ck_ref[r, :] = _apply_rope(sec(w_kv), tables).astype(BF16)
        cv_ref[r, :] = sec(w_kv).astype(BF16)


def _inproj(kernel_fn, name, x2, pos2, invf, w, out_widths, tm):
    m, d = x2.shape
    return pl.pallas_call(
        kernel_fn,
        grid=(m // tm,),
        in_specs=[pl.BlockSpec((tm, d), lambda i: (i, 0)),
                  pl.BlockSpec((tm, 1), lambda i: (i, 0)),
                  pl.BlockSpec((1, LANES), lambda i: (0, 0)),
                  pl.BlockSpec(w.shape, lambda i: (0, 0))],
        out_specs=[pl.BlockSpec((tm, wd), lambda i: (i, 0)) for wd in out_widths],
        out_shape=[jax.ShapeDtypeStruct((m, wd), BF16) for wd in out_widths],
        compiler_params=pltpu.CompilerParams(dimension_semantics=("parallel",),
                                             vmem_limit_bytes=VMEM_LIMIT_BYTES),
        name=name,
    )(x2, pos2, invf, w)


def _memkv_kernel(mem_ref, w_ref, mk_ref, mv_ref):
    kv = jnp.dot(mem_ref[...].astype(BF16), w_ref[...], preferred_element_type=F32)
    half = mk_ref.shape[1]
    mk_ref[...] = kv[:, :half].astype(BF16)
    mv_ref[...] = kv[:, half:].astype(BF16)


def _memkv(mem2, w):
    rows = mem2.shape[0]
    half = w.shape[1] // 2
    return pl.pallas_call(
        _memkv_kernel,
        out_shape=[jax.ShapeDtypeStruct((rows, half), BF16)] * 2,
        compiler_params=pltpu.CompilerParams(vmem_limit_bytes=VMEM_LIMIT_BYTES),
        name="memkv",
    )(mem2, w)


def _strict_lower(n):
    return (lax.broadcasted_iota(jnp.int32, (n, n), 0)
            > lax.broadcasted_iota(jnp.int32, (n, n), 1)).astype(BF16)


def _sb_tiles(chains, later, fillers=()):
    fillers = list(fillers) + [lambda: None] * (2 - len(fillers))
    zs = [_nt_dot(qq, kk) if strict is None else jnp.where(strict, _nt_dot(qq, kk), NEG_BIG)
          for qq, kk, _, strict, _ in chains]
    fillers[0]()
    costs = []
    for z in zs:
        cost = jnp.maximum(z, 0.0) + jnp.log(1.0 + jnp.exp2(-jnp.abs(z))) * LOG2_E
        hi = cost.astype(BF16)
        lo = (cost - hi.astype(F32)).astype(BF16)
        costs.append((cost, hi, lo))
    afters = [jnp.dot(hi, later, preferred_element_type=F32) + jnp.dot(lo, later, preferred_element_type=F32)
              for _, hi, lo in costs]
    fillers[1]()
    outs = []
    for z, (cost, _, _), after, (_, _, vv, _, carry) in zip(zs, costs, afters, chains):
        w = jnp.exp2(z - (cost + after + carry))
        outs.append((jnp.dot(w.astype(BF16), vv, preferred_element_type=F32),
                     jnp.sum(cost, axis=1, keepdims=True)))
    return outs


def _stick_breaking_rows(step, q_ref, k_ref, v_ref, o_ref, fillers=()):
    blk = SB_BLOCK
    n_sub = q_ref.shape[0] // blk
    rows = lax.broadcasted_iota(jnp.int32, (2 * blk, 2 * blk), 0)
    q_off = jnp.where(rows >= blk, rows - blk, rows)
    col = lax.broadcasted_iota(jnp.int32, (2 * blk, 2 * blk), 1)
    no_carry = jnp.zeros((2 * blk, 1), F32)

    first = []
    for sub in range(n_sub):
        r0 = pl.multiple_of((step * n_sub + sub) * blk, blk)
        p0 = pl.multiple_of(jnp.maximum(r0 - blk, 0), blk)
        strict = (col - blk < q_off) & (col + (r0 - blk) >= 0)
        qq = _head_pair_rows(q_ref[sub * blk:(sub + 1) * blk, :])
        kk = jnp.concatenate([k_ref[pl.ds(p0, blk), :], k_ref[pl.ds(r0, blk), :]], axis=0)
        vv = jnp.concatenate([v_ref[pl.ds(p0, blk), :], v_ref[pl.ds(r0, blk), :]], axis=0)
        first.append((qq, kk, vv, strict, no_carry))
    started = _sb_tiles(first, _strict_lower(2 * blk), fillers)
    qqs = [chain[0] for chain in first]
    later = _strict_lower(blk)

    def key_block(sub, t):
        return step * n_sub + sub - 2 - t

    def cond(c):
        t, carries, _ = c
        pending = [cr + jnp.where(key_block(sub, t) >= 0, 0.0, SB_EXIT_COST) for sub, cr in enumerate(carries)]
        return jnp.min(functools.reduce(jnp.minimum, pending)) < SB_EXIT_COST

    def body(c):
        t, carries, accs = c
        chains, live = [], []
        for sub in range(n_sub):
            j = key_block(sub, t)
            start = pl.multiple_of(jnp.maximum(j, 0) * blk, blk)
            chains.append((qqs[sub], k_ref[pl.ds(start, blk), :], v_ref[pl.ds(start, blk), :], None, carries[sub]))
            live.append(jnp.where(j >= 0, 1.0, 0.0))
        outs = _sb_tiles(chains, later)
        return (t + 1, tuple(cr + lv * dsum for cr, lv, (_, dsum) in zip(carries, live, outs)),
                tuple(a + lv * pv for a, lv, (pv, _) in zip(accs, live, outs)))

    init = (0, tuple(dsum for _, dsum in started), tuple(pv for pv, _ in started))
    _, _, accs = lax.while_loop(cond, body, init)
    for sub, a in enumerate(accs):
        o_ref[sub * blk:(sub + 1) * blk, :] = _merge_head_pair(a, blk).astype(o_ref.dtype)


def _row_block_attention(kernel_fn, name, q, k, v, extra_inputs, extra_specs, blk):
    b, s, wq = q.shape
    q_spec = pl.BlockSpec((None, blk, wq), lambda bi, qi: (bi, qi, 0))
    kv_spec = pl.BlockSpec((None, s, k.shape[2]), lambda bi, qi: (bi, 0, 0))
    return pl.pallas_call(
        kernel_fn,
        grid=(b, s // blk),
        in_specs=list(extra_specs) + [q_spec, kv_spec, kv_spec],
        out_specs=q_spec,
        out_shape=jax.ShapeDtypeStruct(q.shape, BF16),
        compiler_params=pltpu.CompilerParams(
            dimension_semantics=("parallel", "arbitrary"),
            vmem_limit_bytes=VMEM_LIMIT_BYTES),
        name=name,
    )(*extra_inputs, q, k, v)


def _even_attention_kernel(dl_ref, subln_ref, q_ref, k_ref, v_ref, sbq_ref, sbk_ref, sbv_ref, o_ref, sbo_ref,
                           vp_ref, sa_ref, sb_ref, acc_ref, *, blk, lambda_init):
    qi = pl.program_id(2)
    seq = v_ref.shape[0]

    @pl.when(qi == 0)
    def _():
        low = lax.broadcasted_iota(jnp.int32, (blk, LANES), 1) < HEAD_DIM

        def build(c, _):
            rows = pl.ds(pl.multiple_of(c * blk, blk), blk)
            vf = v_ref[rows, :].astype(F32)
            vp_ref[rows, :LANES] = jnp.where(low, vf, 1.0).astype(BF16)
            vp_ref[rows, LANES:] = jnp.where(low, pltpu.roll(vf, HEAD_DIM, 1), 1.0).astype(BF16)
            return 0

        lax.fori_loop(0, seq // blk, build, 0)

    q2 = q_ref[...]
    lane = lax.broadcasted_iota(jnp.int32, q2.shape, 1)
    zero = jnp.zeros_like(q2)
    n_maps = LANES // DIFF_QK_DIM
    qq = jnp.concatenate([jnp.where(lane // DIFF_QK_DIM == c, q2, zero) for c in range(n_maps)], axis=0)
    rows_n = n_maps * blk
    half = rows_n // 2

    wide = 2 * blk

    def wide_keys(block):
        return pl.ds(pl.multiple_of(block * wide, wide), wide)

    def narrow_keys(start):
        return pl.ds(pl.multiple_of(start, blk), blk)

    def scores(s_ref, block):
        s_ref[...] = _nt_dot(qq, k_ref[wide_keys(block), :])

    def update(s, keys, m, q_shift=None):
        if q_shift is not None:
            q_off = lax.broadcasted_iota(jnp.int32, s.shape, 0) % blk
            k_off = lax.broadcasted_iota(jnp.int32, s.shape, 1)
            s = jnp.where(k_off <= q_off + q_shift, s, NEG_BIG)
        m_new = jnp.maximum(m, jnp.max(s, axis=1, keepdims=True))
        p = jnp.exp2(s - m_new).astype(BF16)
        alpha = jnp.exp2(m - m_new)
        for h, v_lanes in enumerate((slice(0, LANES), slice(LANES, 2 * LANES))):
            rows = slice(h * half, (h + 1) * half)
            acc_ref[rows, :] = alpha[rows] * acc_ref[rows, :] + jnp.dot(
                p[rows], vp_ref[keys, v_lanes], preferred_element_type=F32)
        return m_new

    def finish():
        acc = acc_ref[...]
        dl = dl_ref[...]
        lam = (jnp.exp(jnp.sum(dl[0:1] * dl[1:2], axis=1, keepdims=True))
               - jnp.exp(jnp.sum(dl[2:3] * dl[3:4], axis=1, keepdims=True)) + lambda_init)
        low = lax.broadcasted_iota(jnp.int32, (blk, LANES), 1) < HEAD_DIM
        ones = jnp.ones((LANES, LANES), BF16)
        normed = []
        for h in range(2):
            n0 = acc[2 * h * blk:(2 * h + 1) * blk]
            n1 = acc[(2 * h + 1) * blk:(2 * h + 2) * blk]
            l0 = pltpu.roll(n0, HEAD_DIM, 1)
            l1 = pltpu.roll(n1, HEAD_DIM, 1)
            d = n0 - (lam * l0 / l1) * n1
            dsq = jnp.where(low, d * d, 0.0)
            hi = dsq.astype(BF16)
            lo = (dsq - hi.astype(F32)).astype(BF16)
            ms = (jnp.dot(hi, ones, preferred_element_type=F32)
                  + jnp.dot(lo, ones, preferred_element_type=F32)) * (1.0 / HEAD_DIM)
            normed.append(d * lax.rsqrt(ms + LN_EPS * l0 * l0))
        y = jnp.where(low, normed[0], pltpu.roll(normed[1], HEAD_DIM, 1))
        o_ref[...] = (y * subln_ref[...] * (1.0 - lambda_init)).astype(o_ref.dtype)

    n_wide = qi // 2
    acc_ref[...] = jnp.zeros(acc_ref.shape, F32)

    def first_scores(half):
        def run():
            keys = narrow_keys(half * blk)
            sa_ref[:, half * blk:(half + 1) * blk] = _nt_dot(qq, k_ref[keys, :])
        return run

    _stick_breaking_rows(qi, sbq_ref, sbk_ref, sbv_ref, sbo_ref, (first_scores(0), first_scores(1)))

    def wide_update(s_ref, block, m):
        return update(s_ref[...], wide_keys(block), m)

    def body(i, m):
        scores(sb_ref, 2 * i + 1)
        m = wide_update(sa_ref, 2 * i, m)
        scores(sa_ref, 2 * i + 2)
        return wide_update(sb_ref, 2 * i + 1, m)

    m = lax.fori_loop(0, n_wide // 2, body, jnp.full((rows_n, 1), NEG_BIG, F32))

    def last(s_ref, m):
        @pl.when(qi % 2 == 0)
        def _():
            update(s_ref[:, :blk], narrow_keys(qi * blk), m, q_shift=0)
            finish()

        @pl.when(qi % 2 == 1)
        def _():
            update(s_ref[...], wide_keys(n_wide), m, q_shift=blk)
            finish()

    @pl.when(n_wide % 2 == 0)
    def _():
        last(sa_ref, m)

    @pl.when(n_wide % 2 == 1)
    def _():
        scores(sb_ref, n_wide)
        last(sb_ref, wide_update(sa_ref, n_wide - 1, m))


def _even_attention(q, k, v, sbq, sbk, sbv, diff_lambda, subln_lanes, blk, lambda_init):
    b, s, wq = q.shape
    assert s % (2 * blk) == 0 and blk % SB_BLOCK == 0 and sbq.shape == q.shape
    rows_n = LANES // DIFF_QK_DIM * blk
    q_spec = pl.BlockSpec((None, blk, LANES), lambda bi, hp, qi: (bi, qi, hp))
    kv_spec = pl.BlockSpec((None, s, LANES), lambda bi, hp, qi: (bi, 0, hp))
    return pl.pallas_call(
        functools.partial(_even_attention_kernel, blk=blk, lambda_init=lambda_init),
        grid=(b, wq // LANES, s // blk),
        in_specs=[pl.BlockSpec(diff_lambda.shape, lambda bi, hp, qi: (0, 0)),
                  pl.BlockSpec((1, LANES), lambda bi, hp, qi: (0, 0)),
                  q_spec, kv_spec, kv_spec, q_spec, kv_spec, kv_spec],
        out_specs=[q_spec, q_spec],
        out_shape=[jax.ShapeDtypeStruct(q.shape, BF16)] * 2,
        scratch_shapes=[pltpu.VMEM((s, 2 * LANES), BF16),
                        pltpu.VMEM((rows_n, 2 * blk), F32),
                        pltpu.VMEM((rows_n, 2 * blk), F32),
                        pltpu.VMEM((rows_n, LANES), F32)],
        compiler_params=pltpu.CompilerParams(
            dimension_semantics=("parallel", "parallel", "arbitrary"),
            vmem_limit_bytes=EVEN_ATTENTION_VMEM_LIMIT_BYTES),
        name="differential_stick_breaking",
    )(diff_lambda, subln_lanes, q, k, v, sbq, sbk, sbv)


def _swa_kernel(sinks_ref, q_ref, k_ref, v_ref, o_ref, *, blk, pairs_per_kv_block):
    n_sub = q_ref.shape[0] // blk

    def lanes(hb):
        return slice(hb * LANES, (hb + 1) * LANES)

    shape = (2 * blk, WINDOW + blk)
    rows = lax.broadcasted_iota(jnp.int32, shape, 0)
    q_off = jnp.where(rows >= blk, rows - blk, rows)
    col = lax.broadcasted_iota(jnp.int32, shape, 1)
    rel = (col - WINDOW) - q_off
    in_window = (rel <= 0) & (rel > -WINDOW)
    second_head = lax.broadcasted_iota(jnp.int32, (2 * blk, 1), 0) >= blk
    sink_lane = lax.broadcasted_iota(jnp.int32, (2 * blk, WINDOW), 1) == 0
    sink_row = lax.broadcasted_iota(jnp.int32, (WINDOW, LANES), 0) == 0
    masked_cur = jnp.full((2 * blk, blk), NEG_BIG, F32)
    n_blocks = q_ref.shape[1] // LANES
    chains = []
    for sub in range(n_sub):
        r0 = pl.multiple_of((pl.program_id(1) * n_sub + sub) * blk, blk)
        p0 = pl.multiple_of(jnp.maximum(r0 - WINDOW, 0), WINDOW)
        band = in_window & (col + (r0 - WINDOW) >= 0)
        keys_values = {}
        for hb in range(n_blocks):
            kvb = hb // pairs_per_kv_block
            if kvb not in keys_values:
                v_prev = v_ref[pl.ds(p0, WINDOW), lanes(kvb)]
                keys_values[kvb] = (
                    jnp.concatenate([k_ref[pl.ds(p0, WINDOW), lanes(kvb)], k_ref[pl.ds(r0, blk), lanes(kvb)]], axis=0),
                    jnp.concatenate([jnp.where(sink_row, jnp.zeros_like(v_prev), v_prev),
                                     v_ref[pl.ds(r0, blk), lanes(kvb)]], axis=0))
            kk, vv = keys_values[kvb]
            qq = _head_pair_rows(q_ref[sub * blk:(sub + 1) * blk, lanes(hb)])
            sink = jnp.where(second_head, sinks_ref[SW_HEAD_ORDER[2 * hb + 1]],
                             sinks_ref[SW_HEAD_ORDER[2 * hb]]) * LOG2_E
            fill = jnp.concatenate([jnp.where(sink_lane, sink, NEG_BIG), masked_cur], axis=1)
            chains.append((qq, kk, vv, band, fill))
    scores = [jnp.where(band, _nt_dot(qq, kk), fill) for qq, kk, _, band, fill in chains]
    probs = []
    for s in scores:
        p = jnp.exp2(s - jnp.max(s, axis=1, keepdims=True))
        probs.append((p.astype(BF16), jnp.sum(p, axis=1, keepdims=True)))
    outs = [_merge_head_pair(jnp.dot(p, vv, preferred_element_type=F32) / denom, blk)
            for (p, denom), (_, _, vv, _, _) in zip(probs, chains)]
    for sub in range(n_sub):
        o_ref[sub * blk:(sub + 1) * blk, :] = jnp.concatenate(
            outs[sub * n_blocks:(sub + 1) * n_blocks], axis=1).astype(o_ref.dtype)


def _out_kernel(*refs, n_attn, alpha, n_sub):
    x_ref = refs[0]
    attn_refs = refs[1:1 + n_attn]
    wgm_ref, mk_ref, mv_ref, w_ref, g_ref, b_ref, o_ref = refs[1 + n_attn:]
    ts = x_ref.shape[0] // n_sub
    subs = [slice(i * ts, (i + 1) * ts) for i in range(n_sub)]
    w_m = MEM_HEADS * HEAD_DIM
    blocks = [slice(hb * LANES, (hb + 1) * LANES) for hb in range(w_m // LANES)]

    hs = [jnp.dot(x_ref[r, :].astype(BF16), wgm_ref[...], preferred_element_type=F32) for r in subs]
    mqs = [(h[:, :w_m] * (HEAD_DIM ** -0.5 * LOG2_E)).astype(BF16) for h in hs]
    scores = [[_nt_dot(_head_pair_rows(mq[:, c]), mk_ref[:, c]) for c in blocks] for mq in mqs]
    probs = [[jnp.exp2(s - jnp.max(s, axis=1, keepdims=True)) for s in row] for row in scores]
    mems = [[_merge_head_pair(jnp.dot(p.astype(BF16), mv_ref[:, c], preferred_element_type=F32)
                              / jnp.sum(p, axis=1, keepdims=True), ts) for p, c in zip(row, blocks)]
            for row in probs]
    gated = []
    for h, r, mem in zip(hs, subs, mems):
        gate = h[:, w_m:]
        mixed = jnp.concatenate([a[r, :].astype(F32) for a in attn_refs] + mem, axis=1)
        gated.append((mixed * (gate / (1.0 + jnp.exp(-gate)))).astype(BF16))
    ys = [jnp.dot(g, w_ref[...], preferred_element_type=F32) for g in gated]
    for y, r in zip(ys, subs):
        res = alpha * x_ref[r, :] + y
        c = res - jnp.mean(res, axis=1, keepdims=True)
        var = jnp.mean(c * c, axis=1, keepdims=True)
        o_ref[r, :] = c * lax.rsqrt(var + LN_EPS) * g_ref[...] + b_ref[...]


def _out_layer(x2, attn_outs, w_gm, mk, mv, w_out, ln_g, ln_b, seq, tm, alpha, name):
    m, d = x2.shape
    mem_len = mk.shape[0] // (m // seq)
    steps_per_batch = seq // tm

    def row(wd):
        return pl.BlockSpec((tm, wd), lambda i: (i, 0))

    def whole(a):
        return pl.BlockSpec(a.shape, lambda i: (0, 0))

    mem_spec = pl.BlockSpec((mem_len, mk.shape[1]), lambda i: (i // steps_per_batch, 0))
    in_specs = ([row(d)] + [row(a.shape[1]) for a in attn_outs]
                + [whole(w_gm), mem_spec, mem_spec, whole(w_out), whole(ln_g), whole(ln_b)])
    return pl.pallas_call(
        functools.partial(_out_kernel, n_attn=len(attn_outs), alpha=alpha, n_sub=OUT_SUB_TILES),
        grid=(m // tm,),
        in_specs=in_specs,
        out_specs=row(d),
        out_shape=jax.ShapeDtypeStruct((m, d), F32),
        compiler_params=pltpu.CompilerParams(dimension_semantics=("parallel",),
                                             vmem_limit_bytes=VMEM_LIMIT_BYTES),
        name=name,
    )(x2, *attn_outs, w_gm, mk, mv, w_out, ln_g, ln_b)


ROW_TILE = 1024
OUT_SUB_TILES = 4
IN_ROW_TILE = 1024
IN_EVEN_SUB_TILES = 1
IN_ODD_SUB_TILES = 2
SB_BLOCK = 128
DIFF_BLOCK = 512
SWA_BLOCK = 128
SWA_ROWS_PER_STEP = 512


def _even_layer(x2, mem2, pos2, w_in, w_memkv, diff_lambda, diff_subln, w_out, ln_g, ln_b,
                layer_idx, batch, seq, alpha):
    lambda_init = 0.8 - 0.6 * math.exp(-0.3 * layer_idx)
    w_sb = SB_HEADS * HEAD_DIM
    w_df = DIFF_HEADS * HEAD_DIM
    w_attn = 3 * w_sb + 3 * w_df
    w_in = w_in.astype(BF16)
    sbq, sbk, sbv, dfq, dfk, dfv = _inproj(
        functools.partial(_inproj_even_kernel, n_sub=IN_EVEN_SUB_TILES), "inproj_even", x2, pos2, _inv_freq_lanes(DIFF_QK_DIM), w_in[:, :w_attn],
        (w_sb, w_sb, w_sb, w_df, w_df, w_df), IN_ROW_TILE)
    mk, mv = _memkv(mem2, w_memkv.astype(BF16))

    def b3(a):
        return a.reshape(batch, seq, a.shape[1])

    subln_lanes = jnp.tile(diff_subln.astype(F32), LANES // HEAD_DIM)[None, :]
    df_o, sb_o = _even_attention(b3(dfq), b3(dfk), b3(dfv), b3(sbq), b3(sbk), b3(sbv),
                                 diff_lambda.astype(F32), subln_lanes, DIFF_BLOCK, lambda_init)
    return _out_layer(x2, [sb_o.reshape(x2.shape[0], -1), df_o.reshape(x2.shape[0], -1)], w_in[:, w_attn:],
                      mk, mv, w_out.astype(BF16), ln_g[None, :], ln_b[None, :], seq, ROW_TILE, alpha, "out_even")


def _odd_layer(x2, mem2, pos2, w_in, w_memkv, sinks, w_out, ln_g, ln_b, batch, seq, alpha):
    w_q = SWA_Q_HEADS * HEAD_DIM
    w_kv = SWA_KV_HEADS * HEAD_DIM
    w_m = MEM_HEADS * HEAD_DIM
    def heads_in_order(a, axis, lo):
        return [lax.slice_in_dim(a, lo + h * HEAD_DIM, lo + (h + 1) * HEAD_DIM, axis=axis) for h in SW_HEAD_ORDER]

    gate_lo = w_q + 2 * w_kv + w_m
    last_kv = slice((SWA_KV_HEADS - 1) * HEAD_DIM, w_kv)
    w_in = w_in.astype(BF16)
    w_out = w_out.astype(BF16)
    k_w = w_in[:, w_q:w_q + w_kv]
    v_w = w_in[:, w_q + w_kv:w_q + 2 * w_kv]
    w_attn = jnp.concatenate(heads_in_order(w_in, 1, 0) + [k_w, k_w[:, last_kv], v_w, v_w[:, last_kv]], axis=1)
    w_gm = jnp.concatenate([w_in[:, w_q + 2 * w_kv:gate_lo]] + heads_in_order(w_in, 1, gate_lo)
                           + [w_in[:, gate_lo + w_q:]], axis=1)
    w_out_perm = jnp.concatenate(heads_in_order(w_out, 0, 0) + [w_out[w_q:]], axis=0)

    cq, ck, cv = _inproj(
        functools.partial(_inproj_odd_kernel, n_sub=IN_ODD_SUB_TILES), "inproj_odd", x2, pos2, _inv_freq_lanes(HEAD_DIM), w_attn,
        (w_q, 2 * LANES, 2 * LANES), IN_ROW_TILE)
    mk, mv = _memkv(mem2, w_memkv.astype(BF16))

    def b3(a):
        return a.reshape(batch, seq, a.shape[1])

    pairs_per_kv_block = LANES // HEAD_DIM * SWA_GROUP // 2
    sink_spec = [pl.BlockSpec(memory_space=pltpu.SMEM)]
    c_o = _row_block_attention(
        functools.partial(_swa_kernel, blk=SWA_BLOCK, pairs_per_kv_block=pairs_per_kv_block),
        "sliding_window", b3(cq), b3(ck), b3(cv), (sinks.astype(F32),), sink_spec,
        SWA_ROWS_PER_STEP)
    return _out_layer(x2, [c_o.reshape(x2.shape[0], -1)], w_gm, mk, mv, w_out_perm,
                      ln_g[None, :], ln_b[None, :], seq, ROW_TILE, alpha, "out_odd")


def kernel(x, mem, positions, w_in_even, w_memkv_even, diff_lambda_even, diff_subln_even, w_out_even,
           ln_g_even, ln_b_even, w_in_odd, w_memkv_odd, sinks_odd, w_out_odd, ln_g_odd, ln_b_odd):
    batch, seq, d = x.shape
    depth = w_in_even.shape[0] + w_in_odd.shape[0]
    alpha = (2 * depth) ** 0.25
    x2 = x.reshape(batch * seq, d)
    mem2 = mem.reshape(batch * mem.shape[1], d)
    pos2 = positions.reshape(batch * seq, 1)
    for i in range(depth):
        j = i // 2
        if i % 2 == 0:
            x2 = _even_layer(x2, mem2, pos2, w_in_even[j], w_memkv_even[j], diff_lambda_even[j],
                             diff_subln_even[j], w_out_even[j], ln_g_even[j], ln_b_even[j], i,
                             batch, seq, alpha)
        else:
            x2 = _odd_layer(x2, mem2, pos2, w_in_odd[j], w_memkv_odd[j], sinks_odd[j], w_out_odd[j],
                            ln_g_odd[j], ln_b_odd[j], batch, seq, alpha)
    return x2.reshape(batch, seq, d)
```

```python
import functools
import math

import jax
import jax.numpy as jnp
import numpy as np
from jax import lax
from jax.experimental import pallas as pl
from jax.experimental.pallas import tpu as pltpu

F32 = jnp.float32
BF16 = jnp.bfloat16

HEAD_DIM = 64
LANES = 128
MEM_HEADS = 4
SB_HEADS = 6
DIFF_HEADS = 6
DIFF_QK_DIM = 32
SWA_Q_HEADS = 12
SWA_KV_HEADS = 3
SWA_GROUP = 4
WINDOW = 128
ROPE_THETA = 500000.0
ROPE_FRACTION = 4
LN_EPS = 1e-5
NEG_BIG = -1e30
VMEM_LIMIT_BYTES = 48 * 1024 * 1024

SB_EXIT_COST = 92.0
LOG2_E = math.log2(math.e)

SW_HEAD_ORDER = (0, 4, 1, 5, 2, 6, 3, 7, 8, 9, 10, 11)


def _nt_dot(a, b):
    return lax.dot_general(a, b, (((1,), (1,)), ((), ())), preferred_element_type=F32)


def _head_pair_rows(q2):
    lane = lax.broadcasted_iota(jnp.int32, q2.shape, 1)
    low = lane < HEAD_DIM
    zero = jnp.zeros_like(q2)
    return jnp.concatenate([jnp.where(low, q2, zero), jnp.where(low, zero, q2)], axis=0)


def _merge_head_pair(o, rows):
    lane = lax.broadcasted_iota(jnp.int32, (rows, LANES), 1)
    return jnp.where(lane < HEAD_DIM, o[:rows], o[rows:])


def _rope_lane_tables(group, pos_f32, inv_freq_row):
    rot = group // ROPE_FRACTION
    half = rot // 2
    lane = lax.broadcasted_iota(jnp.int32, (1, LANES), 1)
    r = lane % group
    ang = pos_f32 * inv_freq_row
    cos = jnp.cos(ang)
    sin = jnp.sin(ang)
    cos_t = jnp.where(r < rot, cos, 1.0)
    sin_first = jnp.where(r < half, -sin, 0.0)
    sin_second = jnp.where((r >= half) & (r < rot), sin, 0.0)
    return cos_t, sin_first, sin_second, half


def _apply_rope(h, tables):
    cos_t, sin_first, sin_second, half = tables
    outs = []
    for c in range(h.shape[1] // LANES):
        blk = h[:, c * LANES:(c + 1) * LANES]
        outs.append(blk * cos_t + pltpu.roll(blk, LANES - half, 1) * sin_first
                    + pltpu.roll(blk, half, 1) * sin_second)
    return jnp.concatenate(outs, axis=1)


def _inv_freq_lanes(group):
    half = group // ROPE_FRACTION // 2
    inv_freq = np.exp(-(np.arange(half, dtype=np.float32) / np.float32(half)) * np.float32(math.log(ROPE_THETA)))
    lane = np.arange(LANES)
    return jnp.asarray(inv_freq.astype(np.float32)[(lane % group) % half][None, :])


def _projected_sub_tiles(x_ref, w_ref, n_sub):
    ts = x_ref.shape[0] // n_sub
    subs = [slice(i * ts, (i + 1) * ts) for i in range(n_sub)]
    return [(r, jnp.dot(x_ref[r, :].astype(BF16), w_ref[...], preferred_element_type=F32)) for r in subs]


def _sections(h):
    edges = [0]

    def sec(width):
        edges.append(edges[-1] + width)
        return h[:, edges[-2]:edges[-1]]

    return sec


def _inproj_even_kernel(x_ref, pos_ref, invf_ref, w_ref, sbq_ref, sbk_ref, sbv_ref,
                        dfq_ref, dfk_ref, dfv_ref, *, n_sub):
    w_sb = SB_HEADS * HEAD_DIM
    w_df = DIFF_HEADS * 2 * DIFF_QK_DIM
    q_scale = DIFF_QK_DIM ** -0.5 * LOG2_E
    for r, h in _projected_sub_tiles(x_ref, w_ref, n_sub):
        sec = _sections(h)
        tables = _rope_lane_tables(DIFF_QK_DIM, pos_ref[r, :].astype(F32), invf_ref[...])
        sbq_ref[r, :] = (sec(w_sb) * (HEAD_DIM ** -0.5 * LOG2_E)).astype(BF16)
        sbk_ref[r, :] = sec(w_sb).astype(BF16)
        sbv_ref[r, :] = sec(w_sb).astype(BF16)
        dfq_ref[r, :] = (_apply_rope(sec(w_df), tables) * q_scale).astype(BF16)
        dfk_ref[r, :] = _apply_rope(sec(w_df), tables).astype(BF16)
        dfv_ref[r, :] = sec(w_df).astype(BF16)


def _inproj_odd_kernel(x_ref, pos_ref, invf_ref, w_ref, cq_ref, ck_ref, cv_ref, *, n_sub):
    w_kv = ck_ref.shape[1]
    for r, h in _projected_sub_tiles(x_ref, w_ref, n_sub):
        sec = _sections(h)
        tables = _rope_lane_tables(HEAD_DIM, pos_ref[r, :].astype(F32), invf_ref[...])
        cq_ref[r, :] = (_apply_rope(sec(SWA_Q_HEADS * HEAD_DIM), tables)
                        * (HEAD_DIM ** -0.5 * LOG2_E)).astype(BF16)
        ck_ref[r, :] = _apply_rope(sec(w_kv), tables).astype(BF16)
        cv_ref[r, :] = sec(w_kv).astype(BF16)


def _inproj(kernel_fn, name, x2, pos2, invf, w, out_widths, tm):
    m, d = x2.shape
    return pl.pallas_call(
        kernel_fn,
        grid=(m // tm,),
        in_specs=[pl.BlockSpec((tm, d), lambda i: (i, 0)),
                  pl.BlockSpec((tm, 1), lambda i: (i, 0)),
                  pl.BlockSpec((1, LANES), lambda i: (0, 0)),
                  pl.BlockSpec(w.shape, lambda i: (0, 0))],
        out_specs=[pl.BlockSpec((tm, wd), lambda i: (i, 0)) for wd in out_widths],
        out_shape=[jax.ShapeDtypeStruct((m, wd), BF16) for wd in out_widths],
        compiler_params=pltpu.CompilerParams(dimension_semantics=("parallel",),
                                             vmem_limit_bytes=VMEM_LIMIT_BYTES),
        name=name,
    )(x2, pos2, invf, w)


def _memkv_kernel(mem_ref, w_ref, mk_ref, mv_ref):
    kv = jnp.dot(mem_ref[...].astype(BF16), w_ref[...], preferred_element_type=F32)
    half = mk_ref.shape[1]
    mk_ref[...] = kv[:, :half].astype(BF16)
    mv_ref[...] = kv[:, half:].astype(BF16)


def _memkv(mem2, w):
    rows = mem2.shape[0]
    half = w.shape[1] // 2
    return pl.pallas_call(
        _memkv_kernel,
        out_shape=[jax.ShapeDtypeStruct((rows, half), BF16)] * 2,
        compiler_params=pltpu.CompilerParams(vmem_limit_bytes=VMEM_LIMIT_BYTES),
        name="memkv",
    )(mem2, w)


def _strict_lower(n):
    return (lax.broadcasted_iota(jnp.int32, (n, n), 0)
            > lax.broadcasted_iota(jnp.int32, (n, n), 1)).astype(BF16)


def _sb_tiles(chains, later):
    zs = [_nt_dot(qq, kk) if strict is None else jnp.where(strict, _nt_dot(qq, kk), NEG_BIG)
          for qq, kk, _, strict, _ in chains]
    costs = []
    for z in zs:
        cost = jnp.maximum(z, 0.0) + jnp.log(1.0 + jnp.exp2(-jnp.abs(z))) * LOG2_E
        hi = cost.astype(BF16)
        lo = (cost - hi.astype(F32)).astype(BF16)
        costs.append((cost, hi, lo))
    afters = [jnp.dot(hi, later, preferred_element_type=F32) + jnp.dot(lo, later, preferred_element_type=F32)
              for _, hi, lo in costs]
    outs = []
    for z, (cost, _, _), after, (_, _, vv, _, carry) in zip(zs, costs, afters, chains):
        w = jnp.exp2(z - (cost + after + carry))
        outs.append((jnp.dot(w.astype(BF16), vv, preferred_element_type=F32),
                     jnp.sum(cost, axis=1, keepdims=True)))
    return outs


def _sb_kernel(q_ref, k_ref, v_ref, o_ref, *, blk):
    n_pairs = q_ref.shape[1] // LANES
    n_sub = q_ref.shape[0] // blk

    def lanes(hp):
        return slice(hp * LANES, (hp + 1) * LANES)

    rows = lax.broadcasted_iota(jnp.int32, (2 * blk, 2 * blk), 0)
    q_off = jnp.where(rows >= blk, rows - blk, rows)
    col = lax.broadcasted_iota(jnp.int32, (2 * blk, 2 * blk), 1)
    later2 = _strict_lower(2 * blk)
    later1 = _strict_lower(blk)
    no_carry = jnp.zeros((2 * blk, 1), F32)

    first = []
    for sub in range(n_sub):
        r0 = pl.multiple_of((pl.program_id(1) * n_sub + sub) * blk, blk)
        p0 = pl.multiple_of(jnp.maximum(r0 - blk, 0), blk)
        strict = (col - blk < q_off) & (col + (r0 - blk) >= 0)
        for hp in range(n_pairs):
            qq = _head_pair_rows(q_ref[sub * blk:(sub + 1) * blk, lanes(hp)])
            kk = jnp.concatenate([k_ref[pl.ds(p0, blk), lanes(hp)], k_ref[pl.ds(r0, blk), lanes(hp)]], axis=0)
            vv = jnp.concatenate([v_ref[pl.ds(p0, blk), lanes(hp)], v_ref[pl.ds(r0, blk), lanes(hp)]], axis=0)
            first.append((qq, kk, vv, strict, no_carry))
    started = _sb_tiles(first, later2)

    def finish(sub):
        bi = pl.program_id(1) * n_sub + sub
        mine = slice(sub * n_pairs, (sub + 1) * n_pairs)
        qqs = [chain[0] for chain in first[mine]]
        accs = tuple(pv for pv, _ in started[mine])
        carries = tuple(dsum for _, dsum in started[mine])

        def cond(c):
            j, carries, _ = c
            return (j >= 0) & (jnp.min(functools.reduce(jnp.minimum, carries)) < SB_EXIT_COST)

        def body(c):
            j, carries, accs = c
            start = pl.multiple_of(j * blk, blk)
            outs = _sb_tiles([(qqs[hp], k_ref[pl.ds(start, blk), lanes(hp)], v_ref[pl.ds(start, blk), lanes(hp)],
                               None, carries[hp]) for hp in range(n_pairs)], later1)
            return (j - 1, tuple(c + dsum for c, (_, dsum) in zip(carries, outs)),
                    tuple(a + pv for a, (pv, _) in zip(accs, outs)))

        _, _, accs = lax.while_loop(cond, body, (bi - 2, carries, accs))
        o_ref[sub * blk:(sub + 1) * blk, :] = jnp.concatenate(
            [_merge_head_pair(a, blk) for a in accs], axis=1).astype(o_ref.dtype)

    for sub in range(n_sub):
        finish(sub)


def _row_block_attention(kernel_fn, name, q, k, v, extra_inputs, extra_specs, blk):
    b, s, wq = q.shape
    q_spec = pl.BlockSpec((None, blk, wq), lambda bi, qi: (bi, qi, 0))
    kv_spec = pl.BlockSpec((None, s, k.shape[2]), lambda bi, qi: (bi, 0, 0))
    return pl.pallas_call(
        kernel_fn,
        grid=(b, s // blk),
        in_specs=list(extra_specs) + [q_spec, kv_spec, kv_spec],
        out_specs=q_spec,
        out_shape=jax.ShapeDtypeStruct(q.shape, BF16),
        compiler_params=pltpu.CompilerParams(
            dimension_semantics=("parallel", "arbitrary"),
            vmem_limit_bytes=VMEM_LIMIT_BYTES),
        name=name,
    )(*extra_inputs, q, k, v)


def _diff_kernel(dl_ref, subln_ref, q_ref, k_ref, v_ref, o_ref, vp_ref, sa_ref, sb_ref, sd_ref, acc_ref, *,
                 blk, lambda_init):
    qi = pl.program_id(2)
    seq = v_ref.shape[0]

    @pl.when(qi == 0)
    def _():
        low = lax.broadcasted_iota(jnp.int32, (blk, LANES), 1) < HEAD_DIM

        def build(c, _):
            rows = pl.ds(pl.multiple_of(c * blk, blk), blk)
            vf = v_ref[rows, :].astype(F32)
            vp_ref[rows, :LANES] = jnp.where(low, vf, 1.0).astype(BF16)
            vp_ref[rows, LANES:] = jnp.where(low, pltpu.roll(vf, HEAD_DIM, 1), 1.0).astype(BF16)
            return 0

        lax.fori_loop(0, seq // blk, build, 0)

    q2 = q_ref[...]
    lane = lax.broadcasted_iota(jnp.int32, q2.shape, 1)
    zero = jnp.zeros_like(q2)
    n_maps = LANES // DIFF_QK_DIM
    qq = jnp.concatenate([jnp.where(lane // DIFF_QK_DIM == c, q2, zero) for c in range(n_maps)], axis=0)
    rows_n = n_maps * blk
    half = rows_n // 2

    wide = 2 * blk

    def wide_keys(block):
        return pl.ds(pl.multiple_of(block * wide, wide), wide)

    def narrow_keys(start):
        return pl.ds(pl.multiple_of(start, blk), blk)

    def scores(s_ref, block):
        s_ref[...] = _nt_dot(qq, k_ref[wide_keys(block), :])

    def update(s, keys, m, diagonal=False):
        if diagonal:
            q_off = lax.broadcasted_iota(jnp.int32, s.shape, 0) % blk
            k_off = lax.broadcasted_iota(jnp.int32, s.shape, 1)
            s = jnp.where(k_off <= q_off, s, NEG_BIG)
        m_new = jnp.maximum(m, jnp.max(s, axis=1, keepdims=True))
        p = jnp.exp2(s - m_new).astype(BF16)
        alpha = jnp.exp2(m - m_new)
        for h, v_lanes in enumerate((slice(0, LANES), slice(LANES, 2 * LANES))):
            rows = slice(h * half, (h + 1) * half)
            acc_ref[rows, :] = alpha[rows] * acc_ref[rows, :] + jnp.dot(
                p[rows], vp_ref[keys, v_lanes], preferred_element_type=F32)
        return m_new

    def finish():
        acc = acc_ref[...]
        dl = dl_ref[...]
        lam = (jnp.exp(jnp.sum(dl[0:1] * dl[1:2], axis=1, keepdims=True))
               - jnp.exp(jnp.sum(dl[2:3] * dl[3:4], axis=1, keepdims=True)) + lambda_init)
        low = lax.broadcasted_iota(jnp.int32, (blk, LANES), 1) < HEAD_DIM
        ones = jnp.ones((LANES, LANES), BF16)
        normed = []
        for h in range(2):
            n0 = acc[2 * h * blk:(2 * h + 1) * blk]
            n1 = acc[(2 * h + 1) * blk:(2 * h + 2) * blk]
            l0 = pltpu.roll(n0, HEAD_DIM, 1)
            l1 = pltpu.roll(n1, HEAD_DIM, 1)
            d = n0 - (lam * l0 / l1) * n1
            dsq = jnp.where(low, d * d, 0.0)
            hi = dsq.astype(BF16)
            lo = (dsq - hi.astype(F32)).astype(BF16)
            ms = (jnp.dot(hi, ones, preferred_element_type=F32)
                  + jnp.dot(lo, ones, preferred_element_type=F32)) * (1.0 / HEAD_DIM)
            normed.append(d * lax.rsqrt(ms + LN_EPS * l0 * l0))
        y = jnp.where(low, normed[0], pltpu.roll(normed[1], HEAD_DIM, 1))
        o_ref[...] = (y * subln_ref[...] * (1.0 - lambda_init)).astype(o_ref.dtype)

    n_wide = qi // 2
    acc_ref[...] = jnp.zeros(acc_ref.shape, F32)
    sd_ref[...] = _nt_dot(qq, k_ref[narrow_keys(qi * blk), :])
    scores(sa_ref, 0)
    m = update(sd_ref[...], narrow_keys(qi * blk), jnp.full((rows_n, 1), NEG_BIG, F32), diagonal=True)

    def wide_update(s_ref, block, m):
        return update(s_ref[...], wide_keys(block), m)

    def leftover_update(s_ref, block, m):
        return update(s_ref[:, :blk], narrow_keys(block * wide), m)

    def body(i, m):
        scores(sb_ref, 2 * i + 1)
        m = wide_update(sa_ref, 2 * i, m)
        scores(sa_ref, 2 * i + 2)
        return wide_update(sb_ref, 2 * i + 1, m)

    n_loop = jnp.maximum(n_wide - 1, 0) // 2
    m = lax.fori_loop(0, n_loop, body, m)
    first = 2 * n_loop
    left = n_wide - first
    odd = qi % 2

    def tail(n_left, is_odd):
        def run():
            mm = m
            if n_left >= 1:
                if n_left == 2 or is_odd:
                    scores(sb_ref, first + 1)
                mm = wide_update(sa_ref, first, mm)
            if n_left == 2:
                if is_odd:
                    scores(sa_ref, first + 2)
                mm = wide_update(sb_ref, first + 1, mm)
            if is_odd:
                leftover_update(sa_ref if n_left in (0, 2) else sb_ref, n_wide, mm)
            finish()
        pl.when(jnp.logical_and(left == n_left, odd == int(is_odd)))(run)

    for n_left in range(3):
        for is_odd in (False, True):
            tail(n_left, is_odd)


def _diff_attention(q, k, v, diff_lambda, subln_lanes, blk, lambda_init):
    b, s, wq = q.shape
    assert s % (2 * blk) == 0
    rows_n = LANES // DIFF_QK_DIM * blk
    q_spec = pl.BlockSpec((None, blk, LANES), lambda bi, hp, qi: (bi, qi, hp))
    kv_spec = pl.BlockSpec((None, s, LANES), lambda bi, hp, qi: (bi, 0, hp))
    return pl.pallas_call(
        functools.partial(_diff_kernel, blk=blk, lambda_init=lambda_init),
        grid=(b, wq // LANES, s // blk),
        in_specs=[pl.BlockSpec(diff_lambda.shape, lambda bi, hp, qi: (0, 0)),
                  pl.BlockSpec((1, LANES), lambda bi, hp, qi: (0, 0)), q_spec, kv_spec, kv_spec],
        out_specs=q_spec,
        out_shape=jax.ShapeDtypeStruct(q.shape, BF16),
        scratch_shapes=[pltpu.VMEM((s, 2 * LANES), BF16),
                        pltpu.VMEM((rows_n, 2 * blk), F32),
                        pltpu.VMEM((rows_n, 2 * blk), F32),
                        pltpu.VMEM((rows_n, blk), F32),
                        pltpu.VMEM((rows_n, LANES), F32)],
        compiler_params=pltpu.CompilerParams(
            dimension_semantics=("parallel", "parallel", "arbitrary"),
            vmem_limit_bytes=VMEM_LIMIT_BYTES),
        name="differential",
    )(diff_lambda, subln_lanes, q, k, v)


def _swa_kernel(sinks_ref, q_ref, k_ref, v_ref, o_ref, *, blk, pairs_per_kv_block):
    n_sub = q_ref.shape[0] // blk

    def lanes(hb):
        return slice(hb * LANES, (hb + 1) * LANES)

    shape = (2 * blk, WINDOW + blk)
    rows = lax.broadcasted_iota(jnp.int32, shape, 0)
    q_off = jnp.where(rows >= blk, rows - blk, rows)
    col = lax.broadcasted_iota(jnp.int32, shape, 1)
    rel = (col - WINDOW) - q_off
    in_window = (rel <= 0) & (rel > -WINDOW)
    second_head = lax.broadcasted_iota(jnp.int32, (2 * blk, 1), 0) >= blk
    sink_lane = lax.broadcasted_iota(jnp.int32, (2 * blk, WINDOW), 1) == 0
    sink_row = lax.broadcasted_iota(jnp.int32, (WINDOW, LANES), 0) == 0
    masked_cur = jnp.full((2 * blk, blk), NEG_BIG, F32)
    n_blocks = q_ref.shape[1] // LANES
    chains = []
    for sub in range(n_sub):
        r0 = pl.multiple_of((pl.program_id(1) * n_sub + sub) * blk, blk)
        p0 = pl.multiple_of(jnp.maximum(r0 - WINDOW, 0), WINDOW)
        band = in_window & (col + (r0 - WINDOW) >= 0)
        keys_values = {}
        for hb in range(n_blocks):
            kvb = hb // pairs_per_kv_block
            if kvb not in keys_values:
                v_prev = v_ref[pl.ds(p0, WINDOW), lanes(kvb)]
                keys_values[kvb] = (
                    jnp.concatenate([k_ref[pl.ds(p0, WINDOW), lanes(kvb)], k_ref[pl.ds(r0, blk), lanes(kvb)]], axis=0),
                    jnp.concatenate([jnp.where(sink_row, jnp.zeros_like(v_prev), v_prev),
                                     v_ref[pl.ds(r0, blk), lanes(kvb)]], axis=0))
            kk, vv = keys_values[kvb]
            qq = _head_pair_rows(q_ref[sub * blk:(sub + 1) * blk, lanes(hb)])
            sink = jnp.where(second_head, sinks_ref[SW_HEAD_ORDER[2 * hb + 1]],
                             sinks_ref[SW_HEAD_ORDER[2 * hb]]) * LOG2_E
            fill = jnp.concatenate([jnp.where(sink_lane, sink, NEG_BIG), masked_cur], axis=1)
            chains.append((qq, kk, vv, band, fill))
    scores = [jnp.where(band, _nt_dot(qq, kk), fill) for qq, kk, _, band, fill in chains]
    probs = []
    for s in scores:
        p = jnp.exp2(s - jnp.max(s, axis=1, keepdims=True))
        probs.append((p.astype(BF16), jnp.sum(p, axis=1, keepdims=True)))
    outs = [_merge_head_pair(jnp.dot(p, vv, preferred_element_type=F32) / denom, blk)
            for (p, denom), (_, _, vv, _, _) in zip(probs, chains)]
    for sub in range(n_sub):
        o_ref[sub * blk:(sub + 1) * blk, :] = jnp.concatenate(
            outs[sub * n_blocks:(sub + 1) * n_blocks], axis=1).astype(o_ref.dtype)


def _out_kernel(*refs, n_attn, alpha, n_sub):
    x_ref = refs[0]
    attn_refs = refs[1:1 + n_attn]
    wgm_ref, mk_ref, mv_ref, w_ref, g_ref, b_ref, o_ref = refs[1 + n_attn:]
    ts = x_ref.shape[0] // n_sub
    subs = [slice(i * ts, (i + 1) * ts) for i in range(n_sub)]
    w_m = MEM_HEADS * HEAD_DIM
    blocks = [slice(hb * LANES, (hb + 1) * LANES) for hb in range(w_m // LANES)]

    hs = [jnp.dot(x_ref[r, :].astype(BF16), wgm_ref[...], preferred_element_type=F32) for r in subs]
    mqs = [(h[:, :w_m] * (HEAD_DIM ** -0.5 * LOG2_E)).astype(BF16) for h in hs]
    scores = [[_nt_dot(_head_pair_rows(mq[:, c]), mk_ref[:, c]) for c in blocks] for mq in mqs]
    probs = [[jnp.exp2(s - jnp.max(s, axis=1, keepdims=True)) for s in row] for row in scores]
    mems = [[_merge_head_pair(jnp.dot(p.astype(BF16), mv_ref[:, c], preferred_element_type=F32)
                              / jnp.sum(p, axis=1, keepdims=True), ts) for p, c in zip(row, blocks)]
            for row in probs]
    gated = []
    for h, r, mem in zip(hs, subs, mems):
        gate = h[:, w_m:]
        mixed = jnp.concatenate([a[r, :].astype(F32) for a in attn_refs] + mem, axis=1)
        gated.append((mixed * (gate / (1.0 + jnp.exp(-gate)))).astype(BF16))
    ys = [jnp.dot(g, w_ref[...], preferred_element_type=F32) for g in gated]
    for y, r in zip(ys, subs):
        res = alpha * x_ref[r, :] + y
        c = res - jnp.mean(res, axis=1, keepdims=True)
        var = jnp.mean(c * c, axis=1, keepdims=True)
        o_ref[r, :] = c * lax.rsqrt(var + LN_EPS) * g_ref[...] + b_ref[...]


def _out_layer(x2, attn_outs, w_gm, mk, mv, w_out, ln_g, ln_b, seq, tm, alpha, name):
    m, d = x2.shape
    mem_len = mk.shape[0] // (m // seq)
    steps_per_batch = seq // tm

    def row(wd):
        return pl.BlockSpec((tm, wd), lambda i: (i, 0))

    def whole(a):
        return pl.BlockSpec(a.shape, lambda i: (0, 0))

    mem_spec = pl.BlockSpec((mem_len, mk.shape[1]), lambda i: (i // steps_per_batch, 0))
    in_specs = ([row(d)] + [row(a.shape[1]) for a in attn_outs]
                + [whole(w_gm), mem_spec, mem_spec, whole(w_out), whole(ln_g), whole(ln_b)])
    return pl.pallas_call(
        functools.partial(_out_kernel, n_attn=len(attn_outs), alpha=alpha, n_sub=OUT_SUB_TILES),
        grid=(m // tm,),
        in_specs=in_specs,
        out_specs=row(d),
        out_shape=jax.ShapeDtypeStruct((m, d), F32),
        compiler_params=pltpu.CompilerParams(dimension_semantics=("parallel",),
                                             vmem_limit_bytes=VMEM_LIMIT_BYTES),
        name=name,
    )(x2, *attn_outs, w_gm, mk, mv, w_out, ln_g, ln_b)


ROW_TILE = 1024
OUT_SUB_TILES = 4
IN_ROW_TILE = 1024
IN_EVEN_SUB_TILES = 1
IN_ODD_SUB_TILES = 2
SB_BLOCK = 128
SB_ROWS_PER_STEP = 512
DIFF_BLOCK = 512
SWA_BLOCK = 128
SWA_ROWS_PER_STEP = 512


def _even_layer(x2, mem2, pos2, w_in, w_memkv, diff_lambda, diff_subln, w_out, ln_g, ln_b,
                layer_idx, batch, seq, alpha):
    lambda_init = 0.8 - 0.6 * math.exp(-0.3 * layer_idx)
    w_sb = SB_HEADS * HEAD_DIM
    w_df = DIFF_HEADS * HEAD_DIM
    w_attn = 3 * w_sb + 3 * w_df
    w_in = w_in.astype(BF16)
    sbq, sbk, sbv, dfq, dfk, dfv = _inproj(
        functools.partial(_inproj_even_kernel, n_sub=IN_EVEN_SUB_TILES), "inproj_even", x2, pos2,
        _inv_freq_lanes(DIFF_QK_DIM), w_in[:, :w_attn], (w_sb, w_sb, w_sb, w_df, w_df, w_df), IN_ROW_TILE)
    mk, mv = _memkv(mem2, w_memkv.astype(BF16))

    def b3(a):
        return a.reshape(batch, seq, a.shape[1])

    sb_o = _row_block_attention(functools.partial(_sb_kernel, blk=SB_BLOCK), "stick_breaking",
                                b3(sbq), b3(sbk), b3(sbv), (), (), SB_ROWS_PER_STEP)
    subln_lanes = jnp.tile(diff_subln.astype(F32), LANES // HEAD_DIM)[None, :]
    df_o = _diff_attention(b3(dfq), b3(dfk), b3(dfv), diff_lambda.astype(F32), subln_lanes,
                           DIFF_BLOCK, lambda_init)
    return _out_layer(x2, [sb_o.reshape(x2.shape[0], -1), df_o.reshape(x2.shape[0], -1)], w_in[:, w_attn:],
                      mk, mv, w_out.astype(BF16), ln_g[None, :], ln_b[None, :], seq, ROW_TILE, alpha, "out_even")


def _odd_layer(x2, mem2, pos2, w_in, w_memkv, sinks, w_out, ln_g, ln_b, batch, seq, alpha):
    w_q = SWA_Q_HEADS * HEAD_DIM
    w_kv = SWA_KV_HEADS * HEAD_DIM
    w_m = MEM_HEADS * HEAD_DIM

    def heads_in_order(a, axis, lo):
        return [lax.slice_in_dim(a, lo + h * HEAD_DIM, lo + (h + 1) * HEAD_DIM, axis=axis) for h in SW_HEAD_ORDER]

    gate_lo = w_q + 2 * w_kv + w_m
    last_kv = slice((SWA_KV_HEADS - 1) * HEAD_DIM, w_kv)
    w_in = w_in.astype(BF16)
    w_out = w_out.astype(BF16)
    k_w = w_in[:, w_q:w_q + w_kv]
    v_w = w_in[:, w_q + w_kv:w_q + 2 * w_kv]
    w_attn = jnp.concatenate(heads_in_order(w_in, 1, 0) + [k_w, k_w[:, last_kv], v_w, v_w[:, last_kv]], axis=1)
    w_gm = jnp.concatenate([w_in[:, w_q + 2 * w_kv:gate_lo]] + heads_in_order(w_in, 1, gate_lo)
                           + [w_in[:, gate_lo + w_q:]], axis=1)
    w_out_perm = jnp.concatenate(heads_in_order(w_out, 0, 0) + [w_out[w_q:]], axis=0)

    cq, ck, cv = _inproj(
        functools.partial(_inproj_odd_kernel, n_sub=IN_ODD_SUB_TILES), "inproj_odd", x2, pos2,
        _inv_freq_lanes(HEAD_DIM), w_attn, (w_q, 2 * LANES, 2 * LANES), IN_ROW_TILE)
    mk, mv = _memkv(mem2, w_memkv.astype(BF16))

    def b3(a):
        return a.reshape(batch, seq, a.shape[1])

    pairs_per_kv_block = LANES // HEAD_DIM * SWA_GROUP // 2
    sink_spec = [pl.BlockSpec(memory_space=pltpu.SMEM)]
    c_o = _row_block_attention(
        functools.partial(_swa_kernel, blk=SWA_BLOCK, pairs_per_kv_block=pairs_per_kv_block),
        "sliding_window", b3(cq), b3(ck), b3(cv), (sinks.astype(F32),), sink_spec,
        SWA_ROWS_PER_STEP)
    return _out_layer(x2, [c_o.reshape(x2.shape[0], -1)], w_gm, mk, mv, w_out_perm,
                      ln_g[None, :], ln_b[None, :], seq, ROW_TILE, alpha, "out_odd")


def kernel(x, mem, positions, w_in_even, w_memkv_even, diff_lambda_even, diff_subln_even, w_out_even,
           ln_g_even, ln_b_even, w_in_odd, w_memkv_odd, sinks_odd, w_out_odd, ln_g_odd, ln_b_odd):
    batch, seq, d = x.shape
    depth = w_in_even.shape[0] + w_in_odd.shape[0]
    alpha = (2 * depth) ** 0.25
    x2 = x.reshape(batch * seq, d)
    mem2 = mem.reshape(batch * mem.shape[1], d)
    pos2 = positions.reshape(batch * seq, 1)
    for i in range(depth):
        j = i // 2
        if i % 2 == 0:
            x2 = _even_layer(x2, mem2, pos2, w_in_even[j], w_memkv_even[j], diff_lambda_even[j],
                             diff_subln_even[j], w_out_even[j], ln_g_even[j], ln_b_even[j], i,
                             batch, seq, alpha)
        else:
            x2 = _odd_layer(x2, mem2, pos2, w_in_odd[j], w_memkv_odd[j], sinks_odd[j], w_out_odd[j],
                            ln_g_odd[j], ln_b_odd[j], batch, seq, alpha)
    return x2.reshape(batch, seq, d)
```

```python
import functools
import math

import jax
import jax.numpy as jnp
import numpy as np
from jax import lax
from jax.experimental import pallas as pl
from jax.experimental.pallas import tpu as pltpu

F32 = jnp.float32
BF16 = jnp.bfloat16

HEAD_DIM = 64
LANES = 128
MEM_HEADS = 4
SB_HEADS = 6
DIFF_HEADS = 6
DIFF_QK_DIM = 32
SWA_Q_HEADS = 12
SWA_KV_HEADS = 3
SWA_GROUP = 4
WINDOW = 128
ROPE_THETA = 500000.0
ROPE_FRACTION = 4
LN_EPS = 1e-5
NEG_BIG = -1e30
VMEM_LIMIT_BYTES = 48 * 1024 * 1024

SB_EXIT_COST = 92.0
LOG2_E = math.log2(math.e)

SW_HEAD_ORDER = (0, 4, 1, 5, 2, 6, 3, 7, 8, 9, 10, 11)


def _nt_dot(a, b):
    return lax.dot_general(a, b, (((1,), (1,)), ((), ())), preferred_element_type=F32)


def _head_pair_rows(q2):
    lane = lax.broadcasted_iota(jnp.int32, q2.shape, 1)
    low = lane < HEAD_DIM
    zero = jnp.zeros_like(q2)
    return jnp.concatenate([jnp.where(low, q2, zero), jnp.where(low, zero, q2)], axis=0)


def _merge_head_pair(o, rows):
    lane = lax.broadcasted_iota(jnp.int32, (rows, LANES), 1)
    return jnp.where(lane < HEAD_DIM, o[:rows], o[rows:])


def _rope_lane_tables(group, pos_f32, inv_freq_row):
    rot = group // ROPE_FRACTION
    half = rot // 2
    lane = lax.broadcasted_iota(jnp.int32, (1, LANES), 1)
    r = lane % group
    ang = pos_f32 * inv_freq_row
    cos = jnp.cos(ang)
    sin = jnp.sin(ang)
    cos_t = jnp.where(r < rot, cos, 1.0)
    sin_first = jnp.where(r < half, -sin, 0.0)
    sin_second = jnp.where((r >= half) & (r < rot), sin, 0.0)
    return cos_t, sin_first, sin_second, half


def _apply_rope(h, tables):
    cos_t, sin_first, sin_second, half = tables
    outs = []
    for c in range(h.shape[1] // LANES):
        blk = h[:, c * LANES:(c + 1) * LANES]
        outs.append(blk * cos_t + pltpu.roll(blk, LANES - half, 1) * sin_first
                    + pltpu.roll(blk, half, 1) * sin_second)
    return jnp.concatenate(outs, axis=1)


def _inv_freq_lanes(group):
    half = group // ROPE_FRACTION // 2
    inv_freq = np.exp(-(np.arange(half, dtype=np.float32) / np.float32(half)) * np.float32(math.log(ROPE_THETA)))
    lane = np.arange(LANES)
    return jnp.asarray(inv_freq.astype(np.float32)[(lane % group) % half][None, :])


def _projected_sub_tiles(x_ref, w_ref, n_sub):
    ts = x_ref.shape[0] // n_sub
    subs = [slice(i * ts, (i + 1) * ts) for i in range(n_sub)]
    return [(r, jnp.dot(x_ref[r, :].astype(BF16), w_ref[...], preferred_element_type=F32)) for r in subs]


def _sections(h):
    edges = [0]

    def sec(width):
        edges.append(edges[-1] + width)
        return h[:, edges[-2]:edges[-1]]

    return sec


def _inproj_even_kernel(x_ref, pos_ref, invf_ref, w_ref, sbq_ref, sbk_ref, sbv_ref,
                        dfq_ref, dfk_ref, dfv_ref, *, n_sub):
    w_sb = SB_HEADS * HEAD_DIM
    w_df = DIFF_HEADS * 2 * DIFF_QK_DIM
    q_scale = DIFF_QK_DIM ** -0.5 * LOG2_E
    for r, h in _projected_sub_tiles(x_ref, w_ref, n_sub):
        sec = _sections(h)
        tables = _rope_lane_tables(DIFF_QK_DIM, pos_ref[r, :].astype(F32), invf_ref[...])
        sbq_ref[r, :] = (sec(w_sb) * (HEAD_DIM ** -0.5 * LOG2_E)).astype(BF16)
        sbk_ref[r, :] = sec(w_sb).astype(BF16)
        sbv_ref[r, :] = sec(w_sb).astype(BF16)
        dfq_ref[r, :] = (_apply_rope(sec(w_df), tables) * q_scale).astype(BF16)
        dfk_ref[r, :] = _apply_rope(sec(w_df), tables).astype(BF16)
        dfv_ref[r, :] = sec(w_df).astype(BF16)


def _inproj_odd_kernel(x_ref, pos_ref, invf_ref, w_ref, cq_ref, ck_ref, cv_ref, *, n_sub):
    w_kv = ck_ref.shape[1]
    for r, h in _projected_sub_tiles(x_ref, w_ref, n_sub):
        sec = _sections(h)
        tables = _rope_lane_tables(HEAD_DIM, pos_ref[r, :].astype(F32), invf_ref[...])
        cq_ref[r, :] = (_apply_rope(sec(SWA_Q_HEADS * HEAD_DIM), tables)
                        * (HEAD_DIM ** -0.5 * LOG2_E)).astype(BF16)
        ck_ref[r, :] = _apply_rope(sec(w_kv), tables).astype(BF16)
        cv_ref[r, :] = sec(w_kv).astype(BF16)


def _inproj(kernel_fn, name, x2, pos2, invf, w, out_widths, tm):
    m, d = x2.shape
    return pl.pallas_call(
        kernel_fn,
        grid=(m // tm,),
        in_specs=[pl.BlockSpec((tm, d), lambda i: (i, 0)),
                  pl.BlockSpec((tm, 1), lambda i: (i, 0)),
                  pl.BlockSpec((1, LANES), lambda i: (0, 0)),
                  pl.BlockSpec(w.shape, lambda i: (0, 0))],
        out_specs=[pl.BlockSpec((tm, wd), lambda i: (i, 0)) for wd in out_widths],
        out_shape=[jax.ShapeDtypeStruct((m, wd), BF16) for wd in out_widths],
        compiler_params=pltpu.CompilerParams(dimension_semantics=("parallel",),
                                             vmem_limit_bytes=VMEM_LIMIT_BYTES),
        name=name,
    )(x2, pos2, invf, w)


def _memkv_kernel(mem_ref, w_ref, mk_ref, mv_ref):
    kv = jnp.dot(mem_ref[...].astype(BF16), w_ref[...], preferred_element_type=F32)
    half = mk_ref.shape[1]
    mk_ref[...] = kv[:, :half].astype(BF16)
    mv_ref[...] = kv[:, half:].astype(BF16)


def _memkv(mem2, w):
    rows = mem2.shape[0]
    half = w.shape[1] // 2
    return pl.pallas_call(
        _memkv_kernel,
        out_shape=[jax.ShapeDtypeStruct((rows, half), BF16)] * 2,
        compiler_params=pltpu.CompilerParams(vmem_limit_bytes=VMEM_LIMIT_BYTES),
        name="memkv",
    )(mem2, w)


def _strict_lower(n):
    return (lax.broadcasted_iota(jnp.int32, (n, n), 0)
            > lax.broadcasted_iota(jnp.int32, (n, n), 1)).astype(BF16)


def _sb_tiles(chains, later):
    zs = [_nt_dot(qq, kk) if strict is None else jnp.where(strict, _nt_dot(qq, kk), NEG_BIG)
          for qq, kk, _, strict, _ in chains]
    costs = []
    for z in zs:
        cost = jnp.maximum(z, 0.0) + jnp.log(1.0 + jnp.exp2(-jnp.abs(z))) * LOG2_E
        hi = cost.astype(BF16)
        lo = (cost - hi.astype(F32)).astype(BF16)
        costs.append((cost, hi, lo))
    afters = [jnp.dot(hi, later, preferred_element_type=F32) + jnp.dot(lo, later, preferred_element_type=F32)
              for _, hi, lo in costs]
    outs = []
    for z, (cost, _, _), after, (_, _, vv, _, carry) in zip(zs, costs, afters, chains):
        w = jnp.exp2(z - (cost + after + carry))
        outs.append((jnp.dot(w.astype(BF16), vv, preferred_element_type=F32),
                     jnp.sum(cost, axis=1, keepdims=True)))
    return outs


def _sb_kernel(q_ref, k_ref, v_ref, o_ref, *, blk):
    n_pairs = q_ref.shape[1] // LANES
    n_sub = q_ref.shape[0] // blk

    def lanes(hp):
        return slice(hp * LANES, (hp + 1) * LANES)

    rows = lax.broadcasted_iota(jnp.int32, (2 * blk, 2 * blk), 0)
    q_off = jnp.where(rows >= blk, rows - blk, rows)
    col = lax.broadcasted_iota(jnp.int32, (2 * blk, 2 * blk), 1)
    later2 = _strict_lower(2 * blk)
    later1 = _strict_lower(blk)
    no_carry = jnp.zeros((2 * blk, 1), F32)

    first = []
    for sub in range(n_sub):
        r0 = pl.multiple_of((pl.program_id(1) * n_sub + sub) * blk, blk)
        p0 = pl.multiple_of(jnp.maximum(r0 - blk, 0), blk)
        strict = (col - blk < q_off) & (col + (r0 - blk) >= 0)
        for hp in range(n_pairs):
            qq = _head_pair_rows(q_ref[sub * blk:(sub + 1) * blk, lanes(hp)])
            kk = jnp.concatenate([k_ref[pl.ds(p0, blk), lanes(hp)], k_ref[pl.ds(r0, blk), lanes(hp)]], axis=0)
            vv = jnp.concatenate([v_ref[pl.ds(p0, blk), lanes(hp)], v_ref[pl.ds(r0, blk), lanes(hp)]], axis=0)
            first.append((qq, kk, vv, strict, no_carry))
    started = _sb_tiles(first, later2)

    def finish(sub):
        bi = pl.program_id(1) * n_sub + sub
        mine = slice(sub * n_pairs, (sub + 1) * n_pairs)
        qqs = [chain[0] for chain in first[mine]]
        accs = tuple(pv for pv, _ in started[mine])
        carries = tuple(dsum for _, dsum in started[mine])

        def cond(c):
            j, carries, _ = c
            return (j >= 0) & (jnp.min(functools.reduce(jnp.minimum, carries)) < SB_EXIT_COST)

        def body(c):
            j, carries, accs = c
            start = pl.multiple_of(j * blk, blk)
            outs = _sb_tiles([(qqs[hp], k_ref[pl.ds(start, blk), lanes(hp)], v_ref[pl.ds(start, blk), lanes(hp)],
                               None, carries[hp]) for hp in range(n_pairs)], later1)
            return (j - 1, tuple(c + dsum for c, (_, dsum) in zip(carries, outs)),
                    tuple(a + pv for a, (pv, _) in zip(accs, outs)))

        _, _, accs = lax.while_loop(cond, body, (bi - 2, carries, accs))
        o_ref[sub * blk:(sub + 1) * blk, :] = jnp.concatenate(
            [_merge_head_pair(a, blk) for a in accs], axis=1).astype(o_ref.dtype)

    for sub in range(n_sub):
        finish(sub)


def _row_block_attention(kernel_fn, name, q, k, v, extra_inputs, extra_specs, blk):
    b, s, wq = q.shape
    q_spec = pl.BlockSpec((None, blk, wq), lambda bi, qi: (bi, qi, 0))
    kv_spec = pl.BlockSpec((None, s, k.shape[2]), lambda bi, qi: (bi, 0, 0))
    return pl.pallas_call(
        kernel_fn,
        grid=(b, s // blk),
        in_specs=list(extra_specs) + [q_spec, kv_spec, kv_spec],
        out_specs=q_spec,
        out_shape=jax.ShapeDtypeStruct(q.shape, BF16),
        compiler_params=pltpu.CompilerParams(
            dimension_semantics=("parallel", "arbitrary"),
            vmem_limit_bytes=VMEM_LIMIT_BYTES),
        name=name,
    )(*extra_inputs, q, k, v)


def _diff_kernel(dl_ref, subln_ref, q_ref, k_ref, v_ref, o_ref, vp_ref, sa_ref, sb_ref, sd_ref, acc_ref, *,
                 blk, lambda_init):
    qi = pl.program_id(2)
    seq = v_ref.shape[0]

    @pl.when(qi == 0)
    def _():
        low = lax.broadcasted_iota(jnp.int32, (blk, LANES), 1) < HEAD_DIM

        def build(c, _):
            rows = pl.ds(pl.multiple_of(c * blk, blk), blk)
            vf = v_ref[rows, :].astype(F32)
            vp_ref[rows, :LANES] = jnp.where(low, vf, 1.0).astype(BF16)
            vp_ref[rows, LANES:] = jnp.where(low, pltpu.roll(vf, HEAD_DIM, 1), 1.0).astype(BF16)
            return 0

        lax.fori_loop(0, seq // blk, build, 0)

    q2 = q_ref[...]
    lane = lax.broadcasted_iota(jnp.int32, q2.shape, 1)
    zero = jnp.zeros_like(q2)
    n_maps = LANES // DIFF_QK_DIM
    qq = jnp.concatenate([jnp.where(lane // DIFF_QK_DIM == c, q2, zero) for c in range(n_maps)], axis=0)
    rows_n = n_maps * blk
    half = rows_n // 2

    wide = 2 * blk

    def wide_keys(block):
        return pl.ds(pl.multiple_of(block * wide, wide), wide)

    def narrow_keys(start):
        return pl.ds(pl.multiple_of(start, blk), blk)

    def scores(s_ref, block):
        s_ref[...] = _nt_dot(qq, k_ref[wide_keys(block), :])

    def update(s, keys, m, diagonal=False):
        if diagonal:
            q_off = lax.broadcasted_iota(jnp.int32, s.shape, 0) % blk
            k_off = lax.broadcasted_iota(jnp.int32, s.shape, 1)
            s = jnp.where(k_off <= q_off, s, NEG_BIG)
        m_new = jnp.maximum(m, jnp.max(s, axis=1, keepdims=True))
        p = jnp.exp2(s - m_new).astype(BF16)
        alpha = jnp.exp2(m - m_new)
        for h, v_lanes in enumerate((slice(0, LANES), slice(LANES, 2 * LANES))):
            rows = slice(h * half, (h + 1) * half)
            acc_ref[rows, :] = alpha[rows] * acc_ref[rows, :] + jnp.dot(
                p[rows], vp_ref[keys, v_lanes], preferred_element_type=F32)
        return m_new

    def finish():
        acc = acc_ref[...]
        dl = dl_ref[...]
        lam = (jnp.exp(jnp.sum(dl[0:1] * dl[1:2], axis=1, keepdims=True))
               - jnp.exp(jnp.sum(dl[2:3] * dl[3:4], axis=1, keepdims=True)) + lambda_init)
        low = lax.broadcasted_iota(jnp.int32, (blk, LANES), 1) < HEAD_DIM
        ones = jnp.ones((LANES, LANES), BF16)
        normed = []
        for h in range(2):
            n0 = acc[2 * h * blk:(2 * h + 1) * blk]
            n1 = acc[(2 * h + 1) * blk:(2 * h + 2) * blk]
            l0 = pltpu.roll(n0, HEAD_DIM, 1)
            l1 = pltpu.roll(n1, HEAD_DIM, 1)
            d = n0 - (lam * l0 / l1) * n1
            dsq = jnp.where(low, d * d, 0.0)
            hi = dsq.astype(BF16)
            lo = (dsq - hi.astype(F32)).astype(BF16)
            ms = (jnp.dot(hi, ones, preferred_element_type=F32)
                  + jnp.dot(lo, ones, preferred_element_type=F32)) * (1.0 / HEAD_DIM)
            normed.append(d * lax.rsqrt(ms + LN_EPS * l0 * l0))
        y = jnp.where(low, normed[0], pltpu.roll(normed[1], HEAD_DIM, 1))
        o_ref[...] = (y * subln_ref[...] * (1.0 - lambda_init)).astype(o_ref.dtype)

    n_wide = qi // 2
    acc_ref[...] = jnp.zeros(acc_ref.shape, F32)
    sd_ref[...] = _nt_dot(qq, k_ref[narrow_keys(qi * blk), :])
    scores(sa_ref, 0)
    m = update(sd_ref[...], narrow_keys(qi * blk), jnp.full((rows_n, 1), NEG_BIG, F32), diagonal=True)

    def wide_update(s_ref, block, m):
        return update(s_ref[...], wide_keys(block), m)

    def leftover_update(s_ref, block, m):
        return update(s_ref[:, :blk], narrow_keys(block * wide), m)

    def body(i, m):
        scores(sb_ref, 2 * i + 1)
        m = wide_update(sa_ref, 2 * i, m)
        scores(sa_ref, 2 * i + 2)
        return wide_update(sb_ref, 2 * i + 1, m)

    n_loop = jnp.maximum(n_wide - 1, 0) // 2
    m = lax.fori_loop(0, n_loop, body, m)
    first = 2 * n_loop
    left = n_wide - first
    odd = qi % 2

    def tail(n_left, is_odd):
        def run():
            mm = m
            if n_left >= 1:
                if n_left == 2 or is_odd:
                    scores(sb_ref, first + 1)
                mm = wide_update(sa_ref, first, mm)
            if n_left == 2:
                if is_odd:
                    scores(sa_ref, first + 2)
                mm = wide_update(sb_ref, first + 1, mm)
            if is_odd:
                leftover_update(sa_ref if n_left in (0, 2) else sb_ref, n_wide, mm)
            finish()
        pl.when(jnp.logical_and(left == n_left, odd == int(is_odd)))(run)

    for n_left in range(3):
        for is_odd in (False, True):
            tail(n_left, is_odd)


def _diff_attention(q, k, v, diff_lambda, subln_lanes, blk, lambda_init):
    b, s, wq = q.shape
    assert s % (2 * blk) == 0
    rows_n = LANES // DIFF_QK_DIM * blk
    q_spec = pl.BlockSpec((None, blk, LANES), lambda bi, hp, qi: (bi, qi, hp))
    kv_spec = pl.BlockSpec((None, s, LANES), lambda bi, hp, qi: (bi, 0, hp))
    return pl.pallas_call(
        functools.partial(_diff_kernel, blk=blk, lambda_init=lambda_init),
        grid=(b, wq // LANES, s // blk),
        in_specs=[pl.BlockSpec(diff_lambda.shape, lambda bi, hp, qi: (0, 0)),
                  pl.BlockSpec((1, LANES), lambda bi, hp, qi: (0, 0)), q_spec, kv_spec, kv_spec],
        out_specs=q_spec,
        out_shape=jax.ShapeDtypeStruct(q.shape, BF16),
        scratch_shapes=[pltpu.VMEM((s, 2 * LANES), BF16),
                        pltpu.VMEM((rows_n, 2 * blk), F32),
                        pltpu.VMEM((rows_n, 2 * blk), F32),
                        pltpu.VMEM((rows_n, blk), F32),
                        pltpu.VMEM((rows_n, LANES), F32)],
        compiler_params=pltpu.CompilerParams(
            dimension_semantics=("parallel", "parallel", "arbitrary"),
            vmem_limit_bytes=VMEM_LIMIT_BYTES),
        name="differential",
    )(diff_lambda, subln_lanes, q, k, v)


def _swa_kernel(sinks_ref, q_ref, k_ref, v_ref, o_ref, *, blk, pairs_per_kv_block):
    n_sub = q_ref.shape[0] // blk

    def lanes(hb):
        return slice(hb * LANES, (hb + 1) * LANES)

    shape = (2 * blk, WINDOW + blk)
    rows = lax.broadcasted_iota(jnp.int32, shape, 0)
    q_off = jnp.where(rows >= blk, rows - blk, rows)
    col = lax.broadcasted_iota(jnp.int32, shape, 1)
    rel = (col - WINDOW) - q_off
    in_window = (rel <= 0) & (rel > -WINDOW)
    second_head = lax.broadcasted_iota(jnp.int32, (2 * blk, 1), 0) >= blk
    sink_lane = lax.broadcasted_iota(jnp.int32, (2 * blk, WINDOW), 1) == 0
    sink_row = lax.broadcasted_iota(jnp.int32, (WINDOW, LANES), 0) == 0
    masked_cur = jnp.full((2 * blk, blk), NEG_BIG, F32)
    n_blocks = q_ref.shape[1] // LANES
    chains = []
    for sub in range(n_sub):
        r0 = pl.multiple_of((pl.program_id(1) * n_sub + sub) * blk, blk)
        p0 = pl.multiple_of(jnp.maximum(r0 - WINDOW, 0), WINDOW)
        band = in_window & (col + (r0 - WINDOW) >= 0)
        keys_values = {}
        for hb in range(n_blocks):
            kvb = hb // pairs_per_kv_block
            if kvb not in keys_values:
                v_prev = v_ref[pl.ds(p0, WINDOW), lanes(kvb)]
                keys_values[kvb] = (
                    jnp.concatenate([k_ref[pl.ds(p0, WINDOW), lanes(kvb)], k_ref[pl.ds(r0, blk), lanes(kvb)]], axis=0),
                    jnp.concatenate([jnp.where(sink_row, jnp.zeros_like(v_prev), v_prev),
                                     v_ref[pl.ds(r0, blk), lanes(kvb)]], axis=0))
            kk, vv = keys_values[kvb]
            qq = _head_pair_rows(q_ref[sub * blk:(sub + 1) * blk, lanes(hb)])
            sink = jnp.where(second_head, sinks_ref[SW_HEAD_ORDER[2 * hb + 1]],
                             sinks_ref[SW_HEAD_ORDER[2 * hb]]) * LOG2_E
            fill = jnp.concatenate([jnp.where(sink_lane, sink, NEG_BIG), masked_cur], axis=1)
            chains.append((qq, kk, vv, band, fill))
    scores = [jnp.where(band, _nt_dot(qq, kk), fill) for qq, kk, _, band, fill in chains]
    probs = []
    for s in scores:
        p = jnp.exp2(s - jnp.max(s, axis=1, keepdims=True))
        probs.append((p.astype(BF16), jnp.sum(p, axis=1, keepdims=True)))
    outs = [_merge_head_pair(jnp.dot(p, vv, preferred_element_type=F32) / denom, blk)
            for (p, denom), (_, _, vv, _, _) in zip(probs, chains)]
    for sub in range(n_sub):
        o_ref[sub * blk:(sub + 1) * blk, :] = jnp.concatenate(
            outs[sub * n_blocks:(sub + 1) * n_blocks], axis=1).astype(o_ref.dtype)


def _out_kernel(*refs, n_attn, alpha, n_sub):
    x_ref = refs[0]
    attn_refs = refs[1:1 + n_attn]
    wgm_ref, mk_ref, mv_ref, w_ref, g_ref, b_ref, o_ref = refs[1 + n_attn:]
    ts = x_ref.shape[0] // n_sub
    subs = [slice(i * ts, (i + 1) * ts) for i in range(n_sub)]
    w_m = MEM_HEADS * HEAD_DIM
    blocks = [slice(hb * LANES, (hb + 1) * LANES) for hb in range(w_m // LANES)]

    hs = [jnp.dot(x_ref[r, :].astype(BF16), wgm_ref[...], preferred_element_type=F32) for r in subs]
    mqs = [(h[:, :w_m] * (HEAD_DIM ** -0.5 * LOG2_E)).astype(BF16) for h in hs]
    scores = [[_nt_dot(_head_pair_rows(mq[:, c]), mk_ref[:, c]) for c in blocks] for mq in mqs]
    probs = [[jnp.exp2(s - jnp.max(s, axis=1, keepdims=True)) for s in row] for row in scores]
    mems = [[_merge_head_pair(jnp.dot(p.astype(BF16), mv_ref[:, c], preferred_element_type=F32)
                              / jnp.sum(p, axis=1, keepdims=True), ts) for p, c in zip(row, blocks)]
            for row in probs]
    gated = []
    for h, r, mem in zip(hs, subs, mems):
        gate = h[:, w_m:]
        mixed = jnp.concatenate([a[r, :].astype(F32) for a in attn_refs] + mem, axis=1)
        gated.append((mixed * (gate / (1.0 + jnp.exp(-gate)))).astype(BF16))
    ys = [jnp.dot(g, w_ref[...], preferred_element_type=F32) for g in gated]
    for y, r in zip(ys, subs):
        res = alpha * x_ref[r, :] + y
        c = res - jnp.mean(res, axis=1, keepdims=True)
        var = jnp.mean(c * c, axis=1, keepdims=True)
        o_ref[r, :] = c * lax.rsqrt(var + LN_EPS) * g_ref[...] + b_ref[...]


def _out_layer(x2, attn_outs, w_gm, mk, mv, w_out, ln_g, ln_b, seq, tm, alpha, name):
    m, d = x2.shape
    mem_len = mk.shape[0] // (m // seq)
    steps_per_batch = seq // tm

    def row(wd):
        return pl.BlockSpec((tm, wd), lambda i: (i, 0))

    def whole(a):
        return pl.BlockSpec(a.shape, lambda i: (0, 0))

    mem_spec = pl.BlockSpec((mem_len, mk.shape[1]), lambda i: (i // steps_per_batch, 0))
    in_specs = ([row(d)] + [row(a.shape[1]) for a in attn_outs]
                + [whole(w_gm), mem_spec, mem_spec, whole(w_out), whole(ln_g), whole(ln_b)])
    return pl.pallas_call(
        functools.partial(_out_kernel, n_attn=len(attn_outs), alpha=alpha, n_sub=OUT_SUB_TILES),
        grid=(m // tm,),
        in_specs=in_specs,
        out_specs=row(d),
        out_shape=jax.ShapeDtypeStruct((m, d), F32),
        compiler_params=pltpu.CompilerParams(dimension_semantics=("parallel",),
                                             vmem_limit_bytes=VMEM_LIMIT_BYTES),
        name=name,
    )(x2, *attn_outs, w_gm, mk, mv, w_out, ln_g, ln_b)


ROW_TILE = 1024
OUT_SUB_TILES = 4
IN_ROW_TILE = 1024
IN_EVEN_SUB_TILES = 1
IN_ODD_SUB_TILES = 4
SB_BLOCK = 128
SB_ROWS_PER_STEP = 512
DIFF_BLOCK = 512
SWA_BLOCK = 128
SWA_ROWS_PER_STEP = 512


def _even_layer(x2, mem2, pos2, w_in, w_memkv, diff_lambda, diff_subln, w_out, ln_g, ln_b,
                layer_idx, batch, seq, alpha):
    lambda_init = 0.8 - 0.6 * math.exp(-0.3 * layer_idx)
    w_sb = SB_HEADS * HEAD_DIM
    w_df = DIFF_HEADS * HEAD_DIM
    w_attn = 3 * w_sb + 3 * w_df
    w_in = w_in.astype(BF16)
    sbq, sbk, sbv, dfq, dfk, dfv = _inproj(
        functools.partial(_inproj_even_kernel, n_sub=IN_EVEN_SUB_TILES), "inproj_even", x2, pos2,
        _inv_freq_lanes(DIFF_QK_DIM), w_in[:, :w_attn], (w_sb, w_sb, w_sb, w_df, w_df, w_df), IN_ROW_TILE)
    mk, mv = _memkv(mem2, w_memkv.astype(BF16))

    def b3(a):
        return a.reshape(batch, seq, a.shape[1])

    sb_o = _row_block_attention(functools.partial(_sb_kernel, blk=SB_BLOCK), "stick_breaking",
                                b3(sbq), b3(sbk), b3(sbv), (), (), SB_ROWS_PER_STEP)
    subln_lanes = jnp.tile(diff_subln.astype(F32), LANES // HEAD_DIM)[None, :]
    df_o = _diff_attention(b3(dfq), b3(dfk), b3(dfv), diff_lambda.astype(F32), subln_lanes,
                           DIFF_BLOCK, lambda_init)
    return _out_layer(x2, [sb_o.reshape(x2.shape[0], -1), df_o.reshape(x2.shape[0], -1)], w_in[:, w_attn:],
                      mk, mv, w_out.astype(BF16), ln_g[None, :], ln_b[None, :], seq, ROW_TILE, alpha, "out_even")


def _odd_layer(x2, mem2, pos2, w_in, w_memkv, sinks, w_out, ln_g, ln_b, batch, seq, alpha):
    w_q = SWA_Q_HEADS * HEAD_DIM
    w_kv = SWA_KV_HEADS * HEAD_DIM
    w_m = MEM_HEADS * HEAD_DIM

    def heads_in_order(a, axis, lo):
        return [lax.slice_in_dim(a, lo + h * HEAD_DIM, lo + (h + 1) * HEAD_DIM, axis=axis) for h in SW_HEAD_ORDER]

    gate_lo = w_q + 2 * w_kv + w_m
    last_kv = slice((SWA_KV_HEADS - 1) * HEAD_DIM, w_kv)
    w_in = w_in.astype(BF16)
    w_out = w_out.astype(BF16)
    k_w = w_in[:, w_q:w_q + w_kv]
    v_w = w_in[:, w_q + w_kv:w_q + 2 * w_kv]
    w_attn = jnp.concatenate(heads_in_order(w_in, 1, 0) + [k_w, k_w[:, last_kv], v_w, v_w[:, last_kv]], axis=1)
    w_gm = jnp.concatenate([w_in[:, w_q + 2 * w_kv:gate_lo]] + heads_in_order(w_in, 1, gate_lo)
                           + [w_in[:, gate_lo + w_q:]], axis=1)
    w_out_perm = jnp.concatenate(heads_in_order(w_out, 0, 0) + [w_out[w_q:]], axis=0)

    cq, ck, cv = _inproj(
        functools.partial(_inproj_odd_kernel, n_sub=IN_ODD_SUB_TILES), "inproj_odd", x2, pos2,
        _inv_freq_lanes(HEAD_DIM), w_attn, (w_q, 2 * LANES, 2 * LANES), IN_ROW_TILE)
    mk, mv = _memkv(mem2, w_memkv.astype(BF16))

    def b3(a):
        return a.reshape(batch, seq, a.shape[1])

    pairs_per_kv_block = LANES // HEAD_DIM * SWA_GROUP // 2
    sink_spec = [pl.BlockSpec(memory_space=pltpu.SMEM)]
    c_o = _row_block_attention(
        functools.partial(_swa_kernel, blk=SWA_BLOCK, pairs_per_kv_block=pairs_per_kv_block),
        "sliding_window", b3(cq), b3(ck), b3(cv), (sinks.astype(F32),), sink_spec,
        SWA_ROWS_PER_STEP)
    return _out_layer(x2, [c_o.reshape(x2.shape[0], -1)], w_gm, mk, mv, w_out_perm,
                      ln_g[None, :], ln_b[None, :], seq, ROW_TILE, alpha, "out_odd")


def kernel(x, mem, positions, w_in_even, w_memkv_even, diff_lambda_even, diff_subln_even, w_out_even,
           ln_g_even, ln_b_even, w_in_odd, w_memkv_odd, sinks_odd, w_out_odd, ln_g_odd, ln_b_odd):
    batch, seq, d = x.shape
    depth = w_in_even.shape[0] + w_in_odd.shape[0]
    alpha = (2 * depth) ** 0.25
    x2 = x.reshape(batch * seq, d)
    mem2 = mem.reshape(batch * mem.shape[1], d)
    pos2 = positions.reshape(batch * seq, 1)
    for i in range(depth):
        j = i // 2
        if i % 2 == 0:
            x2 = _even_layer(x2, mem2, pos2, w_in_even[j], w_memkv_even[j], diff_lambda_even[j],
                             diff_subln_even[j], w_out_even[j], ln_g_even[j], ln_b_even[j], i,
                             batch, seq, alpha)
        else:
            x2 = _odd_layer(x2, mem2, pos2, w_in_odd[j], w_memkv_odd[j], sinks_odd[j], w_out_odd[j],
                            ln_g_odd[j], ln_b_odd[j], batch, seq, alpha)
    return x2.reshape(batch, seq, d)
```

```python
import functools
import math

import jax
import jax.numpy as jnp
import numpy as np
from jax import lax
from jax.experimental import pallas as pl
from jax.experimental.pallas import tpu as pltpu

F32 = jnp.float32
BF16 = jnp.bfloat16

HEAD_DIM = 64
LANES = 128
MEM_HEADS = 4
SB_HEADS = 6
DIFF_HEADS = 6
DIFF_QK_DIM = 32
SWA_Q_HEADS = 12
SWA_KV_HEADS = 3
SWA_GROUP = 4
WINDOW = 128
ROPE_THETA = 500000.0
ROPE_FRACTION = 4
LN_EPS = 1e-5
NEG_BIG = -1e30
VMEM_LIMIT_BYTES = 48 * 1024 * 1024

SB_EXIT_COST = 92.0
LOG2_E = math.log2(math.e)

SW_HEAD_ORDER = (0, 4, 1, 5, 2, 6, 3, 7, 8, 9, 10, 11)


def _nt_dot(a, b):
    return lax.dot_general(a, b, (((1,), (1,)), ((), ())), preferred_element_type=F32)


def _head_pair_rows(q2):
    lane = lax.broadcasted_iota(jnp.int32, q2.shape, 1)
    low = lane < HEAD_DIM
    zero = jnp.zeros_like(q2)
    return jnp.concatenate([jnp.where(low, q2, zero), jnp.where(low, zero, q2)], axis=0)


def _merge_head_pair(o, rows):
    lane = lax.broadcasted_iota(jnp.int32, (rows, LANES), 1)
    return jnp.where(lane < HEAD_DIM, o[:rows], o[rows:])


def _rope_lane_tables(group, pos_f32, inv_freq_row):
    rot = group // ROPE_FRACTION
    half = rot // 2
    lane = lax.broadcasted_iota(jnp.int32, (1, LANES), 1)
    r = lane % group
    ang = pos_f32 * inv_freq_row
    cos = jnp.cos(ang)
    sin = jnp.sin(ang)
    cos_t = jnp.where(r < rot, cos, 1.0)
    sin_first = jnp.where(r < half, -sin, 0.0)
    sin_second = jnp.where((r >= half) & (r < rot), sin, 0.0)
    return cos_t, sin_first, sin_second, half


def _apply_rope(h, tables):
    cos_t, sin_first, sin_second, half = tables
    outs = []
    for c in range(h.shape[1] // LANES):
        blk = h[:, c * LANES:(c + 1) * LANES]
        outs.append(blk * cos_t + pltpu.roll(blk, LANES - half, 1) * sin_first
                    + pltpu.roll(blk, half, 1) * sin_second)
    return jnp.concatenate(outs, axis=1)


def _inv_freq_lanes(group):
    half = group // ROPE_FRACTION // 2
    inv_freq = np.exp(-(np.arange(half, dtype=np.float32) / np.float32(half)) * np.float32(math.log(ROPE_THETA)))
    lane = np.arange(LANES)
    return jnp.asarray(inv_freq.astype(np.float32)[(lane % group) % half][None, :])


def _projected_sub_tiles(x_ref, w_ref, n_sub):
    ts = x_ref.shape[0] // n_sub
    subs = [slice(i * ts, (i + 1) * ts) for i in range(n_sub)]
    return [(r, jnp.dot(x_ref[r, :].astype(BF16), w_ref[...], preferred_element_type=F32)) for r in subs]


def _sections(h):
    edges = [0]

    def sec(width):
        edges.append(edges[-1] + width)
        return h[:, edges[-2]:edges[-1]]

    return sec


def _inproj_even_kernel(x_ref, pos_ref, invf_ref, w_ref, sbq_ref, sbk_ref, sbv_ref,
                        dfq_ref, dfk_ref, dfv_ref, *, n_sub):
    w_sb = SB_HEADS * HEAD_DIM
    w_df = DIFF_HEADS * 2 * DIFF_QK_DIM
    q_scale = DIFF_QK_DIM ** -0.5 * LOG2_E
    for r, h in _projected_sub_tiles(x_ref, w_ref, n_sub):
        sec = _sections(h)
        tables = _rope_lane_tables(DIFF_QK_DIM, pos_ref[r, :].astype(F32), invf_ref[...])
        sbq_ref[r, :] = (sec(w_sb) * (HEAD_DIM ** -0.5 * LOG2_E)).astype(BF16)
        sbk_ref[r, :] = sec(w_sb).astype(BF16)
        sbv_ref[r, :] = sec(w_sb).astype(BF16)
        dfq_ref[r, :] = (_apply_rope(sec(w_df), tables) * q_scale).astype(BF16)
        dfk_ref[r, :] = _apply_rope(sec(w_df), tables).astype(BF16)
        dfv_ref[r, :] = sec(w_df).astype(BF16)


def _inproj_odd_kernel(x_ref, pos_ref, invf_ref, w_ref, cq_ref, ck_ref, cv_ref, *, n_sub):
    w_kv = ck_ref.shape[1]
    for r, h in _projected_sub_tiles(x_ref, w_ref, n_sub):
        sec = _sections(h)
        tables = _rope_lane_tables(HEAD_DIM, pos_ref[r, :].astype(F32), invf_ref[...])
        cq_ref[r, :] = (_apply_rope(sec(SWA_Q_HEADS * HEAD_DIM), tables)
                        * (HEAD_DIM ** -0.5 * LOG2_E)).astype(BF16)
        ck_ref[r, :] = _apply_rope(sec(w_kv), tables).astype(BF16)
        cv_ref[r, :] = sec(w_kv).astype(BF16)


def _inproj(kernel_fn, name, x2, pos2, invf, w, out_widths, tm):
    m, d = x2.shape
    return pl.pallas_call(
        kernel_fn,
        grid=(m // tm,),
        in_specs=[pl.BlockSpec((tm, d), lambda i: (i, 0)),
                  pl.BlockSpec((tm, 1), lambda i: (i, 0)),
                  pl.BlockSpec((1, LANES), lambda i: (0, 0)),
                  pl.BlockSpec(w.shape, lambda i: (0, 0))],
        out_specs=[pl.BlockSpec((tm, wd), lambda i: (i, 0)) for wd in out_widths],
        out_shape=[jax.ShapeDtypeStruct((m, wd), BF16) for wd in out_widths],
        compiler_params=pltpu.CompilerParams(dimension_semantics=("parallel",),
                                             vmem_limit_bytes=VMEM_LIMIT_BYTES),
        name=name,
    )(x2, pos2, invf, w)


def _memkv_kernel(mem_ref, w_ref, mk_ref, mv_ref):
    kv = jnp.dot(mem_ref[...].astype(BF16), w_ref[...], preferred_element_type=F32)
    half = mk_ref.shape[1]
    mk_ref[...] = kv[:, :half].astype(BF16)
    mv_ref[...] = kv[:, half:].astype(BF16)


def _memkv(mem2, w):
    rows = mem2.shape[0]
    half = w.shape[1] // 2
    return pl.pallas_call(
        _memkv_kernel,
        out_shape=[jax.ShapeDtypeStruct((rows, half), BF16)] * 2,
        compiler_params=pltpu.CompilerParams(vmem_limit_bytes=VMEM_LIMIT_BYTES),
        name="memkv",
    )(mem2, w)


def _strict_lower(n):
    return (lax.broadcasted_iota(jnp.int32, (n, n), 0)
            > lax.broadcasted_iota(jnp.int32, (n, n), 1)).astype(BF16)


def _sb_tiles(chains, later):
    zs = [_nt_dot(qq, kk) if strict is None else jnp.where(strict, _nt_dot(qq, kk), NEG_BIG)
          for qq, kk, _, strict, _ in chains]
    costs = []
    for z in zs:
        cost = jnp.maximum(z, 0.0) + jnp.log(1.0 + jnp.exp2(-jnp.abs(z))) * LOG2_E
        hi = cost.astype(BF16)
        lo = (cost - hi.astype(F32)).astype(BF16)
        costs.append((cost, hi, lo))
    afters = [jnp.dot(hi, later, preferred_element_type=F32) + jnp.dot(lo, later, preferred_element_type=F32)
              for _, hi, lo in costs]
    outs = []
    for z, (cost, _, _), after, (_, _, vv, _, carry) in zip(zs, costs, afters, chains):
        w = jnp.exp2(z - (cost + after + carry))
        outs.append((jnp.dot(w.astype(BF16), vv, preferred_element_type=F32),
                     jnp.sum(cost, axis=1, keepdims=True)))
    return outs


def _sb_kernel(q_ref, k_ref, v_ref, o_ref, *, blk):
    n_pairs = q_ref.shape[1] // LANES
    n_sub = q_ref.shape[0] // blk

    def lanes(hp):
        return slice(hp * LANES, (hp + 1) * LANES)

    rows = lax.broadcasted_iota(jnp.int32, (2 * blk, 2 * blk), 0)
    q_off = jnp.where(rows >= blk, rows - blk, rows)
    col = lax.broadcasted_iota(jnp.int32, (2 * blk, 2 * blk), 1)
    later2 = _strict_lower(2 * blk)
    later1 = _strict_lower(blk)
    no_carry = jnp.zeros((2 * blk, 1), F32)

    first = []
    for sub in range(n_sub):
        r0 = pl.multiple_of((pl.program_id(1) * n_sub + sub) * blk, blk)
        p0 = pl.multiple_of(jnp.maximum(r0 - blk, 0), blk)
        strict = (col - blk < q_off) & (col + (r0 - blk) >= 0)
        for hp in range(n_pairs):
            qq = _head_pair_rows(q_ref[sub * blk:(sub + 1) * blk, lanes(hp)])
            kk = jnp.concatenate([k_ref[pl.ds(p0, blk), lanes(hp)], k_ref[pl.ds(r0, blk), lanes(hp)]], axis=0)
            vv = jnp.concatenate([v_ref[pl.ds(p0, blk), lanes(hp)], v_ref[pl.ds(r0, blk), lanes(hp)]], axis=0)
            first.append((qq, kk, vv, strict, no_carry))
    started = _sb_tiles(first, later2)

    def finish(sub):
        bi = pl.program_id(1) * n_sub + sub
        mine = slice(sub * n_pairs, (sub + 1) * n_pairs)
        qqs = [chain[0] for chain in first[mine]]
        accs = tuple(pv for pv, _ in started[mine])
        carries = tuple(dsum for _, dsum in started[mine])

        def cond(c):
            j, carries, _ = c
            return (j >= 0) & (jnp.min(functools.reduce(jnp.minimum, carries)) < SB_EXIT_COST)

        def body(c):
            j, carries, accs = c
            start = pl.multiple_of(j * blk, blk)
            outs = _sb_tiles([(qqs[hp], k_ref[pl.ds(start, blk), lanes(hp)], v_ref[pl.ds(start, blk), lanes(hp)],
                               None, carries[hp]) for hp in range(n_pairs)], later1)
            return (j - 1, tuple(c + dsum for c, (_, dsum) in zip(carries, outs)),
                    tuple(a + pv for a, (pv, _) in zip(accs, outs)))

        _, _, accs = lax.while_loop(cond, body, (bi - 2, carries, accs))
        o_ref[sub * blk:(sub + 1) * blk, :] = jnp.concatenate(
            [_merge_head_pair(a, blk) for a in accs], axis=1).astype(o_ref.dtype)

    for sub in range(n_sub):
        finish(sub)


def _row_block_attention(kernel_fn, name, q, k, v, extra_inputs, extra_specs, blk):
    b, s, wq = q.shape
    q_spec = pl.BlockSpec((None, blk, wq), lambda bi, qi: (bi, qi, 0))
    kv_spec = pl.BlockSpec((None, s, k.shape[2]), lambda bi, qi: (bi, 0, 0))
    return pl.pallas_call(
        kernel_fn,
        grid=(b, s // blk),
        in_specs=list(extra_specs) + [q_spec, kv_spec, kv_spec],
        out_specs=q_spec,
        out_shape=jax.ShapeDtypeStruct(q.shape, BF16),
        compiler_params=pltpu.CompilerParams(
            dimension_semantics=("parallel", "arbitrary"),
            vmem_limit_bytes=VMEM_LIMIT_BYTES),
        name=name,
    )(*extra_inputs, q, k, v)


def _diff_kernel(dl_ref, subln_ref, q_ref, k_ref, v_ref, o_ref, vp_ref, sa_ref, sb_ref, sd_ref, acc_ref, *,
                 blk, lambda_init):
    qi = pl.program_id(2)
    seq = v_ref.shape[0]

    @pl.when(qi == 0)
    def _():
        low = lax.broadcasted_iota(jnp.int32, (blk, LANES), 1) < HEAD_DIM

        def build(c, _):
            rows = pl.ds(pl.multiple_of(c * blk, blk), blk)
            vf = v_ref[rows, :].astype(F32)
            vp_ref[rows, :LANES] = jnp.where(low, vf, 1.0).astype(BF16)
            vp_ref[rows, LANES:] = jnp.where(low, pltpu.roll(vf, HEAD_DIM, 1), 1.0).astype(BF16)
            return 0

        lax.fori_loop(0, seq // blk, build, 0)

    q2 = q_ref[...]
    lane = lax.broadcasted_iota(jnp.int32, q2.shape, 1)
    zero = jnp.zeros_like(q2)
    n_maps = LANES // DIFF_QK_DIM
    qq = jnp.concatenate([jnp.where(lane // DIFF_QK_DIM == c, q2, zero) for c in range(n_maps)], axis=0)
    rows_n = n_maps * blk
    half = rows_n // 2

    wide = 2 * blk

    def wide_keys(block):
        return pl.ds(pl.multiple_of(block * wide, wide), wide)

    def narrow_keys(start):
        return pl.ds(pl.multiple_of(start, blk), blk)

    def scores(s_ref, block):
        s_ref[...] = _nt_dot(qq, k_ref[wide_keys(block), :])

    def update(s, keys, m, diagonal=False):
        if diagonal:
            q_off = lax.broadcasted_iota(jnp.int32, s.shape, 0) % blk
            k_off = lax.broadcasted_iota(jnp.int32, s.shape, 1)
            s = jnp.where(k_off <= q_off, s, NEG_BIG)
        m_new = jnp.maximum(m, jnp.max(s, axis=1, keepdims=True))
        p = jnp.exp2(s - m_new).astype(BF16)
        alpha = jnp.exp2(m - m_new)
        for h, v_lanes in enumerate((slice(0, LANES), slice(LANES, 2 * LANES))):
            rows = slice(h * half, (h + 1) * half)
            acc_ref[rows, :] = alpha[rows] * acc_ref[rows, :] + jnp.dot(
                p[rows], vp_ref[keys, v_lanes], preferred_element_type=F32)
        return m_new

    def finish():
        acc = acc_ref[...]
        dl = dl_ref[...]
        lam = (jnp.exp(jnp.sum(dl[0:1] * dl[1:2], axis=1, keepdims=True))
               - jnp.exp(jnp.sum(dl[2:3] * dl[3:4], axis=1, keepdims=True)) + lambda_init)
        low = lax.broadcasted_iota(jnp.int32, (blk, LANES), 1) < HEAD_DIM
        ones = jnp.ones((LANES, LANES), BF16)
        normed = []
        for h in range(2):
            n0 = acc[2 * h * blk:(2 * h + 1) * blk]
            n1 = acc[(2 * h + 1) * blk:(2 * h + 2) * blk]
            l0 = pltpu.roll(n0, HEAD_DIM, 1)
            l1 = pltpu.roll(n1, HEAD_DIM, 1)
            d = n0 - (lam * l0 / l1) * n1
            dsq = jnp.where(low, d * d, 0.0)
            hi = dsq.astype(BF16)
            lo = (dsq - hi.astype(F32)).astype(BF16)
            ms = (jnp.dot(hi, ones, preferred_element_type=F32)
                  + jnp.dot(lo, ones, preferred_element_type=F32)) * (1.0 / HEAD_DIM)
            normed.append(d * lax.rsqrt(ms + LN_EPS * l0 * l0))
        y = jnp.where(low, normed[0], pltpu.roll(normed[1], HEAD_DIM, 1))
        o_ref[...] = (y * subln_ref[...] * (1.0 - lambda_init)).astype(o_ref.dtype)

    n_wide = qi // 2
    acc_ref[...] = jnp.zeros(acc_ref.shape, F32)
    sd_ref[...] = _nt_dot(qq, k_ref[narrow_keys(qi * blk), :])
    scores(sa_ref, 0)
    m = update(sd_ref[...], narrow_keys(qi * blk), jnp.full((rows_n, 1), NEG_BIG, F32), diagonal=True)

    def wide_update(s_ref, block, m):
        return update(s_ref[...], wide_keys(block), m)

    def leftover_update(s_ref, block, m):
        return update(s_ref[:, :blk], narrow_keys(block * wide), m)

    def body(i, m):
        scores(sb_ref, 2 * i + 1)
        m = wide_update(sa_ref, 2 * i, m)
        scores(sa_ref, 2 * i + 2)
        return wide_update(sb_ref, 2 * i + 1, m)

    n_loop = jnp.maximum(n_wide - 1, 0) // 2
    m = lax.fori_loop(0, n_loop, body, m)
    first = 2 * n_loop
    left = n_wide - first
    odd = qi % 2

    def tail(n_left, is_odd):
        def run():
            mm = m
            if n_left >= 1:
                if n_left == 2 or is_odd:
                    scores(sb_ref, first + 1)
                mm = wide_update(sa_ref, first, mm)
            if n_left == 2:
                if is_odd:
                    scores(sa_ref, first + 2)
                mm = wide_update(sb_ref, first + 1, mm)
            if is_odd:
                leftover_update(sa_ref if n_left in (0, 2) else sb_ref, n_wide, mm)
            finish()
        pl.when(jnp.logical_and(left == n_left, odd == int(is_odd)))(run)

    for n_left in range(3):
        for is_odd in (False, True):
            tail(n_left, is_odd)


def _diff_attention(q, k, v, diff_lambda, subln_lanes, blk, lambda_init):
    b, s, wq = q.shape
    assert s % (2 * blk) == 0
    rows_n = LANES // DIFF_QK_DIM * blk
    q_spec = pl.BlockSpec((None, blk, LANES), lambda bi, hp, qi: (bi, qi, hp))
    kv_spec = pl.BlockSpec((None, s, LANES), lambda bi, hp, qi: (bi, 0, hp))
    return pl.pallas_call(
        functools.partial(_diff_kernel, blk=blk, lambda_init=lambda_init),
        grid=(b, wq // LANES, s // blk),
        in_specs=[pl.BlockSpec(diff_lambda.shape, lambda bi, hp, qi: (0, 0)),
                  pl.BlockSpec((1, LANES), lambda bi, hp, qi: (0, 0)), q_spec, kv_spec, kv_spec],
        out_specs=q_spec,
        out_shape=jax.ShapeDtypeStruct(q.shape, BF16),
        scratch_shapes=[pltpu.VMEM((s, 2 * LANES), BF16),
                        pltpu.VMEM((rows_n, 2 * blk), F32),
                        pltpu.VMEM((rows_n, 2 * blk), F32),
                        pltpu.VMEM((rows_n, blk), F32),
                        pltpu.VMEM((rows_n, LANES), F32)],
        compiler_params=pltpu.CompilerParams(
            dimension_semantics=("parallel", "parallel", "arbitrary"),
            vmem_limit_bytes=VMEM_LIMIT_BYTES),
        name="differential",
    )(diff_lambda, subln_lanes, q, k, v)


def _swa_kernel(sinks_ref, q_ref, k_ref, v_ref, o_ref, *, blk, pairs_per_kv_block):
    n_sub = q_ref.shape[0] // blk

    def lanes(hb):
        return slice(hb * LANES, (hb + 1) * LANES)

    shape = (2 * blk, WINDOW + blk)
    rows = lax.broadcasted_iota(jnp.int32, shape, 0)
    q_off = jnp.where(rows >= blk, rows - blk, rows)
    col = lax.broadcasted_iota(jnp.int32, shape, 1)
    rel = (col - WINDOW) - q_off
    in_window = (rel <= 0) & (rel > -WINDOW)
    second_head = lax.broadcasted_iota(jnp.int32, (2 * blk, 1), 0) >= blk
    sink_lane = lax.broadcasted_iota(jnp.int32, (2 * blk, WINDOW), 1) == 0
    sink_row = lax.broadcasted_iota(jnp.int32, (WINDOW, LANES), 0) == 0
    masked_cur = jnp.full((2 * blk, blk), NEG_BIG, F32)
    n_blocks = q_ref.shape[1] // LANES
    chains = []
    for sub in range(n_sub):
        r0 = pl.multiple_of((pl.program_id(1) * n_sub + sub) * blk, blk)
        p0 = pl.multiple_of(jnp.maximum(r0 - WINDOW, 0), WINDOW)
        band = in_window & (col + (r0 - WINDOW) >= 0)
        keys_values = {}
        for hb in range(n_blocks):
            kvb = hb // pairs_per_kv_block
            if kvb not in keys_values:
                v_prev = v_ref[pl.ds(p0, WINDOW), lanes(kvb)]
                keys_values[kvb] = (
                    jnp.concatenate([k_ref[pl.ds(p0, WINDOW), lanes(kvb)], k_ref[pl.ds(r0, blk), lanes(kvb)]], axis=0),
                    jnp.concatenate([jnp.where(sink_row, jnp.zeros_like(v_prev), v_prev),
                                     v_ref[pl.ds(r0, blk), lanes(kvb)]], axis=0))
            kk, vv = keys_values[kvb]
            qq = _head_pair_rows(q_ref[sub * blk:(sub + 1) * blk, lanes(hb)])
            sink = jnp.where(second_head, sinks_ref[SW_HEAD_ORDER[2 * hb + 1]],
                             sinks_ref[SW_HEAD_ORDER[2 * hb]]) * LOG2_E
            fill = jnp.concatenate([jnp.where(sink_lane, sink, NEG_BIG), masked_cur], axis=1)
            chains.append((qq, kk, vv, band, fill))
    scores = [jnp.where(band, _nt_dot(qq, kk), fill) for qq, kk, _, band, fill in chains]
    probs = []
    for s in scores:
        p = jnp.exp2(s - jnp.max(s, axis=1, keepdims=True))
        probs.append((p.astype(BF16), jnp.sum(p, axis=1, keepdims=True)))
    outs = [_merge_head_pair(jnp.dot(p, vv, preferred_element_type=F32) / denom, blk)
            for (p, denom), (_, _, vv, _, _) in zip(probs, chains)]
    for sub in range(n_sub):
        o_ref[sub * blk:(sub + 1) * blk, :] = jnp.concatenate(
            outs[sub * n_blocks:(sub + 1) * n_blocks], axis=1).astype(o_ref.dtype)


def _out_kernel(*refs, n_attn, alpha, n_sub):
    x_ref = refs[0]
    attn_refs = refs[1:1 + n_attn]
    wgm_ref, mk_ref, mv_ref, w_ref, g_ref, b_ref, o_ref = refs[1 + n_attn:]
    ts = x_ref.shape[0] // n_sub
    subs = [slice(i * ts, (i + 1) * ts) for i in range(n_sub)]
    w_m = MEM_HEADS * HEAD_DIM
    blocks = [slice(hb * LANES, (hb + 1) * LANES) for hb in range(w_m // LANES)]

    hs = [jnp.dot(x_ref[r, :].astype(BF16), wgm_ref[...], preferred_element_type=F32) for r in subs]
    mqs = [(h[:, :w_m] * (HEAD_DIM ** -0.5 * LOG2_E)).astype(BF16) for h in hs]
    scores = [[_nt_dot(_head_pair_rows(mq[:, c]), mk_ref[:, c]) for c in blocks] for mq in mqs]
    probs = [[jnp.exp2(s - jnp.max(s, axis=1, keepdims=True)) for s in row] for row in scores]
    mems = [[_merge_head_pair(jnp.dot(p.astype(BF16), mv_ref[:, c], preferred_element_type=F32)
                              / jnp.sum(p, axis=1, keepdims=True), ts) for p, c in zip(row, blocks)]
            for row in probs]
    gated = []
    for h, r, mem in zip(hs, subs, mems):
        gate = h[:, w_m:]
        mixed = jnp.concatenate([a[r, :].astype(F32) for a in attn_refs] + mem, axis=1)
        gated.append((mixed * (gate / (1.0 + jnp.exp(-gate)))).astype(BF16))
    ys = [jnp.dot(g, w_ref[...], preferred_element_type=F32) for g in gated]
    for y, r in zip(ys, subs):
        res = alpha * x_ref[r, :] + y
        c = res - jnp.mean(res, axis=1, keepdims=True)
        var = jnp.mean(c * c, axis=1, keepdims=True)
        o_ref[r, :] = c * lax.rsqrt(var + LN_EPS) * g_ref[...] + b_ref[...]


def _out_layer(x2, attn_outs, w_gm, mk, mv, w_out, ln_g, ln_b, seq, tm, alpha, name):
    m, d = x2.shape
    mem_len = mk.shape[0] // (m // seq)
    steps_per_batch = seq // tm

    def row(wd):
        return pl.BlockSpec((tm, wd), lambda i: (i, 0))

    def whole(a):
        return pl.BlockSpec(a.shape, lambda i: (0, 0))

    mem_spec = pl.BlockSpec((mem_len, mk.shape[1]), lambda i: (i // steps_per_batch, 0))
    in_specs = ([row(d)] + [row(a.shape[1]) for a in attn_outs]
                + [whole(w_gm), mem_spec, mem_spec, whole(w_out), whole(ln_g), whole(ln_b)])
    return pl.pallas_call(
        functools.partial(_out_kernel, n_attn=len(attn_outs), alpha=alpha, n_sub=OUT_SUB_TILES),
        grid=(m // tm,),
        in_specs=in_specs,
        out_specs=row(d),
        out_shape=jax.ShapeDtypeStruct((m, d), F32),
        compiler_params=pltpu.CompilerParams(dimension_semantics=("parallel",),
                                             vmem_limit_bytes=VMEM_LIMIT_BYTES),
        name=name,
    )(x2, *attn_outs, w_gm, mk, mv, w_out, ln_g, ln_b)


ROW_TILE = 1024
OUT_SUB_TILES = 4
IN_ROW_TILE = 1024
IN_EVEN_SUB_TILES = 1
IN_ODD_SUB_TILES = 4
SB_BLOCK = 128
SB_ROWS_PER_STEP = 1024
DIFF_BLOCK = 512
SWA_BLOCK = 128
SWA_ROWS_PER_STEP = 512


def _even_layer(x2, mem2, pos2, w_in, w_memkv, diff_lambda, diff_subln, w_out, ln_g, ln_b,
                layer_idx, batch, seq, alpha):
    lambda_init = 0.8 - 0.6 * math.exp(-0.3 * layer_idx)
    w_sb = SB_HEADS * HEAD_DIM
    w_df = DIFF_HEADS * HEAD_DIM
    w_attn = 3 * w_sb + 3 * w_df
    w_in = w_in.astype(BF16)
    sbq, sbk, sbv, dfq, dfk, dfv = _inproj(
        functools.partial(_inproj_even_kernel, n_sub=IN_EVEN_SUB_TILES), "inproj_even", x2, pos2,
        _inv_freq_lanes(DIFF_QK_DIM), w_in[:, :w_attn], (w_sb, w_sb, w_sb, w_df, w_df, w_df), IN_ROW_TILE)
    mk, mv = _memkv(mem2, w_memkv.astype(BF16))

    def b3(a):
        return a.reshape(batch, seq, a.shape[1])

    sb_o = _row_block_attention(functools.partial(_sb_kernel, blk=SB_BLOCK), "stick_breaking",
                                b3(sbq), b3(sbk), b3(sbv), (), (), SB_ROWS_PER_STEP)
    subln_lanes = jnp.tile(diff_subln.astype(F32), LANES // HEAD_DIM)[None, :]
    df_o = _diff_attention(b3(dfq), b3(dfk), b3(dfv), diff_lambda.astype(F32), subln_lanes,
                           DIFF_BLOCK, lambda_init)
    return _out_layer(x2, [sb_o.reshape(x2.shape[0], -1), df_o.reshape(x2.shape[0], -1)], w_in[:, w_attn:],
                      mk, mv, w_out.astype(BF16), ln_g[None, :], ln_b[None, :], seq, ROW_TILE, alpha, "out_even")


def _odd_layer(x2, mem2, pos2, w_in, w_memkv, sinks, w_out, ln_g, ln_b, batch, seq, alpha):
    w_q = SWA_Q_HEADS * HEAD_DIM
    w_kv = SWA_KV_HEADS * HEAD_DIM
    w_m = MEM_HEADS * HEAD_DIM

    def heads_in_order(a, axis, lo):
        return [lax.slice_in_dim(a, lo + h * HEAD_DIM, lo + (h + 1) * HEAD_DIM, axis=axis) for h in SW_HEAD_ORDER]

    gate_lo = w_q + 2 * w_kv + w_m
    last_kv = slice((SWA_KV_HEADS - 1) * HEAD_DIM, w_kv)
    w_in = w_in.astype(BF16)
    w_out = w_out.astype(BF16)
    k_w = w_in[:, w_q:w_q + w_kv]
    v_w = w_in[:, w_q + w_kv:w_q + 2 * w_kv]
    w_attn = jnp.concatenate(heads_in_order(w_in, 1, 0) + [k_w, k_w[:, last_kv], v_w, v_w[:, last_kv]], axis=1)
    w_gm = jnp.concatenate([w_in[:, w_q + 2 * w_kv:gate_lo]] + heads_in_order(w_in, 1, gate_lo)
                           + [w_in[:, gate_lo + w_q:]], axis=1)
    w_out_perm = jnp.concatenate(heads_in_order(w_out, 0, 0) + [w_out[w_q:]], axis=0)

    cq, ck, cv = _inproj(
        functools.partial(_inproj_odd_kernel, n_sub=IN_ODD_SUB_TILES), "inproj_odd", x2, pos2,
        _inv_freq_lanes(HEAD_DIM), w_attn, (w_q, 2 * LANES, 2 * LANES), IN_ROW_TILE)
    mk, mv = _memkv(mem2, w_memkv.astype(BF16))

    def b3(a):
        return a.reshape(batch, seq, a.shape[1])

    pairs_per_kv_block = LANES // HEAD_DIM * SWA_GROUP // 2
    sink_spec = [pl.BlockSpec(memory_space=pltpu.SMEM)]
    c_o = _row_block_attention(
        functools.partial(_swa_kernel, blk=SWA_BLOCK, pairs_per_kv_block=pairs_per_kv_block),
        "sliding_window", b3(cq), b3(ck), b3(cv), (sinks.astype(F32),), sink_spec,
        SWA_ROWS_PER_STEP)
    return _out_layer(x2, [c_o.reshape(x2.shape[0], -1)], w_gm, mk, mv, w_out_perm,
                      ln_g[None, :], ln_b[None, :], seq, ROW_TILE, alpha, "out_odd")


def kernel(x, mem, positions, w_in_even, w_memkv_even, diff_lambda_even, diff_subln_even, w_out_even,
           ln_g_even, ln_b_even, w_in_odd, w_memkv_odd, sinks_odd, w_out_odd, ln_g_odd, ln_b_odd):
    batch, seq, d = x.shape
    depth = w_in_even.shape[0] + w_in_odd.shape[0]
    alpha = (2 * depth) ** 0.25
    x2 = x.reshape(batch * seq, d)
    mem2 = mem.reshape(batch * mem.shape[1], d)
    pos2 = positions.reshape(batch * seq, 1)
    for i in range(depth):
        j = i // 2
        if i % 2 == 0:
            x2 = _even_layer(x2, mem2, pos2, w_in_even[j], w_memkv_even[j], diff_lambda_even[j],
                             diff_subln_even[j], w_out_even[j], ln_g_even[j], ln_b_even[j], i,
                             batch, seq, alpha)
        else:
            x2 = _odd_layer(x2, mem2, pos2, w_in_odd[j], w_memkv_odd[j], sinks_odd[j], w_out_odd[j],
                            ln_g_odd[j], ln_b_odd[j], batch, seq, alpha)
    return x2.reshape(batch, seq, d)
```

```python
import functools
import math

import jax
import jax.numpy as jnp
import numpy as np
from jax import lax
from jax.experimental import pallas as pl
from jax.experimental.pallas import tpu as pltpu

F32 = jnp.float32
BF16 = jnp.bfloat16

HEAD_DIM = 64
LANES = 128
MEM_HEADS = 4
SB_HEADS = 6
DIFF_HEADS = 6
DIFF_QK_DIM = 32
SWA_Q_HEADS = 12
SWA_KV_HEADS = 3
SWA_GROUP = 4
WINDOW = 128
ROPE_THETA = 500000.0
ROPE_FRACTION = 4
LN_EPS = 1e-5
NEG_BIG = -1e30
VMEM_LIMIT_BYTES = 48 * 1024 * 1024

SB_EXIT_COST = 92.0
LOG2_E = math.log2(math.e)

SW_HEAD_ORDER = (0, 4, 1, 5, 2, 6, 3, 7, 8, 9, 10, 11)


def _nt_dot(a, b):
    return lax.dot_general(a, b, (((1,), (1,)), ((), ())), preferred_element_type=F32)


def _head_pair_rows(q2):
    lane = lax.broadcasted_iota(jnp.int32, q2.shape, 1)
    low = lane < HEAD_DIM
    zero = jnp.zeros_like(q2)
    return jnp.concatenate([jnp.where(low, q2, zero), jnp.where(low, zero, q2)], axis=0)


def _merge_head_pair(o, rows):
    lane = lax.broadcasted_iota(jnp.int32, (rows, LANES), 1)
    return jnp.where(lane < HEAD_DIM, o[:rows], o[rows:])


def _rope_lane_tables(group, pos_f32, inv_freq_row):
    rot = group // ROPE_FRACTION
    half = rot // 2
    lane = lax.broadcasted_iota(jnp.int32, (1, LANES), 1)
    r = lane % group
    ang = pos_f32 * inv_freq_row
    cos = jnp.cos(ang)
    sin = jnp.sin(ang)
    cos_t = jnp.where(r < rot, cos, 1.0)
    sin_first = jnp.where(r < half, -sin, 0.0)
    sin_second = jnp.where((r >= half) & (r < rot), sin, 0.0)
    return cos_t, sin_first, sin_second, half


def _apply_rope(h, tables):
    cos_t, sin_first, sin_second, half = tables
    outs = []
    for c in range(h.shape[1] // LANES):
        blk = h[:, c * LANES:(c + 1) * LANES]
        outs.append(blk * cos_t + pltpu.roll(blk, LANES - half, 1) * sin_first
                    + pltpu.roll(blk, half, 1) * sin_second)
    return jnp.concatenate(outs, axis=1)


def _inv_freq_lanes(group):
    half = group // ROPE_FRACTION // 2
    inv_freq = np.exp(-(np.arange(half, dtype=np.float32) / np.float32(half)) * np.float32(math.log(ROPE_THETA)))
    lane = np.arange(LANES)
    return jnp.asarray(inv_freq.astype(np.float32)[(lane % group) % half][None, :])


def _projected_sub_tiles(x_ref, w_ref, n_sub):
    ts = x_ref.shape[0] // n_sub
    subs = [slice(i * ts, (i + 1) * ts) for i in range(n_sub)]
    return [(r, jnp.dot(x_ref[r, :].astype(BF16), w_ref[...], preferred_element_type=F32)) for r in subs]


def _sections(h):
    edges = [0]

    def sec(width):
        edges.append(edges[-1] + width)
        return h[:, edges[-2]:edges[-1]]

    return sec


def _inproj_even_kernel(x_ref, pos_ref, invf_ref, w_ref, sbq_ref, sbk_ref, sbv_ref,
                        dfq_ref, dfk_ref, dfv_ref, *, n_sub):
    w_sb = SB_HEADS * HEAD_DIM
    w_df = DIFF_HEADS * 2 * DIFF_QK_DIM
    q_scale = DIFF_QK_DIM ** -0.5 * LOG2_E
    for r, h in _projected_sub_tiles(x_ref, w_ref, n_sub):
        sec = _sections(h)
        tables = _rope_lane_tables(DIFF_QK_DIM, pos_ref[r, :].astype(F32), invf_ref[...])
        sbq_ref[r, :] = (sec(w_sb) * (HEAD_DIM ** -0.5 * LOG2_E)).astype(BF16)
        sbk_ref[r, :] = sec(w_sb).astype(BF16)
        sbv_ref[r, :] = sec(w_sb).astype(BF16)
        dfq_ref[r, :] = (_apply_rope(sec(w_df), tables) * q_scale).astype(BF16)
        dfk_ref[r, :] = _apply_rope(sec(w_df), tables).astype(BF16)
        dfv_ref[r, :] = sec(w_df).astype(BF16)


def _inproj_odd_kernel(x_ref, pos_ref, invf_ref, w_ref, cq_ref, ck_ref, cv_ref, *, n_sub):
    w_kv = ck_ref.shape[1]
    for r, h in _projected_sub_tiles(x_ref, w_ref, n_sub):
        sec = _sections(h)
        tables = _rope_lane_tables(HEAD_DIM, pos_ref[r, :].astype(F32), invf_ref[...])
        cq_ref[r, :] = (_apply_rope(sec(SWA_Q_HEADS * HEAD_DIM), tables)
                        * (HEAD_DIM ** -0.5 * LOG2_E)).astype(BF16)
        ck_ref[r, :] = _apply_rope(sec(w_kv), tables).astype(BF16)
        cv_ref[r, :] = sec(w_kv).astype(BF16)


def _inproj(kernel_fn, name, x2, pos2, invf, w, out_widths, tm):
    m, d = x2.shape
    return pl.pallas_call(
        kernel_fn,
        grid=(m // tm,),
        in_specs=[pl.BlockSpec((tm, d), lambda i: (i, 0)),
                  pl.BlockSpec((tm, 1), lambda i: (i, 0)),
                  pl.BlockSpec((1, LANES), lambda i: (0, 0)),
                  pl.BlockSpec(w.shape, lambda i: (0, 0))],
        out_specs=[pl.BlockSpec((tm, wd), lambda i: (i, 0)) for wd in out_widths],
        out_shape=[jax.ShapeDtypeStruct((m, wd), BF16) for wd in out_widths],
        compiler_params=pltpu.CompilerParams(dimension_semantics=("parallel",),
                                             vmem_limit_bytes=VMEM_LIMIT_BYTES),
        name=name,
    )(x2, pos2, invf, w)


def _memkv_kernel(mem_ref, w_ref, mk_ref, mv_ref):
    kv = jnp.dot(mem_ref[...].astype(BF16), w_ref[...], preferred_element_type=F32)
    half = mk_ref.shape[1]
    mk_ref[...] = kv[:, :half].astype(BF16)
    mv_ref[...] = kv[:, half:].astype(BF16)


def _memkv(mem2, w):
    rows = mem2.shape[0]
    half = w.shape[1] // 2
    return pl.pallas_call(
        _memkv_kernel,
        out_shape=[jax.ShapeDtypeStruct((rows, half), BF16)] * 2,
        compiler_params=pltpu.CompilerParams(vmem_limit_bytes=VMEM_LIMIT_BYTES),
        name="memkv",
    )(mem2, w)


def _strict_lower(n):
    return (lax.broadcasted_iota(jnp.int32, (n, n), 0)
            > lax.broadcasted_iota(jnp.int32, (n, n), 1)).astype(BF16)


def _sb_tiles(chains, later):
    zs = [_nt_dot(qq, kk) if strict is None else jnp.where(strict, _nt_dot(qq, kk), NEG_BIG)
          for qq, kk, _, strict, _ in chains]
    costs = []
    for z in zs:
        cost = jnp.maximum(z, 0.0) + jnp.log(1.0 + jnp.exp2(-jnp.abs(z))) * LOG2_E
        hi = cost.astype(BF16)
        lo = (cost - hi.astype(F32)).astype(BF16)
        costs.append((cost, hi, lo))
    afters = [jnp.dot(hi, later, preferred_element_type=F32) + jnp.dot(lo, later, preferred_element_type=F32)
              for _, hi, lo in costs]
    outs = []
    for z, (cost, _, _), after, (_, _, vv, _, carry) in zip(zs, costs, afters, chains):
        w = jnp.exp2(z - (cost + after + carry))
        outs.append((jnp.dot(w.astype(BF16), vv, preferred_element_type=F32),
                     jnp.sum(cost, axis=1, keepdims=True)))
    return outs


def _sb_kernel(q_ref, k_ref, v_ref, o_ref, *, blk):
    n_pairs = q_ref.shape[1] // LANES
    n_sub = q_ref.shape[0] // blk

    def lanes(hp):
        return slice(hp * LANES, (hp + 1) * LANES)

    rows = lax.broadcasted_iota(jnp.int32, (2 * blk, 2 * blk), 0)
    q_off = jnp.where(rows >= blk, rows - blk, rows)
    col = lax.broadcasted_iota(jnp.int32, (2 * blk, 2 * blk), 1)
    later2 = _strict_lower(2 * blk)
    later1 = _strict_lower(blk)
    no_carry = jnp.zeros((2 * blk, 1), F32)

    first = []
    for sub in range(n_sub):
        r0 = pl.multiple_of((pl.program_id(1) * n_sub + sub) * blk, blk)
        p0 = pl.multiple_of(jnp.maximum(r0 - blk, 0), blk)
        strict = (col - blk < q_off) & (col + (r0 - blk) >= 0)
        for hp in range(n_pairs):
            qq = _head_pair_rows(q_ref[sub * blk:(sub + 1) * blk, lanes(hp)])
            kk = jnp.concatenate([k_ref[pl.ds(p0, blk), lanes(hp)], k_ref[pl.ds(r0, blk), lanes(hp)]], axis=0)
            vv = jnp.concatenate([v_ref[pl.ds(p0, blk), lanes(hp)], v_ref[pl.ds(r0, blk), lanes(hp)]], axis=0)
            first.append((qq, kk, vv, strict, no_carry))
    started = _sb_tiles(first, later2)

    def finish(sub):
        bi = pl.program_id(1) * n_sub + sub
        mine = slice(sub * n_pairs, (sub + 1) * n_pairs)
        qqs = [chain[0] for chain in first[mine]]
        accs = tuple(pv for pv, _ in started[mine])
        carries = tuple(dsum for _, dsum in started[mine])

        def cond(c):
            j, carries, _ = c
            return (j >= 0) & (jnp.min(functools.reduce(jnp.minimum, carries)) < SB_EXIT_COST)

        def body(c):
            j, carries, accs = c
            start = pl.multiple_of(j * blk, blk)
            outs = _sb_tiles([(qqs[hp], k_ref[pl.ds(start, blk), lanes(hp)], v_ref[pl.ds(start, blk), lanes(hp)],
                               None, carries[hp]) for hp in range(n_pairs)], later1)
            return (j - 1, tuple(c + dsum for c, (_, dsum) in zip(carries, outs)),
                    tuple(a + pv for a, (pv, _) in zip(accs, outs)))

        _, _, accs = lax.while_loop(cond, body, (bi - 2, carries, accs))
        o_ref[sub * blk:(sub + 1) * blk, :] = jnp.concatenate(
            [_merge_head_pair(a, blk) for a in accs], axis=1).astype(o_ref.dtype)

    for sub in range(n_sub):
        finish(sub)


def _row_block_attention(kernel_fn, name, q, k, v, extra_inputs, extra_specs, blk):
    b, s, wq = q.shape
    q_spec = pl.BlockSpec((None, blk, wq), lambda bi, qi: (bi, qi, 0))
    kv_spec = pl.BlockSpec((None, s, k.shape[2]), lambda bi, qi: (bi, 0, 0))
    return pl.pallas_call(
        kernel_fn,
        grid=(b, s // blk),
        in_specs=list(extra_specs) + [q_spec, kv_spec, kv_spec],
        out_specs=q_spec,
        out_shape=jax.ShapeDtypeStruct(q.shape, BF16),
        compiler_params=pltpu.CompilerParams(
            dimension_semantics=("parallel", "arbitrary"),
            vmem_limit_bytes=VMEM_LIMIT_BYTES),
        name=name,
    )(*extra_inputs, q, k, v)


def _diff_kernel(dl_ref, subln_ref, q_ref, k_ref, v_ref, o_ref, vp_ref, sa_ref, sb_ref, sd_ref, acc_ref, *,
                 blk, lambda_init):
    qi = pl.program_id(2)
    seq = v_ref.shape[0]

    @pl.when(qi == 0)
    def _():
        low = lax.broadcasted_iota(jnp.int32, (blk, LANES), 1) < HEAD_DIM

        def build(c, _):
            rows = pl.ds(pl.multiple_of(c * blk, blk), blk)
            vf = v_ref[rows, :].astype(F32)
            vp_ref[rows, :LANES] = jnp.where(low, vf, 1.0).astype(BF16)
            vp_ref[rows, LANES:] = jnp.where(low, pltpu.roll(vf, HEAD_DIM, 1), 1.0).astype(BF16)
            return 0

        lax.fori_loop(0, seq // blk, build, 0)

    q2 = q_ref[...]
    lane = lax.broadcasted_iota(jnp.int32, q2.shape, 1)
    zero = jnp.zeros_like(q2)
    n_maps = LANES // DIFF_QK_DIM
    qq = jnp.concatenate([jnp.where(lane // DIFF_QK_DIM == c, q2, zero) for c in range(n_maps)], axis=0)
    rows_n = n_maps * blk
    half = rows_n // 2

    wide = 2 * blk

    def wide_keys(block):
        return pl.ds(pl.multiple_of(block * wide, wide), wide)

    def narrow_keys(start):
        return pl.ds(pl.multiple_of(start, blk), blk)

    def scores(s_ref, block):
        s_ref[...] = _nt_dot(qq, k_ref[wide_keys(block), :])

    def update(s, keys, m, diagonal=False):
        if diagonal:
            q_off = lax.broadcasted_iota(jnp.int32, s.shape, 0) % blk
            k_off = lax.broadcasted_iota(jnp.int32, s.shape, 1)
            s = jnp.where(k_off <= q_off, s, NEG_BIG)
        m_new = jnp.maximum(m, jnp.max(s, axis=1, keepdims=True))
        p = jnp.exp2(s - m_new).astype(BF16)
        alpha = jnp.exp2(m - m_new)
        for h, v_lanes in enumerate((slice(0, LANES), slice(LANES, 2 * LANES))):
            rows = slice(h * half, (h + 1) * half)
            acc_ref[rows, :] = alpha[rows] * acc_ref[rows, :] + jnp.dot(
                p[rows], vp_ref[keys, v_lanes], preferred_element_type=F32)
        return m_new

    def finish():
        acc = acc_ref[...]
        dl = dl_ref[...]
        lam = (jnp.exp(jnp.sum(dl[0:1] * dl[1:2], axis=1, keepdims=True))
               - jnp.exp(jnp.sum(dl[2:3] * dl[3:4], axis=1, keepdims=True)) + lambda_init)
        low = lax.broadcasted_iota(jnp.int32, (blk, LANES), 1) < HEAD_DIM
        ones = jnp.ones((LANES, LANES), BF16)
        normed = []
        for h in range(2):
            n0 = acc[2 * h * blk:(2 * h + 1) * blk]
            n1 = acc[(2 * h + 1) * blk:(2 * h + 2) * blk]
            l0 = pltpu.roll(n0, HEAD_DIM, 1)
            l1 = pltpu.roll(n1, HEAD_DIM, 1)
            d = n0 - (lam * l0 / l1) * n1
            dsq = jnp.where(low, d * d, 0.0)
            hi = dsq.astype(BF16)
            lo = (dsq - hi.astype(F32)).astype(BF16)
            ms = (jnp.dot(hi, ones, preferred_element_type=F32)
                  + jnp.dot(lo, ones, preferred_element_type=F32)) * (1.0 / HEAD_DIM)
            normed.append(d * lax.rsqrt(ms + LN_EPS * l0 * l0))
        y = jnp.where(low, normed[0], pltpu.roll(normed[1], HEAD_DIM, 1))
        o_ref[...] = (y * subln_ref[...] * (1.0 - lambda_init)).astype(o_ref.dtype)

    n_wide = qi // 2
    acc_ref[...] = jnp.zeros(acc_ref.shape, F32)
    sd_ref[...] = _nt_dot(qq, k_ref[narrow_keys(qi * blk), :])
    scores(sa_ref, 0)
    m = update(sd_ref[...], narrow_keys(qi * blk), jnp.full((rows_n, 1), NEG_BIG, F32), diagonal=True)

    def wide_update(s_ref, block, m):
        return update(s_ref[...], wide_keys(block), m)

    def leftover_update(s_ref, block, m):
        return update(s_ref[:, :blk], narrow_keys(block * wide), m)

    def body(i, m):
        scores(sb_ref, 2 * i + 1)
        m = wide_update(sa_ref, 2 * i, m)
        scores(sa_ref, 2 * i + 2)
        return wide_update(sb_ref, 2 * i + 1, m)

    n_loop = jnp.maximum(n_wide - 1, 0) // 2
    m = lax.fori_loop(0, n_loop, body, m)
    first = 2 * n_loop
    left = n_wide - first
    odd = qi % 2

    def tail(n_left, is_odd):
        def run():
            mm = m
            if n_left >= 1:
                if n_left == 2 or is_odd:
                    scores(sb_ref, first + 1)
                mm = wide_update(sa_ref, first, mm)
            if n_left == 2:
                if is_odd:
                    scores(sa_ref, first + 2)
                mm = wide_update(sb_ref, first + 1, mm)
            if is_odd:
                leftover_update(sa_ref if n_left in (0, 2) else sb_ref, n_wide, mm)
            finish()
        pl.when(jnp.logical_and(left == n_left, odd == int(is_odd)))(run)

    for n_left in range(3):
        for is_odd in (False, True):
            tail(n_left, is_odd)


def _diff_attention(q, k, v, diff_lambda, subln_lanes, blk, lambda_init):
    b, s, wq = q.shape
    assert s % (2 * blk) == 0
    rows_n = LANES // DIFF_QK_DIM * blk
    q_spec = pl.BlockSpec((None, blk, LANES), lambda bi, hp, qi: (bi, qi, hp))
    kv_spec = pl.BlockSpec((None, s, LANES), lambda bi, hp, qi: (bi, 0, hp))
    return pl.pallas_call(
        functools.partial(_diff_kernel, blk=blk, lambda_init=lambda_init),
        grid=(b, wq // LANES, s // blk),
        in_specs=[pl.BlockSpec(diff_lambda.shape, lambda bi, hp, qi: (0, 0)),
                  pl.BlockSpec((1, LANES), lambda bi, hp, qi: (0, 0)), q_spec, kv_spec, kv_spec],
        out_specs=q_spec,
        out_shape=jax.ShapeDtypeStruct(q.shape, BF16),
        scratch_shapes=[pltpu.VMEM((s, 2 * LANES), BF16),
                        pltpu.VMEM((rows_n, 2 * blk), F32),
                        pltpu.VMEM((rows_n, 2 * blk), F32),
                        pltpu.VMEM((rows_n, blk), F32),
                        pltpu.VMEM((rows_n, LANES), F32)],
        compiler_params=pltpu.CompilerParams(
            dimension_semantics=("parallel", "parallel", "arbitrary"),
            vmem_limit_bytes=VMEM_LIMIT_BYTES),
        name="differential",
    )(diff_lambda, subln_lanes, q, k, v)


def _swa_kernel(sinks_ref, q_ref, k_ref, v_ref, o_ref, *, blk, pairs_per_kv_block):
    n_sub = q_ref.shape[0] // blk

    def lanes(hb):
        return slice(hb * LANES, (hb + 1) * LANES)

    shape = (2 * blk, WINDOW + blk)
    rows = lax.broadcasted_iota(jnp.int32, shape, 0)
    q_off = jnp.where(rows >= blk, rows - blk, rows)
    col = lax.broadcasted_iota(jnp.int32, shape, 1)
    rel = (col - WINDOW) - q_off
    in_window = (rel <= 0) & (rel > -WINDOW)
    second_head = lax.broadcasted_iota(jnp.int32, (2 * blk, 1), 0) >= blk
    sink_lane = lax.broadcasted_iota(jnp.int32, (2 * blk, WINDOW), 1) == 0
    sink_row = lax.broadcasted_iota(jnp.int32, (WINDOW, LANES), 0) == 0
    masked_cur = jnp.full((2 * blk, blk), NEG_BIG, F32)
    n_blocks = q_ref.shape[1] // LANES
    chains = []
    for sub in range(n_sub):
        r0 = pl.multiple_of((pl.program_id(1) * n_sub + sub) * blk, blk)
        p0 = pl.multiple_of(jnp.maximum(r0 - WINDOW, 0), WINDOW)
        band = in_window & (col + (r0 - WINDOW) >= 0)
        keys_values = {}
        for hb in range(n_blocks):
            kvb = hb // pairs_per_kv_block
            if kvb not in keys_values:
                v_prev = v_ref[pl.ds(p0, WINDOW), lanes(kvb)]
                keys_values[kvb] = (
                    jnp.concatenate([k_ref[pl.ds(p0, WINDOW), lanes(kvb)], k_ref[pl.ds(r0, blk), lanes(kvb)]], axis=0),
                    jnp.concatenate([jnp.where(sink_row, jnp.zeros_like(v_prev), v_prev),
                                     v_ref[pl.ds(r0, blk), lanes(kvb)]], axis=0))
            kk, vv = keys_values[kvb]
            qq = _head_pair_rows(q_ref[sub * blk:(sub + 1) * blk, lanes(hb)])
            sink = jnp.where(second_head, sinks_ref[SW_HEAD_ORDER[2 * hb + 1]],
                             sinks_ref[SW_HEAD_ORDER[2 * hb]]) * LOG2_E
            fill = jnp.concatenate([jnp.where(sink_lane, sink, NEG_BIG), masked_cur], axis=1)
            chains.append((qq, kk, vv, band, fill))
    scores = [jnp.where(band, _nt_dot(qq, kk), fill) for qq, kk, _, band, fill in chains]
    probs = []
    for s in scores:
        p = jnp.exp2(s - jnp.max(s, axis=1, keepdims=True))
        probs.append((p.astype(BF16), jnp.sum(p, axis=1, keepdims=True)))
    outs = [_merge_head_pair(jnp.dot(p, vv, preferred_element_type=F32) / denom, blk)
            for (p, denom), (_, _, vv, _, _) in zip(probs, chains)]
    for sub in range(n_sub):
        o_ref[sub * blk:(sub + 1) * blk, :] = jnp.concatenate(
            outs[sub * n_blocks:(sub + 1) * n_blocks], axis=1).astype(o_ref.dtype)


def _out_kernel(*refs, n_attn, alpha, n_sub):
    x_ref = refs[0]
    attn_refs = refs[1:1 + n_attn]
    wgm_ref, mk_ref, mv_ref, w_ref, g_ref, b_ref, o_ref = refs[1 + n_attn:]
    ts = x_ref.shape[0] // n_sub
    subs = [slice(i * ts, (i + 1) * ts) for i in range(n_sub)]
    w_m = MEM_HEADS * HEAD_DIM
    blocks = [slice(hb * LANES, (hb + 1) * LANES) for hb in range(w_m // LANES)]

    hs = [jnp.dot(x_ref[r, :].astype(BF16), wgm_ref[...], preferred_element_type=F32) for r in subs]
    mqs = [(h[:, :w_m] * (HEAD_DIM ** -0.5 * LOG2_E)).astype(BF16) for h in hs]
    scores = [[_nt_dot(_head_pair_rows(mq[:, c]), mk_ref[:, c]) for c in blocks] for mq in mqs]
    probs = [[jnp.exp2(s - jnp.max(s, axis=1, keepdims=True)) for s in row] for row in scores]
    mems = [[_merge_head_pair(jnp.dot(p.astype(BF16), mv_ref[:, c], preferred_element_type=F32)
                              / jnp.sum(p, axis=1, keepdims=True), ts) for p, c in zip(row, blocks)]
            for row in probs]
    gated = []
    for h, r, mem in zip(hs, subs, mems):
        gate = h[:, w_m:]
        mixed = jnp.concatenate([a[r, :].astype(F32) for a in attn_refs] + mem, axis=1)
        gated.append((mixed * (gate / (1.0 + jnp.exp(-gate)))).astype(BF16))
    ys = [jnp.dot(g, w_ref[...], preferred_element_type=F32) for g in gated]
    for y, r in zip(ys, subs):
        res = alpha * x_ref[r, :] + y
        c = res - jnp.mean(res, axis=1, keepdims=True)
        var = jnp.mean(c * c, axis=1, keepdims=True)
        o_ref[r, :] = c * lax.rsqrt(var + LN_EPS) * g_ref[...] + b_ref[...]


def _out_layer(x2, attn_outs, w_gm, mk, mv, w_out, ln_g, ln_b, seq, tm, alpha, name):
    m, d = x2.shape
    mem_len = mk.shape[0] // (m // seq)
    steps_per_batch = seq // tm

    def row(wd):
        return pl.BlockSpec((tm, wd), lambda i: (i, 0))

    def whole(a):
        return pl.BlockSpec(a.shape, lambda i: (0, 0))

    mem_spec = pl.BlockSpec((mem_len, mk.shape[1]), lambda i: (i // steps_per_batch, 0))
    in_specs = ([row(d)] + [row(a.shape[1]) for a in attn_outs]
                + [whole(w_gm), mem_spec, mem_spec, whole(w_out), whole(ln_g), whole(ln_b)])
    return pl.pallas_call(
        functools.partial(_out_kernel, n_attn=len(attn_outs), alpha=alpha, n_sub=OUT_SUB_TILES),
        grid=(m // tm,),
        in_specs=in_specs,
        out_specs=row(d),
        out_shape=jax.ShapeDtypeStruct((m, d), F32),
        compiler_params=pltpu.CompilerParams(dimension_semantics=("parallel",),
                                             vmem_limit_bytes=VMEM_LIMIT_BYTES),
        name=name,
    )(x2, *attn_outs, w_gm, mk, mv, w_out, ln_g, ln_b)


ROW_TILE = 1024
OUT_SUB_TILES = 8
IN_ROW_TILE = 1024
IN_EVEN_SUB_TILES = 1
IN_ODD_SUB_TILES = 4
SB_BLOCK = 128
SB_ROWS_PER_STEP = 1024
DIFF_BLOCK = 512
SWA_BLOCK = 128
SWA_ROWS_PER_STEP = 512


def _even_layer(x2, mem2, pos2, w_in, w_memkv, diff_lambda, diff_subln, w_out, ln_g, ln_b,
                layer_idx, batch, seq, alpha):
    lambda_init = 0.8 - 0.6 * math.exp(-0.3 * layer_idx)
    w_sb = SB_HEADS * HEAD_DIM
    w_df = DIFF_HEADS * HEAD_DIM
    w_attn = 3 * w_sb + 3 * w_df
    w_in = w_in.astype(BF16)
    sbq, sbk, sbv, dfq, dfk, dfv = _inproj(
        functools.partial(_inproj_even_kernel, n_sub=IN_EVEN_SUB_TILES), "inproj_even", x2, pos2,
        _inv_freq_lanes(DIFF_QK_DIM), w_in[:, :w_attn], (w_sb, w_sb, w_sb, w_df, w_df, w_df), IN_ROW_TILE)
    mk, mv = _memkv(mem2, w_memkv.astype(BF16))

    def b3(a):
        return a.reshape(batch, seq, a.shape[1])

    sb_o = _row_block_attention(functools.partial(_sb_kernel, blk=SB_BLOCK), "stick_breaking",
                                b3(sbq), b3(sbk), b3(sbv), (), (), SB_ROWS_PER_STEP)
    subln_lanes = jnp.tile(diff_subln.astype(F32), LANES // HEAD_DIM)[None, :]
    df_o = _diff_attention(b3(dfq), b3(dfk), b3(dfv), diff_lambda.astype(F32), subln_lanes,
                           DIFF_BLOCK, lambda_init)
    return _out_layer(x2, [sb_o.reshape(x2.shape[0], -1), df_o.reshape(x2.shape[0], -1)], w_in[:, w_attn:],
                      mk, mv, w_out.astype(BF16), ln_g[None, :], ln_b[None, :], seq, ROW_TILE, alpha, "out_even")


def _odd_layer(x2, mem2, pos2, w_in, w_memkv, sinks, w_out, ln_g, ln_b, batch, seq, alpha):
    w_q = SWA_Q_HEADS * HEAD_DIM
    w_kv = SWA_KV_HEADS * HEAD_DIM
    w_m = MEM_HEADS * HEAD_DIM

    def heads_in_order(a, axis, lo):
        return [lax.slice_in_dim(a, lo + h * HEAD_DIM, lo + (h + 1) * HEAD_DIM, axis=axis) for h in SW_HEAD_ORDER]

    gate_lo = w_q + 2 * w_kv + w_m
    last_kv = slice((SWA_KV_HEADS - 1) * HEAD_DIM, w_kv)
    w_in = w_in.astype(BF16)
    w_out = w_out.astype(BF16)
    k_w = w_in[:, w_q:w_q + w_kv]
    v_w = w_in[:, w_q + w_kv:w_q + 2 * w_kv]
    w_attn = jnp.concatenate(heads_in_order(w_in, 1, 0) + [k_w, k_w[:, last_kv], v_w, v_w[:, last_kv]], axis=1)
    w_gm = jnp.concatenate([w_in[:, w_q + 2 * w_kv:gate_lo]] + heads_in_order(w_in, 1, gate_lo)
                           + [w_in[:, gate_lo + w_q:]], axis=1)
    w_out_perm = jnp.concatenate(heads_in_order(w_out, 0, 0) + [w_out[w_q:]], axis=0)

    cq, ck, cv = _inproj(
        functools.partial(_inproj_odd_kernel, n_sub=IN_ODD_SUB_TILES), "inproj_odd", x2, pos2,
        _inv_freq_lanes(HEAD_DIM), w_attn, (w_q, 2 * LANES, 2 * LANES), IN_ROW_TILE)
    mk, mv = _memkv(mem2, w_memkv.astype(BF16))

    def b3(a):
        return a.reshape(batch, seq, a.shape[1])

    pairs_per_kv_block = LANES // HEAD_DIM * SWA_GROUP // 2
    sink_spec = [pl.BlockSpec(memory_space=pltpu.SMEM)]
    c_o = _row_block_attention(
        functools.partial(_swa_kernel, blk=SWA_BLOCK, pairs_per_kv_block=pairs_per_kv_block),
        "sliding_window", b3(cq), b3(ck), b3(cv), (sinks.astype(F32),), sink_spec,
        SWA_ROWS_PER_STEP)
    return _out_layer(x2, [c_o.reshape(x2.shape[0], -1)], w_gm, mk, mv, w_out_perm,
                      ln_g[None, :], ln_b[None, :], seq, ROW_TILE, alpha, "out_odd")


def kernel(x, mem, positions, w_in_even, w_memkv_even, diff_lambda_even, diff_subln_even, w_out_even,
           ln_g_even, ln_b_even, w_in_odd, w_memkv_odd, sinks_odd, w_out_odd, ln_g_odd, ln_b_odd):
    batch, seq, d = x.shape
    depth = w_in_even.shape[0] + w_in_odd.shape[0]
    alpha = (2 * depth) ** 0.25
    x2 = x.reshape(batch * seq, d)
    mem2 = mem.reshape(batch * mem.shape[1], d)
    pos2 = positions.reshape(batch * seq, 1)
    for i in range(depth):
        j = i // 2
        if i % 2 == 0:
            x2 = _even_layer(x2, mem2, pos2, w_in_even[j], w_memkv_even[j], diff_lambda_even[j],
                             diff_subln_even[j], w_out_even[j], ln_g_even[j], ln_b_even[j], i,
                             batch, seq, alpha)
        else:
            x2 = _odd_layer(x2, mem2, pos2, w_in_odd[j], w_memkv_odd[j], sinks_odd[j], w_out_odd[j],
                            ln_g_odd[j], ln_b_odd[j], batch, seq, alpha)
    return x2.reshape(batch, seq, d)
```

```python
import functools
import math

import jax
import jax.numpy as jnp
import numpy as np
from jax import lax
from jax.experimental import pallas as pl
from jax.experimental.pallas import tpu as pltpu

F32 = jnp.float32
BF16 = jnp.bfloat16

HEAD_DIM = 64
LANES = 128
MEM_HEADS = 4
SB_HEADS = 6
DIFF_HEADS = 6
DIFF_QK_DIM = 32
SWA_Q_HEADS = 12
SWA_KV_HEADS = 3
SWA_GROUP = 4
WINDOW = 128
ROPE_THETA = 500000.0
ROPE_FRACTION = 4
LN_EPS = 1e-5
NEG_BIG = -1e30
VMEM_LIMIT_BYTES = 48 * 1024 * 1024

SB_EXIT_COST = 92.0
LOG2_E = math.log2(math.e)

SW_HEAD_ORDER = (0, 4, 1, 5, 2, 6, 3, 7, 8, 9, 10, 11)


def _nt_dot(a, b):
    return lax.dot_general(a, b, (((1,), (1,)), ((), ())), preferred_element_type=F32)


def _head_pair_rows(q2):
    lane = lax.broadcasted_iota(jnp.int32, q2.shape, 1)
    low = lane < HEAD_DIM
    zero = jnp.zeros_like(q2)
    return jnp.concatenate([jnp.where(low, q2, zero), jnp.where(low, zero, q2)], axis=0)


def _merge_head_pair(o, rows):
    lane = lax.broadcasted_iota(jnp.int32, (rows, LANES), 1)
    return jnp.where(lane < HEAD_DIM, o[:rows], o[rows:])


def _rope_lane_tables(group, pos_f32, inv_freq_row):
    rot = group // ROPE_FRACTION
    half = rot // 2
    lane = lax.broadcasted_iota(jnp.int32, (1, LANES), 1)
    r = lane % group
    ang = pos_f32 * inv_freq_row
    cos = jnp.cos(ang)
    sin = jnp.sin(ang)
    cos_t = jnp.where(r < rot, cos, 1.0)
    sin_first = jnp.where(r < half, -sin, 0.0)
    sin_second = jnp.where((r >= half) & (r < rot), sin, 0.0)
    return cos_t, sin_first, sin_second, half


def _apply_rope(h, tables):
    cos_t, sin_first, sin_second, half = tables
    outs = []
    for c in range(h.shape[1] // LANES):
        blk = h[:, c * LANES:(c + 1) * LANES]
        outs.append(blk * cos_t + pltpu.roll(blk, LANES - half, 1) * sin_first
                    + pltpu.roll(blk, half, 1) * sin_second)
    return jnp.concatenate(outs, axis=1)


def _inv_freq_lanes(group):
    half = group // ROPE_FRACTION // 2
    inv_freq = np.exp(-(np.arange(half, dtype=np.float32) / np.float32(half)) * np.float32(math.log(ROPE_THETA)))
    lane = np.arange(LANES)
    return jnp.asarray(inv_freq.astype(np.float32)[(lane % group) % half][None, :])


def _projected_sub_tiles(x_ref, w_ref, n_sub):
    ts = x_ref.shape[0] // n_sub
    subs = [slice(i * ts, (i + 1) * ts) for i in range(n_sub)]
    return [(r, jnp.dot(x_ref[r, :].astype(BF16), w_ref[...], preferred_element_type=F32)) for r in subs]


def _sections(h):
    edges = [0]

    def sec(width):
        edges.append(edges[-1] + width)
        return h[:, edges[-2]:edges[-1]]

    return sec


def _inproj_even_kernel(x_ref, pos_ref, invf_ref, w_ref, sbq_ref, sbk_ref, sbv_ref,
                        dfq_ref, dfk_ref, dfv_ref, *, n_sub):
    w_sb = SB_HEADS * HEAD_DIM
    w_df = DIFF_HEADS * 2 * DIFF_QK_DIM
    q_scale = DIFF_QK_DIM ** -0.5 * LOG2_E
    for r, h in _projected_sub_tiles(x_ref, w_ref, n_sub):
        sec = _sections(h)
        tables = _rope_lane_tables(DIFF_QK_DIM, pos_ref[r, :].astype(F32), invf_ref[...])
        sbq_ref[r, :] = (sec(w_sb) * (HEAD_DIM ** -0.5 * LOG2_E)).astype(BF16)
        sbk_ref[r, :] = sec(w_sb).astype(BF16)
        sbv_ref[r, :] = sec(w_sb).astype(BF16)
        dfq_ref[r, :] = (_apply_rope(sec(w_df), tables) * q_scale).astype(BF16)
        dfk_ref[r, :] = _apply_rope(sec(w_df), tables).astype(BF16)
        dfv_ref[r, :] = sec(w_df).astype(BF16)


def _inproj_odd_kernel(x_ref, pos_ref, invf_ref, w_ref, cq_ref, ck_ref, cv_ref, *, n_sub):
    w_kv = ck_ref.shape[1]
    for r, h in _projected_sub_tiles(x_ref, w_ref, n_sub):
        sec = _sections(h)
        tables = _rope_lane_tables(HEAD_DIM, pos_ref[r, :].astype(F32), invf_ref[...])
        cq_ref[r, :] = (_apply_rope(sec(SWA_Q_HEADS * HEAD_DIM), tables)
                        * (HEAD_DIM ** -0.5 * LOG2_E)).astype(BF16)
        ck_ref[r, :] = _apply_rope(sec(w_kv), tables).astype(BF16)
        cv_ref[r, :] = sec(w_kv).astype(BF16)


def _inproj(kernel_fn, name, x2, pos2, invf, w, out_widths, tm):
    m, d = x2.shape
    return pl.pallas_call(
        kernel_fn,
        grid=(m // tm,),
        in_specs=[pl.BlockSpec((tm, d), lambda i: (i, 0)),
                  pl.BlockSpec((tm, 1), lambda i: (i, 0)),
                  pl.BlockSpec((1, LANES), lambda i: (0, 0)),
                  pl.BlockSpec(w.shape, lambda i: (0, 0))],
        out_specs=[pl.BlockSpec((tm, wd), lambda i: (i, 0)) for wd in out_widths],
        out_shape=[jax.ShapeDtypeStruct((m, wd), BF16) for wd in out_widths],
        compiler_params=pltpu.CompilerParams(dimension_semantics=("parallel",),
                                             vmem_limit_bytes=VMEM_LIMIT_BYTES),
        name=name,
    )(x2, pos2, invf, w)


def _strict_lower(n):
    return (lax.broadcasted_iota(jnp.int32, (n, n), 0)
            > lax.broadcasted_iota(jnp.int32, (n, n), 1)).astype(BF16)


def _sb_tiles(chains, later):
    zs = [_nt_dot(qq, kk) if strict is None else jnp.where(strict, _nt_dot(qq, kk), NEG_BIG)
          for qq, kk, _, strict, _ in chains]
    costs = []
    for z in zs:
        cost = jnp.maximum(z, 0.0) + jnp.log(1.0 + jnp.exp2(-jnp.abs(z))) * LOG2_E
        hi = cost.astype(BF16)
        lo = (cost - hi.astype(F32)).astype(BF16)
        costs.append((cost, hi, lo))
    afters = [jnp.dot(hi, later, preferred_element_type=F32) + jnp.dot(lo, later, preferred_element_type=F32)
              for _, hi, lo in costs]
    outs = []
    for z, (cost, _, _), after, (_, _, vv, _, carry) in zip(zs, costs, afters, chains):
        w = jnp.exp2(z - (cost + after + carry))
        outs.append((jnp.dot(w.astype(BF16), vv, preferred_element_type=F32),
                     jnp.sum(cost, axis=1, keepdims=True)))
    return outs


def _sb_kernel(q_ref, k_ref, v_ref, o_ref, *, blk):
    n_pairs = q_ref.shape[1] // LANES
    n_sub = q_ref.shape[0] // blk

    def lanes(hp):
        return slice(hp * LANES, (hp + 1) * LANES)

    rows = lax.broadcasted_iota(jnp.int32, (2 * blk, 2 * blk), 0)
    q_off = jnp.where(rows >= blk, rows - blk, rows)
    col = lax.broadcasted_iota(jnp.int32, (2 * blk, 2 * blk), 1)
    later2 = _strict_lower(2 * blk)
    later1 = _strict_lower(blk)
    no_carry = jnp.zeros((2 * blk, 1), F32)

    first = []
    for sub in range(n_sub):
        r0 = pl.multiple_of((pl.program_id(1) * n_sub + sub) * blk, blk)
        p0 = pl.multiple_of(jnp.maximum(r0 - blk, 0), blk)
        strict = (col - blk < q_off) & (col + (r0 - blk) >= 0)
        for hp in range(n_pairs):
            qq = _head_pair_rows(q_ref[sub * blk:(sub + 1) * blk, lanes(hp)])
            kk = jnp.concatenate([k_ref[pl.ds(p0, blk), lanes(hp)], k_ref[pl.ds(r0, blk), lanes(hp)]], axis=0)
            vv = jnp.concatenate([v_ref[pl.ds(p0, blk), lanes(hp)], v_ref[pl.ds(r0, blk), lanes(hp)]], axis=0)
            first.append((qq, kk, vv, strict, no_carry))
    started = _sb_tiles(first, later2)

    def finish(sub):
        bi = pl.program_id(1) * n_sub + sub
        mine = slice(sub * n_pairs, (sub + 1) * n_pairs)
        qqs = [chain[0] for chain in first[mine]]
        accs = tuple(pv for pv, _ in started[mine])
        carries = tuple(dsum for _, dsum in started[mine])

        def cond(c):
            j, carries, _ = c
            return (j >= 0) & (jnp.min(functools.reduce(jnp.minimum, carries)) < SB_EXIT_COST)

        def body(c):
            j, carries, accs = c
            start = pl.multiple_of(j * blk, blk)
            outs = _sb_tiles([(qqs[hp], k_ref[pl.ds(start, blk), lanes(hp)], v_ref[pl.ds(start, blk), lanes(hp)],
                               None, carries[hp]) for hp in range(n_pairs)], later1)
            return (j - 1, tuple(c + dsum for c, (_, dsum) in zip(carries, outs)),
                    tuple(a + pv for a, (pv, _) in zip(accs, outs)))

        _, _, accs = lax.while_loop(cond, body, (bi - 2, carries, accs))
        o_ref[sub * blk:(sub + 1) * blk, :] = jnp.concatenate(
            [_merge_head_pair(a, blk) for a in accs], axis=1).astype(o_ref.dtype)

    for sub in range(n_sub):
        finish(sub)


def _row_block_attention(kernel_fn, name, q, k, v, extra_inputs, extra_specs, blk):
    b, s, wq = q.shape
    q_spec = pl.BlockSpec((None, blk, wq), lambda bi, qi: (bi, qi, 0))
    kv_spec = pl.BlockSpec((None, s, k.shape[2]), lambda bi, qi: (bi, 0, 0))
    return pl.pallas_call(
        kernel_fn,
        grid=(b, s // blk),
        in_specs=list(extra_specs) + [q_spec, kv_spec, kv_spec],
        out_specs=q_spec,
        out_shape=jax.ShapeDtypeStruct(q.shape, BF16),
        compiler_params=pltpu.CompilerParams(
            dimension_semantics=("parallel", "arbitrary"),
            vmem_limit_bytes=VMEM_LIMIT_BYTES),
        name=name,
    )(*extra_inputs, q, k, v)


def _diff_kernel(dl_ref, subln_ref, q_ref, k_ref, v_ref, o_ref, vp_ref, sa_ref, sb_ref, sd_ref, acc_ref, *,
                 blk, lambda_init):
    qi = pl.program_id(2)
    seq = v_ref.shape[0]

    @pl.when(qi == 0)
    def _():
        low = lax.broadcasted_iota(jnp.int32, (blk, LANES), 1) < HEAD_DIM

        def build(c, _):
            rows = pl.ds(pl.multiple_of(c * blk, blk), blk)
            vf = v_ref[rows, :].astype(F32)
            vp_ref[rows, :LANES] = jnp.where(low, vf, 1.0).astype(BF16)
            vp_ref[rows, LANES:] = jnp.where(low, pltpu.roll(vf, HEAD_DIM, 1), 1.0).astype(BF16)
            return 0

        lax.fori_loop(0, seq // blk, build, 0)

    q2 = q_ref[...]
    lane = lax.broadcasted_iota(jnp.int32, q2.shape, 1)
    zero = jnp.zeros_like(q2)
    n_maps = LANES // DIFF_QK_DIM
    qq = jnp.concatenate([jnp.where(lane // DIFF_QK_DIM == c, q2, zero) for c in range(n_maps)], axis=0)
    rows_n = n_maps * blk
    half = rows_n // 2

    wide = 2 * blk

    def wide_keys(block):
        return pl.ds(pl.multiple_of(block * wide, wide), wide)

    def narrow_keys(start):
        return pl.ds(pl.multiple_of(start, blk), blk)

    def scores(s_ref, block):
        s_ref[...] = _nt_dot(qq, k_ref[wide_keys(block), :])

    def update(s, keys, m, diagonal=False):
        if diagonal:
            q_off = lax.broadcasted_iota(jnp.int32, s.shape, 0) % blk
            k_off = lax.broadcasted_iota(jnp.int32, s.shape, 1)
            s = jnp.where(k_off <= q_off, s, NEG_BIG)
        m_new = jnp.maximum(m, jnp.max(s, axis=1, keepdims=True))
        p = jnp.exp2(s - m_new).astype(BF16)
        alpha = jnp.exp2(m - m_new)
        for h, v_lanes in enumerate((slice(0, LANES), slice(LANES, 2 * LANES))):
            rows = slice(h * half, (h + 1) * half)
            acc_ref[rows, :] = alpha[rows] * acc_ref[rows, :] + jnp.dot(
                p[rows], vp_ref[keys, v_lanes], preferred_element_type=F32)
        return m_new

    def finish():
        acc = acc_ref[...]
        dl = dl_ref[...]
        lam = (jnp.exp(jnp.sum(dl[0:1] * dl[1:2], axis=1, keepdims=True))
               - jnp.exp(jnp.sum(dl[2:3] * dl[3:4], axis=1, keepdims=True)) + lambda_init)
        low = lax.broadcasted_iota(jnp.int32, (blk, LANES), 1) < HEAD_DIM
        ones = jnp.ones((LANES, LANES), BF16)
        normed = []
        for h in range(2):
            n0 = acc[2 * h * blk:(2 * h + 1) * blk]
            n1 = acc[(2 * h + 1) * blk:(2 * h + 2) * blk]
            l0 = pltpu.roll(n0, HEAD_DIM, 1)
            l1 = pltpu.roll(n1, HEAD_DIM, 1)
            d = n0 - (lam * l0 / l1) * n1
            dsq = jnp.where(low, d * d, 0.0)
            hi = dsq.astype(BF16)
            lo = (dsq - hi.astype(F32)).astype(BF16)
            ms = (jnp.dot(hi, ones, preferred_element_type=F32)
                  + jnp.dot(lo, ones, preferred_element_type=F32)) * (1.0 / HEAD_DIM)
            normed.append(d * lax.rsqrt(ms + LN_EPS * l0 * l0))
        y = jnp.where(low, normed[0], pltpu.roll(normed[1], HEAD_DIM, 1))
        o_ref[...] = (y * subln_ref[...] * (1.0 - lambda_init)).astype(o_ref.dtype)

    n_wide = qi // 2
    acc_ref[...] = jnp.zeros(acc_ref.shape, F32)
    sd_ref[...] = _nt_dot(qq, k_ref[narrow_keys(qi * blk), :])
    scores(sa_ref, 0)
    m = update(sd_ref[...], narrow_keys(qi * blk), jnp.full((rows_n, 1), NEG_BIG, F32), diagonal=True)

    def wide_update(s_ref, block, m):
        return update(s_ref[...], wide_keys(block), m)

    def leftover_update(s_ref, block, m):
        return update(s_ref[:, :blk], narrow_keys(block * wide), m)

    def body(i, m):
        scores(sb_ref, 2 * i + 1)
        m = wide_update(sa_ref, 2 * i, m)
        scores(sa_ref, 2 * i + 2)
        return wide_update(sb_ref, 2 * i + 1, m)

    n_loop = jnp.maximum(n_wide - 1, 0) // 2
    m = lax.fori_loop(0, n_loop, body, m)
    first = 2 * n_loop
    left = n_wide - first
    odd = qi % 2

    def tail(n_left, is_odd):
        def run():
            mm = m
            if n_left >= 1:
                if n_left == 2 or is_odd:
                    scores(sb_ref, first + 1)
                mm = wide_update(sa_ref, first, mm)
            if n_left == 2:
                if is_odd:
                    scores(sa_ref, first + 2)
                mm = wide_update(sb_ref, first + 1, mm)
            if is_odd:
                leftover_update(sa_ref if n_left in (0, 2) else sb_ref, n_wide, mm)
            finish()
        pl.when(jnp.logical_and(left == n_left, odd == int(is_odd)))(run)

    for n_left in range(3):
        for is_odd in (False, True):
            tail(n_left, is_odd)


def _diff_attention(q, k, v, diff_lambda, subln_lanes, blk, lambda_init):
    b, s, wq = q.shape
    assert s % (2 * blk) == 0
    rows_n = LANES // DIFF_QK_DIM * blk
    q_spec = pl.BlockSpec((None, blk, LANES), lambda bi, hp, qi: (bi, qi, hp))
    kv_spec = pl.BlockSpec((None, s, LANES), lambda bi, hp, qi: (bi, 0, hp))
    return pl.pallas_call(
        functools.partial(_diff_kernel, blk=blk, lambda_init=lambda_init),
        grid=(b, wq // LANES, s // blk),
        in_specs=[pl.BlockSpec(diff_lambda.shape, lambda bi, hp, qi: (0, 0)),
                  pl.BlockSpec((1, LANES), lambda bi, hp, qi: (0, 0)), q_spec, kv_spec, kv_spec],
        out_specs=q_spec,
        out_shape=jax.ShapeDtypeStruct(q.shape, BF16),
        scratch_shapes=[pltpu.VMEM((s, 2 * LANES), BF16),
                        pltpu.VMEM((rows_n, 2 * blk), F32),
                        pltpu.VMEM((rows_n, 2 * blk), F32),
                        pltpu.VMEM((rows_n, blk), F32),
                        pltpu.VMEM((rows_n, LANES), F32)],
        compiler_params=pltpu.CompilerParams(
            dimension_semantics=("parallel", "parallel", "arbitrary"),
            vmem_limit_bytes=VMEM_LIMIT_BYTES),
        name="differential",
    )(diff_lambda, subln_lanes, q, k, v)


def _swa_kernel(sinks_ref, q_ref, k_ref, v_ref, o_ref, *, blk, pairs_per_kv_block):
    n_sub = q_ref.shape[0] // blk

    def lanes(hb):
        return slice(hb * LANES, (hb + 1) * LANES)

    shape = (2 * blk, WINDOW + blk)
    rows = lax.broadcasted_iota(jnp.int32, shape, 0)
    q_off = jnp.where(rows >= blk, rows - blk, rows)
    col = lax.broadcasted_iota(jnp.int32, shape, 1)
    rel = (col - WINDOW) - q_off
    in_window = (rel <= 0) & (rel > -WINDOW)
    second_head = lax.broadcasted_iota(jnp.int32, (2 * blk, 1), 0) >= blk
    sink_lane = lax.broadcasted_iota(jnp.int32, (2 * blk, WINDOW), 1) == 0
    sink_row = lax.broadcasted_iota(jnp.int32, (WINDOW, LANES), 0) == 0
    masked_cur = jnp.full((2 * blk, blk), NEG_BIG, F32)
    n_blocks = q_ref.shape[1] // LANES
    chains = []
    for sub in range(n_sub):
        r0 = pl.multiple_of((pl.program_id(1) * n_sub + sub) * blk, blk)
        p0 = pl.multiple_of(jnp.maximum(r0 - WINDOW, 0), WINDOW)
        band = in_window & (col + (r0 - WINDOW) >= 0)
        keys_values = {}
        for hb in range(n_blocks):
            kvb = hb // pairs_per_kv_block
            if kvb not in keys_values:
                v_prev = v_ref[pl.ds(p0, WINDOW), lanes(kvb)]
                keys_values[kvb] = (
                    jnp.concatenate([k_ref[pl.ds(p0, WINDOW), lanes(kvb)], k_ref[pl.ds(r0, blk), lanes(kvb)]], axis=0),
                    jnp.concatenate([jnp.where(sink_row, jnp.zeros_like(v_prev), v_prev),
                                     v_ref[pl.ds(r0, blk), lanes(kvb)]], axis=0))
            kk, vv = keys_values[kvb]
            qq = _head_pair_rows(q_ref[sub * blk:(sub + 1) * blk, lanes(hb)])
            sink = jnp.where(second_head, sinks_ref[SW_HEAD_ORDER[2 * hb + 1]],
                             sinks_ref[SW_HEAD_ORDER[2 * hb]]) * LOG2_E
            fill = jnp.concatenate([jnp.where(sink_lane, sink, NEG_BIG), masked_cur], axis=1)
            chains.append((qq, kk, vv, band, fill))
    scores = [jnp.where(band, _nt_dot(qq, kk), fill) for qq, kk, _, band, fill in chains]
    probs = []
    for s in scores:
        p = jnp.exp2(s - jnp.max(s, axis=1, keepdims=True))
        probs.append((p.astype(BF16), jnp.sum(p, axis=1, keepdims=True)))
    outs = [_merge_head_pair(jnp.dot(p, vv, preferred_element_type=F32) / denom, blk)
            for (p, denom), (_, _, vv, _, _) in zip(probs, chains)]
    for sub in range(n_sub):
        o_ref[sub * blk:(sub + 1) * blk, :] = jnp.concatenate(
            outs[sub * n_blocks:(sub + 1) * n_blocks], axis=1).astype(o_ref.dtype)


def _out_kernel(*refs, n_attn, alpha, n_sub, steps_per_batch, mem_len):
    x_ref = refs[0]
    attn_refs = refs[1:1 + n_attn]
    wgm_ref, mem_ref, wkv_ref, w_ref, g_ref, b_ref, o_ref, mkv_ref = refs[1 + n_attn:]
    ts = x_ref.shape[0] // n_sub
    subs = [slice(i * ts, (i + 1) * ts) for i in range(n_sub)]
    w_m = MEM_HEADS * HEAD_DIM
    blocks = [slice(hb * LANES, (hb + 1) * LANES) for hb in range(w_m // LANES)]

    @pl.when(pl.program_id(0) == 0)
    def _():
        mkv_ref[...] = jnp.dot(mem_ref[...].astype(BF16), wkv_ref[...], preferred_element_type=F32).astype(BF16)

    mem_rows = pl.ds(pl.multiple_of(pl.program_id(0) // steps_per_batch * mem_len, mem_len), mem_len)
    mk_ref = mkv_ref.at[mem_rows, :w_m]
    mv_ref = mkv_ref.at[mem_rows, w_m:]

    hs = [jnp.dot(x_ref[r, :].astype(BF16), wgm_ref[...], preferred_element_type=F32) for r in subs]
    mqs = [(h[:, :w_m] * (HEAD_DIM ** -0.5 * LOG2_E)).astype(BF16) for h in hs]
    scores = [[_nt_dot(_head_pair_rows(mq[:, c]), mk_ref[:, c]) for c in blocks] for mq in mqs]
    probs = [[jnp.exp2(s - jnp.max(s, axis=1, keepdims=True)) for s in row] for row in scores]
    mems = [[_merge_head_pair(jnp.dot(p.astype(BF16), mv_ref[:, c], preferred_element_type=F32)
                              / jnp.sum(p, axis=1, keepdims=True), ts) for p, c in zip(row, blocks)]
            for row in probs]
    gated = []
    for h, r, mem in zip(hs, subs, mems):
        gate = h[:, w_m:]
        mixed = jnp.concatenate([a[r, :].astype(F32) for a in attn_refs] + mem, axis=1)
        gated.append((mixed * (gate / (1.0 + jnp.exp(-gate)))).astype(BF16))
    ys = [jnp.dot(g, w_ref[...], preferred_element_type=F32) for g in gated]
    for y, r in zip(ys, subs):
        res = alpha * x_ref[r, :] + y
        c = res - jnp.mean(res, axis=1, keepdims=True)
        var = jnp.mean(c * c, axis=1, keepdims=True)
        o_ref[r, :] = c * lax.rsqrt(var + LN_EPS) * g_ref[...] + b_ref[...]


def _out_layer(x2, attn_outs, w_gm, mem2, w_memkv, w_out, ln_g, ln_b, seq, tm, alpha, name):
    m, d = x2.shape
    mem_len = mem2.shape[0] // (m // seq)
    steps_per_batch = seq // tm

    def row(wd):
        return pl.BlockSpec((tm, wd), lambda i: (i, 0))

    def whole(a):
        return pl.BlockSpec(a.shape, lambda i: (0, 0), pipeline_mode=pl.Buffered(1))

    in_specs = ([row(d)] + [row(a.shape[1]) for a in attn_outs]
                + [whole(w_gm), whole(mem2), whole(w_memkv), whole(w_out), whole(ln_g), whole(ln_b)])
    return pl.pallas_call(
        functools.partial(_out_kernel, n_attn=len(attn_outs), alpha=alpha, n_sub=OUT_SUB_TILES,
                          steps_per_batch=steps_per_batch, mem_len=mem_len),
        grid=(m // tm,),
        in_specs=in_specs,
        out_specs=row(d),
        out_shape=jax.ShapeDtypeStruct((m, d), F32),
        scratch_shapes=[pltpu.VMEM((mem2.shape[0], w_memkv.shape[1]), BF16)],
        compiler_params=pltpu.CompilerParams(dimension_semantics=("arbitrary",),
                                             vmem_limit_bytes=VMEM_LIMIT_BYTES),
        name=name,
    )(x2, *attn_outs, w_gm, mem2, w_memkv, w_out, ln_g, ln_b)


ROW_TILE = 1024
OUT_SUB_TILES = 4
IN_ROW_TILE = 1024
IN_EVEN_SUB_TILES = 1
IN_ODD_SUB_TILES = 4
SB_BLOCK = 128
SB_ROWS_PER_STEP = 512
DIFF_BLOCK = 512
SWA_BLOCK = 128
SWA_ROWS_PER_STEP = 512


def _even_layer(x2, mem2, pos2, w_in, w_memkv, diff_lambda, diff_subln, w_out, ln_g, ln_b,
                layer_idx, batch, seq, alpha):
    lambda_init = 0.8 - 0.6 * math.exp(-0.3 * layer_idx)
    w_sb = SB_HEADS * HEAD_DIM
    w_df = DIFF_HEADS * HEAD_DIM
    w_attn = 3 * w_sb + 3 * w_df
    w_in = w_in.astype(BF16)
    sbq, sbk, sbv, dfq, dfk, dfv = _inproj(
        functools.partial(_inproj_even_kernel, n_sub=IN_EVEN_SUB_TILES), "inproj_even", x2, pos2,
        _inv_freq_lanes(DIFF_QK_DIM), w_in[:, :w_attn], (w_sb, w_sb, w_sb, w_df, w_df, w_df), IN_ROW_TILE)

    def b3(a):
        return a.reshape(batch, seq, a.shape[1])

    sb_o = _row_block_attention(functools.partial(_sb_kernel, blk=SB_BLOCK), "stick_breaking",
                                b3(sbq), b3(sbk), b3(sbv), (), (), SB_ROWS_PER_STEP)
    subln_lanes = jnp.tile(diff_subln.astype(F32), LANES // HEAD_DIM)[None, :]
    df_o = _diff_attention(b3(dfq), b3(dfk), b3(dfv), diff_lambda.astype(F32), subln_lanes,
                           DIFF_BLOCK, lambda_init)
    return _out_layer(x2, [sb_o.reshape(x2.shape[0], -1), df_o.reshape(x2.shape[0], -1)], w_in[:, w_attn:],
                      mem2, w_memkv.astype(BF16), w_out.astype(BF16), ln_g[None, :], ln_b[None, :], seq, ROW_TILE,
                      alpha, "out_even")


def _odd_layer(x2, mem2, pos2, w_in, w_memkv, sinks, w_out, ln_g, ln_b, batch, seq, alpha):
    w_q = SWA_Q_HEADS * HEAD_DIM
    w_kv = SWA_KV_HEADS * HEAD_DIM
    w_m = MEM_HEADS * HEAD_DIM

    def heads_in_order(a, axis, lo):
        return [lax.slice_in_dim(a, lo + h * HEAD_DIM, lo + (h + 1) * HEAD_DIM, axis=axis) for h in SW_HEAD_ORDER]

    gate_lo = w_q + 2 * w_kv + w_m
    last_kv = slice((SWA_KV_HEADS - 1) * HEAD_DIM, w_kv)
    w_in = w_in.astype(BF16)
    w_out = w_out.astype(BF16)
    k_w = w_in[:, w_q:w_q + w_kv]
    v_w = w_in[:, w_q + w_kv:w_q + 2 * w_kv]
    w_attn = jnp.concatenate(heads_in_order(w_in, 1, 0) + [k_w, k_w[:, last_kv], v_w, v_w[:, last_kv]], axis=1)
    w_gm = jnp.concatenate([w_in[:, w_q + 2 * w_kv:gate_lo]] + heads_in_order(w_in, 1, gate_lo)
                           + [w_in[:, gate_lo + w_q:]], axis=1)
    w_out_perm = jnp.concatenate(heads_in_order(w_out, 0, 0) + [w_out[w_q:]], axis=0)

    cq, ck, cv = _inproj(
        functools.partial(_inproj_odd_kernel, n_sub=IN_ODD_SUB_TILES), "inproj_odd", x2, pos2,
        _inv_freq_lanes(HEAD_DIM), w_attn, (w_q, 2 * LANES, 2 * LANES), IN_ROW_TILE)

    def b3(a):
        return a.reshape(batch, seq, a.shape[1])

    pairs_per_kv_block = LANES // HEAD_DIM * SWA_GROUP // 2
    sink_spec = [pl.BlockSpec(memory_space=pltpu.SMEM)]
    c_o = _row_block_attention(
        functools.partial(_swa_kernel, blk=SWA_BLOCK, pairs_per_kv_block=pairs_per_kv_block),
        "sliding_window", b3(cq), b3(ck), b3(cv), (sinks.astype(F32),), sink_spec,
        SWA_ROWS_PER_STEP)
    return _out_layer(x2, [c_o.reshape(x2.shape[0], -1)], w_gm, mem2, w_memkv.astype(BF16), w_out_perm,
                      ln_g[None, :], ln_b[None, :], seq, ROW_TILE, alpha, "out_odd")


def kernel(x, mem, positions, w_in_even, w_memkv_even, diff_lambda_even, diff_subln_even, w_out_even,
           ln_g_even, ln_b_even, w_in_odd, w_memkv_odd, sinks_odd, w_out_odd, ln_g_odd, ln_b_odd):
    batch, seq, d = x.shape
    depth = w_in_even.shape[0] + w_in_odd.shape[0]
    alpha = (2 * depth) ** 0.25
    x2 = x.reshape(batch * seq, d)
    mem2 = mem.reshape(batch * mem.shape[1], d)
    pos2 = positions.reshape(batch * seq, 1)
    for i in range(depth):
        j = i // 2
        if i % 2 == 0:
            x2 = _even_layer(x2, mem2, pos2, w_in_even[j], w_memkv_even[j], diff_lambda_even[j],
                             diff_subln_even[j], w_out_even[j], ln_g_even[j], ln_b_even[j], i,
                             batch, seq, alpha)
        else:
            x2 = _odd_layer(x2, mem2, pos2, w_in_odd[j], w_memkv_odd[j], sinks_odd[j], w_out_odd[j],
                            ln_g_odd[j], ln_b_odd[j], batch, seq, alpha)
    return x2.reshape(batch, seq, d)
```
